```python
import math
import jax, jax.numpy as jnp
from jax import lax
import numpy as np

D_MODEL = 1024
BATCH = 1
SEQ = 16384
DEPTH = 1
DEC_BATCH = 128
DEC_SEQ = 4
PAST_LEN = 16384
PAGE_SIZE = 128

N_META = 16
EPS = 1e-6
NEG_INF = -1e30
GDN_HEADS = 4
GDN_DK = 128
GDN_DV = 128
GDN_QK_WIDTH = GDN_HEADS * GDN_DK
GDN_V_WIDTH = GDN_HEADS * GDN_DV
CONV_WIDTH = 4
CONV_DIM = 2 * GDN_QK_WIDTH + GDN_V_WIDTH
CHUNK = 64
MLA_HEADS = 8
Q_RANK = 384
KV_RANK = 256
NOPE_DIM = 64
ROPE_DIM = 32
V_DIM = 64
QK_DIM = NOPE_DIM + ROPE_DIM
ROPE_THETA = 10000.0
Q_BLOCK = 128
MIX_WIDTH = GDN_V_WIDTH + MLA_HEADS * V_DIM
D_IN = CONV_DIM + GDN_V_WIDTH + 2 * GDN_HEADS + Q_RANK + KV_RANK + ROPE_DIM
D_FF = ((8 * D_MODEL + 3 * 256 - 1) // (3 * 256)) * 256

kernel_name = 'hymba_gdn_mla_step'


def rms_norm(x, gain):
    xf = x.astype(jnp.float32)
    y = xf * lax.rsqrt(jnp.mean(xf * xf, axis=-1, keepdims=True) + EPS)
    return (y * gain.astype(jnp.float32)).astype(x.dtype)


def l2_norm(x):
    xf = x.astype(jnp.float32)
    return xf * lax.rsqrt(jnp.sum(xf * xf, axis=-1, keepdims=True) + EPS)


def rope_cos_sin(pos):
    inv_freq = jnp.power(ROPE_THETA, -jnp.arange(0, ROPE_DIM, 2, dtype=jnp.float32) / ROPE_DIM)
    ang = pos.astype(jnp.float32)[:, None] * inv_freq[None, :]
    return jnp.cos(ang), jnp.sin(ang)


def apply_rope(x, cos, sin):
    x1, x2 = jnp.split(x.astype(jnp.float32), 2, axis=-1)
    return jnp.concatenate([x1 * cos - x2 * sin, x1 * sin + x2 * cos], axis=-1).astype(x.dtype)


def split_proj(proj):
    sizes = (CONV_DIM, GDN_V_WIDTH, GDN_HEADS, GDN_HEADS, Q_RANK, KV_RANK, ROPE_DIM)
    points, acc = [], 0
    for s in sizes[:-1]:
        acc += s
        points.append(acc)
    return jnp.split(proj, points, axis=-1)


def causal_conv_silu(x_pad, w):
    L = x_pad.shape[1] - (CONV_WIDTH - 1)
    out = x_pad[:, 0:L] * w[0]
    for i in range(1, CONV_WIDTH):
        out = out + x_pad[:, i:i + L] * w[i]
    return jax.nn.silu(out)


def gdn_inputs(qkv_pre, conv_prev, a, b, lp):
    B, L, _ = qkv_pre.shape
    x_pad = jnp.concatenate([conv_prev.astype(qkv_pre.dtype), qkv_pre], axis=1)
    qkv = causal_conv_silu(x_pad, lp['conv_w']).astype(jnp.float32)
    q, k, v = jnp.split(qkv, [GDN_QK_WIDTH, 2 * GDN_QK_WIDTH], axis=-1)
    q = l2_norm(q.reshape(B, L, GDN_HEADS, GDN_DK)) * (GDN_DK ** -0.5)
    k = l2_norm(k.reshape(B, L, GDN_HEADS, GDN_DK))
    v = v.reshape(B, L, GDN_HEADS, GDN_DV)
    g = -jnp.exp(lp['a_log'].astype(jnp.float32)) * jax.nn.softplus(
        a.astype(jnp.float32) + lp['dt_bias'].astype(jnp.float32))
    beta = jax.nn.sigmoid(b.astype(jnp.float32))
    return q, k, v, g, beta, x_pad[:, -(CONV_WIDTH - 1):]


def gdn_chunked(q, k, v, g, beta, s0):
    q, k, v = (jnp.swapaxes(t, 2, 3) for t in (q, k, v))
    g, beta = jnp.swapaxes(g, 2, 3), jnp.swapaxes(beta, 2, 3)
    C = q.shape[-2]
    G = jnp.cumsum(g, axis=-1)
    incl = jnp.tril(jnp.ones((C, C), dtype=bool))
    strict = jnp.tril(jnp.ones((C, C), dtype=bool), -1)
    diff = G[..., :, None] - G[..., None, :]
    decay = jnp.where(incl, jnp.exp(jnp.where(incl, diff, 0.0)), 0.0)
    kk = jnp.einsum('bnhid,bnhjd->bnhij', k, k)
    m = jnp.where(strict, beta[..., :, None] * kk * decay, 0.0)
    lhs = jnp.eye(C, dtype=jnp.float32) + m
    rhs = jnp.concatenate([(beta * jnp.exp(G))[..., None] * k, beta[..., None] * v], axis=-1)
    sol = lax.linalg.triangular_solve(lhs, rhs, left_side=True, lower=True, unit_diagonal=True)
    w, u_tilde = sol[..., :GDN_DK], sol[..., GDN_DK:]
    qk = jnp.einsum('bnhid,bnhjd->bnhij', q, k) * decay
    q_dec = jnp.exp(G)[..., None] * q
    k_dec = jnp.exp(G[..., -1:] - G)[..., None] * k
    g_last = jnp.exp(G[..., -1])
    xs = tuple(jnp.moveaxis(t, 1, 0) for t in (w, u_tilde, qk, q_dec, k_dec, g_last))

    def step(s, inp):
        w_c, ut_c, qk_c, qd_c, kd_c, gl_c = inp
        u = ut_c - jnp.einsum('bhld,bhde->bhle', w_c, s)
        o = jnp.einsum('bhld,bhde->bhle', qd_c, s) + jnp.einsum('bhij,bhje->bhie', qk_c, u)
        s = gl_c[..., None, None] * s + jnp.einsum('bhld,bhle->bhde', kd_c, u)
        return s, o

    s_final, o = lax.scan(step, s0, xs)
    o = jnp.swapaxes(jnp.moveaxis(o, 0, 1), 2, 3)
    return o, s_final


def gdn_output(o, z, gain, dtype):
    B, L = o.shape[0], o.shape[1]
    zf = z.astype(jnp.float32).reshape(B, L, GDN_HEADS, GDN_DV)
    y = rms_norm(o, gain) * jax.nn.silu(zf)
    return y.reshape(B, L, GDN_V_WIDTH).astype(dtype)


def mla_project(c_q, c_kv, k_pe, pos, lp):
    B, L, _ = c_q.shape
    cos, sin = rope_cos_sin(pos)
    q = (rms_norm(c_q, lp['q_a_norm']) @ lp['w_q_b']).reshape(B, L, MLA_HEADS, QK_DIM)
    q_nope = rms_norm(q[..., :NOPE_DIM], lp['q_nope_norm'])
    q_pe = apply_rope(rms_norm(q[..., NOPE_DIM:], lp['q_pe_norm']), cos[:, None], sin[:, None])
    ckv = rms_norm(c_kv, lp['kv_a_norm'])
    kpe = apply_rope(rms_norm(k_pe, lp['k_pe_norm']), cos, sin)
    return q_nope, q_pe, ckv, kpe


def expand_k_nope(ckv, lp):
    return rms_norm(jnp.einsum('btr,rhd->bthd', ckv, lp['w_uk']), lp['k_nope_norm'])


def mla_prompt_attn(q_nope, q_pe, ckv, kpe, lp):
    B, T = q_nope.shape[0], q_nope.shape[1]
    scale = QK_DIM ** -0.5
    k_nope = expand_k_nope(ckv, lp)
    v = jnp.einsum('btr,rhd->bthd', ckv, lp['w_uv'])
    n_blocks = -(-T // Q_BLOCK)
    pad = n_blocks * Q_BLOCK - T

    def to_blocks(t):
        t = jnp.pad(t, ((0, 0), (0, pad), (0, 0), (0, 0)))
        return jnp.moveaxis(t.reshape((B, n_blocks, Q_BLOCK) + t.shape[2:]), 1, 0)

    k_pos = jnp.arange(T)

    def block(args):
        qn_b, qp_b, start = args
        s = (jnp.einsum('bqhd,bkhd->bhqk', qn_b, k_nope).astype(jnp.float32)
             + jnp.einsum('bqhd,bkd->bhqk', qp_b, kpe).astype(jnp.float32)) * scale
        q_pos = start + jnp.arange(Q_BLOCK)
        s = jnp.where(k_pos[None, :] <= q_pos[:, None], s, NEG_INF)
        p = jax.nn.softmax(s, axis=-1).astype(v.dtype)
        return jnp.einsum('bhqk,bkhd->bqhd', p, v)

    o = lax.map(block, (to_blocks(q_nope), to_blocks(q_pe), jnp.arange(n_blocks) * Q_BLOCK))
    o = jnp.moveaxis(o, 0, 1).reshape(B, n_blocks * Q_BLOCK, MLA_HEADS * V_DIM)
    return o[:, :T]


def mla_sample_attn(q_nope, q_pe, ckv_new, kpe_new, cache_ckv, cache_kpe, page_table, lp):
    B, L = q_nope.shape[0], q_nope.shape[1]
    scale = QK_DIM ** -0.5

    def partial(ckv, kpe, mask):
        k_nope = expand_k_nope(ckv, lp)
        s = (jnp.einsum('blhd,bphd->blhp', q_nope, k_nope).astype(jnp.float32)
             + jnp.einsum('blhd,bpd->blhp', q_pe, kpe).astype(jnp.float32)) * scale
        if mask is not None:
            s = jnp.where(mask, s, NEG_INF)
        m = jnp.max(s, axis=-1)
        e = jnp.exp(s - m[..., None])
        acc = jnp.einsum('blhp,bpr->blhr', e, ckv.astype(jnp.float32))
        return m, jnp.sum(e, axis=-1), acc

    self_mask = jnp.tril(jnp.ones((L, L), dtype=bool))[None, :, None, :]
    carry0 = partial(ckv_new, kpe_new, self_mask)

    def step(carry, phys):
        m0, l0, a0 = carry
        m1, l1, a1 = partial(cache_ckv[phys], cache_kpe[phys], None)
        m = jnp.maximum(m0, m1)
        c0, c1 = jnp.exp(m0 - m), jnp.exp(m1 - m)
        return (m, l0 * c0 + l1 * c1, a0 * c0[..., None] + a1 * c1[..., None]), None

    (m, l, acc), _ = lax.scan(step, carry0, page_table.T)
    o_lat = acc / l[..., None]
    o = jnp.einsum('blhr,rhd->blhd', o_lat, lp['w_uv'].astype(jnp.float32))
    return o.reshape(B, L, MLA_HEADS * V_DIM).astype(q_nope.dtype)


def layer_front(h, pos, lp):
    hn = rms_norm(h, lp['norm1'])
    qkv_pre, z, a, b, c_q, c_kv, k_pe = split_proj(hn @ lp['w_in'])
    q_nope, q_pe, ckv, kpe = mla_project(c_q, c_kv, k_pe, pos, lp)
    return qkv_pre, z, a, b, q_nope, q_pe, ckv, kpe


def layer_back(h, gdn_o, z, mla_o, lp):
    mix = jnp.concatenate([gdn_output(gdn_o, z, lp['gdn_norm'], h.dtype), mla_o.astype(h.dtype)], axis=-1)
    h = h + mix @ lp['w_o']
    hn = rms_norm(h, lp['norm2'])
    return h + (jax.nn.silu(hn @ lp['w_gate']) * (hn @ lp['w_up'])) @ lp['w_down']


def prompt_layer(h, lp):
    B, T, _ = h.shape
    qkv_pre, z, a, b, q_nope, q_pe, ckv, kpe = layer_front(h, jnp.arange(T), lp)
    conv0 = jnp.zeros((B, CONV_WIDTH - 1, CONV_DIM), qkv_pre.dtype)
    q, k, v, g, beta, new_conv = gdn_inputs(qkv_pre, conv0, a, b, lp)
    n_front = (-N_META) % CHUNK
    n_back = (-(n_front + T)) % CHUNK

    def to_chunks(t):
        t = jnp.pad(t, [(0, 0), (n_front, n_back)] + [(0, 0)] * (t.ndim - 2))
        return t.reshape((B, -1, CHUNK) + t.shape[2:])

    s0 = jnp.zeros((B, GDN_HEADS, GDN_DK, GDN_DV), jnp.float32)
    o, s_final = gdn_chunked(to_chunks(q), to_chunks(k), to_chunks(v), to_chunks(g), to_chunks(beta), s0)
    o = o.reshape(B, -1, GDN_HEADS, GDN_DV)[:, n_front:n_front + T]
    mla_o = mla_prompt_attn(q_nope, q_pe, ckv, kpe, lp)
    return layer_back(h, o, z, mla_o, lp), new_conv, s_final, ckv, kpe


def sample_layer(h, conv_prev, s0, cache_ckv, cache_kpe, page_table, lp):
    L = h.shape[1]
    qkv_pre, z, a, b, q_nope, q_pe, ckv, kpe = layer_front(h, PAST_LEN + jnp.arange(L), lp)
    q, k, v, g, beta, new_conv = gdn_inputs(qkv_pre, conv_prev, a, b, lp)
    o, s_final = gdn_chunked(q[:, None], k[:, None], v[:, None], g[:, None], beta[:, None],
                             s0.astype(jnp.float32))
    mla_o = mla_sample_attn(q_nope, q_pe, ckv, kpe, cache_ckv, cache_kpe, page_table, lp)
    return layer_back(h, o[:, 0], z, mla_o, lp), new_conv, s_final, ckv, kpe


def setup_inputs(seed: int = 0) -> dict:
    key = jax.random.key(seed)
    ks = iter(jax.random.split(key, 40))
    f32 = jnp.float32

    def nrm(shape, scale):
        return jax.random.normal(next(ks), shape, f32) * scale

    def gain(shape):
        return 1.0 + 0.02 * jax.random.normal(next(ks), shape, f32)

    n_pages = PAST_LEN // PAGE_SIZE
    n_used = DEC_BATCH * n_pages
    n_pool = n_used + max(1, n_used // 4)
    page_table = jax.random.permutation(next(ks), n_pool)[:n_used].reshape(DEC_BATCH, n_pages).astype(jnp.int32)
    a_log = jnp.log(jax.random.uniform(next(ks), (DEPTH, GDN_HEADS), f32, 1.0, 16.0))
    dt = jnp.exp(jax.random.uniform(next(ks), (DEPTH, GDN_HEADS), f32, math.log(1e-3), math.log(1e-1)))
    dt_bias = dt + jnp.log(-jnp.expm1(-dt))
    return {
        'x_prompt': nrm((BATCH, SEQ, D_MODEL), 1.0),
        'x_sample': nrm((DEC_BATCH, DEC_SEQ, D_MODEL), 1.0),
        'cache_conv': nrm((DEPTH, DEC_BATCH, CONV_WIDTH - 1, CONV_DIM), 1.0),
        'state_gdn': nrm((DEPTH, DEC_BATCH, GDN_HEADS, GDN_DK, GDN_DV), 0.1),
        'cache_ckv': nrm((DEPTH, n_pool, PAGE_SIZE, KV_RANK), 1.0),
        'cache_kpe': nrm((DEPTH, n_pool, PAGE_SIZE, ROPE_DIM), 1.0),
        'page_table': page_table,
        'meta_tokens': nrm((N_META, D_MODEL), 1.0),
        'norm1': gain((DEPTH, D_MODEL)),
        'w_in': nrm((DEPTH, D_MODEL, D_IN), D_MODEL ** -0.5),
        'conv_w': nrm((DEPTH, CONV_WIDTH, CONV_DIM), CONV_WIDTH ** -0.5),
        'a_log': a_log,
        'dt_bias': dt_bias,
        'gdn_norm': gain((DEPTH, GDN_DV)),
        'q_a_norm': gain((DEPTH, Q_RANK)),
        'w_q_b': nrm((DEPTH, Q_RANK, MLA_HEADS * QK_DIM), Q_RANK ** -0.5),
        'q_nope_norm': gain((DEPTH, NOPE_DIM)),
        'q_pe_norm': gain((DEPTH, ROPE_DIM)),
        'kv_a_norm': gain((DEPTH, KV_RANK)),
        'k_pe_norm': gain((DEPTH, ROPE_DIM)),
        'w_uk': nrm((DEPTH, KV_RANK, MLA_HEADS, NOPE_DIM), KV_RANK ** -0.5),
        'w_uv': nrm((DEPTH, KV_RANK, MLA_HEADS, V_DIM), KV_RANK ** -0.5),
        'k_nope_norm': gain((DEPTH, NOPE_DIM)),
        'w_o': nrm((DEPTH, MIX_WIDTH, D_MODEL), MIX_WIDTH ** -0.5),
        'norm2': gain((DEPTH, D_MODEL)),
        'w_gate': nrm((DEPTH, D_MODEL, D_FF), D_MODEL ** -0.5),
        'w_up': nrm((DEPTH, D_MODEL, D_FF), D_MODEL ** -0.5),
        'w_down': nrm((DEPTH, D_FF, D_MODEL), D_FF ** -0.5),
    }


def reference(x_prompt, x_sample, cache_conv, state_gdn, cache_ckv, cache_kpe, page_table,
              meta_tokens, norm1, w_in, conv_w, a_log, dt_bias, gdn_norm,
              q_a_norm, w_q_b, q_nope_norm, q_pe_norm, kv_a_norm, k_pe_norm,
              w_uk, w_uv, k_nope_norm, w_o, norm2, w_gate, w_up, w_down):
    B = x_prompt.shape[0]
    meta = jnp.broadcast_to(meta_tokens.astype(x_prompt.dtype)[None], (B, N_META, D_MODEL))
    h_p = jnp.concatenate([meta, x_prompt], axis=1)
    h_s = x_sample
    conv_p, st_p, ckv_p, kpe_p = [], [], [], []
    conv_s, st_s, ckv_s, kpe_s = [], [], [], []
    for l in range(DEPTH):
        lp = dict(norm1=norm1[l], w_in=w_in[l], conv_w=conv_w[l], a_log=a_log[l], dt_bias=dt_bias[l],
                  gdn_norm=gdn_norm[l], q_a_norm=q_a_norm[l], w_q_b=w_q_b[l], q_nope_norm=q_nope_norm[l],
                  q_pe_norm=q_pe_norm[l], kv_a_norm=kv_a_norm[l], k_pe_norm=k_pe_norm[l], w_uk=w_uk[l],
                  w_uv=w_uv[l], k_nope_norm=k_nope_norm[l], w_o=w_o[l], norm2=norm2[l],
                  w_gate=w_gate[l], w_up=w_up[l], w_down=w_down[l])
        h_p, c1, s1, ck1, kp1 = prompt_layer(h_p, lp)
        h_s, c2, s2, ck2, kp2 = sample_layer(h_s, cache_conv[l], state_gdn[l], cache_ckv[l],
                                             cache_kpe[l], page_table, lp)
        conv_p.append(c1); st_p.append(s1); ckv_p.append(ck1); kpe_p.append(kp1)
        conv_s.append(c2); st_s.append(s2); ckv_s.append(ck2); kpe_s.append(kp2)
    y_prompt = h_p[:, N_META:]
    return (y_prompt, h_s,
            jnp.stack(conv_p), jnp.stack(st_p), jnp.stack(ckv_p), jnp.stack(kpe_p),
            jnp.stack(conv_s), jnp.stack(st_s), jnp.stack(ckv_s), jnp.stack(kpe_s))
```

```python
import functools
import math

import jax
import jax.numpy as jnp
from jax import lax
from jax.experimental import pallas as pl
from jax.experimental.pallas import tpu as pltpu

F32 = jnp.float32
BF16 = jnp.bfloat16
HIGHEST = lax.Precision.HIGHEST

N_META = 16
EPS = 1e-6
NEG_INF = -1e30
GDN_HEADS = 4
GDN_DK = 128
GDN_DV = 128
GDN_QK_WIDTH = GDN_HEADS * GDN_DK
GDN_V_WIDTH = GDN_HEADS * GDN_DV
CONV_WIDTH = 4
CONV_DIM = 2 * GDN_QK_WIDTH + GDN_V_WIDTH
CHUNK = 64
MLA_HEADS = 8
Q_RANK = 384
KV_RANK = 256
NOPE_DIM = 64
ROPE_DIM = 32
V_DIM = 64
QK_DIM = NOPE_DIM + ROPE_DIM
ROPE_THETA = 10000.0

LANES = 128
HEAD_PAD = LANES
PE_LO = NOPE_DIM
PE_HALF = ROPE_DIM // 2
MISC_W = LANES
P0 = 512
VMEM_LIMIT = 56 * 1024 * 1024


def _cp(sem, vmem=None):
    return pltpu.CompilerParams(dimension_semantics=sem, vmem_limit_bytes=vmem or VMEM_LIMIT)


def _sigmoid(x):
    return 1.0 / (1.0 + jnp.exp(-x))


def _softplus(x):
    return jnp.maximum(x, 0.0) + jnp.log1p(jnp.exp(-jnp.abs(x)))


def _dot(a, b, precision=None):
    return jnp.dot(a, b, preferred_element_type=F32, precision=precision)


def _dot_nt(a, b, precision=None):
    return lax.dot_general(a, b, (((1,), (1,)), ((), ())), preferred_element_type=F32,
                           precision=precision)


def _dot_tn(a, b, precision=None):
    return lax.dot_general(a, b, (((0,), (0,)), ((), ())), preferred_element_type=F32,
                           precision=precision)


W_SPLITS = (CONV_DIM, GDN_V_WIDTH, Q_RANK, KV_RANK, MISC_W)


def _front_body(x_ref, g_ref, w_ref, qkv_ref, z_ref, cq_ref, ckv_ref, misc_ref):
    x = x_ref[...]
    hn = ((x * lax.rsqrt(jnp.mean(x * x, axis=-1, keepdims=True) + EPS)) * g_ref[...]).astype(BF16)
    off = 0
    for width, o_ref in zip(W_SPLITS, (qkv_ref, z_ref, cq_ref, ckv_ref, misc_ref)):
        o_ref[...] = _dot(hn, w_ref[:, off:off + width])
        off += width


def front_proj(x, norm1, w_r, tm):
    m, d = x.shape
    n_all = sum(W_SPLITS)
    row = lambda i: (i, 0)
    fixed = lambda i: (0, 0)
    return pl.pallas_call(
        _front_body,
        grid=(m // tm,),
        in_specs=[pl.BlockSpec((tm, d), row), pl.BlockSpec((1, d), fixed),
                  pl.BlockSpec((d, n_all), fixed)],
        out_specs=[pl.BlockSpec((tm, w), row) for w in W_SPLITS],
        out_shape=[jax.ShapeDtypeStruct((m, w), F32) for w in W_SPLITS],
        compiler_params=_cp(("parallel",)),
        name="front_proj",
    )(x, norm1.reshape(1, d), w_r)


def _gates(misc, alog, dtb):
    lane = lax.broadcasted_iota(jnp.int32, misc.shape, 1)
    g = -jnp.exp(alog) * _softplus(misc + dtb)
    beta = _sigmoid(misc)
    return jnp.where(lane < GDN_HEADS, g, jnp.where(lane < 2 * GDN_HEADS, beta, 0.0))


def _conv_cols(taps, w_ref, c, q_ref, k_ref, v_ref):
    cols = slice(c * LANES, (c + 1) * LANES)
    acc = taps(0, cols) * w_ref[0:1, cols]
    for j in range(1, CONV_WIDTH):
        acc = acc + taps(j, cols) * w_ref[j:j + 1, cols]
    y = acc * _sigmoid(acc)
    if c < 2 * GDN_HEADS:
        yn = y * lax.rsqrt(jnp.sum(y * y, axis=-1, keepdims=True) + EPS)
        if c < GDN_HEADS:
            q_ref[:, cols] = yn * (GDN_DK ** -0.5)
        else:
            k_ref[:, (c - GDN_HEADS) * LANES:(c - GDN_HEADS + 1) * LANES] = yn
    else:
        v_ref[:, (c - 2 * GDN_HEADS) * LANES:(c - 2 * GDN_HEADS + 1) * LANES] = y


def _conv_prompt_body(x_ref, halo_ref, misc_ref, w_ref, alog_ref, dtb_ref,
                      q_ref, k_ref, v_ref, gb_ref, xs, *, tm, first_valid_row):
    i = pl.program_id(0)
    xs[8:tm + 8, :] = x_ref[...]
    xs[0:8, :] = jnp.where(i > 0, halo_ref[...], 0.0)
    taps = lambda j, cols: xs[8 - (CONV_WIDTH - 1) + j:8 - (CONV_WIDTH - 1) + j + tm, cols]
    for c in range(CONV_DIM // LANES):
        _conv_cols(taps, w_ref, c, q_ref, k_ref, v_ref)
    gb = _gates(misc_ref[...], alog_ref[...], dtb_ref[...])
    row = i * tm + lax.broadcasted_iota(jnp.int32, gb.shape, 0)
    gb_ref[...] = jnp.where(row >= first_valid_row, gb, 0.0)


def conv_prompt(qkv_pre, misc, conv_w, alog, dtb, tm, first_valid_row):
    m = qkv_pre.shape[0]
    row = lambda i: (i, 0)
    fixed = lambda i: (0, 0)
    halo = lambda i: (jnp.maximum(i * (tm // 8) - 1, 0), 0)
    body = functools.partial(_conv_prompt_body, tm=tm, first_valid_row=first_valid_row)
    return pl.pallas_call(
        body,
        grid=(m // tm,),
        in_specs=[pl.BlockSpec((tm, CONV_DIM), row), pl.BlockSpec((8, CONV_DIM), halo),
                  pl.BlockSpec((tm, MISC_W), row), pl.BlockSpec((CONV_WIDTH, CONV_DIM), fixed),
                  pl.BlockSpec((1, MISC_W), fixed), pl.BlockSpec((1, MISC_W), fixed)],
        out_specs=[pl.BlockSpec((tm, GDN_QK_WIDTH), row)] * 3 + [pl.BlockSpec((tm, MISC_W), row)],
        out_shape=[jax.ShapeDtypeStruct((m, GDN_QK_WIDTH), F32)] * 3
        + [jax.ShapeDtypeStruct((m, MISC_W), F32)],
        scratch_shapes=[pltpu.VMEM((tm + 8, CONV_DIM), F32)],
        compiler_params=_cp(("parallel",)),
        name="conv_prompt",
    )(qkv_pre, qkv_pre, misc, conv_w, alog, dtb)


def _conv_sample_body(xp_ref, misc_ref, w_ref, alog_ref, dtb_ref, q_ref, k_ref, v_ref, gb_ref, *, L):
    for l in range(L):
        taps = lambda j, cols, l=l: xp_ref[l + j, :, cols]
        qv, kv, vv = (r.at[l] for r in (q_ref, k_ref, v_ref))
        for c in range(CONV_DIM // LANES):
            _conv_cols(taps, w_ref, c, qv, kv, vv)
        gb_ref[l] = _gates(misc_ref[l], alog_ref[...], dtb_ref[...])


def conv_sample(xp, misc, conv_w, alog, dtb):
    lp, b, _ = xp.shape
    L = lp - (CONV_WIDTH - 1)
    body = functools.partial(_conv_sample_body, L=L)
    return pl.pallas_call(
        body,
        out_shape=[jax.ShapeDtypeStruct((L, b, GDN_QK_WIDTH), F32)] * 3
        + [jax.ShapeDtypeStruct((L, b, MISC_W), F32)],
        compiler_params=pltpu.CompilerParams(vmem_limit_bytes=VMEM_LIMIT),
        name="conv_sample",
    )(xp, misc, conv_w, alog, dtb)


def _gdn_body(q_ref, k_ref, v_ref, gb_ref, s0_ref, o_ref, sfin_ref, s_scr, *, C, n_chunks):
    c = pl.program_id(1)

    @pl.when(c == 0)
    def _():
        s_scr[...] = s0_ref[0]

    gb = gb_ref[...]
    row = lax.broadcasted_iota(jnp.int32, (C, C), 0)
    col = lax.broadcasted_iota(jnp.int32, (C, C), 1)
    incl = row >= col
    strict = row > col
    eye = (row == col).astype(F32)
    g_cum = _dot(incl.astype(F32), gb, HIGHEST)
    g_cum_t = g_cum.T
    n_sq = int(math.log2(C)) - 1
    for h in range(GDN_HEADS):
        cols = slice(h * LANES, (h + 1) * LANES)
        q = q_ref[:, cols]
        k = k_ref[:, cols]
        v = v_ref[:, cols]
        gc = g_cum[:, h:h + 1]
        gr = g_cum_t[h:h + 1, :]
        beta = gb[:, GDN_HEADS + h:GDN_HEADS + h + 1]
        decay = jnp.where(incl, jnp.exp(jnp.where(incl, gc - gr, 0.0)), 0.0)
        kk = _dot_nt(k, k, HIGHEST)
        neg_m = jnp.where(strict, -(beta * kk * decay), 0.0)
        t = eye + neg_m
        p = neg_m
        for _ in range(n_sq):
            p = _dot(p, p, HIGHEST)
            t = t + _dot(t, p, HIGHEST)
        e_g = jnp.exp(gc)
        w = _dot(t, (beta * e_g) * k, HIGHEST)
        ut = _dot(t, beta * v, HIGHEST)
        qk = _dot_nt(q, k, HIGHEST) * decay
        g_last = g_cum[C - 1:C, h:h + 1]
        k_dec = jnp.exp(g_last - gc) * k
        s = s_scr[h]
        u = ut - _dot(w, s, HIGHEST)
        o_ref[:, cols] = _dot(e_g * q, s, HIGHEST) + _dot(qk, u, HIGHEST)
        s_scr[h] = jnp.exp(g_last) * s + _dot_tn(k_dec, u, HIGHEST)

    @pl.when(c == n_chunks - 1)
    def _():
        sfin_ref[0] = s_scr[...]


def gdn_scan(q, k, v, gb, s0, *, C, n_seq, n_chunks, first_chunk):
    rows = q.shape[0]
    blk = lambda s, c: (first_chunk + s * n_chunks + c, 0)
    st = lambda s, c: (s, 0, 0, 0)
    body = functools.partial(_gdn_body, C=C, n_chunks=n_chunks)
    return pl.pallas_call(
        body,
        grid=(n_seq, n_chunks),
        in_specs=[pl.BlockSpec((C, GDN_QK_WIDTH), blk)] * 3 + [pl.BlockSpec((C, MISC_W), blk),
                  pl.BlockSpec((1, GDN_HEADS, GDN_DK, GDN_DV), st)],
        out_specs=[pl.BlockSpec((C, GDN_V_WIDTH), blk),
                   pl.BlockSpec((1, GDN_HEADS, GDN_DK, GDN_DV), st)],
        out_shape=[jax.ShapeDtypeStruct((rows, GDN_V_WIDTH), F32),
                   jax.ShapeDtypeStruct(s0.shape, F32)],
        scratch_shapes=[pltpu.VMEM((GDN_HEADS, GDN_DK, GDN_DV), F32)],
        compiler_params=_cp(("arbitrary", "arbitrary")),
        name="gdn_scan",
    )(q, k, v, gb, s0)


def _rope(y, cosb, sin_lo, sin_hi):
    return (y * cosb + pltpu.roll(y, LANES - PE_HALF, 1) * sin_lo + pltpu.roll(y, PE_HALF, 1) * sin_hi)


def _slab_norm(x, gain, lane):
    x2 = x * x
    is_nope = lane < NOPE_DIM
    is_pe = (lane >= PE_LO) & (lane < PE_LO + ROPE_DIM)
    ms_n = jnp.sum(jnp.where(is_nope, x2, 0.0), axis=-1, keepdims=True) * (1.0 / NOPE_DIM)
    ms_p = jnp.sum(jnp.where(is_pe, x2, 0.0), axis=-1, keepdims=True) * (1.0 / ROPE_DIM)
    rinv = jnp.where(is_nope, lax.rsqrt(ms_n + EPS), lax.rsqrt(ms_p + EPS))
    return (x * rinv) * gain


def _mla_body(cq_ref, ckvr_ref, misc_ref, cos_ref, slo_ref, shi_ref, qan_ref, wqb_ref, qg_ref,
              kvn_ref, wuk_ref, kg_ref, wuv_ref, ckv_ref, kpe_ref, q_ref, k_ref, v_ref, *, with_kv):
    cosb, slo, shi = cos_ref[...], slo_ref[...], shi_ref[...]
    lane = lax.broadcasted_iota(jnp.int32, cosb.shape, 1)
    cq = cq_ref[...]
    cqn = ((cq * lax.rsqrt(jnp.mean(cq * cq, axis=-1, keepdims=True) + EPS)) * qan_ref[...]).astype(BF16)
    c = ckvr_ref[...]
    ckv = (c * lax.rsqrt(jnp.mean(c * c, axis=-1, keepdims=True) + EPS)) * kvn_ref[...]
    ckv_ref[...] = ckv
    ckv_b = ckv.astype(BF16)
    kpe_raw = jnp.where((lane >= PE_LO) & (lane < PE_LO + ROPE_DIM), misc_ref[...], 0.0)
    kpe = _rope(_slab_norm(kpe_raw, kg_ref[...], lane), cosb, slo, shi)
    kpe = jnp.where(lane >= PE_LO, kpe, 0.0)
    kpe_ref[...] = kpe
    scale = QK_DIM ** -0.5
    for h in range(MLA_HEADS):
        cols = slice(h * HEAD_PAD, (h + 1) * HEAD_PAD)
        qh = _dot(cqn, wqb_ref[:, cols])
        qh = _rope(_slab_norm(qh, qg_ref[...], lane), cosb, slo, shi)
        q_ref[h] = (qh * scale).astype(BF16)
        if with_kv:
            kh = _dot(ckv_b, wuk_ref[:, cols])
            k_ref[h] = (_slab_norm(kh, kg_ref[...], lane) + kpe).astype(BF16)
            v_ref[h] = _dot(ckv_b, wuv_ref[:, cols]).astype(BF16)


def mla_prep(cq, ckv_raw, misc, rope_tabs, p, tm, with_kv):
    m = cq.shape[0]
    row = lambda i: (i, 0)
    fixed = lambda i: (0, 0)
    hrow = lambda i: (0, i, 0)
    hw = MLA_HEADS * HEAD_PAD
    body = functools.partial(_mla_body, with_kv=with_kv)
    kv_rows = m if with_kv else tm
    kv_map = hrow if with_kv else (lambda i: (0, 0, 0))
    return pl.pallas_call(
        body,
        grid=(m // tm,),
        in_specs=[pl.BlockSpec((tm, Q_RANK), row), pl.BlockSpec((tm, KV_RANK), row),
                  pl.BlockSpec((tm, MISC_W), row)] + [pl.BlockSpec((tm, LANES), row)] * 3
        + [pl.BlockSpec((1, Q_RANK), fixed), pl.BlockSpec((Q_RANK, hw), fixed),
           pl.BlockSpec((1, LANES), fixed), pl.BlockSpec((1, KV_RANK), fixed),
           pl.BlockSpec((KV_RANK, hw), fixed), pl.BlockSpec((1, LANES), fixed),
           pl.BlockSpec((KV_RANK, hw), fixed)],
        out_specs=[pl.BlockSpec((tm, KV_RANK), row), pl.BlockSpec((tm, LANES), row),
                   pl.BlockSpec((MLA_HEADS, tm, HEAD_PAD), hrow),
                   pl.BlockSpec((MLA_HEADS, tm, HEAD_PAD), kv_map),
                   pl.BlockSpec((MLA_HEADS, tm, HEAD_PAD), kv_map)],
        out_shape=[jax.ShapeDtypeStruct((m, KV_RANK), F32), jax.ShapeDtypeStruct((m, LANES), F32),
                   jax.ShapeDtypeStruct((MLA_HEADS, m, HEAD_PAD), BF16),
                   jax.ShapeDtypeStruct((MLA_HEADS, kv_rows, HEAD_PAD), BF16),
                   jax.ShapeDtypeStruct((MLA_HEADS, kv_rows, HEAD_PAD), BF16)],
        compiler_params=_cp(("arbitrary",)),
        name="mla_prep",
    )(cq, ckv_raw, misc, *rope_tabs, p["q_a_norm"], p["w_q_b"], p["q_gain"], p["kv_a_norm"],
      p["w_uk"], p["k_gain"], p["w_uv"])


def _flash_body(qmap_ref, kmap_ref, q_ref, k_ref, v_ref, o_ref, m_scr, l_scr, acc_scr,
                *, tq, tk, q_block0, first_key):
    step = pl.program_id(0)
    qi = qmap_ref[step]
    ki = kmap_ref[step]

    @pl.when(ki == first_key // tk)
    def _():
        m_scr[...] = jnp.full(m_scr.shape, NEG_INF, F32)
        l_scr[...] = jnp.zeros(l_scr.shape, F32)
        acc_scr[...] = jnp.zeros(acc_scr.shape, F32)

    qpos = (qi + q_block0) * tq + lax.broadcasted_iota(jnp.int32, (tq, tk), 0)
    kpos = ki * tk + lax.broadcasted_iota(jnp.int32, (tq, tk), 1)
    valid = (kpos <= qpos) & (kpos >= first_key)
    for h in range(MLA_HEADS):
        s = _dot_nt(q_ref[h], k_ref[h])
        s = jnp.where(valid, s, NEG_INF)
        m_prev = m_scr[h]
        m_new = jnp.maximum(m_prev, jnp.max(s, axis=-1, keepdims=True))
        alpha = jnp.exp(m_prev - m_new)
        e = jnp.exp(s - m_new)
        l_scr[h] = alpha * l_scr[h] + jnp.sum(e, axis=-1, keepdims=True)
        acc_scr[h] = alpha * acc_scr[h] + _dot(e.astype(BF16), v_ref[h])
        m_scr[h] = m_new

    @pl.when(ki * tk + tk >= (qi + q_block0 + 1) * tq)
    def _():
        for h in range(MLA_HEADS):
            o_ref[:, h * HEAD_PAD:(h + 1) * HEAD_PAD] = (acc_scr[h] / l_scr[h]).astype(BF16)


def flash_prompt(q, k, v, *, tq, tk, first_key):
    tp = q.shape[1]
    q_block0 = P0 // tq
    nq = (tp - P0) // tq
    qs, ks = [], []
    for qi in range(nq):
        last_k = ((qi + q_block0 + 1) * tq - 1) // tk
        for ki in range(first_key // tk, last_k + 1):
            qs.append(qi)
            ks.append(ki)
    qmap = jnp.asarray(qs, jnp.int32)
    kmap = jnp.asarray(ks, jnp.int32)
    body = functools.partial(_flash_body, tq=tq, tk=tk, q_block0=q_block0, first_key=first_key)
    grid_spec = pltpu.PrefetchScalarGridSpec(
        num_scalar_prefetch=2,
        grid=(len(qs),),
        in_specs=[pl.BlockSpec((MLA_HEADS, tq, HEAD_PAD), lambda s, qm, km: (0, qm[s] + q_block0, 0)),
                  pl.BlockSpec((MLA_HEADS, tk, HEAD_PAD), lambda s, qm, km: (0, km[s], 0)),
                  pl.BlockSpec((MLA_HEADS, tk, HEAD_PAD), lambda s, qm, km: (0, km[s], 0))],
        out_specs=pl.BlockSpec((tq, MLA_HEADS * HEAD_PAD), lambda s, qm, km: (qm[s], 0)),
        scratch_shapes=[pltpu.VMEM((MLA_HEADS, tq, 1), F32), pltpu.VMEM((MLA_HEADS, tq, 1), F32),
                        pltpu.VMEM((MLA_HEADS, tq, HEAD_PAD), F32)],
    )
    return pl.pallas_call(
        body,
        grid_spec=grid_spec,
        out_shape=jax.ShapeDtypeStruct((tp - P0, MLA_HEADS * HEAD_PAD), BF16),
        compiler_params=_cp(("arbitrary",)),
        name="flash_prompt",
    )(qmap, kmap, q, k, v)


def _qabs_body(q_ref, w_ref, o_ref):
    o_ref[0] = _dot(q_ref[0], w_ref[0]).astype(BF16)


def q_absorb(q, w_abs):
    h, m, _ = q.shape
    blk = lambda i: (i, 0, 0)
    return pl.pallas_call(
        _qabs_body,
        grid=(h,),
        in_specs=[pl.BlockSpec((1, m, HEAD_PAD), blk), pl.BlockSpec((1, HEAD_PAD, KV_RANK), blk)],
        out_specs=pl.BlockSpec((1, m, KV_RANK), blk),
        out_shape=jax.ShapeDtypeStruct((h, m, KV_RANK), BF16),
        compiler_params=_cp(("parallel",)),
        name="q_absorb",
    )(q, w_abs)


def _decode_scores(waug, qpe, ckv_b, kpe_b, n_lq):
    kp = _dot_nt(waug, ckv_b)
    n_k = MLA_HEADS * NOPE_DIM
    kproj = kp[:n_k]
    ms = jnp.sum((kproj * kproj).reshape(MLA_HEADS, NOPE_DIM, kproj.shape[-1]), axis=1) * (1.0 / NOPE_DIM)
    rinv = lax.rsqrt(ms + EPS)
    s_nope = kp[n_k:]
    s_pe = _dot_nt(qpe, kpe_b)
    rinv_rows = jnp.concatenate([rinv] * n_lq, axis=0)
    return s_nope * rinv_rows + s_pe


def _decode_body(pt_ref, wukt_ref, qabs_ref, qpe_ref, nckv_ref, nkpe_ref, *rest, PG, L):
    ckv_refs = rest[:PG]
    kpe_refs = rest[PG:2 * PG]
    o_ref = rest[2 * PG]
    waug, ckvb, s_scr, m_scr, l_scr, acc_scr = rest[2 * PG + 1:]
    j = pl.program_id(1)
    n_k = MLA_HEADS * NOPE_DIM
    R = MLA_HEADS * L
    page = ckv_refs[0].shape[2]

    @pl.when(j == 0)
    def _():
        waug[0:n_k, :] = wukt_ref[...]
        waug[n_k:n_k + R, :] = qabs_ref[0]
        c_new = nckv_ref[0].astype(BF16)
        s = _decode_scores(waug[...], qpe_ref[0], c_new, nkpe_ref[0].astype(BF16), L)
        r_l = lax.shift_right_logical(lax.broadcasted_iota(jnp.int32, s.shape, 0),
                                      int(math.log2(MLA_HEADS)))
        t = lax.broadcasted_iota(jnp.int32, s.shape, 1)
        s = jnp.where((t <= r_l) & (t < L), s, NEG_INF)
        m = jnp.max(s, axis=-1, keepdims=True)
        e = jnp.exp(s - m)
        m_scr[...] = m
        l_scr[...] = jnp.sum(e, axis=-1, keepdims=True)
        acc_scr[...] = _dot(e.astype(BF16), c_new)

    w = waug[...]
    qpe = qpe_ref[0]
    for p in range(PG // 2):
        c2 = jnp.concatenate([ckv_refs[2 * p][0, 0], ckv_refs[2 * p + 1][0, 0]], axis=0).astype(BF16)
        k2 = jnp.concatenate([kpe_refs[2 * p][0, 0], kpe_refs[2 * p + 1][0, 0]], axis=0).astype(BF16)
        ckvb[2 * p * page:(2 * p + 2) * page, :] = c2
        s_scr[:, 2 * p * page:(2 * p + 2) * page] = _decode_scores(w, qpe, c2, k2, L)
    s = s_scr[...]
    m_prev = m_scr[...]
    m_new = jnp.maximum(m_prev, jnp.max(s, axis=-1, keepdims=True))
    alpha = jnp.exp(m_prev - m_new)
    e = jnp.exp(s - m_new)
    l_scr[...] = alpha * l_scr[...] + jnp.sum(e, axis=-1, keepdims=True)
    acc_scr[...] = alpha * acc_scr[...] + _dot(e.astype(BF16), ckvb[...])
    m_scr[...] = m_new

    @pl.when(j == pl.num_programs(1) - 1)
    def _():
        o_ref[0] = acc_scr[...] / l_scr[...]


def decode_attn(page_table, wuk_t, qabs, qpe, new_ckv, new_kpe, cache_ckv, cache_kpe, *, PG, L):
    b, n_pages = page_table.shape
    page = cache_ckv.shape[2]
    R = MLA_HEADS * L
    n_k = MLA_HEADS * NOPE_DIM
    per_b = lambda i, j, pt: (i, 0, 0)
    fixed = lambda i, j, pt: (0, 0)

    def page_map(slot):
        return lambda i, j, pt: (0, pt[i * n_pages + j * PG + slot], 0, 0)

    body = functools.partial(_decode_body, PG=PG, L=L)
    grid_spec = pltpu.PrefetchScalarGridSpec(
        num_scalar_prefetch=1,
        grid=(b, n_pages // PG),
        in_specs=[pl.BlockSpec((n_k, KV_RANK), fixed), pl.BlockSpec((1, R, KV_RANK), per_b),
                  pl.BlockSpec((1, R, ROPE_DIM), per_b), pl.BlockSpec((1, 8, KV_RANK), per_b),
                  pl.BlockSpec((1, 8, ROPE_DIM), per_b)]
        + [pl.BlockSpec((1, 1, page, KV_RANK), page_map(s)) for s in range(PG)]
        + [pl.BlockSpec((1, 1, page, ROPE_DIM), page_map(s)) for s in range(PG)],
        out_specs=pl.BlockSpec((1, R, KV_RANK), per_b),
        scratch_shapes=[pltpu.VMEM((n_k + R, KV_RANK), BF16), pltpu.VMEM((PG * page, KV_RANK), BF16),
                        pltpu.VMEM((R, PG * page), F32), pltpu.VMEM((R, 1), F32),
                        pltpu.VMEM((R, 1), F32), pltpu.VMEM((R, KV_RANK), F32)],
    )
    return pl.pallas_call(
        body,
        grid_spec=grid_spec,
        out_shape=jax.ShapeDtypeStruct((b, R, KV_RANK), F32),
        compiler_params=_cp(("arbitrary", "arbitrary")),
        name="decode_attn",
    )(page_table.reshape(-1), wuk_t, qabs, qpe, new_ckv, new_kpe,
      *([cache_ckv] * PG), *([cache_kpe] * PG))


def _latent_out_body(x_ref, w_ref, o_ref):
    o_ref[...] = _dot(x_ref[0].astype(BF16), w_ref[0]).astype(BF16)


def latent_out(o_lat, w_uv_h):
    h, m, _ = o_lat.shape
    return pl.pallas_call(
        _latent_out_body,
        grid=(h,),
        in_specs=[pl.BlockSpec((1, m, KV_RANK), lambda i: (i, 0, 0)),
                  pl.BlockSpec((1, KV_RANK, HEAD_PAD), lambda i: (i, 0, 0))],
        out_specs=pl.BlockSpec((m, HEAD_PAD), lambda i: (0, i)),
        out_shape=jax.ShapeDtypeStruct((m, h * HEAD_PAD), BF16),
        compiler_params=_cp(("parallel",)),
        name="latent_out",
    )(o_lat, w_uv_h)


def _mix_body(h_ref, o_ref, z_ref, mla_ref, gg_ref, wog_ref, wom_ref, n2_ref, h1_ref, hn_ref):
    acc = h_ref[...] + _dot(mla_ref[...], wom_ref[...])
    for h in range(GDN_HEADS):
        cols = slice(h * LANES, (h + 1) * LANES)
        o = o_ref[:, cols]
        z = z_ref[:, cols]
        y = ((o * lax.rsqrt(jnp.mean(o * o, axis=-1, keepdims=True) + EPS)) * gg_ref[...]) * (z * _sigmoid(z))
        acc = acc + _dot(y.astype(BF16), wog_ref[cols, :])
    h1_ref[...] = acc
    hn_ref[...] = ((acc * lax.rsqrt(jnp.mean(acc * acc, axis=-1, keepdims=True) + EPS)) * n2_ref[...]).astype(BF16)


def mix_out(h, gdn_o, z, mla_o, p, tm, row_block0):
    m, d = h.shape
    row = lambda i: (i, 0)
    prow = lambda i: (i + row_block0, 0)
    fixed = lambda i: (0, 0)
    hw = MLA_HEADS * HEAD_PAD
    return pl.pallas_call(
        _mix_body,
        grid=(m // tm,),
        in_specs=[pl.BlockSpec((tm, d), row), pl.BlockSpec((tm, GDN_V_WIDTH), prow),
                  pl.BlockSpec((tm, GDN_V_WIDTH), prow), pl.BlockSpec((tm, hw), row),
                  pl.BlockSpec((1, LANES), fixed), pl.BlockSpec((GDN_V_WIDTH, d), fixed),
                  pl.BlockSpec((hw, d), fixed), pl.BlockSpec((1, d), fixed)],
        out_specs=[pl.BlockSpec((tm, d), row), pl.BlockSpec((tm, d), row)],
        out_shape=[jax.ShapeDtypeStruct((m, d), F32), jax.ShapeDtypeStruct((m, d), BF16)],
        compiler_params=_cp(("parallel",)),
        name="mix_out",
    )(h, gdn_o, z, mla_o, p["gdn_norm"], p["w_o_gdn"], p["w_o_mla"], p["norm2"])


def _ffn_body(h1_ref, hn_ref, wg_ref, wu_ref, wd_ref, y_ref):
    f = pl.program_id(1)

    @pl.when(f == 0)
    def _():
        y_ref[...] = h1_ref[...]

    x = hn_ref[...]
    g = _dot(x, wg_ref[...])
    u = _dot(x, wu_ref[...])
    y_ref[...] += _dot(((g * _sigmoid(g)) * u).astype(BF16), wd_ref[...])


def ffn(h1, hn, wg, wu, wd, tm, tf):
    m, d = h1.shape
    dff = wg.shape[1]
    row = lambda i, f: (i, 0)
    return pl.pallas_call(
        _ffn_body,
        grid=(m // tm, dff // tf),
        in_specs=[pl.BlockSpec((tm, d), row), pl.BlockSpec((tm, d), row),
                  pl.BlockSpec((d, tf), lambda i, f: (0, f)), pl.BlockSpec((d, tf), lambda i, f: (0, f)),
                  pl.BlockSpec((tf, d), lambda i, f: (f, 0))],
        out_specs=pl.BlockSpec((tm, d), row),
        out_shape=jax.ShapeDtypeStruct((m, d), F32),
        compiler_params=_cp(("parallel", "arbitrary")),
        name="ffn",
    )(h1, hn, wg, wu, wd)


def _head_slabs(w, width):
    k, h, _ = w.shape
    return jnp.pad(w, ((0, 0), (0, 0), (0, HEAD_PAD - width))).reshape(k, h * HEAD_PAD)


def _pad_lanes(v, lo=0):
    return jnp.pad(v.astype(F32), (lo, LANES - lo - v.shape[0])).reshape(1, LANES)


def _prep_params(norm1, w_in, conv_w, a_log, dt_bias, gdn_norm, q_a_norm, w_q_b, q_nope_norm,
                 q_pe_norm, kv_a_norm, k_pe_norm, w_uk, w_uv, k_nope_norm, w_o, norm2,
                 w_gate, w_up, w_down):
    d = w_in.shape[0]
    o = 0
    cuts = {}
    for name, wdt in (("qkv", CONV_DIM), ("z", GDN_V_WIDTH), ("a", GDN_HEADS), ("b", GDN_HEADS),
                      ("cq", Q_RANK), ("ckv", KV_RANK), ("kpe", ROPE_DIM)):
        cuts[name] = w_in[:, o:o + wdt]
        o += wdt
    zeros = lambda n: jnp.zeros((d, n), w_in.dtype)
    misc = jnp.concatenate([cuts["a"], cuts["b"], zeros(PE_LO - 2 * GDN_HEADS), cuts["kpe"],
                            zeros(MISC_W - PE_LO - ROPE_DIM)], axis=1)
    w_r = jnp.concatenate([cuts["qkv"], cuts["z"], cuts["cq"], cuts["ckv"], misc], axis=1).astype(BF16)
    wqb = w_q_b.reshape(Q_RANK, MLA_HEADS, QK_DIM)
    gain_q = jnp.concatenate([q_nope_norm, q_pe_norm]).astype(F32)
    gain_k = jnp.concatenate([k_nope_norm, k_pe_norm]).astype(F32)
    w_uk_s = _head_slabs(w_uk, NOPE_DIM)
    w_abs = jnp.pad(jnp.transpose(w_uk, (1, 2, 0)) * k_nope_norm[None, :, None],
                    ((0, 0), (0, HEAD_PAD - NOPE_DIM), (0, 0))).astype(BF16)
    w_o_mla = jnp.pad(w_o[GDN_V_WIDTH:].reshape(MLA_HEADS, V_DIM, d),
                      ((0, 0), (0, HEAD_PAD - V_DIM), (0, 0))).reshape(MLA_HEADS * HEAD_PAD, d)
    return dict(
        norm1=norm1, w_in=w_r, conv_w=conv_w.astype(F32),
        a_log=_pad_lanes(a_log), dt_bias=_pad_lanes(dt_bias),
        gdn_norm=gdn_norm.reshape(1, GDN_DV).astype(F32),
        q_a_norm=q_a_norm.reshape(1, Q_RANK).astype(F32), w_q_b=_head_slabs(wqb, QK_DIM).astype(BF16),
        q_gain=_pad_lanes(gain_q), kv_a_norm=kv_a_norm.reshape(1, KV_RANK).astype(F32),
        w_uk=w_uk_s.astype(BF16), k_gain=_pad_lanes(gain_k), w_uv=_head_slabs(w_uv, V_DIM).astype(BF16),
        w_uk_t=jnp.transpose(w_uk, (1, 2, 0)).reshape(MLA_HEADS * NOPE_DIM, KV_RANK).astype(BF16),
        w_abs=w_abs,
        w_uv_h=jnp.pad(jnp.transpose(w_uv, (1, 0, 2)), ((0, 0), (0, 0), (0, HEAD_PAD - V_DIM))).astype(BF16),
        w_o_gdn=w_o[:GDN_V_WIDTH].astype(BF16), w_o_mla=w_o_mla.astype(BF16),
        norm2=norm2.reshape(1, d).astype(F32),
        w_gate=w_gate.astype(BF16), w_up=w_up.astype(BF16), w_down=w_down.astype(BF16),
    )


def _rope_tables(pos):
    inv_freq = jnp.power(ROPE_THETA, -jnp.arange(0, ROPE_DIM, 2, dtype=F32) / ROPE_DIM)
    ang = pos.astype(F32)[:, None] * inv_freq[None, :]
    cos, sin = jnp.cos(ang), jnp.sin(ang)
    n = pos.shape[0]
    ones = jnp.ones((n, PE_LO), F32)
    z = lambda w: jnp.zeros((n, w), F32)
    tail = LANES - PE_LO - ROPE_DIM
    cosb = jnp.concatenate([ones, cos, cos, z(tail)], axis=1)
    sin_lo = jnp.concatenate([z(PE_LO), -sin, z(PE_HALF + tail)], axis=1)
    sin_hi = jnp.concatenate([z(PE_LO + PE_HALF), sin, z(tail)], axis=1)
    return cosb, sin_lo, sin_hi


def _ff_tile(dff):
    for t in (1408, 1024, 512, 256, 128):
        if dff % t == 0:
            return t
    return dff


def _prompt_group(x_prompt, meta_tokens, p):
    seq, d = x_prompt.shape
    tp = P0 + seq
    meta_row0 = P0 - N_META
    x_pad = jnp.concatenate([jnp.zeros((meta_row0, d), F32), meta_tokens.astype(F32), x_prompt], axis=0)
    qkv_pre, z, cq, ckv_raw, misc = front_proj(x_pad, p["norm1"], p["w_in"], tm=512)
    q, k, v, gb = conv_prompt(qkv_pre, misc, p["conv_w"], p["a_log"], p["dt_bias"], tm=512,
                              first_valid_row=meta_row0)
    first_chunk = meta_row0 // CHUNK
    n_chunks = tp // CHUNK - first_chunk
    s0 = jnp.zeros((1, GDN_HEADS, GDN_DK, GDN_DV), F32)
    gdn_o, s_fin = gdn_scan(q, k, v, gb, s0, C=CHUNK, n_seq=1, n_chunks=n_chunks, first_chunk=first_chunk)
    tabs = _rope_tables(jnp.arange(tp) - meta_row0)
    ckv, kpe, qh, kh, vh = mla_prep(cq, ckv_raw, misc, tabs, p, tm=256, with_kv=True)
    mla_o = flash_prompt(qh, kh, vh, tq=512, tk=512, first_key=meta_row0)
    h1, hn2 = mix_out(x_prompt, gdn_o, z, mla_o, p, tm=512, row_block0=P0 // 512)
    y = ffn(h1, hn2, p["w_gate"], p["w_up"], p["w_down"], tm=512, tf=_ff_tile(p["w_gate"].shape[1]))
    new_conv = qkv_pre[tp - (CONV_WIDTH - 1):]
    return (y, new_conv, s_fin[0], ckv[meta_row0:], kpe[meta_row0:, PE_LO:PE_LO + ROPE_DIM])


def _sample_group(x_sample, cache_conv, state_gdn, cache_ckv, cache_kpe, page_table, p):
    b, L, d = x_sample.shape
    m = b * L
    page = cache_ckv.shape[2]
    past_len = page_table.shape[1] * page
    x = x_sample.reshape(m, d)
    tm = min(m, 256)
    qkv_pre, z, cq, ckv_raw, misc = front_proj(x, p["norm1"], p["w_in"], tm=tm)
    xp = jnp.concatenate([jnp.transpose(cache_conv.astype(F32), (1, 0, 2)),
                          jnp.transpose(qkv_pre.reshape(b, L, CONV_DIM), (1, 0, 2))], axis=0)
    misc_t = jnp.transpose(misc.reshape(b, L, MISC_W), (1, 0, 2))
    q, k, v, gb = conv_sample(xp, misc_t, p["conv_w"], p["a_log"], p["dt_bias"])
    C = 8
    to_chunks = lambda t: jnp.pad(jnp.transpose(t, (1, 0, 2)), ((0, 0), (0, C - L), (0, 0))).reshape(b * C, -1)
    gdn_o, s_fin = gdn_scan(to_chunks(q), to_chunks(k), to_chunks(v), to_chunks(gb),
                            state_gdn.astype(F32), C=C, n_seq=b, n_chunks=1, first_chunk=0)
    gdn_o = gdn_o.reshape(b, C, GDN_V_WIDTH)[:, :L].reshape(m, GDN_V_WIDTH)
    tabs = _rope_tables(jnp.tile(past_len + jnp.arange(L), b))
    ckv, kpe, qh, _, _ = mla_prep(cq, ckv_raw, misc, tabs, p, tm=tm, with_kv=False)
    kpe32 = kpe[:, PE_LO:PE_LO + ROPE_DIM]
    qabs = q_absorb(qh, p["w_abs"])
    R = L * MLA_HEADS
    rows_lh = lambda t: jnp.transpose(t.reshape(MLA_HEADS, b, L, -1), (1, 2, 0, 3)).reshape(b, R, -1)
    qpe = rows_lh(qh[:, :, PE_LO:PE_LO + ROPE_DIM])
    new_ckv = jnp.pad(ckv.reshape(b, L, KV_RANK), ((0, 0), (0, 8 - L), (0, 0)))
    new_kpe = jnp.pad(kpe32.reshape(b, L, ROPE_DIM), ((0, 0), (0, 8 - L), (0, 0)))
    n_pages = page_table.shape[1]
    PG = 16 if n_pages % 16 == 0 else 2
    o_lat = decode_attn(page_table, p["w_uk_t"], rows_lh(qabs), qpe, new_ckv, new_kpe,
                        cache_ckv, cache_kpe, PG=PG, L=L)
    o_lat_h = jnp.transpose(o_lat.reshape(b, L, MLA_HEADS, KV_RANK), (2, 0, 1, 3)).reshape(MLA_HEADS, m, KV_RANK)
    mla_o = latent_out(o_lat_h, p["w_uv_h"])
    h1, hn2 = mix_out(x, gdn_o, z, mla_o, p, tm=tm, row_block0=0)
    y = ffn(h1, hn2, p["w_gate"], p["w_up"], p["w_down"], tm=tm, tf=_ff_tile(p["w_gate"].shape[1]))
    new_conv = qkv_pre.reshape(b, L, CONV_DIM)[:, L - (CONV_WIDTH - 1):]
    return (y.reshape(b, L, d), new_conv, s_fin, ckv.reshape(b, L, KV_RANK), kpe32.reshape(b, L, ROPE_DIM))


def kernel(x_prompt, x_sample, cache_conv, state_gdn, cache_ckv, cache_kpe, page_table, meta_tokens, norm1, w_in, conv_w, a_log, dt_bias, gdn_norm, q_a_norm, w_q_b, q_nope_norm, q_pe_norm, kv_a_norm, k_pe_norm, w_uk, w_uv, k_nope_norm, w_o, norm2, w_gate, w_up, w_down):
    assert x_prompt.shape[0] == 1 and norm1.shape[0] == 1, "one prompt sequence, one layer"
    p = _prep_params(norm1[0], w_in[0], conv_w[0], a_log[0], dt_bias[0], gdn_norm[0], q_a_norm[0],
                     w_q_b[0], q_nope_norm[0], q_pe_norm[0], kv_a_norm[0], k_pe_norm[0], w_uk[0],
                     w_uv[0], k_nope_norm[0], w_o[0], norm2[0], w_gate[0], w_up[0], w_down[0])
    y_p, conv_p, st_p, ckv_p, kpe_p = _prompt_group(x_prompt[0], meta_tokens, p)
    y_s, conv_s, st_s, ckv_s, kpe_s = _sample_group(x_sample, cache_conv[0], state_gdn[0], cache_ckv,
                                                     cache_kpe, page_table, p)
    lead = lambda t: t[None, None]
    return (y_p[None], y_s, lead(conv_p), lead(st_p), lead(ckv_p), lead(kpe_p),
            conv_s[None], st_s[None], ckv_s[None], kpe_s[None])
```

```python
import functools
import math

import jax
import jax.numpy as jnp
import numpy as np
from jax import lax
from jax.experimental import pallas as pl
from jax.experimental.pallas import tpu as pltpu

F32 = jnp.float32
BF16 = jnp.bfloat16
HIGHEST = lax.Precision.HIGHEST

N_META = 16
EPS = 1e-6
NEG_INF = -1e30
GDN_HEADS = 4
GDN_DK = 128
GDN_DV = 128
GDN_QK_WIDTH = GDN_HEADS * GDN_DK
GDN_V_WIDTH = GDN_HEADS * GDN_DV
CONV_WIDTH = 4
CONV_DIM = 2 * GDN_QK_WIDTH + GDN_V_WIDTH
CHUNK = 64
MLA_HEADS = 8
Q_RANK = 384
KV_RANK = 256
NOPE_DIM = 64
ROPE_DIM = 32
V_DIM = 64
QK_DIM = NOPE_DIM + ROPE_DIM
ROPE_THETA = 10000.0
SCORE_SCALE_LOG2 = QK_DIM ** -0.5 * math.log2(math.e)

LANES = 128
HEAD_PAD = LANES
PE_LO = NOPE_DIM
PE_HALF = ROPE_DIM // 2
MISC_W = LANES
P0 = 512
VMEM_LIMIT = 56 * 1024 * 1024


def _cp(sem, vmem=None):
    return pltpu.CompilerParams(dimension_semantics=sem, vmem_limit_bytes=vmem or VMEM_LIMIT)


def _sigmoid(x):
    return 1.0 / (1.0 + jnp.exp(-x))


def _softplus(x):
    return jnp.maximum(x, 0.0) + jnp.log1p(jnp.exp(-jnp.abs(x)))


def _dot(a, b, precision=None):
    return jnp.dot(a, b, preferred_element_type=F32, precision=precision)


def _dot_nt(a, b, precision=None):
    return lax.dot_general(a, b, (((1,), (1,)), ((), ())), preferred_element_type=F32,
                           precision=precision)


W_SPLITS = (CONV_DIM, GDN_V_WIDTH, Q_RANK, KV_RANK, MISC_W)


def _front_body(*refs, has_lead):
    if has_lead:
        lead_ref, x_ref, g_ref, w_ref = refs[:4]
        x = jnp.where(pl.program_id(0) == 0, lead_ref[...], x_ref[...])
    else:
        x_ref, g_ref, w_ref = refs[:3]
        x = x_ref[...]
    outs = refs[-len(W_SPLITS):]
    hn = ((x * lax.rsqrt(jnp.mean(x * x, axis=-1, keepdims=True) + EPS)) * g_ref[...]).astype(BF16)
    off = 0
    for width, o_ref in zip(W_SPLITS, outs):
        o_ref[...] = _dot(hn, w_ref[:, off:off + width])
        off += width


def front_proj(x, norm1, w_r, tm, lead=None):
    m, d = x.shape
    n_all = sum(W_SPLITS)
    row = lambda i: (i, 0)
    fixed = lambda i: (0, 0)
    n_blocks = m // tm
    in_specs = [pl.BlockSpec((tm, d), row)]
    args = [x]
    if lead is not None:
        n_blocks += 1
        in_specs = [pl.BlockSpec((tm, d), fixed), pl.BlockSpec((tm, d), lambda i: (jnp.maximum(i - 1, 0), 0))]
        args = [lead, x]
    return pl.pallas_call(
        functools.partial(_front_body, has_lead=lead is not None),
        grid=(n_blocks,),
        in_specs=in_specs + [pl.BlockSpec((1, d), fixed), pl.BlockSpec((d, n_all), fixed)],
        out_specs=[pl.BlockSpec((tm, w), row) for w in W_SPLITS],
        out_shape=[jax.ShapeDtypeStruct((n_blocks * tm, w), F32) for w in W_SPLITS],
        compiler_params=_cp(("arbitrary",)),
        name="front_proj",
    )(*args, norm1.reshape(1, d), w_r)


def _gates(misc, alog, dtb):
    lane = lax.broadcasted_iota(jnp.int32, misc.shape, 1)
    g = -jnp.exp(alog) * _softplus(misc + dtb)
    beta = _sigmoid(misc)
    return jnp.where(lane < GDN_HEADS, g, jnp.where(lane < 2 * GDN_HEADS, beta, 0.0))


def _conv_cols(taps, w_ref, c, q_ref, k_ref, v_ref):
    cols = slice(c * LANES, (c + 1) * LANES)
    acc = taps(0, cols) * w_ref[0:1, cols]
    for j in range(1, CONV_WIDTH):
        acc = acc + taps(j, cols) * w_ref[j:j + 1, cols]
    y = acc * _sigmoid(acc)
    if c < 2 * GDN_HEADS:
        yn = y * lax.rsqrt(jnp.sum(y * y, axis=-1, keepdims=True) + EPS)
        if c < GDN_HEADS:
            q_ref[:, cols] = yn * (GDN_DK ** -0.5)
        else:
            k_ref[:, (c - GDN_HEADS) * LANES:(c - GDN_HEADS + 1) * LANES] = yn
    else:
        v_ref[:, (c - 2 * GDN_HEADS) * LANES:(c - 2 * GDN_HEADS + 1) * LANES] = y


def _conv_prompt_body(x_ref, halo_ref, misc_ref, w_ref, alog_ref, dtb_ref,
                      q_ref, k_ref, v_ref, gb_ref, xs, *, tm, first_valid_row):
    i = pl.program_id(0)
    xs[8:tm + 8, :] = x_ref[...]
    xs[0:8, :] = jnp.where(i > 0, halo_ref[...], 0.0)
    taps = lambda j, cols: xs[8 - (CONV_WIDTH - 1) + j:8 - (CONV_WIDTH - 1) + j + tm, cols]
    for c in range(CONV_DIM // LANES):
        _conv_cols(taps, w_ref, c, q_ref, k_ref, v_ref)
    gb = _gates(misc_ref[...], alog_ref[...], dtb_ref[...])
    row = i * tm + lax.broadcasted_iota(jnp.int32, gb.shape, 0)
    gb_ref[...] = jnp.where(row >= first_valid_row, gb, 0.0)


def conv_prompt(qkv_pre, misc, conv_w, alog, dtb, tm, first_valid_row):
    m = qkv_pre.shape[0]
    row = lambda i: (i, 0)
    fixed = lambda i: (0, 0)
    halo = lambda i: (jnp.maximum(i * (tm // 8) - 1, 0), 0)
    body = functools.partial(_conv_prompt_body, tm=tm, first_valid_row=first_valid_row)
    return pl.pallas_call(
        body,
        grid=(m // tm,),
        in_specs=[pl.BlockSpec((tm, CONV_DIM), row), pl.BlockSpec((8, CONV_DIM), halo),
                  pl.BlockSpec((tm, MISC_W), row), pl.BlockSpec((CONV_WIDTH, CONV_DIM), fixed),
                  pl.BlockSpec((1, MISC_W), fixed), pl.BlockSpec((1, MISC_W), fixed)],
        out_specs=[pl.BlockSpec((tm, GDN_QK_WIDTH), row)] * 3 + [pl.BlockSpec((tm, MISC_W), row)],
        out_shape=[jax.ShapeDtypeStruct((m, GDN_QK_WIDTH), F32)] * 3
        + [jax.ShapeDtypeStruct((m, MISC_W), F32)],
        scratch_shapes=[pltpu.VMEM((tm + 8, CONV_DIM), F32)],
        compiler_params=_cp(("parallel",)),
        name="conv_prompt",
    )(qkv_pre, qkv_pre, misc, conv_w, alog, dtb)


def _conv_sample_body(xp_ref, misc_ref, w_ref, alog_ref, dtb_ref, q_ref, k_ref, v_ref, gb_ref, *, L):
    for l in range(L):
        taps = lambda j, cols, l=l: xp_ref[l + j, :, cols]
        qv, kv, vv = (r.at[l] for r in (q_ref, k_ref, v_ref))
        for c in range(CONV_DIM // LANES):
            _conv_cols(taps, w_ref, c, qv, kv, vv)
        gb_ref[l] = _gates(misc_ref[l], alog_ref[...], dtb_ref[...])


def conv_sample(xp, misc, conv_w, alog, dtb):
    lp, b, _ = xp.shape
    L = lp - (CONV_WIDTH - 1)
    body = functools.partial(_conv_sample_body, L=L)
    return pl.pallas_call(
        body,
        out_shape=[jax.ShapeDtypeStruct((L, b, GDN_QK_WIDTH), F32)] * 3
        + [jax.ShapeDtypeStruct((L, b, MISC_W), F32)],
        compiler_params=pltpu.CompilerParams(vmem_limit_bytes=VMEM_LIMIT),
        name="conv_sample",
    )(xp, misc, conv_w, alog, dtb)


def _gdn_prep_body(q_ref, k_ref, v_ref, gb_ref, wq_ref, ut_ref, qk_ref, kdt_ref, gl_ref, *, C, CB):
    row = lax.broadcasted_iota(jnp.int32, (C, C), 0)
    col = lax.broadcasted_iota(jnp.int32, (C, C), 1)
    incl = row >= col
    strict = row > col
    tri = incl.astype(F32)
    n_sq = int(math.log2(C)) - 1
    inst = [(cc, h) for cc in range(CB) for h in range(GDN_HEADS)]
    rows_of = lambda cc: slice(cc * C, (cc + 1) * C)
    cols_of = lambda h: slice(h * LANES, (h + 1) * LANES)
    g_cum, g_cum_t, gbs = [], [], []
    for cc in range(CB):
        gb = gb_ref[rows_of(cc), :]
        gc_all = _dot(tri, gb, HIGHEST)
        gl_ref[cc] = jnp.broadcast_to(jnp.exp(gc_all[C - 1:C, :]), (8, LANES))
        gbs.append(gb)
        g_cum.append(gc_all)
        g_cum_t.append(gc_all.T)
    ks = [k_ref[rows_of(cc), cols_of(h)] for cc, h in inst]
    both = [_dot_nt(jnp.concatenate([q_ref[rows_of(cc), cols_of(h)], k], axis=0).astype(BF16), k.astype(BF16))
            for (cc, h), k in zip(inst, ks)]
    a, r, gcs = [], [], []
    for i, (cc, h) in enumerate(inst):
        gc = g_cum[cc][:, h:h + 1]
        gr = g_cum_t[cc][h:h + 1, :]
        beta = gbs[cc][:, GDN_HEADS + h:GDN_HEADS + h + 1]
        decay = jnp.where(incl, jnp.exp(jnp.where(incl, gc - gr, 0.0)), 0.0)
        a.append(jnp.where(strict, -(beta * both[i][C:2 * C] * decay), 0.0))
        qk = both[i][0:C] * decay
        qk_ref[rows_of(cc), cols_of(h)] = jnp.concatenate(
            [qk, jnp.zeros((C, LANES - C), F32)], axis=1).astype(BF16)
        e_g = jnp.exp(gc)
        wq_ref[cc, C:2 * C, cols_of(h)] = (e_g * q_ref[rows_of(cc), cols_of(h)]).astype(BF16)
        k_dec = jnp.exp(g_cum[cc][C - 1:C, h:h + 1] - gc) * ks[i]
        kdt_ref[cc, h] = k_dec.T.astype(BF16)
        r.append(jnp.concatenate([(beta * e_g) * ks[i], beta * v_ref[rows_of(cc), cols_of(h)]], axis=1))
    for step in range(n_sq + 1):
        ab = [x.astype(BF16) for x in a]
        r = [x + _dot(y, x.astype(BF16)) for x, y in zip(r, ab)]
        if step < n_sq:
            a = [_dot(y, y) for y in ab]
    for i, (cc, h) in enumerate(inst):
        wq_ref[cc, 0:C, cols_of(h)] = r[i][:, 0:LANES].astype(BF16)
        ut_ref[rows_of(cc), cols_of(h)] = r[i][:, LANES:2 * LANES]


def gdn_prep(q, k, v, gb, *, C, CB):
    rows = q.shape[0]
    nc = rows // C
    blk = lambda i: (i, 0)
    body = functools.partial(_gdn_prep_body, C=C, CB=CB)
    return pl.pallas_call(
        body,
        grid=(nc // CB,),
        in_specs=[pl.BlockSpec((CB * C, GDN_QK_WIDTH), blk)] * 3 + [pl.BlockSpec((CB * C, MISC_W), blk)],
        out_specs=[pl.BlockSpec((CB, 2 * C, GDN_QK_WIDTH), lambda i: (i, 0, 0)),
                   pl.BlockSpec((CB * C, GDN_V_WIDTH), blk),
                   pl.BlockSpec((CB * C, GDN_HEADS * LANES), blk),
                   pl.BlockSpec((CB, GDN_HEADS, GDN_DK, C), lambda i: (i, 0, 0, 0)),
                   pl.BlockSpec((CB, 8, LANES), lambda i: (i, 0, 0))],
        out_shape=[jax.ShapeDtypeStruct((nc, 2 * C, GDN_QK_WIDTH), BF16),
                   jax.ShapeDtypeStruct((rows, GDN_V_WIDTH), F32),
                   jax.ShapeDtypeStruct((rows, GDN_HEADS * LANES), BF16),
                   jax.ShapeDtypeStruct((nc, GDN_HEADS, GDN_DK, C), BF16),
                   jax.ShapeDtypeStruct((nc, 8, LANES), F32)],
        compiler_params=_cp(("parallel",)),
        name="gdn_prep",
    )(q, k, v, gb)


def _gdn_scan_body(gl_ref, wq_ref, ut_ref, qk_ref, kdt_ref, s0_ref, o_ref, sfin_ref, s_scr,
                   *, C, CB, n_steps):
    sq = pl.program_id(0)
    st = pl.program_id(1)

    @pl.when(st == 0)
    def _():
        s_scr[...] = s0_ref[0]

    for cc in range(CB):
        rows = slice(cc * C, (cc + 1) * C)
        chunk = (sq * n_steps + st) * CB + cc
        heads = range(GDN_HEADS)
        cols = [slice(h * LANES, (h + 1) * LANES) for h in heads]
        ss = [s_scr[h] for h in heads]
        ws = [_dot(wq_ref[cc, :, cols[h]], ss[h].astype(BF16)) for h in heads]
        ub = [(ut_ref[rows, cols[h]] - ws[h][0:C]).astype(BF16) for h in heads]
        for h in heads:
            s_scr[h] = gl_ref[chunk * GDN_HEADS + h] * ss[h] + _dot(kdt_ref[cc, h], ub[h])
        for h in heads:
            o_ref[rows, cols[h]] = ws[h][C:2 * C] + _dot(qk_ref[rows, h * LANES:h * LANES + C], ub[h])

    @pl.when(st == n_steps - 1)
    def _():
        sfin_ref[0] = s_scr[...]


def gdn_scan(gl, wq, ut, qk, kdt, s0, *, C, CB, n_seq):
    rows = ut.shape[0]
    nc = rows // C
    n_steps = nc // (n_seq * CB)
    blk = lambda s, t, gl: (s * n_steps + t, 0)
    blk3 = lambda s, t, gl: (s * n_steps + t, 0, 0)
    blk4 = lambda s, t, gl: (s * n_steps + t, 0, 0, 0)
    st4 = lambda s, t, gl: (s, 0, 0, 0)
    body = functools.partial(_gdn_scan_body, C=C, CB=CB, n_steps=n_steps)
    grid_spec = pltpu.PrefetchScalarGridSpec(
        num_scalar_prefetch=1,
        grid=(n_seq, n_steps),
        in_specs=[pl.BlockSpec((CB, 2 * C, GDN_QK_WIDTH), blk3), pl.BlockSpec((CB * C, GDN_V_WIDTH), blk),
                  pl.BlockSpec((CB * C, GDN_HEADS * LANES), blk),
                  pl.BlockSpec((CB, GDN_HEADS, GDN_DK, C), blk4),
                  pl.BlockSpec((1, GDN_HEADS, GDN_DK, GDN_DV), st4)],
        out_specs=[pl.BlockSpec((CB * C, GDN_V_WIDTH), blk),
                   pl.BlockSpec((1, GDN_HEADS, GDN_DK, GDN_DV), st4)],
        scratch_shapes=[pltpu.VMEM((GDN_HEADS, GDN_DK, GDN_DV), F32)],
    )
    return pl.pallas_call(
        body,
        grid_spec=grid_spec,
        out_shape=[jax.ShapeDtypeStruct((rows, GDN_V_WIDTH), F32), jax.ShapeDtypeStruct(s0.shape, F32)],
        compiler_params=_cp(("arbitrary", "arbitrary")),
        name="gdn_scan",
    )(gl, wq, ut, qk, kdt, s0)


def gdn(q, k, v, gb, s0, *, C, CB_prep, CB_scan, n_seq):
    wq, ut, qk, kdt, glv = gdn_prep(q, k, v, gb, C=C, CB=CB_prep)
    gl = glv[:, 0, :GDN_HEADS].reshape(-1)
    return gdn_scan(gl, wq, ut, qk, kdt, s0, C=C, CB=CB_scan, n_seq=n_seq)


def _rope(y, cosb, sin_lo, sin_hi):
    return (y * cosb + pltpu.roll(y, LANES - PE_HALF, 1) * sin_lo + pltpu.roll(y, PE_HALF, 1) * sin_hi)


def _slab_norm(x, gain, lane):
    x2 = x * x
    is_nope = lane < NOPE_DIM
    is_pe = (lane >= PE_LO) & (lane < PE_LO + ROPE_DIM)
    ms_n = jnp.sum(jnp.where(is_nope, x2, 0.0), axis=-1, keepdims=True) * (1.0 / NOPE_DIM)
    ms_p = jnp.sum(jnp.where(is_pe, x2, 0.0), axis=-1, keepdims=True) * (1.0 / ROPE_DIM)
    rinv = jnp.where(is_nope, lax.rsqrt(ms_n + EPS), lax.rsqrt(ms_p + EPS))
    return (x * rinv) * gain


def _mla_body(cq_ref, ckvr_ref, misc_ref, cos_ref, slo_ref, shi_ref, qan_ref, wqb_ref, qg_ref,
              kvn_ref, kg_ref, *rest, with_kv):
    if with_kv:
        wuk_ref, wuvt_ref, ckv_ref, kpe_ref, q_ref, k_ref, vt_ref = rest
    else:
        ckv_ref, kpe_ref, q_ref = rest
    cosb, slo, shi = cos_ref[...], slo_ref[...], shi_ref[...]
    lane = lax.broadcasted_iota(jnp.int32, cosb.shape, 1)
    cq = cq_ref[...]
    cqn = ((cq * lax.rsqrt(jnp.mean(cq * cq, axis=-1, keepdims=True) + EPS)) * qan_ref[...]).astype(BF16)
    c = ckvr_ref[...]
    ckv = (c * lax.rsqrt(jnp.mean(c * c, axis=-1, keepdims=True) + EPS)) * kvn_ref[...]
    ckv_ref[...] = ckv
    ckv_b = ckv.astype(BF16)
    kpe_raw = jnp.where((lane >= PE_LO) & (lane < PE_LO + ROPE_DIM), misc_ref[...], 0.0)
    kpe = _rope(_slab_norm(kpe_raw, kg_ref[...], lane), cosb, slo, shi)
    kpe = jnp.where(lane >= PE_LO, kpe, 0.0)
    kpe_ref[...] = kpe
    for h in range(MLA_HEADS):
        cols = slice(h * HEAD_PAD, (h + 1) * HEAD_PAD)
        qh = _dot(cqn, wqb_ref[:, cols])
        qh = _rope(_slab_norm(qh, qg_ref[...], lane), cosb, slo, shi)
        q_ref[h] = (qh * SCORE_SCALE_LOG2).astype(BF16)
        if with_kv:
            kh = _dot(ckv_b, wuk_ref[:, cols])
            k_ref[h] = (_slab_norm(kh, kg_ref[...], lane) + kpe).astype(BF16)
            vt = _dot_nt(wuvt_ref[h], ckv_b)
            ones_row = lax.broadcasted_iota(jnp.int32, vt.shape, 0) == V_DIM
            vt_ref[h] = jnp.where(ones_row, 1.0, vt).astype(BF16)


def mla_prep(cq, ckv_raw, misc, rope_tabs, p, tm, with_kv):
    m = cq.shape[0]
    row = lambda i: (i, 0)
    fixed = lambda i: (0, 0)
    hw = MLA_HEADS * HEAD_PAD
    in_specs = ([pl.BlockSpec((tm, Q_RANK), row), pl.BlockSpec((tm, KV_RANK), row),
                 pl.BlockSpec((tm, MISC_W), row)] + [pl.BlockSpec((tm, LANES), row)] * 3
                + [pl.BlockSpec((1, Q_RANK), fixed), pl.BlockSpec((Q_RANK, hw), fixed),
                   pl.BlockSpec((1, LANES), fixed), pl.BlockSpec((1, KV_RANK), fixed),
                   pl.BlockSpec((1, LANES), fixed)])
    args = [cq, ckv_raw, misc, *rope_tabs, p["q_a_norm"], p["w_q_b"], p["q_gain"], p["kv_a_norm"], p["k_gain"]]
    out_specs = [pl.BlockSpec((tm, KV_RANK), row), pl.BlockSpec((tm, LANES), row),
                 pl.BlockSpec((MLA_HEADS, tm, HEAD_PAD), lambda i: (0, i, 0))]
    out_shape = [jax.ShapeDtypeStruct((m, KV_RANK), F32), jax.ShapeDtypeStruct((m, LANES), F32),
                 jax.ShapeDtypeStruct((MLA_HEADS, m, HEAD_PAD), BF16)]
    if with_kv:
        in_specs += [pl.BlockSpec((KV_RANK, hw), fixed),
                     pl.BlockSpec((MLA_HEADS, HEAD_PAD, KV_RANK), lambda i: (0, 0, 0))]
        args += [p["w_uk"], p["w_uv_t"]]
        out_specs += [pl.BlockSpec((MLA_HEADS, tm, HEAD_PAD), lambda i: (0, i, 0)),
                      pl.BlockSpec((MLA_HEADS, HEAD_PAD, tm), lambda i: (0, 0, i))]
        out_shape += [jax.ShapeDtypeStruct((MLA_HEADS, m, HEAD_PAD), BF16),
                      jax.ShapeDtypeStruct((MLA_HEADS, HEAD_PAD, m), BF16)]
    return pl.pallas_call(
        functools.partial(_mla_body, with_kv=with_kv),
        grid=(m // tm,),
        in_specs=in_specs,
        out_specs=out_specs,
        out_shape=out_shape,
        compiler_params=_cp(("parallel",)),
        name="mla_prep",
    )(*args)


def _flash_body(qmap_ref, kmap_ref, q_ref, k_ref, vt_ref, o_ref, m_scr, acc_scr,
                *, tq, tk, q_block0, first_key):
    step = pl.program_id(0)
    q0 = (qmap_ref[step] + q_block0) * tq
    k0 = kmap_ref[step] * tk
    first_block = k0 == (first_key // tk) * tk

    @pl.when(first_block)
    def _():
        m_scr[...] = jnp.full(m_scr.shape, NEG_INF, F32)
        acc_scr[...] = jnp.zeros(acc_scr.shape, F32)

    def update(masked):
        if masked:
            kpos = k0 + lax.broadcasted_iota(jnp.int32, (tk, tq), 0)
            qpos = q0 + lax.broadcasted_iota(jnp.int32, (tk, tq), 1)
            valid = (kpos <= qpos) & (kpos >= first_key)
        st_next = _dot_nt(k_ref[0], q_ref[0])
        for h in range(MLA_HEADS):
            st = st_next
            if h + 1 < MLA_HEADS:
                st_next = _dot_nt(k_ref[h + 1], q_ref[h + 1])
            if masked:
                st = jnp.where(valid, st, NEG_INF)
            m_prev = m_scr[h]
            m_new = jnp.maximum(m_prev, jnp.max(st, axis=0, keepdims=True))
            e = jnp.exp2(st - m_new).astype(BF16)
            acc_scr[h] = jnp.exp2(m_prev - m_new) * acc_scr[h] + _dot(vt_ref[h], e)
            m_scr[h] = m_new

    needs_mask = first_block | (k0 + tk - 1 > q0)
    pl.when(needs_mask)(lambda: update(True))
    pl.when(jnp.logical_not(needs_mask))(lambda: update(False))

    @pl.when(k0 + tk >= q0 + tq)
    def _():
        for h in range(MLA_HEADS):
            a = acc_scr[h]
            o = a * (1.0 / a[V_DIM:V_DIM + 1, :])
            o_ref[:, h * HEAD_PAD:(h + 1) * HEAD_PAD] = o.T.astype(BF16)


def flash_prompt(q, k, vt, *, tq, tk, first_key):
    tp = q.shape[1]
    q_block0 = P0 // tq
    nq = (tp - P0) // tq
    qs, ks = [], []
    for qi in range(nq):
        last_k = ((qi + q_block0 + 1) * tq - 1) // tk
        for ki in range(first_key // tk, last_k + 1):
            qs.append(qi)
            ks.append(ki)
    qmap = jnp.asarray(qs, jnp.int32)
    kmap = jnp.asarray(ks, jnp.int32)
    body = functools.partial(_flash_body, tq=tq, tk=tk, q_block0=q_block0, first_key=first_key)
    grid_spec = pltpu.PrefetchScalarGridSpec(
        num_scalar_prefetch=2,
        grid=(len(qs),),
        in_specs=[pl.BlockSpec((MLA_HEADS, tq, HEAD_PAD), lambda s, qm, km: (0, qm[s] + q_block0, 0)),
                  pl.BlockSpec((MLA_HEADS, tk, HEAD_PAD), lambda s, qm, km: (0, km[s], 0)),
                  pl.BlockSpec((MLA_HEADS, HEAD_PAD, tk), lambda s, qm, km: (0, 0, km[s]))],
        out_specs=pl.BlockSpec((tq, MLA_HEADS * HEAD_PAD), lambda s, qm, km: (qm[s], 0)),
        scratch_shapes=[pltpu.VMEM((MLA_HEADS, 1, tq), F32), pltpu.VMEM((MLA_HEADS, HEAD_PAD, tq), F32)],
    )
    return pl.pallas_call(
        body,
        grid_spec=grid_spec,
        out_shape=jax.ShapeDtypeStruct((tp - P0, MLA_HEADS * HEAD_PAD), BF16),
        compiler_params=_cp(("arbitrary",)),
        name="flash_prompt",
    )(qmap, kmap, q, k, vt)


def _qabs_body(q_ref, w_ref, o_ref):
    o_ref[0] = _dot(q_ref[0], w_ref[0]).astype(BF16)


def q_absorb(q, w_abs):
    h, m, _ = q.shape
    blk = lambda i: (i, 0, 0)
    return pl.pallas_call(
        _qabs_body,
        grid=(h,),
        in_specs=[pl.BlockSpec((1, m, HEAD_PAD), blk), pl.BlockSpec((1, HEAD_PAD, KV_RANK), blk)],
        out_specs=pl.BlockSpec((1, m, KV_RANK), blk),
        out_shape=jax.ShapeDtypeStruct((h, m, KV_RANK), BF16),
        compiler_params=_cp(("parallel",)),
        name="q_absorb",
    )(q, w_abs)


def _decode_scores(waug, qpe, ckv_b, kpe_t, n_lq):
    kp = _dot_nt(waug, ckv_b)
    n_k = MLA_HEADS * NOPE_DIM
    kproj = kp[:n_k]
    ms = jnp.sum((kproj * kproj).reshape(MLA_HEADS, NOPE_DIM, kproj.shape[-1]), axis=1) * (1.0 / NOPE_DIM)
    rinv = lax.rsqrt(ms + EPS)
    s_nope = kp[n_k:]
    s_pe = _dot(qpe, kpe_t)
    rinv_rows = jnp.concatenate([rinv] * n_lq, axis=0)
    return s_nope * rinv_rows + s_pe


def _decode_body(pt_ref, wukt_ref, qabs_ref, qpe_ref, nckv_ref, nkpet_ref, *rest, PG, L):
    ckv_refs = rest[:PG]
    kpe_refs = rest[PG:2 * PG]
    o_ref = rest[2 * PG]
    waug, ckvb, s_scr, m_scr, l_scr, acc_scr = rest[2 * PG + 1:]
    j = pl.program_id(1)
    n_k = MLA_HEADS * NOPE_DIM
    R = MLA_HEADS * L
    page = ckv_refs[0].shape[2]

    @pl.when(j == 0)
    def _():
        waug[0:n_k, :] = wukt_ref[...]
        waug[n_k:n_k + R, :] = qabs_ref[0]
        c_new = nckv_ref[0].astype(BF16)
        s = _decode_scores(waug[...], qpe_ref[0], c_new, nkpet_ref[0].astype(BF16), L)
        r_l = lax.shift_right_logical(lax.broadcasted_iota(jnp.int32, s.shape, 0),
                                      int(math.log2(MLA_HEADS)))
        t = lax.broadcasted_iota(jnp.int32, s.shape, 1)
        s = jnp.where((t <= r_l) & (t < L), s, NEG_INF)
        m = jnp.max(s, axis=-1, keepdims=True)
        e = jnp.exp2(s - m)
        m_scr[...] = m
        l_scr[...] = jnp.sum(e, axis=-1, keepdims=True)
        acc_scr[...] = _dot(e.astype(BF16), c_new)

    w = waug[...]
    qpe = qpe_ref[0]
    for p in range(PG // 2):
        c2 = jnp.concatenate([ckv_refs[2 * p][0, 0], ckv_refs[2 * p + 1][0, 0]], axis=0).astype(BF16)
        k2 = jnp.concatenate([kpe_refs[2 * p][0, 0], kpe_refs[2 * p + 1][0, 0]], axis=1).astype(BF16)
        ckvb[2 * p * page:(2 * p + 2) * page, :] = c2
        s_scr[:, 2 * p * page:(2 * p + 2) * page] = _decode_scores(w, qpe, c2, k2, L)
    s = s_scr[...]
    m_prev = m_scr[...]
    m_new = jnp.maximum(m_prev, jnp.max(s, axis=-1, keepdims=True))
    alpha = jnp.exp2(m_prev - m_new)
    e = jnp.exp2(s - m_new)
    l_scr[...] = alpha * l_scr[...] + jnp.sum(e, axis=-1, keepdims=True)
    acc_scr[...] = alpha * acc_scr[...] + _dot(e.astype(BF16), ckvb[...])
    m_scr[...] = m_new

    @pl.when(j == pl.num_programs(1) - 1)
    def _():
        o_ref[0] = acc_scr[...] / l_scr[...]


def decode_attn(page_table, wuk_t, qabs, qpe, new_ckv, new_kpe_t, cache_ckv, cache_kpe_t, *, PG, L):
    b, n_pages = page_table.shape
    page = cache_ckv.shape[2]
    R = MLA_HEADS * L
    n_k = MLA_HEADS * NOPE_DIM
    per_b = lambda i, j, pt: (i, 0, 0)
    fixed = lambda i, j, pt: (0, 0)

    def page_map(slot):
        return lambda i, j, pt: (0, pt[i * n_pages + j * PG + slot], 0, 0)

    body = functools.partial(_decode_body, PG=PG, L=L)
    grid_spec = pltpu.PrefetchScalarGridSpec(
        num_scalar_prefetch=1,
        grid=(b, n_pages // PG),
        in_specs=[pl.BlockSpec((n_k, KV_RANK), fixed), pl.BlockSpec((1, R, KV_RANK), per_b),
                  pl.BlockSpec((1, R, ROPE_DIM), per_b), pl.BlockSpec((1, 8, KV_RANK), per_b),
                  pl.BlockSpec((1, ROPE_DIM, 8), per_b)]
        + [pl.BlockSpec((1, 1, page, KV_RANK), page_map(s)) for s in range(PG)]
        + [pl.BlockSpec((1, 1, ROPE_DIM, page), page_map(s)) for s in range(PG)],
        out_specs=pl.BlockSpec((1, R, KV_RANK), per_b),
        scratch_shapes=[pltpu.VMEM((n_k + R, KV_RANK), BF16), pltpu.VMEM((PG * page, KV_RANK), BF16),
                        pltpu.VMEM((R, PG * page), F32), pltpu.VMEM((R, 1), F32),
                        pltpu.VMEM((R, 1), F32), pltpu.VMEM((R, KV_RANK), F32)],
    )
    return pl.pallas_call(
        body,
        grid_spec=grid_spec,
        out_shape=jax.ShapeDtypeStruct((b, R, KV_RANK), F32),
        compiler_params=_cp(("arbitrary", "arbitrary")),
        name="decode_attn",
    )(page_table.reshape(-1), wuk_t, qabs, qpe, new_ckv, new_kpe_t,
      *([cache_ckv] * PG), *([cache_kpe_t] * PG))


def _latent_out_body(x_ref, w_ref, o_ref):
    o_ref[...] = _dot(x_ref[0].astype(BF16), w_ref[0]).astype(BF16)


def latent_out(o_lat, w_uv_h):
    h, m, _ = o_lat.shape
    return pl.pallas_call(
        _latent_out_body,
        grid=(h,),
        in_specs=[pl.BlockSpec((1, m, KV_RANK), lambda i: (i, 0, 0)),
                  pl.BlockSpec((1, KV_RANK, HEAD_PAD), lambda i: (i, 0, 0))],
        out_specs=pl.BlockSpec((m, HEAD_PAD), lambda i: (0, i)),
        out_shape=jax.ShapeDtypeStruct((m, h * HEAD_PAD), BF16),
        compiler_params=_cp(("parallel",)),
        name="latent_out",
    )(o_lat, w_uv_h)


def _mix_body(h_ref, o_ref, z_ref, mla_ref, gg_ref, wog_ref, wom_ref, n2_ref, h1_ref, hn_ref):
    acc = h_ref[...] + _dot(mla_ref[...], wom_ref[...])
    for h in range(GDN_HEADS):
        cols = slice(h * LANES, (h + 1) * LANES)
        o = o_ref[:, cols]
        z = z_ref[:, cols]
        y = ((o * lax.rsqrt(jnp.mean(o * o, axis=-1, keepdims=True) + EPS)) * gg_ref[...]) * (z * _sigmoid(z))
        acc = acc + _dot(y.astype(BF16), wog_ref[cols, :])
    h1_ref[...] = acc
    hn_ref[...] = ((acc * lax.rsqrt(jnp.mean(acc * acc, axis=-1, keepdims=True) + EPS)) * n2_ref[...]).astype(BF16)


def mix_out(h, gdn_o, z, mla_o, p, tm, row_block0):
    m, d = h.shape
    row = lambda i: (i, 0)
    prow = lambda i: (i + row_block0, 0)
    fixed = lambda i: (0, 0)
    hw = MLA_HEADS * HEAD_PAD
    return pl.pallas_call(
        _mix_body,
        grid=(m // tm,),
        in_specs=[pl.BlockSpec((tm, d), row), pl.BlockSpec((tm, GDN_V_WIDTH), prow),
                  pl.BlockSpec((tm, GDN_V_WIDTH), prow), pl.BlockSpec((tm, hw), row),
                  pl.BlockSpec((1, LANES), fixed), pl.BlockSpec((GDN_V_WIDTH, d), fixed),
                  pl.BlockSpec((hw, d), fixed), pl.BlockSpec((1, d), fixed)],
        out_specs=[pl.BlockSpec((tm, d), row), pl.BlockSpec((tm, d), row)],
        out_shape=[jax.ShapeDtypeStruct((m, d), F32), jax.ShapeDtypeStruct((m, d), BF16)],
        compiler_params=_cp(("parallel",)),
        name="mix_out",
    )(h, gdn_o, z, mla_o, p["gdn_norm"], p["w_o_gdn"], p["w_o_mla"], p["norm2"])


def _ffn_body(h1_ref, hn_ref, wg_ref, wu_ref, wd_ref, y_ref):
    f = pl.program_id(1)

    @pl.when(f == 0)
    def _():
        y_ref[...] = h1_ref[...]

    x = hn_ref[...]
    g = _dot(x, wg_ref[...])
    u = _dot(x, wu_ref[...])
    y_ref[...] += _dot(((g * _sigmoid(g)) * u).astype(BF16), wd_ref[...])


def ffn(h1, hn, wg, wu, wd, tm, tf):
    m, d = h1.shape
    dff = wg.shape[1]
    row = lambda i, f: (i, 0)
    return pl.pallas_call(
        _ffn_body,
        grid=(m // tm, dff // tf),
        in_specs=[pl.BlockSpec((tm, d), row), pl.BlockSpec((tm, d), row),
                  pl.BlockSpec((d, tf), lambda i, f: (0, f)), pl.BlockSpec((d, tf), lambda i, f: (0, f)),
                  pl.BlockSpec((tf, d), lambda i, f: (f, 0))],
        out_specs=pl.BlockSpec((tm, d), row),
        out_shape=jax.ShapeDtypeStruct((m, d), F32),
        compiler_params=_cp(("parallel", "arbitrary")),
        name="ffn",
    )(h1, hn, wg, wu, wd)


def _head_slabs(w, width):
    k, h, _ = w.shape
    return jnp.pad(w, ((0, 0), (0, 0), (0, HEAD_PAD - width))).reshape(k, h * HEAD_PAD)


def _pad_lanes(v, lo=0):
    return jnp.pad(v.astype(F32), (lo, LANES - lo - v.shape[0])).reshape(1, LANES)


def _prep_params(norm1, w_in, conv_w, a_log, dt_bias, gdn_norm, q_a_norm, w_q_b, q_nope_norm,
                 q_pe_norm, kv_a_norm, k_pe_norm, w_uk, w_uv, k_nope_norm, w_o, norm2,
                 w_gate, w_up, w_down):
    d = w_in.shape[0]
    o = 0
    cuts = {}
    for name, wdt in (("qkv", CONV_DIM), ("z", GDN_V_WIDTH), ("a", GDN_HEADS), ("b", GDN_HEADS),
                      ("cq", Q_RANK), ("ckv", KV_RANK), ("kpe", ROPE_DIM)):
        cuts[name] = w_in[:, o:o + wdt]
        o += wdt
    zeros = lambda n: jnp.zeros((d, n), w_in.dtype)
    misc = jnp.concatenate([cuts["a"], cuts["b"], zeros(PE_LO - 2 * GDN_HEADS), cuts["kpe"],
                            zeros(MISC_W - PE_LO - ROPE_DIM)], axis=1)
    w_r = jnp.concatenate([cuts["qkv"], cuts["z"], cuts["cq"], cuts["ckv"], misc], axis=1).astype(BF16)
    wqb = w_q_b.reshape(Q_RANK, MLA_HEADS, QK_DIM)
    gain_q = jnp.concatenate([q_nope_norm, q_pe_norm]).astype(F32)
    gain_k = jnp.concatenate([k_nope_norm, k_pe_norm]).astype(F32)
    w_uk_hdr = jnp.transpose(w_uk, (1, 2, 0))
    w_abs = jnp.pad(w_uk_hdr * k_nope_norm[None, :, None], ((0, 0), (0, HEAD_PAD - NOPE_DIM), (0, 0)))
    w_o_mla = jnp.pad(w_o[GDN_V_WIDTH:].reshape(MLA_HEADS, V_DIM, d),
                      ((0, 0), (0, HEAD_PAD - V_DIM), (0, 0))).reshape(MLA_HEADS * HEAD_PAD, d)
    return dict(
        norm1=norm1, w_in=w_r, conv_w=conv_w.astype(F32),
        a_log=_pad_lanes(a_log), dt_bias=_pad_lanes(dt_bias),
        gdn_norm=gdn_norm.reshape(1, GDN_DV).astype(F32),
        q_a_norm=q_a_norm.reshape(1, Q_RANK).astype(F32), w_q_b=_head_slabs(wqb, QK_DIM).astype(BF16),
        q_gain=_pad_lanes(gain_q), kv_a_norm=kv_a_norm.reshape(1, KV_RANK).astype(F32),
        w_uk=_head_slabs(w_uk, NOPE_DIM).astype(BF16), k_gain=_pad_lanes(gain_k),
        w_uv_t=jnp.pad(jnp.transpose(w_uv, (1, 2, 0)), ((0, 0), (0, HEAD_PAD - V_DIM), (0, 0))).astype(BF16),
        w_uk_t=w_uk_hdr.reshape(MLA_HEADS * NOPE_DIM, KV_RANK).astype(BF16),
        w_abs=w_abs.astype(BF16),
        w_uv_h=jnp.pad(jnp.transpose(w_uv, (1, 0, 2)), ((0, 0), (0, 0), (0, HEAD_PAD - V_DIM))).astype(BF16),
        w_o_gdn=w_o[:GDN_V_WIDTH].astype(BF16), w_o_mla=w_o_mla.astype(BF16),
        norm2=norm2.reshape(1, d).astype(F32),
        w_gate=w_gate.astype(BF16), w_up=w_up.astype(BF16), w_down=w_down.astype(BF16),
    )


def _rope_tables(pos):
    inv_freq = np.power(ROPE_THETA, -np.arange(0, ROPE_DIM, 2, dtype=np.float64) / ROPE_DIM)
    ang = np.asarray(pos, np.float64)[:, None] * inv_freq[None, :]
    cos, sin = jnp.asarray(np.cos(ang), F32), jnp.asarray(np.sin(ang), F32)
    n = len(pos)
    ones = jnp.ones((n, PE_LO), F32)
    z = lambda w: jnp.zeros((n, w), F32)
    tail = LANES - PE_LO - ROPE_DIM
    cosb = jnp.concatenate([ones, cos, cos, z(tail)], axis=1)
    sin_lo = jnp.concatenate([z(PE_LO), -sin, z(PE_HALF + tail)], axis=1)
    sin_hi = jnp.concatenate([z(PE_LO + PE_HALF), sin, z(tail)], axis=1)
    return cosb, sin_lo, sin_hi


def _ff_tile(dff):
    for t in (1408, 1024, 512, 256, 128):
        if dff % t == 0:
            return t
    return dff


def _prompt_group(x_prompt, meta_tokens, p):
    seq, d = x_prompt.shape
    tp = P0 + seq
    meta_row0 = P0 - N_META
    lead = jnp.concatenate([jnp.zeros((meta_row0, d), F32), meta_tokens.astype(F32)], axis=0)
    qkv_pre, z, cq, ckv_raw, misc = front_proj(x_prompt, p["norm1"], p["w_in"], tm=P0, lead=lead)
    q, k, v, gb = conv_prompt(qkv_pre, misc, p["conv_w"], p["a_log"], p["dt_bias"], tm=512,
                              first_valid_row=meta_row0)
    s0 = jnp.zeros((1, GDN_HEADS, GDN_DK, GDN_DV), F32)
    gdn_o, s_fin = gdn(q, k, v, gb, s0, C=CHUNK, CB_prep=4, CB_scan=8, n_seq=1)
    tabs = _rope_tables(np.arange(tp) - meta_row0)
    ckv, kpe, qh, kh, vt = mla_prep(cq, ckv_raw, misc, tabs, p, tm=256, with_kv=True)
    mla_o = flash_prompt(qh, kh, vt, tq=512, tk=512, first_key=meta_row0)
    h1, hn2 = mix_out(x_prompt, gdn_o, z, mla_o, p, tm=512, row_block0=P0 // 512)
    y = ffn(h1, hn2, p["w_gate"], p["w_up"], p["w_down"], tm=512, tf=_ff_tile(p["w_gate"].shape[1]))
    new_conv = qkv_pre[tp - (CONV_WIDTH - 1):]
    return (y, new_conv, s_fin[0], ckv[meta_row0:], kpe[meta_row0:, PE_LO:PE_LO + ROPE_DIM])


def _sample_group(x_sample, cache_conv, state_gdn, cache_ckv, cache_kpe, page_table, p):
    b, L, d = x_sample.shape
    m = b * L
    page = cache_ckv.shape[2]
    past_len = page_table.shape[1] * page
    x = x_sample.reshape(m, d)
    tm = min(m, 256)
    qkv_pre, z, cq, ckv_raw, misc = front_proj(x, p["norm1"], p["w_in"], tm=tm)
    xp = jnp.concatenate([jnp.transpose(cache_conv.astype(F32), (1, 0, 2)),
                          jnp.transpose(qkv_pre.reshape(b, L, CONV_DIM), (1, 0, 2))], axis=0)
    misc_t = jnp.transpose(misc.reshape(b, L, MISC_W), (1, 0, 2))
    q, k, v, gb = conv_sample(xp, misc_t, p["conv_w"], p["a_log"], p["dt_bias"])
    C = 16
    to_chunks = lambda t: jnp.pad(jnp.transpose(t, (1, 0, 2)), ((0, 0), (0, C - L), (0, 0))).reshape(b * C, -1)
    gdn_o, s_fin = gdn(to_chunks(q), to_chunks(k), to_chunks(v), to_chunks(gb), state_gdn.astype(F32),
                       C=C, CB_prep=4 if b % 4 == 0 else 1, CB_scan=1, n_seq=b)
    gdn_o = gdn_o.reshape(b, C, GDN_V_WIDTH)[:, :L].reshape(m, GDN_V_WIDTH)
    tabs = _rope_tables(np.tile(past_len + np.arange(L), b))
    ckv, kpe, qh = mla_prep(cq, ckv_raw, misc, tabs, p, tm=tm, with_kv=False)
    kpe32 = kpe[:, PE_LO:PE_LO + ROPE_DIM]
    qabs = q_absorb(qh, p["w_abs"])
    R = L * MLA_HEADS
    rows_lh = lambda t: jnp.transpose(t.reshape(MLA_HEADS, b, L, -1), (1, 2, 0, 3)).reshape(b, R, -1)
    qpe = rows_lh(qh[:, :, PE_LO:PE_LO + ROPE_DIM])
    new_ckv = jnp.pad(ckv.reshape(b, L, KV_RANK), ((0, 0), (0, 8 - L), (0, 0)))
    new_kpe_t = jnp.pad(jnp.swapaxes(kpe32.reshape(b, L, ROPE_DIM), 1, 2), ((0, 0), (0, 0), (0, 8 - L)))
    n_pages = page_table.shape[1]
    PG = 16 if n_pages % 16 == 0 else 2
    o_lat = decode_attn(page_table, p["w_uk_t"], rows_lh(qabs), qpe, new_ckv, new_kpe_t,
                        cache_ckv, jnp.swapaxes(cache_kpe, 2, 3), PG=PG, L=L)
    o_lat_h = jnp.transpose(o_lat.reshape(b, L, MLA_HEADS, KV_RANK), (2, 0, 1, 3)).reshape(MLA_HEADS, m, KV_RANK)
    mla_o = latent_out(o_lat_h, p["w_uv_h"])
    h1, hn2 = mix_out(x, gdn_o, z, mla_o, p, tm=tm, row_block0=0)
    y = ffn(h1, hn2, p["w_gate"], p["w_up"], p["w_down"], tm=tm, tf=_ff_tile(p["w_gate"].shape[1]))
    new_conv = qkv_pre.reshape(b, L, CONV_DIM)[:, L - (CONV_WIDTH - 1):]
    return (y.reshape(b, L, d), new_conv, s_fin, ckv.reshape(b, L, KV_RANK), kpe32.reshape(b, L, ROPE_DIM))


def kernel(x_prompt, x_sample, cache_conv, state_gdn, cache_ckv, cache_kpe, page_table, meta_tokens, norm1, w_in, conv_w, a_log, dt_bias, gdn_norm, q_a_norm, w_q_b, q_nope_norm, q_pe_norm, kv_a_norm, k_pe_norm, w_uk, w_uv, k_nope_norm, w_o, norm2, w_gate, w_up, w_down):
    assert x_prompt.shape[0] == 1 and norm1.shape[0] == 1, "one prompt sequence, one layer"
    p = _prep_params(norm1[0], w_in[0], conv_w[0], a_log[0], dt_bias[0], gdn_norm[0], q_a_norm[0],
                     w_q_b[0], q_nope_norm[0], q_pe_norm[0], kv_a_norm[0], k_pe_norm[0], w_uk[0],
                     w_uv[0], k_nope_norm[0], w_o[0], norm2[0], w_gate[0], w_up[0], w_down[0])
    y_p, conv_p, st_p, ckv_p, kpe_p = _prompt_group(x_prompt[0], meta_tokens, p)
    y_s, conv_s, st_s, ckv_s, kpe_s = _sample_group(x_sample, cache_conv[0], state_gdn[0], cache_ckv,
                                                     cache_kpe, page_table, p)
    lead = lambda t: t[None, None]
    return (y_p[None], y_s, lead(conv_p), lead(st_p), lead(ckv_p), lead(kpe_p),
            conv_s[None], st_s[None], ckv_s[None], kpe_s[None])
```

```python
import functools
import math

import jax
import jax.numpy as jnp
import numpy as np
from jax import lax
from jax.experimental import pallas as pl
from jax.experimental.pallas import tpu as pltpu

F32 = jnp.float32
BF16 = jnp.bfloat16
HIGHEST = lax.Precision.HIGHEST

N_META = 16
EPS = 1e-6
NEG_INF = -1e30
GDN_HEADS = 4
GDN_DK = 128
GDN_DV = 128
GDN_QK_WIDTH = GDN_HEADS * GDN_DK
GDN_V_WIDTH = GDN_HEADS * GDN_DV
CONV_WIDTH = 4
CONV_DIM = 2 * GDN_QK_WIDTH + GDN_V_WIDTH
CHUNK = 64
MLA_HEADS = 8
Q_RANK = 384
KV_RANK = 256
NOPE_DIM = 64
ROPE_DIM = 32
V_DIM = 64
QK_DIM = NOPE_DIM + ROPE_DIM
ROPE_THETA = 10000.0
SCORE_SCALE_LOG2 = QK_DIM ** -0.5 * math.log2(math.e)

LANES = 128
HEAD_PAD = LANES
PE_LO = NOPE_DIM
PE_HALF = ROPE_DIM // 2
MISC_W = LANES
P0 = 512
FLASH_AHEAD = 2
DECODE_SCORE_PAGES = 4
VMEM_LIMIT = 56 * 1024 * 1024


def _cp(sem, vmem=None):
    return pltpu.CompilerParams(dimension_semantics=sem, vmem_limit_bytes=vmem or VMEM_LIMIT)


def _sigmoid(x):
    return 1.0 / (1.0 + jnp.exp(-x))


def _softplus(x):
    return jnp.maximum(x, 0.0) + jnp.log1p(jnp.exp(-jnp.abs(x)))


def _dot(a, b, precision=None):
    return jnp.dot(a, b, preferred_element_type=F32, precision=precision)


def _dot_nt(a, b, precision=None):
    return lax.dot_general(a, b, (((1,), (1,)), ((), ())), preferred_element_type=F32,
                           precision=precision)


W_SPLITS = (CONV_DIM, GDN_V_WIDTH, Q_RANK, KV_RANK, MISC_W)


def _front_body(*refs, has_lead):
    if has_lead:
        lead_ref, x_ref, g_ref, w_ref = refs[:4]
        x = jnp.where(pl.program_id(0) == 0, lead_ref[...], x_ref[...])
    else:
        x_ref, g_ref, w_ref = refs[:3]
        x = x_ref[...]
    outs = refs[-len(W_SPLITS):]
    hn = ((x * lax.rsqrt(jnp.mean(x * x, axis=-1, keepdims=True) + EPS)) * g_ref[...]).astype(BF16)
    off = 0
    for width, o_ref in zip(W_SPLITS, outs):
        o_ref[...] = _dot(hn, w_ref[:, off:off + width])
        off += width


def front_proj(x, norm1, w_r, tm, lead=None):
    m, d = x.shape
    n_all = sum(W_SPLITS)
    row = lambda i: (i, 0)
    fixed = lambda i: (0, 0)
    n_blocks = m // tm
    in_specs = [pl.BlockSpec((tm, d), row)]
    args = [x]
    if lead is not None:
        n_blocks += 1
        in_specs = [pl.BlockSpec((tm, d), fixed), pl.BlockSpec((tm, d), lambda i: (jnp.maximum(i - 1, 0), 0))]
        args = [lead, x]
    return pl.pallas_call(
        functools.partial(_front_body, has_lead=lead is not None),
        grid=(n_blocks,),
        in_specs=in_specs + [pl.BlockSpec((1, d), fixed), pl.BlockSpec((d, n_all), fixed)],
        out_specs=[pl.BlockSpec((tm, w), row) for w in W_SPLITS],
        out_shape=[jax.ShapeDtypeStruct((n_blocks * tm, w), F32) for w in W_SPLITS],
        compiler_params=_cp(("arbitrary",)),
        name="front_proj",
    )(*args, norm1.reshape(1, d), w_r)


def _gates(misc, alog, dtb):
    lane = lax.broadcasted_iota(jnp.int32, misc.shape, 1)
    g = -jnp.exp(alog) * _softplus(misc + dtb)
    beta = _sigmoid(misc)
    return jnp.where(lane < GDN_HEADS, g, jnp.where(lane < 2 * GDN_HEADS, beta, 0.0))


def _conv_cols(taps, w_ref, c, q_ref, k_ref, v_ref):
    cols = slice(c * LANES, (c + 1) * LANES)
    acc = taps(0, cols) * w_ref[0:1, cols]
    for j in range(1, CONV_WIDTH):
        acc = acc + taps(j, cols) * w_ref[j:j + 1, cols]
    y = acc * _sigmoid(acc)
    if c < 2 * GDN_HEADS:
        yn = y * lax.rsqrt(jnp.sum(y * y, axis=-1, keepdims=True) + EPS)
        if c < GDN_HEADS:
            q_ref[:, cols] = yn * (GDN_DK ** -0.5)
        else:
            k_ref[:, (c - GDN_HEADS) * LANES:(c - GDN_HEADS + 1) * LANES] = yn
    else:
        v_ref[:, (c - 2 * GDN_HEADS) * LANES:(c - 2 * GDN_HEADS + 1) * LANES] = y


def _conv_prompt_body(x_ref, halo_ref, misc_ref, w_ref, alog_ref, dtb_ref,
                      q_ref, k_ref, v_ref, gb_ref, xs, *, tm, first_valid_row):
    i = pl.program_id(0)
    xs[8:tm + 8, :] = x_ref[...]
    xs[0:8, :] = jnp.where(i > 0, halo_ref[...], 0.0)
    taps = lambda j, cols: xs[8 - (CONV_WIDTH - 1) + j:8 - (CONV_WIDTH - 1) + j + tm, cols]
    for c in range(CONV_DIM // LANES):
        _conv_cols(taps, w_ref, c, q_ref, k_ref, v_ref)
    gb = _gates(misc_ref[...], alog_ref[...], dtb_ref[...])
    row = i * tm + lax.broadcasted_iota(jnp.int32, gb.shape, 0)
    gb_ref[...] = jnp.where(row >= first_valid_row, gb, 0.0)


def conv_prompt(qkv_pre, misc, conv_w, alog, dtb, tm, first_valid_row):
    m = qkv_pre.shape[0]
    row = lambda i: (i, 0)
    fixed = lambda i: (0, 0)
    halo = lambda i: (jnp.maximum(i * (tm // 8) - 1, 0), 0)
    body = functools.partial(_conv_prompt_body, tm=tm, first_valid_row=first_valid_row)
    return pl.pallas_call(
        body,
        grid=(m // tm,),
        in_specs=[pl.BlockSpec((tm, CONV_DIM), row), pl.BlockSpec((8, CONV_DIM), halo),
                  pl.BlockSpec((tm, MISC_W), row), pl.BlockSpec((CONV_WIDTH, CONV_DIM), fixed),
                  pl.BlockSpec((1, MISC_W), fixed), pl.BlockSpec((1, MISC_W), fixed)],
        out_specs=[pl.BlockSpec((tm, GDN_QK_WIDTH), row)] * 3 + [pl.BlockSpec((tm, MISC_W), row)],
        out_shape=[jax.ShapeDtypeStruct((m, GDN_QK_WIDTH), F32)] * 3
        + [jax.ShapeDtypeStruct((m, MISC_W), F32)],
        scratch_shapes=[pltpu.VMEM((tm + 8, CONV_DIM), F32)],
        compiler_params=_cp(("parallel",)),
        name="conv_prompt",
    )(qkv_pre, qkv_pre, misc, conv_w, alog, dtb)


def _conv_sample_body(xp_ref, misc_ref, w_ref, alog_ref, dtb_ref, q_ref, k_ref, v_ref, gb_ref, *, L):
    for l in range(L):
        taps = lambda j, cols, l=l: xp_ref[l + j, :, cols]
        qv, kv, vv = (r.at[l] for r in (q_ref, k_ref, v_ref))
        for c in range(CONV_DIM // LANES):
            _conv_cols(taps, w_ref, c, qv, kv, vv)
        gb_ref[l] = _gates(misc_ref[l], alog_ref[...], dtb_ref[...])


def conv_sample(xp, misc, conv_w, alog, dtb):
    lp, b, _ = xp.shape
    L = lp - (CONV_WIDTH - 1)
    body = functools.partial(_conv_sample_body, L=L)
    return pl.pallas_call(
        body,
        out_shape=[jax.ShapeDtypeStruct((L, b, GDN_QK_WIDTH), F32)] * 3
        + [jax.ShapeDtypeStruct((L, b, MISC_W), F32)],
        compiler_params=pltpu.CompilerParams(vmem_limit_bytes=VMEM_LIMIT),
        name="conv_sample",
    )(xp, misc, conv_w, alog, dtb)


def _split_bf16(x):
    hi = x.astype(BF16)
    return hi, (x - hi.astype(F32)).astype(BF16)


def _stack3(x):
    hi, lo = _split_bf16(x)
    return jnp.concatenate([hi, lo, hi], axis=0), hi


def _cols3(a, order="hhl"):
    hi, lo = _split_bf16(a)
    parts = {"h": hi.astype(F32), "l": lo.astype(F32)}
    return jnp.concatenate([parts[c] for c in order], axis=1).astype(BF16)


def _gdn_prep_body(q_ref, k_ref, v_ref, gb_ref, wq_ref, wql_ref, ut_ref, qk_ref, kdt_ref, gl_ref, *, C, CB):
    row = lax.broadcasted_iota(jnp.int32, (C, C), 0)
    col = lax.broadcasted_iota(jnp.int32, (C, C), 1)
    incl = row >= col
    strict = row > col
    tri = incl.astype(F32)
    n_sq = int(math.log2(C)) - 1
    inst = [(cc, h) for cc in range(CB) for h in range(GDN_HEADS)]
    rows_of = lambda cc: slice(cc * C, (cc + 1) * C)
    cols_of = lambda h: slice(h * LANES, (h + 1) * LANES)
    g_cum, g_cum_t, gbs = [], [], []
    for cc in range(CB):
        gb = gb_ref[rows_of(cc), :]
        gc_all = _dot(tri, gb, HIGHEST)
        gl_ref[cc] = jnp.broadcast_to(jnp.exp(gc_all[C - 1:C, :]), (8, LANES))
        gbs.append(gb)
        g_cum.append(gc_all)
        g_cum_t.append(gc_all.T)
    ks = [k_ref[rows_of(cc), cols_of(h)] for cc, h in inst]
    both = [_dot_nt(_cols3(jnp.concatenate([q_ref[rows_of(cc), cols_of(h)], k], axis=0)), _cols3(k, "hlh"))
            for (cc, h), k in zip(inst, ks)]
    a, r, gcs = [], [], []
    for i, (cc, h) in enumerate(inst):
        gc = g_cum[cc][:, h:h + 1]
        gr = g_cum_t[cc][h:h + 1, :]
        beta = gbs[cc][:, GDN_HEADS + h:GDN_HEADS + h + 1]
        decay = jnp.where(incl, jnp.exp(jnp.where(incl, gc - gr, 0.0)), 0.0)
        a.append(jnp.where(strict, -(beta * both[i][C:2 * C] * decay), 0.0))
        qk = both[i][0:C] * decay
        qk_ref[rows_of(cc), cols_of(h)] = jnp.concatenate(
            [qk, jnp.zeros((C, LANES - C), F32)], axis=1).astype(BF16)
        e_g = jnp.exp(gc)
        qd_hi, qd_lo = _split_bf16(e_g * q_ref[rows_of(cc), cols_of(h)])
        wq_ref[cc, C:2 * C, cols_of(h)] = qd_hi
        wql_ref[cc, C:2 * C, cols_of(h)] = qd_lo
        kd_hi, kd_lo = _split_bf16(jnp.exp(g_cum[cc][C - 1:C, h:h + 1] - gc) * ks[i])
        kd_hi = kd_hi.astype(F32)
        kdt_ref[cc, h] = jnp.concatenate([kd_hi, kd_hi, kd_lo.astype(F32)], axis=0).T.astype(BF16)
        r.append(jnp.concatenate([(beta * e_g) * ks[i], beta * v_ref[rows_of(cc), cols_of(h)]], axis=1))
    for step in range(n_sq + 1):
        a3 = [_cols3(x) for x in a]
        r = [x + _dot(y, _stack3(x)[0]) for x, y in zip(r, a3)]
        if step < n_sq:
            a = [_dot(y, _stack3(x)[0]) for x, y in zip(a, a3)]
    for i, (cc, h) in enumerate(inst):
        wq_ref[cc, 0:C, cols_of(h)], wql_ref[cc, 0:C, cols_of(h)] = _split_bf16(r[i][:, 0:LANES])
        ut_ref[rows_of(cc), cols_of(h)] = r[i][:, LANES:2 * LANES]


def gdn_prep(q, k, v, gb, *, C, CB):
    rows = q.shape[0]
    nc = rows // C
    blk = lambda i: (i, 0)
    body = functools.partial(_gdn_prep_body, C=C, CB=CB)
    return pl.pallas_call(
        body,
        grid=(nc // CB,),
        in_specs=[pl.BlockSpec((CB * C, GDN_QK_WIDTH), blk)] * 3 + [pl.BlockSpec((CB * C, MISC_W), blk)],
        out_specs=[pl.BlockSpec((CB, 2 * C, GDN_QK_WIDTH), lambda i: (i, 0, 0))] * 2
        + [pl.BlockSpec((CB * C, GDN_V_WIDTH), blk),
           pl.BlockSpec((CB * C, GDN_HEADS * LANES), blk),
           pl.BlockSpec((CB, GDN_HEADS, GDN_DK, 3 * C), lambda i: (i, 0, 0, 0)),
           pl.BlockSpec((CB, 8, LANES), lambda i: (i, 0, 0))],
        out_shape=[jax.ShapeDtypeStruct((nc, 2 * C, GDN_QK_WIDTH), BF16)] * 2
        + [jax.ShapeDtypeStruct((rows, GDN_V_WIDTH), F32),
           jax.ShapeDtypeStruct((rows, GDN_HEADS * LANES), BF16),
           jax.ShapeDtypeStruct((nc, GDN_HEADS, GDN_DK, 3 * C), BF16),
           jax.ShapeDtypeStruct((nc, 8, LANES), F32)],
        compiler_params=_cp(("parallel",)),
        name="gdn_prep",
    )(q, k, v, gb)


def _gdn_scan_body(gl_ref, wq_ref, wql_ref, ut_ref, qk_ref, kdt_ref, s0_ref, o_ref, sfin_ref, s_scr,
                   *, C, CB, n_steps):
    sq = pl.program_id(0)
    st = pl.program_id(1)

    @pl.when(st == 0)
    def _():
        s_scr[...] = s0_ref[0]

    for cc in range(CB):
        rows = slice(cc * C, (cc + 1) * C)
        chunk = (sq * n_steps + st) * CB + cc
        heads = range(GDN_HEADS)
        cols = [slice(h * LANES, (h + 1) * LANES) for h in heads]
        ss = [s_scr[h] for h in heads]
        ws = [_dot(jnp.concatenate([wq_ref[cc, :, cols[h]], wq_ref[cc, :, cols[h]], wql_ref[cc, :, cols[h]]],
                                   axis=1), _stack3(ss[h])[0]) for h in heads]
        u3 = [_stack3(ut_ref[rows, cols[h]] - ws[h][0:C]) for h in heads]
        for h in heads:
            s_scr[h] = gl_ref[chunk * GDN_HEADS + h] * ss[h] + _dot(kdt_ref[cc, h], u3[h][0])
        for h in heads:
            o_ref[rows, cols[h]] = ws[h][C:2 * C] + _dot(qk_ref[rows, h * LANES:h * LANES + C], u3[h][1])

    @pl.when(st == n_steps - 1)
    def _():
        sfin_ref[0] = s_scr[...]


def gdn_scan(gl, wq, wql, ut, qk, kdt, s0, *, C, CB, n_seq):
    rows = ut.shape[0]
    nc = rows // C
    n_steps = nc // (n_seq * CB)
    blk = lambda s, t, gl: (s * n_steps + t, 0)
    blk3 = lambda s, t, gl: (s * n_steps + t, 0, 0)
    blk4 = lambda s, t, gl: (s * n_steps + t, 0, 0, 0)
    st4 = lambda s, t, gl: (s, 0, 0, 0)
    body = functools.partial(_gdn_scan_body, C=C, CB=CB, n_steps=n_steps)
    grid_spec = pltpu.PrefetchScalarGridSpec(
        num_scalar_prefetch=1,
        grid=(n_seq, n_steps),
        in_specs=[pl.BlockSpec((CB, 2 * C, GDN_QK_WIDTH), blk3)] * 2 + [pl.BlockSpec((CB * C, GDN_V_WIDTH), blk),
                  pl.BlockSpec((CB * C, GDN_HEADS * LANES), blk),
                  pl.BlockSpec((CB, GDN_HEADS, GDN_DK, 3 * C), blk4),
                  pl.BlockSpec((1, GDN_HEADS, GDN_DK, GDN_DV), st4)],
        out_specs=[pl.BlockSpec((CB * C, GDN_V_WIDTH), blk),
                   pl.BlockSpec((1, GDN_HEADS, GDN_DK, GDN_DV), st4)],
        scratch_shapes=[pltpu.VMEM((GDN_HEADS, GDN_DK, GDN_DV), F32)],
    )
    return pl.pallas_call(
        body,
        grid_spec=grid_spec,
        out_shape=[jax.ShapeDtypeStruct((rows, GDN_V_WIDTH), F32), jax.ShapeDtypeStruct(s0.shape, F32)],
        compiler_params=_cp(("arbitrary", "arbitrary")),
        name="gdn_scan",
    )(gl, wq, wql, ut, qk, kdt, s0)


def gdn(q, k, v, gb, s0, *, C, CB_prep, CB_scan, n_seq):
    wq, wql, ut, qk, kdt, glv = gdn_prep(q, k, v, gb, C=C, CB=CB_prep)
    gl = glv[:, 0, :GDN_HEADS].reshape(-1)
    return gdn_scan(gl, wq, wql, ut, qk, kdt, s0, C=C, CB=CB_scan, n_seq=n_seq)


def _rope(y, cosb, sin_lo, sin_hi):
    return (y * cosb + pltpu.roll(y, LANES - PE_HALF, 1) * sin_lo + pltpu.roll(y, PE_HALF, 1) * sin_hi)


def _slab_norm(x, gain, lane):
    x2 = x * x
    is_nope = lane < NOPE_DIM
    is_pe = (lane >= PE_LO) & (lane < PE_LO + ROPE_DIM)
    ms_n = jnp.sum(jnp.where(is_nope, x2, 0.0), axis=-1, keepdims=True) * (1.0 / NOPE_DIM)
    ms_p = jnp.sum(jnp.where(is_pe, x2, 0.0), axis=-1, keepdims=True) * (1.0 / ROPE_DIM)
    rinv = jnp.where(is_nope, lax.rsqrt(ms_n + EPS), lax.rsqrt(ms_p + EPS))
    return (x * rinv) * gain


def _mla_body(cq_ref, ckvr_ref, misc_ref, cos_ref, slo_ref, shi_ref, qan_ref, wqb_ref, qg_ref,
              kvn_ref, kg_ref, *rest, with_kv):
    if with_kv:
        wuk_ref, wuvt_ref, ckv_ref, kpe_ref, q_ref, k_ref, vt_ref = rest
    else:
        ckv_ref, kpe_ref, q_ref = rest
    cosb, slo, shi = cos_ref[...], slo_ref[...], shi_ref[...]
    lane = lax.broadcasted_iota(jnp.int32, cosb.shape, 1)
    cq = cq_ref[...]
    cqn = ((cq * lax.rsqrt(jnp.mean(cq * cq, axis=-1, keepdims=True) + EPS)) * qan_ref[...]).astype(BF16)
    c = ckvr_ref[...]
    ckv = (c * lax.rsqrt(jnp.mean(c * c, axis=-1, keepdims=True) + EPS)) * kvn_ref[...]
    ckv_ref[...] = ckv
    ckv_b = ckv.astype(BF16)
    kpe_raw = jnp.where((lane >= PE_LO) & (lane < PE_LO + ROPE_DIM), misc_ref[...], 0.0)
    kpe = _rope(_slab_norm(kpe_raw, kg_ref[...], lane), cosb, slo, shi)
    kpe = jnp.where(lane >= PE_LO, kpe, 0.0)
    kpe_ref[...] = kpe
    for h in range(MLA_HEADS):
        cols = slice(h * HEAD_PAD, (h + 1) * HEAD_PAD)
        qh = _dot(cqn, wqb_ref[:, cols])
        qh = _rope(_slab_norm(qh, qg_ref[...], lane), cosb, slo, shi)
        q_ref[h] = (qh * SCORE_SCALE_LOG2).astype(BF16)
        if with_kv:
            kh = _dot(ckv_b, wuk_ref[:, cols])
            k_ref[h] = (_slab_norm(kh, kg_ref[...], lane) + kpe).astype(BF16)
            vt = _dot_nt(wuvt_ref[h], ckv_b)
            ones_row = lax.broadcasted_iota(jnp.int32, vt.shape, 0) == V_DIM
            vt_ref[h] = jnp.where(ones_row, 1.0, vt).astype(BF16)


def mla_prep(cq, ckv_raw, misc, rope_tabs, p, tm, with_kv):
    m = cq.shape[0]
    row = lambda i: (i, 0)
    fixed = lambda i: (0, 0)
    hw = MLA_HEADS * HEAD_PAD
    in_specs = ([pl.BlockSpec((tm, Q_RANK), row), pl.BlockSpec((tm, KV_RANK), row),
                 pl.BlockSpec((tm, MISC_W), row)] + [pl.BlockSpec((tm, LANES), row)] * 3
                + [pl.BlockSpec((1, Q_RANK), fixed), pl.BlockSpec((Q_RANK, hw), fixed),
                   pl.BlockSpec((1, LANES), fixed), pl.BlockSpec((1, KV_RANK), fixed),
                   pl.BlockSpec((1, LANES), fixed)])
    args = [cq, ckv_raw, misc, *rope_tabs, p["q_a_norm"], p["w_q_b"], p["q_gain"], p["kv_a_norm"], p["k_gain"]]
    out_specs = [pl.BlockSpec((tm, KV_RANK), row), pl.BlockSpec((tm, LANES), row),
                 pl.BlockSpec((MLA_HEADS, tm, HEAD_PAD), lambda i: (0, i, 0))]
    out_shape = [jax.ShapeDtypeStruct((m, KV_RANK), F32), jax.ShapeDtypeStruct((m, LANES), F32),
                 jax.ShapeDtypeStruct((MLA_HEADS, m, HEAD_PAD), BF16)]
    if with_kv:
        in_specs += [pl.BlockSpec((KV_RANK, hw), fixed),
                     pl.BlockSpec((MLA_HEADS, HEAD_PAD, KV_RANK), lambda i: (0, 0, 0))]
        args += [p["w_uk"], p["w_uv_t"]]
        out_specs += [pl.BlockSpec((MLA_HEADS, tm, HEAD_PAD), lambda i: (0, i, 0)),
                      pl.BlockSpec((MLA_HEADS, HEAD_PAD, tm), lambda i: (0, 0, i))]
        out_shape += [jax.ShapeDtypeStruct((MLA_HEADS, m, HEAD_PAD), BF16),
                      jax.ShapeDtypeStruct((MLA_HEADS, HEAD_PAD, m), BF16)]
    return pl.pallas_call(
        functools.partial(_mla_body, with_kv=with_kv),
        grid=(m // tm,),
        in_specs=in_specs,
        out_specs=out_specs,
        out_shape=out_shape,
        compiler_params=_cp(("parallel",)),
        name="mla_prep",
    )(*args)


def _flash_body(qmap_ref, kmap_ref, q_ref, k_ref, vt_ref, o_ref, m_scr, acc_scr,
                *, tq, tk, q_block0, first_key):
    step = pl.program_id(0)
    q0 = (qmap_ref[step] + q_block0) * tq
    k0 = kmap_ref[step] * tk
    first_block = k0 == (first_key // tk) * tk

    @pl.when(first_block)
    def _():
        m_scr[...] = jnp.full(m_scr.shape, NEG_INF, F32)
        acc_scr[...] = jnp.zeros(acc_scr.shape, F32)

    def update(masked):
        if masked:
            kpos = k0 + lax.broadcasted_iota(jnp.int32, (tk, tq), 0)
            qpos = q0 + lax.broadcasted_iota(jnp.int32, (tk, tq), 1)
            valid = (kpos <= qpos) & (kpos >= first_key)
        pending = [_dot_nt(k_ref[h], q_ref[h]) for h in range(FLASH_AHEAD)]
        for h in range(MLA_HEADS):
            st = pending.pop(0)
            if h + FLASH_AHEAD < MLA_HEADS:
                pending.append(_dot_nt(k_ref[h + FLASH_AHEAD], q_ref[h + FLASH_AHEAD]))
            if masked:
                st = jnp.where(valid, st, NEG_INF)
            m_prev = m_scr[h]
            m_new = jnp.maximum(m_prev, jnp.max(st, axis=0, keepdims=True))
            e = jnp.exp2(st - m_new).astype(BF16)
            acc_scr[h] = jnp.exp2(m_prev - m_new) * acc_scr[h] + _dot(vt_ref[h], e)
            m_scr[h] = m_new

    needs_mask = first_block | (k0 + tk - 1 > q0)
    pl.when(needs_mask)(lambda: update(True))
    pl.when(jnp.logical_not(needs_mask))(lambda: update(False))

    @pl.when(k0 + tk >= q0 + tq)
    def _():
        for h in range(MLA_HEADS):
            a = acc_scr[h]
            o = a * (1.0 / a[V_DIM:V_DIM + 1, :])
            o_ref[:, h * HEAD_PAD:(h + 1) * HEAD_PAD] = o.T.astype(BF16)


def flash_prompt(q, k, vt, *, tq, tk, first_key):
    tp = q.shape[1]
    q_block0 = P0 // tq
    nq = (tp - P0) // tq
    qs, ks = [], []
    for qi in range(nq):
        last_k = ((qi + q_block0 + 1) * tq - 1) // tk
        for ki in range(first_key // tk, last_k + 1):
            qs.append(qi)
            ks.append(ki)
    qmap = jnp.asarray(qs, jnp.int32)
    kmap = jnp.asarray(ks, jnp.int32)
    body = functools.partial(_flash_body, tq=tq, tk=tk, q_block0=q_block0, first_key=first_key)
    grid_spec = pltpu.PrefetchScalarGridSpec(
        num_scalar_prefetch=2,
        grid=(len(qs),),
        in_specs=[pl.BlockSpec((MLA_HEADS, tq, HEAD_PAD), lambda s, qm, km: (0, qm[s] + q_block0, 0)),
                  pl.BlockSpec((MLA_HEADS, tk, HEAD_PAD), lambda s, qm, km: (0, km[s], 0)),
                  pl.BlockSpec((MLA_HEADS, HEAD_PAD, tk), lambda s, qm, km: (0, 0, km[s]))],
        out_specs=pl.BlockSpec((tq, MLA_HEADS * HEAD_PAD), lambda s, qm, km: (qm[s], 0)),
        scratch_shapes=[pltpu.VMEM((MLA_HEADS, 1, tq), F32), pltpu.VMEM((MLA_HEADS, HEAD_PAD, tq), F32)],
    )
    return pl.pallas_call(
        body,
        grid_spec=grid_spec,
        out_shape=jax.ShapeDtypeStruct((tp - P0, MLA_HEADS * HEAD_PAD), BF16),
        compiler_params=_cp(("arbitrary",)),
        name="flash_prompt",
    )(qmap, kmap, q, k, vt)


def _qabs_body(q_ref, w_ref, o_ref):
    o_ref[0] = _dot(q_ref[0], w_ref[0]).astype(BF16)


def q_absorb(q, w_abs):
    h, m, _ = q.shape
    blk = lambda i: (i, 0, 0)
    return pl.pallas_call(
        _qabs_body,
        grid=(h,),
        in_specs=[pl.BlockSpec((1, m, HEAD_PAD), blk), pl.BlockSpec((1, HEAD_PAD, KV_RANK), blk)],
        out_specs=pl.BlockSpec((1, m, KV_RANK), blk),
        out_shape=jax.ShapeDtypeStruct((h, m, KV_RANK), BF16),
        compiler_params=_cp(("parallel",)),
        name="q_absorb",
    )(q, w_abs)


def _decode_scores(waug, qpe, ckv_b, kpe_t, n_lq):
    kp = _dot_nt(waug, ckv_b)
    n_k = MLA_HEADS * NOPE_DIM
    kproj = kp[:n_k]
    ms = jnp.sum((kproj * kproj).reshape(MLA_HEADS, NOPE_DIM, kproj.shape[-1]), axis=1) * (1.0 / NOPE_DIM)
    rinv = lax.rsqrt(ms + EPS)
    s_nope = kp[n_k:]
    s_pe = _dot(qpe, kpe_t)
    rinv_rows = jnp.concatenate([rinv] * n_lq, axis=0)
    return s_nope * rinv_rows + s_pe


def _page_copies(pt_ref, ckv_hbm, kpet_hbm, ckv_buf, kpe_buf, sem, step, slot, *, PG, page):
    out = []
    for i in range(PG):
        pid = pt_ref[step * PG + i]
        out.append(pltpu.make_async_copy(ckv_hbm.at[0, pid], ckv_buf.at[slot, pl.ds(i * page, page)],
                                         sem.at[0, slot]))
        out.append(pltpu.make_async_copy(kpet_hbm.at[0, pid], kpe_buf.at[slot, :, pl.ds(i * page, page)],
                                         sem.at[1, slot]))
    return out


def _decode_body(pt_ref, wukt_ref, qabs_ref, qpe_ref, nckv_ref, nkpet_ref, ckv_hbm, kpet_hbm, o_ref,
                 ckv_buf, kpe_buf, sem, waug, ckvb, m_scr, l_scr, acc_scr, *, PG, L, n_groups, page):
    s = pl.program_id(0)
    n_steps = pl.num_programs(0)
    j = lax.rem(s, n_groups)
    slot = lax.rem(s, 2)
    n_k = MLA_HEADS * NOPE_DIM
    R = MLA_HEADS * L
    copies = functools.partial(_page_copies, pt_ref, ckv_hbm, kpet_hbm, ckv_buf, kpe_buf, sem, PG=PG, page=page)

    @pl.when(s == 0)
    def _():
        for c in copies(0, 0):
            c.start()

    @pl.when(s + 1 < n_steps)
    def _():
        for c in copies(s + 1, 1 - slot):
            c.start()

    @pl.when(j == 0)
    def _():
        waug[0:n_k, :] = wukt_ref[...]
        waug[n_k:n_k + R, :] = qabs_ref[0]
        c_new = nckv_ref[0].astype(BF16)
        sc = _decode_scores(waug[...], qpe_ref[0], c_new, nkpet_ref[0].astype(BF16), L)
        r_l = lax.shift_right_logical(lax.broadcasted_iota(jnp.int32, sc.shape, 0),
                                      int(math.log2(MLA_HEADS)))
        t = lax.broadcasted_iota(jnp.int32, sc.shape, 1)
        sc = jnp.where((t <= r_l) & (t < L), sc, NEG_INF)
        m0 = jnp.max(sc, axis=-1, keepdims=True)
        e0 = jnp.exp2(sc - m0)
        m_scr[...] = m0
        l_scr[...] = jnp.sum(e0, axis=-1, keepdims=True)
        acc_scr[...] = _dot(e0.astype(BF16), c_new)

    for c in copies(s, slot):
        c.wait()

    w = waug[...]
    qpe = qpe_ref[0]

    def scores(p):
        tok = pl.ds(p * DECODE_SCORE_PAGES * page, DECODE_SCORE_PAGES * page)
        c2 = ckv_buf[slot, tok, :].astype(BF16)
        ckvb[p] = c2
        return _decode_scores(w, qpe, c2, kpe_buf[slot, :, tok].astype(BF16), L)

    ms, ls, accs = [m_scr[...]], [l_scr[...]], [acc_scr[...]]
    pending = [scores(0)]
    for p in range(PG // DECODE_SCORE_PAGES):
        sc = pending.pop(0)
        if p + 1 < PG // DECODE_SCORE_PAGES:
            pending.append(scores(p + 1))
        m_p = jnp.max(sc, axis=-1, keepdims=True)
        e = jnp.exp2(sc - m_p)
        ms.append(m_p)
        ls.append(jnp.sum(e, axis=-1, keepdims=True))
        accs.append(_dot(e.astype(BF16), ckvb[p]))
    m = functools.reduce(jnp.maximum, ms)
    fs = [jnp.exp2(x - m) for x in ms]
    l = functools.reduce(lambda x, y: x + y, [f * x for f, x in zip(fs, ls)])
    acc = functools.reduce(lambda x, y: x + y, [f * x for f, x in zip(fs, accs)])
    m_scr[...], l_scr[...], acc_scr[...] = m, l, acc

    @pl.when(j == n_groups - 1)
    def _():
        o_ref[0] = acc / l


def decode_attn(page_table, wuk_t, qabs, qpe, new_ckv, new_kpe_t, cache_ckv, cache_kpe_t, *, PG, L):
    b, n_pages = page_table.shape
    page = cache_ckv.shape[2]
    R = MLA_HEADS * L
    n_k = MLA_HEADS * NOPE_DIM
    n_groups = n_pages // PG
    per_b = lambda s, pt: (s // n_groups, 0, 0)
    fixed = lambda s, pt: (0, 0)
    body = functools.partial(_decode_body, PG=PG, L=L, n_groups=n_groups, page=page)
    grid_spec = pltpu.PrefetchScalarGridSpec(
        num_scalar_prefetch=1,
        grid=(b * n_groups,),
        in_specs=[pl.BlockSpec((n_k, KV_RANK), fixed), pl.BlockSpec((1, R, KV_RANK), per_b),
                  pl.BlockSpec((1, R, ROPE_DIM), per_b), pl.BlockSpec((1, 8, KV_RANK), per_b),
                  pl.BlockSpec((1, ROPE_DIM, 8), per_b),
                  pl.BlockSpec(memory_space=pl.ANY), pl.BlockSpec(memory_space=pl.ANY)],
        out_specs=pl.BlockSpec((1, R, KV_RANK), per_b),
        scratch_shapes=[pltpu.VMEM((2, PG * page, KV_RANK), F32), pltpu.VMEM((2, ROPE_DIM, PG * page), F32),
                        pltpu.SemaphoreType.DMA((2, 2)),
                        pltpu.VMEM((n_k + R, KV_RANK), BF16), pltpu.VMEM((PG // DECODE_SCORE_PAGES, DECODE_SCORE_PAGES * page, KV_RANK), BF16),
                        pltpu.VMEM((R, 1), F32), pltpu.VMEM((R, 1), F32), pltpu.VMEM((R, KV_RANK), F32)],
    )
    return pl.pallas_call(
        body,
        grid_spec=grid_spec,
        out_shape=jax.ShapeDtypeStruct((b, R, KV_RANK), F32),
        compiler_params=_cp(("arbitrary",)),
        name="decode_attn",
    )(page_table.reshape(-1), wuk_t, qabs, qpe, new_ckv, new_kpe_t, cache_ckv, cache_kpe_t)


def _latent_out_body(x_ref, w_ref, o_ref):
    o_ref[...] = _dot(x_ref[0].astype(BF16), w_ref[0]).astype(BF16)


def latent_out(o_lat, w_uv_h):
    h, m, _ = o_lat.shape
    return pl.pallas_call(
        _latent_out_body,
        grid=(h,),
        in_specs=[pl.BlockSpec((1, m, KV_RANK), lambda i: (i, 0, 0)),
                  pl.BlockSpec((1, KV_RANK, HEAD_PAD), lambda i: (i, 0, 0))],
        out_specs=pl.BlockSpec((m, HEAD_PAD), lambda i: (0, i)),
        out_shape=jax.ShapeDtypeStruct((m, h * HEAD_PAD), BF16),
        compiler_params=_cp(("parallel",)),
        name="latent_out",
    )(o_lat, w_uv_h)


def _mix_body(h_ref, o_ref, z_ref, mla_ref, gg_ref, wog_ref, wom_ref, n2_ref, h1_ref, hn_ref):
    acc = h_ref[...] + _dot(mla_ref[...], wom_ref[...])
    for h in range(GDN_HEADS):
        cols = slice(h * LANES, (h + 1) * LANES)
        o = o_ref[:, cols]
        z = z_ref[:, cols]
        y = ((o * lax.rsqrt(jnp.mean(o * o, axis=-1, keepdims=True) + EPS)) * gg_ref[...]) * (z * _sigmoid(z))
        acc = acc + _dot(y.astype(BF16), wog_ref[cols, :])
    h1_ref[...] = acc
    hn_ref[...] = ((acc * lax.rsqrt(jnp.mean(acc * acc, axis=-1, keepdims=True) + EPS)) * n2_ref[...]).astype(BF16)


def mix_out(h, gdn_o, z, mla_o, p, tm, row_block0):
    m, d = h.shape
    row = lambda i: (i, 0)
    prow = lambda i: (i + row_block0, 0)
    fixed = lambda i: (0, 0)
    hw = MLA_HEADS * HEAD_PAD
    return pl.pallas_call(
        _mix_body,
        grid=(m // tm,),
        in_specs=[pl.BlockSpec((tm, d), row), pl.BlockSpec((tm, GDN_V_WIDTH), prow),
                  pl.BlockSpec((tm, GDN_V_WIDTH), prow), pl.BlockSpec((tm, hw), row),
                  pl.BlockSpec((1, LANES), fixed), pl.BlockSpec((GDN_V_WIDTH, d), fixed),
                  pl.BlockSpec((hw, d), fixed), pl.BlockSpec((1, d), fixed)],
        out_specs=[pl.BlockSpec((tm, d), row), pl.BlockSpec((tm, d), row)],
        out_shape=[jax.ShapeDtypeStruct((m, d), F32), jax.ShapeDtypeStruct((m, d), BF16)],
        compiler_params=_cp(("parallel",)),
        name="mix_out",
    )(h, gdn_o, z, mla_o, p["gdn_norm"], p["w_o_gdn"], p["w_o_mla"], p["norm2"])


def _ffn_body(h1_ref, hn_ref, wg_ref, wu_ref, wd_ref, y_ref):
    f = pl.program_id(1)

    @pl.when(f == 0)
    def _():
        y_ref[...] = h1_ref[...]

    x = hn_ref[...]
    g = _dot(x, wg_ref[...])
    u = _dot(x, wu_ref[...])
    y_ref[...] += _dot(((g * _sigmoid(g)) * u).astype(BF16), wd_ref[...])


def ffn(h1, hn, wg, wu, wd, tm, tf):
    m, d = h1.shape
    dff = wg.shape[1]
    row = lambda i, f: (i, 0)
    return pl.pallas_call(
        _ffn_body,
        grid=(m // tm, dff // tf),
        in_specs=[pl.BlockSpec((tm, d), row), pl.BlockSpec((tm, d), row),
                  pl.BlockSpec((d, tf), lambda i, f: (0, f)), pl.BlockSpec((d, tf), lambda i, f: (0, f)),
                  pl.BlockSpec((tf, d), lambda i, f: (f, 0))],
        out_specs=pl.BlockSpec((tm, d), row),
        out_shape=jax.ShapeDtypeStruct((m, d), F32),
        compiler_params=_cp(("parallel", "arbitrary")),
        name="ffn",
    )(h1, hn, wg, wu, wd)


def _head_slabs(w, width):
    k, h, _ = w.shape
    return jnp.pad(w, ((0, 0), (0, 0), (0, HEAD_PAD - width))).reshape(k, h * HEAD_PAD)


def _pad_lanes(v, lo=0):
    return jnp.pad(v.astype(F32), (lo, LANES - lo - v.shape[0])).reshape(1, LANES)


def _prep_params(norm1, w_in, conv_w, a_log, dt_bias, gdn_norm, q_a_norm, w_q_b, q_nope_norm,
                 q_pe_norm, kv_a_norm, k_pe_norm, w_uk, w_uv, k_nope_norm, w_o, norm2,
                 w_gate, w_up, w_down):
    d = w_in.shape[0]
    o = 0
    cuts = {}
    for name, wdt in (("qkv", CONV_DIM), ("z", GDN_V_WIDTH), ("a", GDN_HEADS), ("b", GDN_HEADS),
                      ("cq", Q_RANK), ("ckv", KV_RANK), ("kpe", ROPE_DIM)):
        cuts[name] = w_in[:, o:o + wdt]
        o += wdt
    zeros = lambda n: jnp.zeros((d, n), w_in.dtype)
    misc = jnp.concatenate([cuts["a"], cuts["b"], zeros(PE_LO - 2 * GDN_HEADS), cuts["kpe"],
                            zeros(MISC_W - PE_LO - ROPE_DIM)], axis=1)
    w_r = jnp.concatenate([cuts["qkv"], cuts["z"], cuts["cq"], cuts["ckv"], misc], axis=1).astype(BF16)
    wqb = w_q_b.reshape(Q_RANK, MLA_HEADS, QK_DIM)
    gain_q = jnp.concatenate([q_nope_norm, q_pe_norm]).astype(F32)
    gain_k = jnp.concatenate([k_nope_norm, k_pe_norm]).astype(F32)
    w_uk_hdr = jnp.transpose(w_uk, (1, 2, 0))
    w_abs = jnp.pad(w_uk_hdr * k_nope_norm[None, :, None], ((0, 0), (0, HEAD_PAD - NOPE_DIM), (0, 0)))
    w_o_mla = jnp.pad(w_o[GDN_V_WIDTH:].reshape(MLA_HEADS, V_DIM, d),
                      ((0, 0), (0, HEAD_PAD - V_DIM), (0, 0))).reshape(MLA_HEADS * HEAD_PAD, d)
    return dict(
        norm1=norm1, w_in=w_r, conv_w=conv_w.astype(F32),
        a_log=_pad_lanes(a_log), dt_bias=_pad_lanes(dt_bias),
        gdn_norm=gdn_norm.reshape(1, GDN_DV).astype(F32),
        q_a_norm=q_a_norm.reshape(1, Q_RANK).astype(F32), w_q_b=_head_slabs(wqb, QK_DIM).astype(BF16),
        q_gain=_pad_lanes(gain_q), kv_a_norm=kv_a_norm.reshape(1, KV_RANK).astype(F32),
        w_uk=_head_slabs(w_uk, NOPE_DIM).astype(BF16), k_gain=_pad_lanes(gain_k),
        w_uv_t=jnp.pad(jnp.transpose(w_uv, (1, 2, 0)), ((0, 0), (0, HEAD_PAD - V_DIM), (0, 0))).astype(BF16),
        w_uk_t=w_uk_hdr.reshape(MLA_HEADS * NOPE_DIM, KV_RANK).astype(BF16),
        w_abs=w_abs.astype(BF16),
        w_uv_h=jnp.pad(jnp.transpose(w_uv, (1, 0, 2)), ((0, 0), (0, 0), (0, HEAD_PAD - V_DIM))).astype(BF16),
        w_o_gdn=w_o[:GDN_V_WIDTH].astype(BF16), w_o_mla=w_o_mla.astype(BF16),
        norm2=norm2.reshape(1, d).astype(F32),
        w_gate=w_gate.astype(BF16), w_up=w_up.astype(BF16), w_down=w_down.astype(BF16),
    )


def _rope_tables(pos):
    inv_freq = np.power(ROPE_THETA, -np.arange(0, ROPE_DIM, 2, dtype=np.float64) / ROPE_DIM)
    ang = np.asarray(pos, np.float64)[:, None] * inv_freq[None, :]
    cos, sin = jnp.asarray(np.cos(ang), F32), jnp.asarray(np.sin(ang), F32)
    n = len(pos)
    ones = jnp.ones((n, PE_LO), F32)
    z = lambda w: jnp.zeros((n, w), F32)
    tail = LANES - PE_LO - ROPE_DIM
    cosb = jnp.concatenate([ones, cos, cos, z(tail)], axis=1)
    sin_lo = jnp.concatenate([z(PE_LO), -sin, z(PE_HALF + tail)], axis=1)
    sin_hi = jnp.concatenate([z(PE_LO + PE_HALF), sin, z(tail)], axis=1)
    return cosb, sin_lo, sin_hi


def _ff_tile(dff):
    for t in (1408, 1024, 512, 256, 128):
        if dff % t == 0:
            return t
    return dff


def _prompt_group(x_prompt, meta_tokens, p):
    seq, d = x_prompt.shape
    tp = P0 + seq
    meta_row0 = P0 - N_META
    lead = jnp.concatenate([jnp.zeros((meta_row0, d), F32), meta_tokens.astype(F32)], axis=0)
    qkv_pre, z, cq, ckv_raw, misc = front_proj(x_prompt, p["norm1"], p["w_in"], tm=P0, lead=lead)
    q, k, v, gb = conv_prompt(qkv_pre, misc, p["conv_w"], p["a_log"], p["dt_bias"], tm=512,
                              first_valid_row=meta_row0)
    s0 = jnp.zeros((1, GDN_HEADS, GDN_DK, GDN_DV), F32)
    gdn_o, s_fin = gdn(q, k, v, gb, s0, C=CHUNK, CB_prep=4, CB_scan=8, n_seq=1)
    tabs = _rope_tables(np.arange(tp) - meta_row0)
    ckv, kpe, qh, kh, vt = mla_prep(cq, ckv_raw, misc, tabs, p, tm=256, with_kv=True)
    mla_o = flash_prompt(qh, kh, vt, tq=512, tk=512, first_key=meta_row0)
    h1, hn2 = mix_out(x_prompt, gdn_o, z, mla_o, p, tm=512, row_block0=P0 // 512)
    y = ffn(h1, hn2, p["w_gate"], p["w_up"], p["w_down"], tm=512, tf=_ff_tile(p["w_gate"].shape[1]))
    new_conv = qkv_pre[tp - (CONV_WIDTH - 1):]
    return (y, new_conv, s_fin[0], ckv[meta_row0:], kpe[meta_row0:, PE_LO:PE_LO + ROPE_DIM])


def _sample_group(x_sample, cache_conv, state_gdn, cache_ckv, cache_kpe, page_table, p):
    b, L, d = x_sample.shape
    m = b * L
    page = cache_ckv.shape[2]
    past_len = page_table.shape[1] * page
    x = x_sample.reshape(m, d)
    tm = min(m, 256)
    qkv_pre, z, cq, ckv_raw, misc = front_proj(x, p["norm1"], p["w_in"], tm=tm)
    xp = jnp.concatenate([jnp.transpose(cache_conv.astype(F32), (1, 0, 2)),
                          jnp.transpose(qkv_pre.reshape(b, L, CONV_DIM), (1, 0, 2))], axis=0)
    misc_t = jnp.transpose(misc.reshape(b, L, MISC_W), (1, 0, 2))
    q, k, v, gb = conv_sample(xp, misc_t, p["conv_w"], p["a_log"], p["dt_bias"])
    C = 16
    to_chunks = lambda t: jnp.pad(jnp.transpose(t, (1, 0, 2)), ((0, 0), (0, C - L), (0, 0))).reshape(b * C, -1)
    gdn_o, s_fin = gdn(to_chunks(q), to_chunks(k), to_chunks(v), to_chunks(gb), state_gdn.astype(F32),
                       C=C, CB_prep=4 if b % 4 == 0 else 1, CB_scan=1, n_seq=b)
    gdn_o = gdn_o.reshape(b, C, GDN_V_WIDTH)[:, :L].reshape(m, GDN_V_WIDTH)
    tabs = _rope_tables(np.tile(past_len + np.arange(L), b))
    ckv, kpe, qh = mla_prep(cq, ckv_raw, misc, tabs, p, tm=tm, with_kv=False)
    kpe32 = kpe[:, PE_LO:PE_LO + ROPE_DIM]
    qabs = q_absorb(qh, p["w_abs"])
    R = L * MLA_HEADS
    rows_lh = lambda t: jnp.transpose(t.reshape(MLA_HEADS, b, L, -1), (1, 2, 0, 3)).reshape(b, R, -1)
    qpe = rows_lh(qh[:, :, PE_LO:PE_LO + ROPE_DIM])
    new_ckv = jnp.pad(ckv.reshape(b, L, KV_RANK), ((0, 0), (0, 8 - L), (0, 0)))
    new_kpe_t = jnp.pad(jnp.swapaxes(kpe32.reshape(b, L, ROPE_DIM), 1, 2), ((0, 0), (0, 0), (0, 8 - L)))
    n_pages = page_table.shape[1]
    PG = 32 if n_pages % 32 == 0 else DECODE_SCORE_PAGES
    o_lat = decode_attn(page_table, p["w_uk_t"], rows_lh(qabs), qpe, new_ckv, new_kpe_t,
                        cache_ckv, jnp.swapaxes(cache_kpe, 2, 3), PG=PG, L=L)
    o_lat_h = jnp.transpose(o_lat.reshape(b, L, MLA_HEADS, KV_RANK), (2, 0, 1, 3)).reshape(MLA_HEADS, m, KV_RANK)
    mla_o = latent_out(o_lat_h, p["w_uv_h"])
    h1, hn2 = mix_out(x, gdn_o, z, mla_o, p, tm=tm, row_block0=0)
    y = ffn(h1, hn2, p["w_gate"], p["w_up"], p["w_down"], tm=tm, tf=_ff_tile(p["w_gate"].shape[1]))
    new_conv = qkv_pre.reshape(b, L, CONV_DIM)[:, L - (CONV_WIDTH - 1):]
    return (y.reshape(b, L, d), new_conv, s_fin, ckv.reshape(b, L, KV_RANK), kpe32.reshape(b, L, ROPE_DIM))


def kernel(x_prompt, x_sample, cache_conv, state_gdn, cache_ckv, cache_kpe, page_table, meta_tokens, norm1, w_in, conv_w, a_log, dt_bias, gdn_norm, q_a_norm, w_q_b, q_nope_norm, q_pe_norm, kv_a_norm, k_pe_norm, w_uk, w_uv, k_nope_norm, w_o, norm2, w_gate, w_up, w_down):
    assert x_prompt.shape[0] == 1 and norm1.shape[0] == 1, "one prompt sequence, one layer"
    p = _prep_params(norm1[0], w_in[0], conv_w[0], a_log[0], dt_bias[0], gdn_norm[0], q_a_norm[0],
                     w_q_b[0], q_nope_norm[0], q_pe_norm[0], kv_a_norm[0], k_pe_norm[0], w_uk[0],
                     w_uv[0], k_nope_norm[0], w_o[0], norm2[0], w_gate[0], w_up[0], w_down[0])
    y_p, conv_p, st_p, ckv_p, kpe_p = _prompt_group(x_prompt[0], meta_tokens, p)
    y_s, conv_s, st_s, ckv_s, kpe_s = _sample_group(x_sample, cache_conv[0], state_gdn[0], cache_ckv,
                                                     cache_kpe, page_table, p)
    lead = lambda t: t[None, None]
    return (y_p[None], y_s, lead(conv_p), lead(st_p), lead(ckv_p), lead(kpe_p),
            conv_s[None], st_s[None], ckv_s[None], kpe_s[None])
```

```python
import functools
import math

import jax
import jax.numpy as jnp
import numpy as np
from jax import lax
from jax.experimental import pallas as pl
from jax.experimental.pallas import tpu as pltpu

F32 = jnp.float32
BF16 = jnp.bfloat16
HIGHEST = lax.Precision.HIGHEST

N_META = 16
EPS = 1e-6
NEG_INF = -1e30
GDN_HEADS = 4
GDN_DK = 128
GDN_DV = 128
GDN_QK_WIDTH = GDN_HEADS * GDN_DK
GDN_V_WIDTH = GDN_HEADS * GDN_DV
CONV_WIDTH = 4
CONV_DIM = 2 * GDN_QK_WIDTH + GDN_V_WIDTH
CHUNK = 64
MLA_HEADS = 8
Q_RANK = 384
KV_RANK = 256
NOPE_DIM = 64
ROPE_DIM = 32
V_DIM = 64
QK_DIM = NOPE_DIM + ROPE_DIM
ROPE_THETA = 10000.0
SCORE_SCALE_LOG2 = QK_DIM ** -0.5 * math.log2(math.e)

LANES = 128
HEAD_PAD = LANES
PE_LO = NOPE_DIM
PE_HALF = ROPE_DIM // 2
MISC_W = LANES
P0 = 512
FLASH_AHEAD = 2
VT_ROWS = V_DIM + 16
DECODE_SCORE_PAGES = 4
VMEM_LIMIT = 56 * 1024 * 1024


def _cp(sem, vmem=None):
    return pltpu.CompilerParams(dimension_semantics=sem, vmem_limit_bytes=vmem or VMEM_LIMIT)


def _sigmoid(x):
    return 1.0 / (1.0 + jnp.exp(-x))


def _softplus(x):
    return jnp.maximum(x, 0.0) + jnp.log1p(jnp.exp(-jnp.abs(x)))


def _dot(a, b, precision=None):
    return jnp.dot(a, b, preferred_element_type=F32, precision=precision)


def _dot_nt(a, b, precision=None):
    return lax.dot_general(a, b, (((1,), (1,)), ((), ())), preferred_element_type=F32,
                           precision=precision)


W_SPLITS = (CONV_DIM, GDN_V_WIDTH, Q_RANK, KV_RANK, MISC_W)


def _front_body(*refs, has_lead):
    if has_lead:
        lead_ref, x_ref, g_ref, w_ref = refs[:4]
        x = jnp.where(pl.program_id(0) == 0, lead_ref[...], x_ref[...])
    else:
        x_ref, g_ref, w_ref = refs[:3]
        x = x_ref[...]
    outs = refs[-len(W_SPLITS):]
    hn = ((x * lax.rsqrt(jnp.mean(x * x, axis=-1, keepdims=True) + EPS)) * g_ref[...]).astype(BF16)
    off = 0
    for width, o_ref in zip(W_SPLITS, outs):
        o_ref[...] = _dot(hn, w_ref[:, off:off + width])
        off += width


def front_proj(x, norm1, w_r, tm, lead=None):
    m, d = x.shape
    n_all = sum(W_SPLITS)
    row = lambda i: (i, 0)
    fixed = lambda i: (0, 0)
    n_blocks = m // tm
    in_specs = [pl.BlockSpec((tm, d), row)]
    args = [x]
    if lead is not None:
        n_blocks += 1
        in_specs = [pl.BlockSpec((tm, d), fixed), pl.BlockSpec((tm, d), lambda i: (jnp.maximum(i - 1, 0), 0))]
        args = [lead, x]
    return pl.pallas_call(
        functools.partial(_front_body, has_lead=lead is not None),
        grid=(n_blocks,),
        in_specs=in_specs + [pl.BlockSpec((1, d), fixed), pl.BlockSpec((d, n_all), fixed)],
        out_specs=[pl.BlockSpec((tm, w), row) for w in W_SPLITS],
        out_shape=[jax.ShapeDtypeStruct((n_blocks * tm, w), F32) for w in W_SPLITS],
        compiler_params=_cp(("arbitrary",)),
        name="front_proj",
    )(*args, norm1.reshape(1, d), w_r)


def _gates(misc, alog, dtb):
    lane = lax.broadcasted_iota(jnp.int32, misc.shape, 1)
    g = -jnp.exp(alog) * _softplus(misc + dtb)
    beta = _sigmoid(misc)
    return jnp.where(lane < GDN_HEADS, g, jnp.where(lane < 2 * GDN_HEADS, beta, 0.0))


def _conv_cols(taps, w_ref, c, q_ref, k_ref, v_ref):
    cols = slice(c * LANES, (c + 1) * LANES)
    acc = taps(0, cols) * w_ref[0:1, cols]
    for j in range(1, CONV_WIDTH):
        acc = acc + taps(j, cols) * w_ref[j:j + 1, cols]
    y = acc * _sigmoid(acc)
    if c < 2 * GDN_HEADS:
        yn = y * lax.rsqrt(jnp.sum(y * y, axis=-1, keepdims=True) + EPS)
        if c < GDN_HEADS:
            q_ref[:, cols] = yn * (GDN_DK ** -0.5)
        else:
            k_ref[:, (c - GDN_HEADS) * LANES:(c - GDN_HEADS + 1) * LANES] = yn
    else:
        v_ref[:, (c - 2 * GDN_HEADS) * LANES:(c - 2 * GDN_HEADS + 1) * LANES] = y


def _conv_prompt_body(x_ref, halo_ref, misc_ref, w_ref, alog_ref, dtb_ref,
                      q_ref, k_ref, v_ref, gb_ref, xs, *, tm, first_valid_row):
    i = pl.program_id(0)
    xs[8:tm + 8, :] = x_ref[...]
    xs[0:8, :] = jnp.where(i > 0, halo_ref[...], 0.0)
    taps = lambda j, cols: xs[8 - (CONV_WIDTH - 1) + j:8 - (CONV_WIDTH - 1) + j + tm, cols]
    for c in range(CONV_DIM // LANES):
        _conv_cols(taps, w_ref, c, q_ref, k_ref, v_ref)
    gb = _gates(misc_ref[...], alog_ref[...], dtb_ref[...])
    row = i * tm + lax.broadcasted_iota(jnp.int32, gb.shape, 0)
    gb_ref[...] = jnp.where(row >= first_valid_row, gb, 0.0)


def conv_prompt(qkv_pre, misc, conv_w, alog, dtb, tm, first_valid_row):
    m = qkv_pre.shape[0]
    row = lambda i: (i, 0)
    fixed = lambda i: (0, 0)
    halo = lambda i: (jnp.maximum(i * (tm // 8) - 1, 0), 0)
    body = functools.partial(_conv_prompt_body, tm=tm, first_valid_row=first_valid_row)
    return pl.pallas_call(
        body,
        grid=(m // tm,),
        in_specs=[pl.BlockSpec((tm, CONV_DIM), row), pl.BlockSpec((8, CONV_DIM), halo),
                  pl.BlockSpec((tm, MISC_W), row), pl.BlockSpec((CONV_WIDTH, CONV_DIM), fixed),
                  pl.BlockSpec((1, MISC_W), fixed), pl.BlockSpec((1, MISC_W), fixed)],
        out_specs=[pl.BlockSpec((tm, GDN_QK_WIDTH), row)] * 3 + [pl.BlockSpec((tm, MISC_W), row)],
        out_shape=[jax.ShapeDtypeStruct((m, GDN_QK_WIDTH), F32)] * 3
        + [jax.ShapeDtypeStruct((m, MISC_W), F32)],
        scratch_shapes=[pltpu.VMEM((tm + 8, CONV_DIM), F32)],
        compiler_params=_cp(("parallel",)),
        name="conv_prompt",
    )(qkv_pre, qkv_pre, misc, conv_w, alog, dtb)


def _conv_sample_body(xp_ref, misc_ref, w_ref, alog_ref, dtb_ref, q_ref, k_ref, v_ref, gb_ref, *, L):
    for l in range(L):
        taps = lambda j, cols, l=l: xp_ref[l + j, :, cols]
        qv, kv, vv = (r.at[l] for r in (q_ref, k_ref, v_ref))
        for c in range(CONV_DIM // LANES):
            _conv_cols(taps, w_ref, c, qv, kv, vv)
        gb_ref[l] = _gates(misc_ref[l], alog_ref[...], dtb_ref[...])


def conv_sample(xp, misc, conv_w, alog, dtb):
    lp, b, _ = xp.shape
    L = lp - (CONV_WIDTH - 1)
    body = functools.partial(_conv_sample_body, L=L)
    return pl.pallas_call(
        body,
        out_shape=[jax.ShapeDtypeStruct((L, b, GDN_QK_WIDTH), F32)] * 3
        + [jax.ShapeDtypeStruct((L, b, MISC_W), F32)],
        compiler_params=pltpu.CompilerParams(vmem_limit_bytes=VMEM_LIMIT),
        name="conv_sample",
    )(xp, misc, conv_w, alog, dtb)


def _split_bf16(x):
    hi = x.astype(BF16)
    return hi, (x - hi.astype(F32)).astype(BF16)


def _stack3(x):
    hi, lo = _split_bf16(x)
    return jnp.concatenate([hi, lo, hi], axis=0), hi


def _cols3(a):
    hi, lo = _split_bf16(a)
    hi = hi.astype(F32)
    return jnp.concatenate([hi, hi, lo.astype(F32)], axis=1).astype(BF16)


def _gdn_prep_body(q_ref, k_ref, v_ref, gb_ref, wq_ref, wql_ref, ut_ref, qk_ref, kdt_ref, gl_ref, *, C, CB):
    row = lax.broadcasted_iota(jnp.int32, (C, C), 0)
    col = lax.broadcasted_iota(jnp.int32, (C, C), 1)
    incl = row >= col
    strict = row > col
    tri = incl.astype(F32)
    n_sq = int(math.log2(C)) - 1
    inst = [(cc, h) for cc in range(CB) for h in range(GDN_HEADS)]
    rows_of = lambda cc: slice(cc * C, (cc + 1) * C)
    cols_of = lambda h: slice(h * LANES, (h + 1) * LANES)
    g_cum, g_cum_t, gbs = [], [], []
    for cc in range(CB):
        gb = gb_ref[rows_of(cc), :]
        gc_all = _dot(tri, gb, HIGHEST)
        gl_ref[cc] = jnp.broadcast_to(jnp.exp(gc_all[C - 1:C, :]), (8, LANES))
        gbs.append(gb)
        g_cum.append(gc_all)
        g_cum_t.append(gc_all.T)
    ks = [k_ref[rows_of(cc), cols_of(h)] for cc, h in inst]
    both = [_dot_nt(jnp.concatenate([q_ref[rows_of(cc), cols_of(h)], k], axis=0).astype(BF16), k.astype(BF16))
            for (cc, h), k in zip(inst, ks)]
    a, r, gcs = [], [], []
    for i, (cc, h) in enumerate(inst):
        gc = g_cum[cc][:, h:h + 1]
        gr = g_cum_t[cc][h:h + 1, :]
        beta = gbs[cc][:, GDN_HEADS + h:GDN_HEADS + h + 1]
        decay = jnp.where(incl, jnp.exp(jnp.where(incl, gc - gr, 0.0)), 0.0)
        a.append(jnp.where(strict, -(beta * both[i][C:2 * C] * decay), 0.0))
        qk = both[i][0:C] * decay
        qk_ref[rows_of(cc), cols_of(h)] = jnp.concatenate(
            [qk, jnp.zeros((C, LANES - C), F32)], axis=1).astype(BF16)
        e_g = jnp.exp(gc)
        qd_hi, qd_lo = _split_bf16(e_g * q_ref[rows_of(cc), cols_of(h)])
        wq_ref[cc, C:2 * C, cols_of(h)] = qd_hi
        wql_ref[cc, C:2 * C, cols_of(h)] = qd_lo
        kd_hi, kd_lo = _split_bf16(jnp.exp(g_cum[cc][C - 1:C, h:h + 1] - gc) * ks[i])
        kd_hi = kd_hi.astype(F32)
        kdt_ref[cc, h] = jnp.concatenate([kd_hi, kd_hi, kd_lo.astype(F32)], axis=0).T.astype(BF16)
        r.append(jnp.concatenate([(beta * e_g) * ks[i], beta * v_ref[rows_of(cc), cols_of(h)]], axis=1))
    for step in range(n_sq + 1):
        a3 = [_cols3(x) for x in a]
        r = [x + _dot(y, _stack3(x)[0]) for x, y in zip(r, a3)]
        if step < n_sq:
            a = [_dot(y, _stack3(x)[0]) for x, y in zip(a, a3)]
    for i, (cc, h) in enumerate(inst):
        wq_ref[cc, 0:C, cols_of(h)], wql_ref[cc, 0:C, cols_of(h)] = _split_bf16(r[i][:, 0:LANES])
        ut_ref[rows_of(cc), cols_of(h)] = r[i][:, LANES:2 * LANES]


def gdn_prep(q, k, v, gb, *, C, CB):
    rows = q.shape[0]
    nc = rows // C
    blk = lambda i: (i, 0)
    body = functools.partial(_gdn_prep_body, C=C, CB=CB)
    return pl.pallas_call(
        body,
        grid=(nc // CB,),
        in_specs=[pl.BlockSpec((CB * C, GDN_QK_WIDTH), blk)] * 3 + [pl.BlockSpec((CB * C, MISC_W), blk)],
        out_specs=[pl.BlockSpec((CB, 2 * C, GDN_QK_WIDTH), lambda i: (i, 0, 0))] * 2
        + [pl.BlockSpec((CB * C, GDN_V_WIDTH), blk),
           pl.BlockSpec((CB * C, GDN_HEADS * LANES), blk),
           pl.BlockSpec((CB, GDN_HEADS, GDN_DK, 3 * C), lambda i: (i, 0, 0, 0)),
           pl.BlockSpec((CB, 8, LANES), lambda i: (i, 0, 0))],
        out_shape=[jax.ShapeDtypeStruct((nc, 2 * C, GDN_QK_WIDTH), BF16)] * 2
        + [jax.ShapeDtypeStruct((rows, GDN_V_WIDTH), F32),
           jax.ShapeDtypeStruct((rows, GDN_HEADS * LANES), BF16),
           jax.ShapeDtypeStruct((nc, GDN_HEADS, GDN_DK, 3 * C), BF16),
           jax.ShapeDtypeStruct((nc, 8, LANES), F32)],
        compiler_params=_cp(("parallel",)),
        name="gdn_prep",
    )(q, k, v, gb)


def _gdn_scan_body(gl_ref, wq_ref, wql_ref, ut_ref, qk_ref, kdt_ref, s0_ref, o_ref, sfin_ref, s_scr,
                   *, C, CB, n_steps):
    sq = pl.program_id(0)
    st = pl.program_id(1)

    @pl.when(st == 0)
    def _():
        s_scr[...] = s0_ref[0]

    for cc in range(CB):
        rows = slice(cc * C, (cc + 1) * C)
        chunk = (sq * n_steps + st) * CB + cc
        heads = range(GDN_HEADS)
        cols = [slice(h * LANES, (h + 1) * LANES) for h in heads]
        ss = [s_scr[h] for h in heads]
        ws = [_dot(jnp.concatenate([wq_ref[cc, :, cols[h]], wq_ref[cc, :, cols[h]], wql_ref[cc, :, cols[h]]],
                                   axis=1), _stack3(ss[h])[0]) for h in heads]
        u3 = [_stack3(ut_ref[rows, cols[h]] - ws[h][0:C]) for h in heads]
        for h in heads:
            s_scr[h] = gl_ref[chunk * GDN_HEADS + h] * ss[h] + _dot(kdt_ref[cc, h], u3[h][0])
        for h in heads:
            o_ref[rows, cols[h]] = ws[h][C:2 * C] + _dot(qk_ref[rows, h * LANES:h * LANES + C], u3[h][1])

    @pl.when(st == n_steps - 1)
    def _():
        sfin_ref[0] = s_scr[...]


def gdn_scan(gl, wq, wql, ut, qk, kdt, s0, *, C, CB, n_seq):
    rows = ut.shape[0]
    nc = rows // C
    n_steps = nc // (n_seq * CB)
    blk = lambda s, t, gl: (s * n_steps + t, 0)
    blk3 = lambda s, t, gl: (s * n_steps + t, 0, 0)
    blk4 = lambda s, t, gl: (s * n_steps + t, 0, 0, 0)
    st4 = lambda s, t, gl: (s, 0, 0, 0)
    body = functools.partial(_gdn_scan_body, C=C, CB=CB, n_steps=n_steps)
    grid_spec = pltpu.PrefetchScalarGridSpec(
        num_scalar_prefetch=1,
        grid=(n_seq, n_steps),
        in_specs=[pl.BlockSpec((CB, 2 * C, GDN_QK_WIDTH), blk3)] * 2 + [pl.BlockSpec((CB * C, GDN_V_WIDTH), blk),
                  pl.BlockSpec((CB * C, GDN_HEADS * LANES), blk),
                  pl.BlockSpec((CB, GDN_HEADS, GDN_DK, 3 * C), blk4),
                  pl.BlockSpec((1, GDN_HEADS, GDN_DK, GDN_DV), st4)],
        out_specs=[pl.BlockSpec((CB * C, GDN_V_WIDTH), blk),
                   pl.BlockSpec((1, GDN_HEADS, GDN_DK, GDN_DV), st4)],
        scratch_shapes=[pltpu.VMEM((GDN_HEADS, GDN_DK, GDN_DV), F32)],
    )
    return pl.pallas_call(
        body,
        grid_spec=grid_spec,
        out_shape=[jax.ShapeDtypeStruct((rows, GDN_V_WIDTH), F32), jax.ShapeDtypeStruct(s0.shape, F32)],
        compiler_params=_cp(("arbitrary", "arbitrary")),
        name="gdn_scan",
    )(gl, wq, wql, ut, qk, kdt, s0)


def gdn(q, k, v, gb, s0, *, C, CB_prep, CB_scan, n_seq):
    wq, wql, ut, qk, kdt, glv = gdn_prep(q, k, v, gb, C=C, CB=CB_prep)
    gl = glv[:, 0, :GDN_HEADS].reshape(-1)
    return gdn_scan(gl, wq, wql, ut, qk, kdt, s0, C=C, CB=CB_scan, n_seq=n_seq)


def _rope(y, cosb, sin_lo, sin_hi):
    return (y * cosb + pltpu.roll(y, LANES - PE_HALF, 1) * sin_lo + pltpu.roll(y, PE_HALF, 1) * sin_hi)


def _slab_norm(x, gain, lane):
    x2 = x * x
    is_nope = lane < NOPE_DIM
    is_pe = (lane >= PE_LO) & (lane < PE_LO + ROPE_DIM)
    ms_n = jnp.sum(jnp.where(is_nope, x2, 0.0), axis=-1, keepdims=True) * (1.0 / NOPE_DIM)
    ms_p = jnp.sum(jnp.where(is_pe, x2, 0.0), axis=-1, keepdims=True) * (1.0 / ROPE_DIM)
    rinv = jnp.where(is_nope, lax.rsqrt(ms_n + EPS), lax.rsqrt(ms_p + EPS))
    return (x * rinv) * gain


def _mla_body(cq_ref, ckvr_ref, misc_ref, cos_ref, slo_ref, shi_ref, qan_ref, wqb_ref, qg_ref,
              kvn_ref, kg_ref, *rest, with_kv):
    if with_kv:
        wuk_ref, wuvt_ref, ckv_ref, kpe_ref, q_ref, k_ref, vt_ref = rest
    else:
        ckv_ref, kpe_ref, q_ref = rest
    cosb, slo, shi = cos_ref[...], slo_ref[...], shi_ref[...]
    lane = lax.broadcasted_iota(jnp.int32, cosb.shape, 1)
    cq = cq_ref[...]
    cqn = ((cq * lax.rsqrt(jnp.mean(cq * cq, axis=-1, keepdims=True) + EPS)) * qan_ref[...]).astype(BF16)
    c = ckvr_ref[...]
    ckv = (c * lax.rsqrt(jnp.mean(c * c, axis=-1, keepdims=True) + EPS)) * kvn_ref[...]
    ckv_ref[...] = ckv
    ckv_b = ckv.astype(BF16)
    kpe_raw = jnp.where((lane >= PE_LO) & (lane < PE_LO + ROPE_DIM), misc_ref[...], 0.0)
    kpe = _rope(_slab_norm(kpe_raw, kg_ref[...], lane), cosb, slo, shi)
    kpe = jnp.where(lane >= PE_LO, kpe, 0.0)
    kpe_ref[...] = kpe
    for h in range(MLA_HEADS):
        cols = slice(h * HEAD_PAD, (h + 1) * HEAD_PAD)
        qh = _dot(cqn, wqb_ref[:, cols])
        qh = _rope(_slab_norm(qh, qg_ref[...], lane), cosb, slo, shi)
        q_ref[h] = (qh * SCORE_SCALE_LOG2).astype(BF16)
        if with_kv:
            kh = _dot(ckv_b, wuk_ref[:, cols])
            k_ref[h] = (_slab_norm(kh, kg_ref[...], lane) + kpe).astype(BF16)
            vt = _dot_nt(wuvt_ref[h], ckv_b)
            ones_row = lax.broadcasted_iota(jnp.int32, vt.shape, 0) == V_DIM
            vt_ref[h] = jnp.where(ones_row, 1.0, vt).astype(BF16)


def mla_prep(cq, ckv_raw, misc, rope_tabs, p, tm, with_kv):
    m = cq.shape[0]
    row = lambda i: (i, 0)
    fixed = lambda i: (0, 0)
    hw = MLA_HEADS * HEAD_PAD
    in_specs = ([pl.BlockSpec((tm, Q_RANK), row), pl.BlockSpec((tm, KV_RANK), row),
                 pl.BlockSpec((tm, MISC_W), row)] + [pl.BlockSpec((tm, LANES), row)] * 3
                + [pl.BlockSpec((1, Q_RANK), fixed), pl.BlockSpec((Q_RANK, hw), fixed),
                   pl.BlockSpec((1, LANES), fixed), pl.BlockSpec((1, KV_RANK), fixed),
                   pl.BlockSpec((1, LANES), fixed)])
    args = [cq, ckv_raw, misc, *rope_tabs, p["q_a_norm"], p["w_q_b"], p["q_gain"], p["kv_a_norm"], p["k_gain"]]
    out_specs = [pl.BlockSpec((tm, KV_RANK), row), pl.BlockSpec((tm, LANES), row),
                 pl.BlockSpec((MLA_HEADS, tm, HEAD_PAD), lambda i: (0, i, 0))]
    out_shape = [jax.ShapeDtypeStruct((m, KV_RANK), F32), jax.ShapeDtypeStruct((m, LANES), F32),
                 jax.ShapeDtypeStruct((MLA_HEADS, m, HEAD_PAD), BF16)]
    if with_kv:
        in_specs += [pl.BlockSpec((KV_RANK, hw), fixed),
                     pl.BlockSpec((MLA_HEADS, VT_ROWS, KV_RANK), lambda i: (0, 0, 0))]
        args += [p["w_uk"], p["w_uv_t"]]
        out_specs += [pl.BlockSpec((MLA_HEADS, tm, HEAD_PAD), lambda i: (0, i, 0)),
                      pl.BlockSpec((MLA_HEADS, VT_ROWS, tm), lambda i: (0, 0, i))]
        out_shape += [jax.ShapeDtypeStruct((MLA_HEADS, m, HEAD_PAD), BF16),
                      jax.ShapeDtypeStruct((MLA_HEADS, VT_ROWS, m), BF16)]
    return pl.pallas_call(
        functools.partial(_mla_body, with_kv=with_kv),
        grid=(m // tm,),
        in_specs=in_specs,
        out_specs=out_specs,
        out_shape=out_shape,
        compiler_params=_cp(("parallel",)),
        name="mla_prep",
    )(*args)


def _flash_body(qmap_ref, kmap_ref, q_ref, k_ref, vt_ref, kpre_ref, vtpre_ref, o_ref, m_scr, acc_scr,
                *, tq, tk, q_block0):
    step = pl.program_id(0)
    q0 = (qmap_ref[step] + q_block0) * tq
    k0 = kmap_ref[step] * tk

    @pl.when(k0 == P0)
    def _():
        sts = [_dot_nt(kpre_ref[h], q_ref[h]) for h in range(MLA_HEADS)]
        m0s = [jnp.max(st, axis=0, keepdims=True) for st in sts]
        for h in range(MLA_HEADS):
            m_scr[h] = m0s[h]
            acc_scr[h] = _dot(vtpre_ref[h], jnp.exp2(sts[h] - m0s[h]).astype(BF16))

    def update(masked):
        if masked:
            kpos = k0 + lax.broadcasted_iota(jnp.int32, (tk, tq), 0)
            qpos = q0 + lax.broadcasted_iota(jnp.int32, (tk, tq), 1)
            valid = kpos <= qpos
        pending = [_dot_nt(k_ref[h], q_ref[h]) for h in range(FLASH_AHEAD)]
        for h in range(MLA_HEADS):
            st = pending.pop(0)
            if h + FLASH_AHEAD < MLA_HEADS:
                pending.append(_dot_nt(k_ref[h + FLASH_AHEAD], q_ref[h + FLASH_AHEAD]))
            if masked:
                st = jnp.where(valid, st, NEG_INF)
            m_prev = m_scr[h]
            m_new = jnp.maximum(m_prev, jnp.max(st, axis=0, keepdims=True))
            e = jnp.exp2(st - m_new).astype(BF16)
            acc_scr[h] = jnp.exp2(m_prev - m_new) * acc_scr[h] + _dot(vt_ref[h], e)
            m_scr[h] = m_new

    needs_mask = k0 + tk - 1 > q0
    pl.when(needs_mask)(lambda: update(True))
    pl.when(jnp.logical_not(needs_mask))(lambda: update(False))

    @pl.when(k0 + tk >= q0 + tq)
    def _():
        for h in range(MLA_HEADS):
            a = acc_scr[h]
            o = (a[0:V_DIM] * (1.0 / a[V_DIM:V_DIM + 1, :])).T
            o_ref[:, h * HEAD_PAD:(h + 1) * HEAD_PAD] = jnp.concatenate(
                [o, jnp.zeros((tq, HEAD_PAD - V_DIM), F32)], axis=1).astype(BF16)


def flash_prompt(q, k, vt, *, tq, tk, n_prefix):
    tp = q.shape[1]
    q_block0 = P0 // tq
    nq = (tp - P0) // tq
    qs, ks = [], []
    for qi in range(nq):
        last_k = ((qi + q_block0 + 1) * tq - 1) // tk
        for ki in range(P0 // tk, last_k + 1):
            qs.append(qi)
            ks.append(ki)
    qmap = jnp.asarray(qs, jnp.int32)
    kmap = jnp.asarray(ks, jnp.int32)
    k_pre = k[:, P0 - n_prefix:P0]
    vt_pre = vt[:, :, P0 - n_prefix:P0]
    body = functools.partial(_flash_body, tq=tq, tk=tk, q_block0=q_block0)
    whole = lambda s, qm, km: (0, 0, 0)
    grid_spec = pltpu.PrefetchScalarGridSpec(
        num_scalar_prefetch=2,
        grid=(len(qs),),
        in_specs=[pl.BlockSpec((MLA_HEADS, tq, HEAD_PAD), lambda s, qm, km: (0, qm[s] + q_block0, 0)),
                  pl.BlockSpec((MLA_HEADS, tk, HEAD_PAD), lambda s, qm, km: (0, km[s], 0)),
                  pl.BlockSpec((MLA_HEADS, VT_ROWS, tk), lambda s, qm, km: (0, 0, km[s])),
                  pl.BlockSpec((MLA_HEADS, n_prefix, HEAD_PAD), whole),
                  pl.BlockSpec((MLA_HEADS, VT_ROWS, n_prefix), whole)],
        out_specs=pl.BlockSpec((tq, MLA_HEADS * HEAD_PAD), lambda s, qm, km: (qm[s], 0)),
        scratch_shapes=[pltpu.VMEM((MLA_HEADS, 1, tq), F32), pltpu.VMEM((MLA_HEADS, VT_ROWS, tq), F32)],
    )
    return pl.pallas_call(
        body,
        grid_spec=grid_spec,
        out_shape=jax.ShapeDtypeStruct((tp - P0, MLA_HEADS * HEAD_PAD), BF16),
        compiler_params=_cp(("arbitrary",)),
        name="flash_prompt",
    )(qmap, kmap, q, k, vt, k_pre, vt_pre)


def _qabs_body(q_ref, w_ref, o_ref):
    o_ref[0] = _dot(q_ref[0], w_ref[0]).astype(BF16)


def q_absorb(q, w_abs):
    h, m, _ = q.shape
    blk = lambda i: (i, 0, 0)
    return pl.pallas_call(
        _qabs_body,
        grid=(h,),
        in_specs=[pl.BlockSpec((1, m, HEAD_PAD), blk), pl.BlockSpec((1, HEAD_PAD, KV_RANK), blk)],
        out_specs=pl.BlockSpec((1, m, KV_RANK), blk),
        out_shape=jax.ShapeDtypeStruct((h, m, KV_RANK), BF16),
        compiler_params=_cp(("parallel",)),
        name="q_absorb",
    )(q, w_abs)


def _decode_scores(waug, qpe, ckv_b, kpe_t, n_lq):
    kp = _dot_nt(waug, ckv_b)
    n_k = MLA_HEADS * NOPE_DIM
    kproj = kp[:n_k]
    ms = jnp.sum((kproj * kproj).reshape(MLA_HEADS, NOPE_DIM, kproj.shape[-1]), axis=1) * (1.0 / NOPE_DIM)
    rinv = lax.rsqrt(ms + EPS)
    s_nope = kp[n_k:]
    s_pe = _dot(qpe, kpe_t)
    rinv_rows = jnp.concatenate([rinv] * n_lq, axis=0)
    return s_nope * rinv_rows + s_pe


def _page_copies(pt_ref, ckv_hbm, kpet_hbm, ckv_buf, kpe_buf, sem, step, slot, *, PG, page):
    out = []
    for i in range(PG):
        pid = pt_ref[step * PG + i]
        out.append(pltpu.make_async_copy(ckv_hbm.at[0, pid], ckv_buf.at[slot, pl.ds(i * page, page)],
                                         sem.at[0, slot]))
        out.append(pltpu.make_async_copy(kpet_hbm.at[0, pid], kpe_buf.at[slot, :, pl.ds(i * page, page)],
                                         sem.at[1, slot]))
    return out


def _decode_body(pt_ref, wukt_ref, qabs_ref, qpe_ref, nckv_ref, nkpet_ref, ckv_hbm, kpet_hbm, o_ref,
                 ckv_buf, kpe_buf, sem, waug, ckvb, m_scr, l_scr, acc_scr, *, PG, L, n_groups, n_steps, page):
    s = pl.program_id(0)
    j = lax.rem(s, n_groups)
    slot = lax.rem(s, 2)
    n_k = MLA_HEADS * NOPE_DIM
    R = MLA_HEADS * L
    copies = functools.partial(_page_copies, pt_ref, ckv_hbm, kpet_hbm, ckv_buf, kpe_buf, sem, PG=PG, page=page)

    nxt = lax.rem(s + 1, n_steps)

    @pl.when(s == 0)
    def _():
        for c in copies(0, 0):
            c.start()

    @pl.when(j == 0)
    def _():
        waug[0:n_k, :] = wukt_ref[...]
        waug[n_k:n_k + R, :] = qabs_ref[0]
        c_new = nckv_ref[0].astype(BF16)
        sc = _decode_scores(waug[...], qpe_ref[0], c_new, nkpet_ref[0].astype(BF16), L)
        r_l = lax.shift_right_logical(lax.broadcasted_iota(jnp.int32, sc.shape, 0),
                                      int(math.log2(MLA_HEADS)))
        t = lax.broadcasted_iota(jnp.int32, sc.shape, 1)
        sc = jnp.where((t <= r_l) & (t < L), sc, NEG_INF)
        m0 = jnp.max(sc, axis=-1, keepdims=True)
        e0 = jnp.exp2(sc - m0)
        m_scr[...] = m0
        l_scr[...] = jnp.sum(e0, axis=-1, keepdims=True)
        acc_scr[...] = _dot(e0.astype(BF16), c_new)

    for c in copies(s, slot):
        c.wait()

    w = waug[...]
    qpe = qpe_ref[0]

    def scores(p):
        tok = pl.ds(p * DECODE_SCORE_PAGES * page, DECODE_SCORE_PAGES * page)
        c2 = ckv_buf[slot, tok, :].astype(BF16)
        ckvb[p] = c2
        return _decode_scores(w, qpe, c2, kpe_buf[slot, :, tok].astype(BF16), L)

    ms, ls, accs = [m_scr[...]], [l_scr[...]], [acc_scr[...]]
    pending = [scores(0)]
    next_copies = copies(nxt, 1 - slot)
    per_group = 2 * DECODE_SCORE_PAGES
    for p in range(PG // DECODE_SCORE_PAGES):
        sc = pending.pop(0)
        if p + 1 < PG // DECODE_SCORE_PAGES:
            pending.append(scores(p + 1))
        for c in next_copies[p * per_group:(p + 1) * per_group]:
            c.start()
        m_p = jnp.max(sc, axis=-1, keepdims=True)
        e = jnp.exp2(sc - m_p)
        ms.append(m_p)
        ls.append(jnp.sum(e, axis=-1, keepdims=True))
        accs.append(_dot(e.astype(BF16), ckvb[p]))
    m = functools.reduce(jnp.maximum, ms)
    fs = [jnp.exp2(x - m) for x in ms]
    l = functools.reduce(lambda x, y: x + y, [f * x for f, x in zip(fs, ls)])
    acc = functools.reduce(lambda x, y: x + y, [f * x for f, x in zip(fs, accs)])
    m_scr[...], l_scr[...], acc_scr[...] = m, l, acc

    @pl.when(j == n_groups - 1)
    def _():
        o_ref[0] = acc / l

    @pl.when(s == n_steps - 1)
    def _():
        for c in copies(nxt, 1 - slot):
            c.wait()


def decode_attn(page_table, wuk_t, qabs, qpe, new_ckv, new_kpe_t, cache_ckv, cache_kpe_t, *, PG, L):
    b, n_pages = page_table.shape
    page = cache_ckv.shape[2]
    R = MLA_HEADS * L
    n_k = MLA_HEADS * NOPE_DIM
    n_groups = n_pages // PG
    per_b = lambda s, pt: (s // n_groups, 0, 0)
    fixed = lambda s, pt: (0, 0)
    body = functools.partial(_decode_body, PG=PG, L=L, n_groups=n_groups, n_steps=b * n_groups, page=page)
    grid_spec = pltpu.PrefetchScalarGridSpec(
        num_scalar_prefetch=1,
        grid=(b * n_groups,),
        in_specs=[pl.BlockSpec((n_k, KV_RANK), fixed), pl.BlockSpec((1, R, KV_RANK), per_b),
                  pl.BlockSpec((1, R, ROPE_DIM), per_b), pl.BlockSpec((1, 8, KV_RANK), per_b),
                  pl.BlockSpec((1, ROPE_DIM, 8), per_b),
                  pl.BlockSpec(memory_space=pl.ANY), pl.BlockSpec(memory_space=pl.ANY)],
        out_specs=pl.BlockSpec((1, R, KV_RANK), per_b),
        scratch_shapes=[pltpu.VMEM((2, PG * page, KV_RANK), F32), pltpu.VMEM((2, ROPE_DIM, PG * page), F32),
                        pltpu.SemaphoreType.DMA((2, 2)),
                        pltpu.VMEM((n_k + R, KV_RANK), BF16), pltpu.VMEM((PG // DECODE_SCORE_PAGES, DECODE_SCORE_PAGES * page, KV_RANK), BF16),
                        pltpu.VMEM((R, 1), F32), pltpu.VMEM((R, 1), F32), pltpu.VMEM((R, KV_RANK), F32)],
    )
    return pl.pallas_call(
        body,
        grid_spec=grid_spec,
        out_shape=jax.ShapeDtypeStruct((b, R, KV_RANK), F32),
        compiler_params=_cp(("arbitrary",)),
        name="decode_attn",
    )(page_table.reshape(-1), wuk_t, qabs, qpe, new_ckv, new_kpe_t, cache_ckv, cache_kpe_t)


def _latent_out_body(x_ref, w_ref, o_ref):
    o_ref[...] = _dot(x_ref[0].astype(BF16), w_ref[0]).astype(BF16)


def latent_out(o_lat, w_uv_h):
    h, m, _ = o_lat.shape
    return pl.pallas_call(
        _latent_out_body,
        grid=(h,),
        in_specs=[pl.BlockSpec((1, m, KV_RANK), lambda i: (i, 0, 0)),
                  pl.BlockSpec((1, KV_RANK, HEAD_PAD), lambda i: (i, 0, 0))],
        out_specs=pl.BlockSpec((m, HEAD_PAD), lambda i: (0, i)),
        out_shape=jax.ShapeDtypeStruct((m, h * HEAD_PAD), BF16),
        compiler_params=_cp(("parallel",)),
        name="latent_out",
    )(o_lat, w_uv_h)


def _mix_body(h_ref, o_ref, z_ref, mla_ref, gg_ref, wog_ref, wom_ref, n2_ref, h1_ref, hn_ref):
    acc = h_ref[...] + _dot(mla_ref[...], wom_ref[...])
    for h in range(GDN_HEADS):
        cols = slice(h * LANES, (h + 1) * LANES)
        o = o_ref[:, cols]
        z = z_ref[:, cols]
        y = ((o * lax.rsqrt(jnp.mean(o * o, axis=-1, keepdims=True) + EPS)) * gg_ref[...]) * (z * _sigmoid(z))
        acc = acc + _dot(y.astype(BF16), wog_ref[cols, :])
    h1_ref[...] = acc
    hn_ref[...] = ((acc * lax.rsqrt(jnp.mean(acc * acc, axis=-1, keepdims=True) + EPS)) * n2_ref[...]).astype(BF16)


def mix_out(h, gdn_o, z, mla_o, p, tm, row_block0):
    m, d = h.shape
    row = lambda i: (i, 0)
    prow = lambda i: (i + row_block0, 0)
    fixed = lambda i: (0, 0)
    hw = MLA_HEADS * HEAD_PAD
    return pl.pallas_call(
        _mix_body,
        grid=(m // tm,),
        in_specs=[pl.BlockSpec((tm, d), row), pl.BlockSpec((tm, GDN_V_WIDTH), prow),
                  pl.BlockSpec((tm, GDN_V_WIDTH), prow), pl.BlockSpec((tm, hw), row),
                  pl.BlockSpec((1, LANES), fixed), pl.BlockSpec((GDN_V_WIDTH, d), fixed),
                  pl.BlockSpec((hw, d), fixed), pl.BlockSpec((1, d), fixed)],
        out_specs=[pl.BlockSpec((tm, d), row), pl.BlockSpec((tm, d), row)],
        out_shape=[jax.ShapeDtypeStruct((m, d), F32), jax.ShapeDtypeStruct((m, d), BF16)],
        compiler_params=_cp(("parallel",)),
        name="mix_out",
    )(h, gdn_o, z, mla_o, p["gdn_norm"], p["w_o_gdn"], p["w_o_mla"], p["norm2"])


def _ffn_body(h1_ref, hn_ref, wg_ref, wu_ref, wd_ref, y_ref):
    f = pl.program_id(1)

    @pl.when(f == 0)
    def _():
        y_ref[...] = h1_ref[...]

    x = hn_ref[...]
    g = _dot(x, wg_ref[...])
    u = _dot(x, wu_ref[...])
    y_ref[...] += _dot(((g * _sigmoid(g)) * u).astype(BF16), wd_ref[...])


def ffn(h1, hn, wg, wu, wd, tm, tf):
    m, d = h1.shape
    dff = wg.shape[1]
    row = lambda i, f: (i, 0)
    return pl.pallas_call(
        _ffn_body,
        grid=(m // tm, dff // tf),
        in_specs=[pl.BlockSpec((tm, d), row), pl.BlockSpec((tm, d), row),
                  pl.BlockSpec((d, tf), lambda i, f: (0, f)), pl.BlockSpec((d, tf), lambda i, f: (0, f)),
                  pl.BlockSpec((tf, d), lambda i, f: (f, 0))],
        out_specs=pl.BlockSpec((tm, d), row),
        out_shape=jax.ShapeDtypeStruct((m, d), F32),
        compiler_params=_cp(("parallel", "arbitrary")),
        name="ffn",
    )(h1, hn, wg, wu, wd)


def _head_slabs(w, width):
    k, h, _ = w.shape
    return jnp.pad(w, ((0, 0), (0, 0), (0, HEAD_PAD - width))).reshape(k, h * HEAD_PAD)


def _pad_lanes(v, lo=0):
    return jnp.pad(v.astype(F32), (lo, LANES - lo - v.shape[0])).reshape(1, LANES)


def _prep_params(norm1, w_in, conv_w, a_log, dt_bias, gdn_norm, q_a_norm, w_q_b, q_nope_norm,
                 q_pe_norm, kv_a_norm, k_pe_norm, w_uk, w_uv, k_nope_norm, w_o, norm2,
                 w_gate, w_up, w_down):
    d = w_in.shape[0]
    o = 0
    cuts = {}
    for name, wdt in (("qkv", CONV_DIM), ("z", GDN_V_WIDTH), ("a", GDN_HEADS), ("b", GDN_HEADS),
                      ("cq", Q_RANK), ("ckv", KV_RANK), ("kpe", ROPE_DIM)):
        cuts[name] = w_in[:, o:o + wdt]
        o += wdt
    zeros = lambda n: jnp.zeros((d, n), w_in.dtype)
    misc = jnp.concatenate([cuts["a"], cuts["b"], zeros(PE_LO - 2 * GDN_HEADS), cuts["kpe"],
                            zeros(MISC_W - PE_LO - ROPE_DIM)], axis=1)
    w_r = jnp.concatenate([cuts["qkv"], cuts["z"], cuts["cq"], cuts["ckv"], misc], axis=1).astype(BF16)
    wqb = w_q_b.reshape(Q_RANK, MLA_HEADS, QK_DIM)
    gain_q = jnp.concatenate([q_nope_norm, q_pe_norm]).astype(F32)
    gain_k = jnp.concatenate([k_nope_norm, k_pe_norm]).astype(F32)
    w_uk_hdr = jnp.transpose(w_uk, (1, 2, 0))
    w_abs = jnp.pad(w_uk_hdr * k_nope_norm[None, :, None], ((0, 0), (0, HEAD_PAD - NOPE_DIM), (0, 0)))
    w_o_mla = jnp.pad(w_o[GDN_V_WIDTH:].reshape(MLA_HEADS, V_DIM, d),
                      ((0, 0), (0, HEAD_PAD - V_DIM), (0, 0))).reshape(MLA_HEADS * HEAD_PAD, d)
    return dict(
        norm1=norm1, w_in=w_r, conv_w=conv_w.astype(F32),
        a_log=_pad_lanes(a_log), dt_bias=_pad_lanes(dt_bias),
        gdn_norm=gdn_norm.reshape(1, GDN_DV).astype(F32),
        q_a_norm=q_a_norm.reshape(1, Q_RANK).astype(F32), w_q_b=_head_slabs(wqb, QK_DIM).astype(BF16),
        q_gain=_pad_lanes(gain_q), kv_a_norm=kv_a_norm.reshape(1, KV_RANK).astype(F32),
        w_uk=_head_slabs(w_uk, NOPE_DIM).astype(BF16), k_gain=_pad_lanes(gain_k),
        w_uv_t=jnp.pad(jnp.transpose(w_uv, (1, 2, 0)), ((0, 0), (0, VT_ROWS - V_DIM), (0, 0))).astype(BF16),
        w_uk_t=w_uk_hdr.reshape(MLA_HEADS * NOPE_DIM, KV_RANK).astype(BF16),
        w_abs=w_abs.astype(BF16),
        w_uv_h=jnp.pad(jnp.transpose(w_uv, (1, 0, 2)), ((0, 0), (0, 0), (0, HEAD_PAD - V_DIM))).astype(BF16),
        w_o_gdn=w_o[:GDN_V_WIDTH].astype(BF16), w_o_mla=w_o_mla.astype(BF16),
        norm2=norm2.reshape(1, d).astype(F32),
        w_gate=w_gate.astype(BF16), w_up=w_up.astype(BF16), w_down=w_down.astype(BF16),
    )


def _rope_tables(pos):
    inv_freq = np.power(ROPE_THETA, -np.arange(0, ROPE_DIM, 2, dtype=np.float64) / ROPE_DIM)
    ang = np.asarray(pos, np.float64)[:, None] * inv_freq[None, :]
    cos, sin = jnp.asarray(np.cos(ang), F32), jnp.asarray(np.sin(ang), F32)
    n = len(pos)
    ones = jnp.ones((n, PE_LO), F32)
    z = lambda w: jnp.zeros((n, w), F32)
    tail = LANES - PE_LO - ROPE_DIM
    cosb = jnp.concatenate([ones, cos, cos, z(tail)], axis=1)
    sin_lo = jnp.concatenate([z(PE_LO), -sin, z(PE_HALF + tail)], axis=1)
    sin_hi = jnp.concatenate([z(PE_LO + PE_HALF), sin, z(tail)], axis=1)
    return cosb, sin_lo, sin_hi


def _ff_tile(dff):
    for t in (1408, 1024, 512, 256, 128):
        if dff % t == 0:
            return t
    return dff


def _prompt_group(x_prompt, meta_tokens, p):
    seq, d = x_prompt.shape
    tp = P0 + seq
    meta_row0 = P0 - N_META
    lead = jnp.concatenate([jnp.zeros((meta_row0, d), F32), meta_tokens.astype(F32)], axis=0)
    qkv_pre, z, cq, ckv_raw, misc = front_proj(x_prompt, p["norm1"], p["w_in"], tm=P0, lead=lead)
    q, k, v, gb = conv_prompt(qkv_pre, misc, p["conv_w"], p["a_log"], p["dt_bias"], tm=512,
                              first_valid_row=meta_row0)
    s0 = jnp.zeros((1, GDN_HEADS, GDN_DK, GDN_DV), F32)
    gdn_o, s_fin = gdn(q, k, v, gb, s0, C=CHUNK, CB_prep=4, CB_scan=8, n_seq=1)
    tabs = _rope_tables(np.arange(tp) - meta_row0)
    ckv, kpe, qh, kh, vt = mla_prep(cq, ckv_raw, misc, tabs, p, tm=256, with_kv=True)
    mla_o = flash_prompt(qh, kh, vt, tq=512, tk=512, n_prefix=N_META)
    h1, hn2 = mix_out(x_prompt, gdn_o, z, mla_o, p, tm=512, row_block0=P0 // 512)
    y = ffn(h1, hn2, p["w_gate"], p["w_up"], p["w_down"], tm=512, tf=_ff_tile(p["w_gate"].shape[1]))
    new_conv = qkv_pre[tp - (CONV_WIDTH - 1):]
    return (y, new_conv, s_fin[0], ckv[meta_row0:], kpe[meta_row0:, PE_LO:PE_LO + ROPE_DIM])


def _sample_group(x_sample, cache_conv, state_gdn, cache_ckv, cache_kpe, page_table, p):
    b, L, d = x_sample.shape
    m = b * L
    page = cache_ckv.shape[2]
    past_len = page_table.shape[1] * page
    x = x_sample.reshape(m, d)
    tm = min(m, 256)
    qkv_pre, z, cq, ckv_raw, misc = front_proj(x, p["norm1"], p["w_in"], tm=tm)
    xp = jnp.concatenate([jnp.transpose(cache_conv.astype(F32), (1, 0, 2)),
                          jnp.transpose(qkv_pre.reshape(b, L, CONV_DIM), (1, 0, 2))], axis=0)
    misc_t = jnp.transpose(misc.reshape(b, L, MISC_W), (1, 0, 2))
    q, k, v, gb = conv_sample(xp, misc_t, p["conv_w"], p["a_log"], p["dt_bias"])
    C = 16
    to_chunks = lambda t: jnp.pad(jnp.transpose(t, (1, 0, 2)), ((0, 0), (0, C - L), (0, 0))).reshape(b * C, -1)
    gdn_o, s_fin = gdn(to_chunks(q), to_chunks(k), to_chunks(v), to_chunks(gb), state_gdn.astype(F32),
                       C=C, CB_prep=4 if b % 4 == 0 else 1, CB_scan=1, n_seq=b)
    gdn_o = gdn_o.reshape(b, C, GDN_V_WIDTH)[:, :L].reshape(m, GDN_V_WIDTH)
    tabs = _rope_tables(np.tile(past_len + np.arange(L), b))
    ckv, kpe, qh = mla_prep(cq, ckv_raw, misc, tabs, p, tm=tm, with_kv=False)
    kpe32 = kpe[:, PE_LO:PE_LO + ROPE_DIM]
    qabs = q_absorb(qh, p["w_abs"])
    R = L * MLA_HEADS
    rows_lh = lambda t: jnp.transpose(t.reshape(MLA_HEADS, b, L, -1), (1, 2, 0, 3)).reshape(b, R, -1)
    qpe = rows_lh(qh[:, :, PE_LO:PE_LO + ROPE_DIM])
    new_ckv = jnp.pad(ckv.reshape(b, L, KV_RANK), ((0, 0), (0, 8 - L), (0, 0)))
    new_kpe_t = jnp.pad(jnp.swapaxes(kpe32.reshape(b, L, ROPE_DIM), 1, 2), ((0, 0), (0, 0), (0, 8 - L)))
    n_pages = page_table.shape[1]
    PG = 64 if n_pages % 64 == 0 else DECODE_SCORE_PAGES
    o_lat = decode_attn(page_table, p["w_uk_t"], rows_lh(qabs), qpe, new_ckv, new_kpe_t,
                        cache_ckv, jnp.swapaxes(cache_kpe, 2, 3), PG=PG, L=L)
    o_lat_h = jnp.transpose(o_lat.reshape(b, L, MLA_HEADS, KV_RANK), (2, 0, 1, 3)).reshape(MLA_HEADS, m, KV_RANK)
    mla_o = latent_out(o_lat_h, p["w_uv_h"])
    h1, hn2 = mix_out(x, gdn_o, z, mla_o, p, tm=tm, row_block0=0)
    y = ffn(h1, hn2, p["w_gate"], p["w_up"], p["w_down"], tm=tm, tf=_ff_tile(p["w_gate"].shape[1]))
    new_conv = qkv_pre.reshape(b, L, CONV_DIM)[:, L - (CONV_WIDTH - 1):]
    return (y.reshape(b, L, d), new_conv, s_fin, ckv.reshape(b, L, KV_RANK), kpe32.reshape(b, L, ROPE_DIM))


def kernel(x_prompt, x_sample, cache_conv, state_gdn, cache_ckv, cache_kpe, page_table, meta_tokens, norm1, w_in, conv_w, a_log, dt_bias, gdn_norm, q_a_norm, w_q_b, q_nope_norm, q_pe_norm, kv_a_norm, k_pe_norm, w_uk, w_uv, k_nope_norm, w_o, norm2, w_gate, w_up, w_down):
    assert x_prompt.shape[0] == 1 and norm1.shape[0] == 1, "one prompt sequence, one layer"
    p = _prep_params(norm1[0], w_in[0], conv_w[0], a_log[0], dt_bias[0], gdn_norm[0], q_a_norm[0],
                     w_q_b[0], q_nope_norm[0], q_pe_norm[0], kv_a_norm[0], k_pe_norm[0], w_uk[0],
                     w_uv[0], k_nope_norm[0], w_o[0], norm2[0], w_gate[0], w_up[0], w_down[0])
    y_p, conv_p, st_p, ckv_p, kpe_p = _prompt_group(x_prompt[0], meta_tokens, p)
    y_s, conv_s, st_s, ckv_s, kpe_s = _sample_group(x_sample, cache_conv[0], state_gdn[0], cache_ckv,
                                                     cache_kpe, page_table, p)
    lead = lambda t: t[None, None]
    return (y_p[None], y_s, lead(conv_p), lead(st_p), lead(ckv_p), lead(kpe_p),
            conv_s[None], st_s[None], ckv_s[None], kpe_s[None])
```

```python
import functools
import math

import jax
import jax.numpy as jnp
import numpy as np
from jax import lax
from jax.experimental import pallas as pl
from jax.experimental.pallas import tpu as pltpu

F32 = jnp.float32
BF16 = jnp.bfloat16
HIGHEST = lax.Precision.HIGHEST

N_META = 16
EPS = 1e-6
NEG_INF = -1e30
GDN_HEADS = 4
GDN_DK = 128
GDN_DV = 128
GDN_QK_WIDTH = GDN_HEADS * GDN_DK
GDN_V_WIDTH = GDN_HEADS * GDN_DV
CONV_WIDTH = 4
CONV_DIM = 2 * GDN_QK_WIDTH + GDN_V_WIDTH
CHUNK = 64
MLA_HEADS = 8
Q_RANK = 384
KV_RANK = 256
NOPE_DIM = 64
ROPE_DIM = 32
V_DIM = 64
QK_DIM = NOPE_DIM + ROPE_DIM
ROPE_THETA = 10000.0
SCORE_SCALE_LOG2 = QK_DIM ** -0.5 * math.log2(math.e)

LANES = 128
HEAD_PAD = LANES
PE_LO = NOPE_DIM
PE_HALF = ROPE_DIM // 2
MISC_W = LANES
P0 = 512
FLASH_AHEAD = 2
VT_ROWS = V_DIM + 16
DECODE_SCORE_PAGES = 4
VMEM_LIMIT = 56 * 1024 * 1024


def _cp(sem, vmem=None):
    return pltpu.CompilerParams(dimension_semantics=sem, vmem_limit_bytes=vmem or VMEM_LIMIT)


def _sigmoid(x):
    return 1.0 / (1.0 + jnp.exp(-x))


def _softplus(x):
    return jnp.maximum(x, 0.0) + jnp.log1p(jnp.exp(-jnp.abs(x)))


def _dot(a, b, precision=None):
    return jnp.dot(a, b, preferred_element_type=F32, precision=precision)


def _dot_nt(a, b, precision=None):
    return lax.dot_general(a, b, (((1,), (1,)), ((), ())), preferred_element_type=F32,
                           precision=precision)


W_SPLITS = (CONV_DIM, GDN_V_WIDTH, Q_RANK, KV_RANK, MISC_W)


def _front_body(x_ref, g_ref, w_ref, *outs):
    x = x_ref[...]
    hn = ((x * lax.rsqrt(jnp.mean(x * x, axis=-1, keepdims=True) + EPS)) * g_ref[...]).astype(BF16)
    off = 0
    for width, o_ref in zip(W_SPLITS, outs):
        o_ref[...] = _dot(hn, w_ref[:, off:off + width])
        off += width


def front_proj(x, norm1, w_r, tm):
    m, d = x.shape
    n_all = sum(W_SPLITS)
    row = lambda i: (i, 0)
    fixed = lambda i: (0, 0)
    return pl.pallas_call(
        _front_body,
        grid=(m // tm,),
        in_specs=[pl.BlockSpec((tm, d), row), pl.BlockSpec((1, d), fixed), pl.BlockSpec((d, n_all), fixed)],
        out_specs=[pl.BlockSpec((tm, w), row) for w in W_SPLITS],
        out_shape=[jax.ShapeDtypeStruct((m, w), F32) for w in W_SPLITS],
        compiler_params=_cp(("parallel",)),
        name="front_proj",
    )(x, norm1.reshape(1, d), w_r)


def _gates(misc, alog, dtb):
    lane = lax.broadcasted_iota(jnp.int32, misc.shape, 1)
    g = -jnp.exp(alog) * _softplus(misc + dtb)
    beta = _sigmoid(misc)
    return jnp.where(lane < GDN_HEADS, g, jnp.where(lane < 2 * GDN_HEADS, beta, 0.0))


def _conv_cols(taps, w_ref, c, q_ref, k_ref, v_ref):
    cols = slice(c * LANES, (c + 1) * LANES)
    acc = taps(0, cols) * w_ref[0:1, cols]
    for j in range(1, CONV_WIDTH):
        acc = acc + taps(j, cols) * w_ref[j:j + 1, cols]
    y = acc * _sigmoid(acc)
    if c < 2 * GDN_HEADS:
        yn = y * lax.rsqrt(jnp.sum(y * y, axis=-1, keepdims=True) + EPS)
        if c < GDN_HEADS:
            q_ref[:, cols] = yn * (GDN_DK ** -0.5)
        else:
            k_ref[:, (c - GDN_HEADS) * LANES:(c - GDN_HEADS + 1) * LANES] = yn
    else:
        v_ref[:, (c - 2 * GDN_HEADS) * LANES:(c - 2 * GDN_HEADS + 1) * LANES] = y


def _front_conv_body(lead_ref, x_ref, g_ref, w_ref, cw_ref, alog_ref, dtb_ref,
                     q_ref, k_ref, v_ref, gb_ref, tail_ref, z_ref, cq_ref, ckv_ref, misc_ref, xs,
                     *, tm, first_valid_row):
    i = pl.program_id(0)
    x = jnp.where(i == 0, lead_ref[...], x_ref[...])
    hn = ((x * lax.rsqrt(jnp.mean(x * x, axis=-1, keepdims=True) + EPS)) * g_ref[...]).astype(BF16)
    xs[0:8, :] = jnp.where(i > 0, xs[tm:tm + 8, :], 0.0)
    xs[8:tm + 8, :] = _dot(hn, w_ref[:, 0:CONV_DIM])
    off = CONV_DIM
    for width, o_ref in zip(W_SPLITS[1:], (z_ref, cq_ref, ckv_ref, misc_ref)):
        o_ref[...] = _dot(hn, w_ref[:, off:off + width])
        off += width
    tail_ref[...] = xs[tm:tm + 8, :]
    taps = lambda j, cols: xs[8 - (CONV_WIDTH - 1) + j:8 - (CONV_WIDTH - 1) + j + tm, cols]
    for c in range(CONV_DIM // LANES):
        _conv_cols(taps, cw_ref, c, q_ref, k_ref, v_ref)
    gb = _gates(misc_ref[...], alog_ref[...], dtb_ref[...])
    row = i * tm + lax.broadcasted_iota(jnp.int32, gb.shape, 0)
    gb_ref[...] = jnp.where(row >= first_valid_row, gb, 0.0)


def front_conv_prompt(x, lead, norm1, w_r, conv_w, alog, dtb, tm, first_valid_row):
    m, d = x.shape
    n_all = sum(W_SPLITS)
    row = lambda i: (i, 0)
    fixed = lambda i: (0, 0)
    n_blocks = m // tm + 1
    rows = n_blocks * tm
    widths = (GDN_QK_WIDTH, GDN_QK_WIDTH, GDN_V_WIDTH, MISC_W) + (CONV_DIM,) + W_SPLITS[1:]
    out_specs = [pl.BlockSpec((tm, w), row) for w in widths]
    out_shape = [jax.ShapeDtypeStruct((rows, w), F32) for w in widths]
    out_specs[4] = pl.BlockSpec((8, CONV_DIM), fixed)
    out_shape[4] = jax.ShapeDtypeStruct((8, CONV_DIM), F32)
    body = functools.partial(_front_conv_body, tm=tm, first_valid_row=first_valid_row)
    return pl.pallas_call(
        body,
        grid=(n_blocks,),
        in_specs=[pl.BlockSpec((tm, d), fixed), pl.BlockSpec((tm, d), lambda i: (jnp.maximum(i - 1, 0), 0)),
                  pl.BlockSpec((1, d), fixed), pl.BlockSpec((d, n_all), fixed),
                  pl.BlockSpec((CONV_WIDTH, CONV_DIM), fixed),
                  pl.BlockSpec((1, MISC_W), fixed), pl.BlockSpec((1, MISC_W), fixed)],
        out_specs=out_specs,
        out_shape=out_shape,
        scratch_shapes=[pltpu.VMEM((tm + 8, CONV_DIM), F32)],
        compiler_params=_cp(("arbitrary",)),
        name="front_conv_prompt",
    )(lead, x, norm1.reshape(1, d), w_r, conv_w, alog, dtb)


def _conv_sample_body(xp_ref, misc_ref, w_ref, alog_ref, dtb_ref, q_ref, k_ref, v_ref, gb_ref, *, L):
    for l in range(L):
        taps = lambda j, cols, l=l: xp_ref[l + j, :, cols]
        qv, kv, vv = (r.at[l] for r in (q_ref, k_ref, v_ref))
        for c in range(CONV_DIM // LANES):
            _conv_cols(taps, w_ref, c, qv, kv, vv)
        gb_ref[l] = _gates(misc_ref[l], alog_ref[...], dtb_ref[...])


def conv_sample(xp, misc, conv_w, alog, dtb):
    lp, b, _ = xp.shape
    L = lp - (CONV_WIDTH - 1)
    body = functools.partial(_conv_sample_body, L=L)
    return pl.pallas_call(
        body,
        out_shape=[jax.ShapeDtypeStruct((L, b, GDN_QK_WIDTH), F32)] * 3
        + [jax.ShapeDtypeStruct((L, b, MISC_W), F32)],
        compiler_params=pltpu.CompilerParams(vmem_limit_bytes=VMEM_LIMIT),
        name="conv_sample",
    )(xp, misc, conv_w, alog, dtb)


def _split_bf16(x):
    hi = x.astype(BF16)
    return hi, (x - hi.astype(F32)).astype(BF16)


def _rows_hi_lo(x):
    return jnp.concatenate(_split_bf16(x), axis=0)


def _gdn_prep_body(q_ref, k_ref, v_ref, gb_ref, wq_ref, ut_ref, qk_ref, kdt_ref, gl_ref, *, C, CB):
    row = lax.broadcasted_iota(jnp.int32, (C, C), 0)
    col = lax.broadcasted_iota(jnp.int32, (C, C), 1)
    incl = row >= col
    strict = row > col
    tri = incl.astype(F32)
    n_sq = int(math.log2(C)) - 1
    inst = [(cc, h) for cc in range(CB) for h in range(GDN_HEADS)]
    rows_of = lambda cc: slice(cc * C, (cc + 1) * C)
    cols_of = lambda h: slice(h * LANES, (h + 1) * LANES)
    g_cum, g_cum_t, gbs = [], [], []
    for cc in range(CB):
        gb = gb_ref[rows_of(cc), :]
        gc_all = _dot(tri, gb, HIGHEST)
        gl_ref[cc] = jnp.broadcast_to(jnp.exp(gc_all[C - 1:C, :]), (8, LANES))
        gbs.append(gb)
        g_cum.append(gc_all)
        g_cum_t.append(gc_all.T)
    ks = [k_ref[rows_of(cc), cols_of(h)] for cc, h in inst]
    both = [_dot_nt(jnp.concatenate([q_ref[rows_of(cc), cols_of(h)], k], axis=0).astype(BF16), k.astype(BF16))
            for (cc, h), k in zip(inst, ks)]
    a, r, gcs = [], [], []
    for i, (cc, h) in enumerate(inst):
        gc = g_cum[cc][:, h:h + 1]
        gr = g_cum_t[cc][h:h + 1, :]
        beta = gbs[cc][:, GDN_HEADS + h:GDN_HEADS + h + 1]
        decay = jnp.where(incl, jnp.exp(jnp.where(incl, gc - gr, 0.0)), 0.0)
        a.append(jnp.where(strict, -(beta * both[i][C:2 * C] * decay), 0.0))
        qk = both[i][0:C] * decay
        qk_ref[rows_of(cc), cols_of(h)] = jnp.concatenate(
            [qk, jnp.zeros((C, LANES - C), F32)], axis=1).astype(BF16)
        e_g = jnp.exp(gc)
        wq_ref[cc, C:2 * C, cols_of(h)] = (e_g * q_ref[rows_of(cc), cols_of(h)]).astype(BF16)
        k_dec = jnp.exp(g_cum[cc][C - 1:C, h:h + 1] - gc) * ks[i]
        kdt_ref[cc, h] = jnp.concatenate([k_dec, k_dec], axis=0).T.astype(BF16)
        r.append(jnp.concatenate([(beta * e_g) * ks[i], beta * v_ref[rows_of(cc), cols_of(h)]], axis=1))
    for step in range(n_sq + 1):
        a2 = [jnp.concatenate([x, x], axis=1).astype(BF16) for x in a]
        r = [x + _dot(y, _rows_hi_lo(x)) for x, y in zip(r, a2)]
        if step < n_sq:
            a = [_dot(y, _rows_hi_lo(x)) for x, y in zip(a, a2)]
    for i, (cc, h) in enumerate(inst):
        wq_ref[cc, 0:C, cols_of(h)] = r[i][:, 0:LANES].astype(BF16)
        ut_ref[rows_of(cc), cols_of(h)] = r[i][:, LANES:2 * LANES]


def gdn_prep(q, k, v, gb, *, C, CB):
    rows = q.shape[0]
    nc = rows // C
    blk = lambda i: (i, 0)
    body = functools.partial(_gdn_prep_body, C=C, CB=CB)
    return pl.pallas_call(
        body,
        grid=(nc // CB,),
        in_specs=[pl.BlockSpec((CB * C, GDN_QK_WIDTH), blk)] * 3 + [pl.BlockSpec((CB * C, MISC_W), blk)],
        out_specs=[pl.BlockSpec((CB, 2 * C, GDN_QK_WIDTH), lambda i: (i, 0, 0)),
                   pl.BlockSpec((CB * C, GDN_V_WIDTH), blk),
                   pl.BlockSpec((CB * C, GDN_HEADS * LANES), blk),
                   pl.BlockSpec((CB, GDN_HEADS, GDN_DK, 2 * C), lambda i: (i, 0, 0, 0)),
                   pl.BlockSpec((CB, 8, LANES), lambda i: (i, 0, 0))],
        out_shape=[jax.ShapeDtypeStruct((nc, 2 * C, GDN_QK_WIDTH), BF16),
                   jax.ShapeDtypeStruct((rows, GDN_V_WIDTH), F32),
                   jax.ShapeDtypeStruct((rows, GDN_HEADS * LANES), BF16),
                   jax.ShapeDtypeStruct((nc, GDN_HEADS, GDN_DK, 2 * C), BF16),
                   jax.ShapeDtypeStruct((nc, 8, LANES), F32)],
        compiler_params=_cp(("parallel",)),
        name="gdn_prep",
    )(q, k, v, gb)


def _gdn_scan_body(gl_ref, wq_ref, ut_ref, qk_ref, kdt_ref, s0_ref, o_ref, sfin_ref, s_scr,
                   *, C, CB, n_steps):
    sq = pl.program_id(0)
    st = pl.program_id(1)

    @pl.when(st == 0)
    def _():
        s_scr[...] = s0_ref[0]

    for cc in range(CB):
        rows = slice(cc * C, (cc + 1) * C)
        chunk = (sq * n_steps + st) * CB + cc
        heads = range(GDN_HEADS)
        cols = [slice(h * LANES, (h + 1) * LANES) for h in heads]
        ss = [s_scr[h] for h in heads]
        ws = [_dot(jnp.concatenate([wq_ref[cc, :, cols[h]], wq_ref[cc, :, cols[h]]], axis=1),
                   _rows_hi_lo(ss[h])) for h in heads]
        u2 = [_rows_hi_lo(ut_ref[rows, cols[h]] - ws[h][0:C]) for h in heads]
        for h in heads:
            s_scr[h] = gl_ref[chunk * GDN_HEADS + h] * ss[h] + _dot(kdt_ref[cc, h], u2[h])
        for h in heads:
            o_ref[rows, cols[h]] = ws[h][C:2 * C] + _dot(qk_ref[rows, h * LANES:h * LANES + C], u2[h][0:C])

    @pl.when(st == n_steps - 1)
    def _():
        sfin_ref[0] = s_scr[...]


def gdn_scan(gl, wq, ut, qk, kdt, s0, *, C, CB, n_seq):
    rows = ut.shape[0]
    nc = rows // C
    n_steps = nc // (n_seq * CB)
    blk = lambda s, t, gl: (s * n_steps + t, 0)
    blk3 = lambda s, t, gl: (s * n_steps + t, 0, 0)
    blk4 = lambda s, t, gl: (s * n_steps + t, 0, 0, 0)
    st4 = lambda s, t, gl: (s, 0, 0, 0)
    body = functools.partial(_gdn_scan_body, C=C, CB=CB, n_steps=n_steps)
    grid_spec = pltpu.PrefetchScalarGridSpec(
        num_scalar_prefetch=1,
        grid=(n_seq, n_steps),
        in_specs=[pl.BlockSpec((CB, 2 * C, GDN_QK_WIDTH), blk3), pl.BlockSpec((CB * C, GDN_V_WIDTH), blk),
                  pl.BlockSpec((CB * C, GDN_HEADS * LANES), blk),
                  pl.BlockSpec((CB, GDN_HEADS, GDN_DK, 2 * C), blk4),
                  pl.BlockSpec((1, GDN_HEADS, GDN_DK, GDN_DV), st4)],
        out_specs=[pl.BlockSpec((CB * C, GDN_V_WIDTH), blk),
                   pl.BlockSpec((1, GDN_HEADS, GDN_DK, GDN_DV), st4)],
        scratch_shapes=[pltpu.VMEM((GDN_HEADS, GDN_DK, GDN_DV), F32)],
    )
    return pl.pallas_call(
        body,
        grid_spec=grid_spec,
        out_shape=[jax.ShapeDtypeStruct((rows, GDN_V_WIDTH), F32), jax.ShapeDtypeStruct(s0.shape, F32)],
        compiler_params=_cp(("arbitrary", "arbitrary")),
        name="gdn_scan",
    )(gl, wq, ut, qk, kdt, s0)


def gdn(q, k, v, gb, s0, *, C, CB_prep, CB_scan, n_seq):
    wq, ut, qk, kdt, glv = gdn_prep(q, k, v, gb, C=C, CB=CB_prep)
    gl = glv[:, 0, :GDN_HEADS].reshape(-1)
    return gdn_scan(gl, wq, ut, qk, kdt, s0, C=C, CB=CB_scan, n_seq=n_seq)


def _rope(y, cosb, sin_lo, sin_hi):
    return (y * cosb + pltpu.roll(y, LANES - PE_HALF, 1) * sin_lo + pltpu.roll(y, PE_HALF, 1) * sin_hi)


def _slab_norm(x, gain, lane):
    x2 = x * x
    is_nope = lane < NOPE_DIM
    is_pe = (lane >= PE_LO) & (lane < PE_LO + ROPE_DIM)
    ms_n = jnp.sum(jnp.where(is_nope, x2, 0.0), axis=-1, keepdims=True) * (1.0 / NOPE_DIM)
    ms_p = jnp.sum(jnp.where(is_pe, x2, 0.0), axis=-1, keepdims=True) * (1.0 / ROPE_DIM)
    rinv = jnp.where(is_nope, lax.rsqrt(ms_n + EPS), lax.rsqrt(ms_p + EPS))
    return (x * rinv) * gain


def _mla_body(cq_ref, ckvr_ref, misc_ref, cos_ref, slo_ref, shi_ref, qan_ref, wqb_ref, qg_ref,
              kvn_ref, kg_ref, *rest, with_kv):
    if with_kv:
        wuk_ref, wuvt_ref, ckv_ref, kpe_ref, q_ref, k_ref, vt_ref = rest
    else:
        ckv_ref, kpe_ref, q_ref = rest
    cosb, slo, shi = cos_ref[...], slo_ref[...], shi_ref[...]
    lane = lax.broadcasted_iota(jnp.int32, cosb.shape, 1)
    cq = cq_ref[...]
    cqn = ((cq * lax.rsqrt(jnp.mean(cq * cq, axis=-1, keepdims=True) + EPS)) * qan_ref[...]).astype(BF16)
    c = ckvr_ref[...]
    ckv = (c * lax.rsqrt(jnp.mean(c * c, axis=-1, keepdims=True) + EPS)) * kvn_ref[...]
    ckv_ref[...] = ckv
    ckv_b = ckv.astype(BF16)
    kpe_raw = jnp.where((lane >= PE_LO) & (lane < PE_LO + ROPE_DIM), misc_ref[...], 0.0)
    kpe = _rope(_slab_norm(kpe_raw, kg_ref[...], lane), cosb, slo, shi)
    kpe = jnp.where(lane >= PE_LO, kpe, 0.0)
    kpe_ref[...] = kpe
    for h in range(MLA_HEADS):
        cols = slice(h * HEAD_PAD, (h + 1) * HEAD_PAD)
        qh = _dot(cqn, wqb_ref[:, cols])
        qh = _rope(_slab_norm(qh, qg_ref[...], lane), cosb, slo, shi)
        q_ref[h] = (qh * SCORE_SCALE_LOG2).astype(BF16)
        if with_kv:
            kh = _dot(ckv_b, wuk_ref[:, cols])
            k_ref[h] = (_slab_norm(kh, kg_ref[...], lane) + kpe).astype(BF16)
            vt = _dot_nt(wuvt_ref[h], ckv_b)
            ones_row = lax.broadcasted_iota(jnp.int32, vt.shape, 0) == V_DIM
            vt_ref[h] = jnp.where(ones_row, 1.0, vt).astype(BF16)


def mla_prep(cq, ckv_raw, misc, rope_tabs, p, tm, with_kv):
    m = cq.shape[0]
    row = lambda i: (i, 0)
    fixed = lambda i: (0, 0)
    hw = MLA_HEADS * HEAD_PAD
    in_specs = ([pl.BlockSpec((tm, Q_RANK), row), pl.BlockSpec((tm, KV_RANK), row),
                 pl.BlockSpec((tm, MISC_W), row)] + [pl.BlockSpec((tm, LANES), row)] * 3
                + [pl.BlockSpec((1, Q_RANK), fixed), pl.BlockSpec((Q_RANK, hw), fixed),
                   pl.BlockSpec((1, LANES), fixed), pl.BlockSpec((1, KV_RANK), fixed),
                   pl.BlockSpec((1, LANES), fixed)])
    args = [cq, ckv_raw, misc, *rope_tabs, p["q_a_norm"], p["w_q_b"], p["q_gain"], p["kv_a_norm"], p["k_gain"]]
    out_specs = [pl.BlockSpec((tm, KV_RANK), row), pl.BlockSpec((tm, LANES), row),
                 pl.BlockSpec((MLA_HEADS, tm, HEAD_PAD), lambda i: (0, i, 0))]
    out_shape = [jax.ShapeDtypeStruct((m, KV_RANK), F32), jax.ShapeDtypeStruct((m, LANES), F32),
                 jax.ShapeDtypeStruct((MLA_HEADS, m, HEAD_PAD), BF16)]
    if with_kv:
        in_specs += [pl.BlockSpec((KV_RANK, hw), fixed),
                     pl.BlockSpec((MLA_HEADS, VT_ROWS, KV_RANK), lambda i: (0, 0, 0))]
        args += [p["w_uk"], p["w_uv_t"]]
        out_specs += [pl.BlockSpec((MLA_HEADS, tm, HEAD_PAD), lambda i: (0, i, 0)),
                      pl.BlockSpec((MLA_HEADS, VT_ROWS, tm), lambda i: (0, 0, i))]
        out_shape += [jax.ShapeDtypeStruct((MLA_HEADS, m, HEAD_PAD), BF16),
                      jax.ShapeDtypeStruct((MLA_HEADS, VT_ROWS, m), BF16)]
    return pl.pallas_call(
        functools.partial(_mla_body, with_kv=with_kv),
        grid=(m // tm,),
        in_specs=in_specs,
        out_specs=out_specs,
        out_shape=out_shape,
        compiler_params=_cp(("parallel",)),
        name="mla_prep",
    )(*args)


def _flash_body(qmap_ref, kmap_ref, q_ref, k_ref, vt_ref, kpre_ref, vtpre_ref, o_ref, m_scr, acc_scr,
                *, tq, tk, q_block0):
    step = pl.program_id(0)
    q0 = (qmap_ref[step] + q_block0) * tq
    k0 = kmap_ref[step] * tk

    @pl.when(k0 == P0)
    def _():
        sts = [_dot_nt(kpre_ref[h], q_ref[h]) for h in range(MLA_HEADS)]
        m0s = [jnp.max(st, axis=0, keepdims=True) for st in sts]
        for h in range(MLA_HEADS):
            m_scr[h] = m0s[h]
            acc_scr[h] = _dot(vtpre_ref[h], jnp.exp2(sts[h] - m0s[h]).astype(BF16))

    def update(masked):
        if masked:
            kpos = k0 + lax.broadcasted_iota(jnp.int32, (tk, tq), 0)
            qpos = q0 + lax.broadcasted_iota(jnp.int32, (tk, tq), 1)
            valid = kpos <= qpos
        pending = [_dot_nt(k_ref[h], q_ref[h]) for h in range(FLASH_AHEAD)]
        for h in range(MLA_HEADS):
            st = pending.pop(0)
            if h + FLASH_AHEAD < MLA_HEADS:
                pending.append(_dot_nt(k_ref[h + FLASH_AHEAD], q_ref[h + FLASH_AHEAD]))
            if masked:
                st = jnp.where(valid, st, NEG_INF)
            m_prev = m_scr[h]
            m_new = jnp.maximum(m_prev, jnp.max(st, axis=0, keepdims=True))
            e = jnp.exp2(st - m_new).astype(BF16)
            acc_scr[h] = jnp.exp2(m_prev - m_new) * acc_scr[h] + _dot(vt_ref[h], e)
            m_scr[h] = m_new

    needs_mask = k0 + tk - 1 > q0
    pl.when(needs_mask)(lambda: update(True))
    pl.when(jnp.logical_not(needs_mask))(lambda: update(False))

    @pl.when(k0 + tk >= q0 + tq)
    def _():
        for h in range(MLA_HEADS):
            a = acc_scr[h]
            o = (a[0:V_DIM] * (1.0 / a[V_DIM:V_DIM + 1, :])).T
            o_ref[:, h * HEAD_PAD:(h + 1) * HEAD_PAD] = jnp.concatenate(
                [o, jnp.zeros((tq, HEAD_PAD - V_DIM), F32)], axis=1).astype(BF16)


def flash_prompt(q, k, vt, *, tq, tk, n_prefix):
    tp = q.shape[1]
    q_block0 = P0 // tq
    nq = (tp - P0) // tq
    qs, ks = [], []
    for qi in range(nq):
        last_k = ((qi + q_block0 + 1) * tq - 1) // tk
        for ki in range(P0 // tk, last_k + 1):
            qs.append(qi)
            ks.append(ki)
    qmap = jnp.asarray(qs, jnp.int32)
    kmap = jnp.asarray(ks, jnp.int32)
    k_pre = k[:, P0 - n_prefix:P0]
    vt_pre = vt[:, :, P0 - n_prefix:P0]
    body = functools.partial(_flash_body, tq=tq, tk=tk, q_block0=q_block0)
    whole = lambda s, qm, km: (0, 0, 0)
    grid_spec = pltpu.PrefetchScalarGridSpec(
        num_scalar_prefetch=2,
        grid=(len(qs),),
        in_specs=[pl.BlockSpec((MLA_HEADS, tq, HEAD_PAD), lambda s, qm, km: (0, qm[s] + q_block0, 0)),
                  pl.BlockSpec((MLA_HEADS, tk, HEAD_PAD), lambda s, qm, km: (0, km[s], 0)),
                  pl.BlockSpec((MLA_HEADS, VT_ROWS, tk), lambda s, qm, km: (0, 0, km[s])),
                  pl.BlockSpec((MLA_HEADS, n_prefix, HEAD_PAD), whole),
                  pl.BlockSpec((MLA_HEADS, VT_ROWS, n_prefix), whole)],
        out_specs=pl.BlockSpec((tq, MLA_HEADS * HEAD_PAD), lambda s, qm, km: (qm[s], 0)),
        scratch_shapes=[pltpu.VMEM((MLA_HEADS, 1, tq), F32), pltpu.VMEM((MLA_HEADS, VT_ROWS, tq), F32)],
    )
    return pl.pallas_call(
        body,
        grid_spec=grid_spec,
        out_shape=jax.ShapeDtypeStruct((tp - P0, MLA_HEADS * HEAD_PAD), BF16),
        compiler_params=_cp(("arbitrary",)),
        name="flash_prompt",
    )(qmap, kmap, q, k, vt, k_pre, vt_pre)


def _qabs_body(q_ref, w_ref, o_ref):
    o_ref[0] = _dot(q_ref[0], w_ref[0]).astype(BF16)


def q_absorb(q, w_abs):
    h, m, _ = q.shape
    blk = lambda i: (i, 0, 0)
    return pl.pallas_call(
        _qabs_body,
        grid=(h,),
        in_specs=[pl.BlockSpec((1, m, HEAD_PAD), blk), pl.BlockSpec((1, HEAD_PAD, KV_RANK), blk)],
        out_specs=pl.BlockSpec((1, m, KV_RANK), blk),
        out_shape=jax.ShapeDtypeStruct((h, m, KV_RANK), BF16),
        compiler_params=_cp(("parallel",)),
        name="q_absorb",
    )(q, w_abs)


def _decode_scores(waug, qpe, ckv_b, kpe_t, n_lq):
    kp = _dot_nt(waug, ckv_b)
    n_k = MLA_HEADS * NOPE_DIM
    kproj = kp[:n_k]
    ms = jnp.sum((kproj * kproj).reshape(MLA_HEADS, NOPE_DIM, kproj.shape[-1]), axis=1) * (1.0 / NOPE_DIM)
    rinv = lax.rsqrt(ms + EPS)
    s_nope = kp[n_k:]
    s_pe = _dot(qpe, kpe_t)
    rinv_rows = jnp.concatenate([rinv] * n_lq, axis=0)
    return s_nope * rinv_rows + s_pe


def _page_copies(pt_ref, ckv_hbm, kpet_hbm, ckv_buf, kpe_buf, sem, step, slot, *, PG, page):
    out = []
    for i in range(PG):
        pid = pt_ref[step * PG + i]
        out.append(pltpu.make_async_copy(ckv_hbm.at[0, pid], ckv_buf.at[slot, pl.ds(i * page, page)],
                                         sem.at[0, slot]))
        out.append(pltpu.make_async_copy(kpet_hbm.at[0, pid], kpe_buf.at[slot, :, pl.ds(i * page, page)],
                                         sem.at[1, slot]))
    return out


def _decode_body(pt_ref, wukt_ref, qabs_ref, qpe_ref, nckv_ref, nkpet_ref, ckv_hbm, kpet_hbm, o_ref,
                 ckv_buf, kpe_buf, sem, waug, ckvb, m_scr, l_scr, acc_scr, *, PG, L, n_groups, n_steps, page):
    s = pl.program_id(0)
    j = lax.rem(s, n_groups)
    slot = lax.rem(s, 2)
    n_k = MLA_HEADS * NOPE_DIM
    R = MLA_HEADS * L
    copies = functools.partial(_page_copies, pt_ref, ckv_hbm, kpet_hbm, ckv_buf, kpe_buf, sem, PG=PG, page=page)

    nxt = lax.rem(s + 1, n_steps)

    @pl.when(s == 0)
    def _():
        for c in copies(0, 0):
            c.start()

    @pl.when(j == 0)
    def _():
        waug[0:n_k, :] = wukt_ref[...]
        waug[n_k:n_k + R, :] = qabs_ref[0]
        c_new = nckv_ref[0].astype(BF16)
        sc = _decode_scores(waug[...], qpe_ref[0], c_new, nkpet_ref[0].astype(BF16), L)
        r_l = lax.shift_right_logical(lax.broadcasted_iota(jnp.int32, sc.shape, 0),
                                      int(math.log2(MLA_HEADS)))
        t = lax.broadcasted_iota(jnp.int32, sc.shape, 1)
        sc = jnp.where((t <= r_l) & (t < L), sc, NEG_INF)
        m0 = jnp.max(sc, axis=-1, keepdims=True)
        e0 = jnp.exp2(sc - m0)
        m_scr[...] = m0
        l_scr[...] = jnp.sum(e0, axis=-1, keepdims=True)
        acc_scr[...] = _dot(e0.astype(BF16), c_new)

    for c in copies(s, slot):
        c.wait()

    w = waug[...]
    qpe = qpe_ref[0]

    def scores(p):
        tok = pl.ds(p * DECODE_SCORE_PAGES * page, DECODE_SCORE_PAGES * page)
        c2 = ckv_buf[slot, tok, :].astype(BF16)
        ckvb[p] = c2
        return _decode_scores(w, qpe, c2, kpe_buf[slot, :, tok].astype(BF16), L)

    ms, ls, accs = [m_scr[...]], [l_scr[...]], [acc_scr[...]]
    pending = [scores(0)]
    next_copies = copies(nxt, 1 - slot)
    n_groups_step = PG // DECODE_SCORE_PAGES
    per_group = -(-len(next_copies) // max(n_groups_step // 4, 1))
    for p in range(n_groups_step):
        sc = pending.pop(0)
        if p + 1 < PG // DECODE_SCORE_PAGES:
            pending.append(scores(p + 1))
        for c in next_copies[p * per_group:(p + 1) * per_group]:
            c.start()
        m_p = jnp.max(sc, axis=-1, keepdims=True)
        e = jnp.exp2(sc - m_p)
        ms.append(m_p)
        ls.append(jnp.sum(e, axis=-1, keepdims=True))
        accs.append(_dot(e.astype(BF16), ckvb[p]))
    m = functools.reduce(jnp.maximum, ms)
    fs = [jnp.exp2(x - m) for x in ms]
    l = functools.reduce(lambda x, y: x + y, [f * x for f, x in zip(fs, ls)])
    acc = functools.reduce(lambda x, y: x + y, [f * x for f, x in zip(fs, accs)])
    m_scr[...], l_scr[...], acc_scr[...] = m, l, acc

    @pl.when(j == n_groups - 1)
    def _():
        o_ref[0] = acc / l

    @pl.when(s == n_steps - 1)
    def _():
        for c in copies(nxt, 1 - slot):
            c.wait()


def decode_attn(page_table, wuk_t, qabs, qpe, new_ckv, new_kpe_t, cache_ckv, cache_kpe_t, *, PG, L):
    b, n_pages = page_table.shape
    page = cache_ckv.shape[2]
    R = MLA_HEADS * L
    n_k = MLA_HEADS * NOPE_DIM
    n_groups = n_pages // PG
    per_b = lambda s, pt: (s // n_groups, 0, 0)
    fixed = lambda s, pt: (0, 0)
    body = functools.partial(_decode_body, PG=PG, L=L, n_groups=n_groups, n_steps=b * n_groups, page=page)
    grid_spec = pltpu.PrefetchScalarGridSpec(
        num_scalar_prefetch=1,
        grid=(b * n_groups,),
        in_specs=[pl.BlockSpec((n_k, KV_RANK), fixed), pl.BlockSpec((1, R, KV_RANK), per_b),
                  pl.BlockSpec((1, R, ROPE_DIM), per_b), pl.BlockSpec((1, 8, KV_RANK), per_b),
                  pl.BlockSpec((1, ROPE_DIM, 8), per_b),
                  pl.BlockSpec(memory_space=pl.ANY), pl.BlockSpec(memory_space=pl.ANY)],
        out_specs=pl.BlockSpec((1, R, KV_RANK), per_b),
        scratch_shapes=[pltpu.VMEM((2, PG * page, KV_RANK), F32), pltpu.VMEM((2, ROPE_DIM, PG * page), F32),
                        pltpu.SemaphoreType.DMA((2, 2)),
                        pltpu.VMEM((n_k + R, KV_RANK), BF16), pltpu.VMEM((PG // DECODE_SCORE_PAGES, DECODE_SCORE_PAGES * page, KV_RANK), BF16),
                        pltpu.VMEM((R, 1), F32), pltpu.VMEM((R, 1), F32), pltpu.VMEM((R, KV_RANK), F32)],
    )
    return pl.pallas_call(
        body,
        grid_spec=grid_spec,
        out_shape=jax.ShapeDtypeStruct((b, R, KV_RANK), F32),
        compiler_params=_cp(("arbitrary",)),
        name="decode_attn",
    )(page_table.reshape(-1), wuk_t, qabs, qpe, new_ckv, new_kpe_t, cache_ckv, cache_kpe_t)


def _latent_out_body(x_ref, w_ref, o_ref):
    o_ref[...] = _dot(x_ref[0].astype(BF16), w_ref[0]).astype(BF16)


def latent_out(o_lat, w_uv_h):
    h, m, _ = o_lat.shape
    return pl.pallas_call(
        _latent_out_body,
        grid=(h,),
        in_specs=[pl.BlockSpec((1, m, KV_RANK), lambda i: (i, 0, 0)),
                  pl.BlockSpec((1, KV_RANK, HEAD_PAD), lambda i: (i, 0, 0))],
        out_specs=pl.BlockSpec((m, HEAD_PAD), lambda i: (0, i)),
        out_shape=jax.ShapeDtypeStruct((m, h * HEAD_PAD), BF16),
        compiler_params=_cp(("parallel",)),
        name="latent_out",
    )(o_lat, w_uv_h)


def _mix_body(h_ref, o_ref, z_ref, mla_ref, gg_ref, wog_ref, wom_ref, n2_ref, h1_ref, hn_ref):
    acc = h_ref[...] + _dot(mla_ref[...], wom_ref[...])
    for h in range(GDN_HEADS):
        cols = slice(h * LANES, (h + 1) * LANES)
        o = o_ref[:, cols]
        z = z_ref[:, cols]
        y = ((o * lax.rsqrt(jnp.mean(o * o, axis=-1, keepdims=True) + EPS)) * gg_ref[...]) * (z * _sigmoid(z))
        acc = acc + _dot(y.astype(BF16), wog_ref[cols, :])
    h1_ref[...] = acc
    hn_ref[...] = ((acc * lax.rsqrt(jnp.mean(acc * acc, axis=-1, keepdims=True) + EPS)) * n2_ref[...]).astype(BF16)


def mix_out(h, gdn_o, z, mla_o, p, tm, row_block0):
    m, d = h.shape
    row = lambda i: (i, 0)
    prow = lambda i: (i + row_block0, 0)
    fixed = lambda i: (0, 0)
    hw = MLA_HEADS * HEAD_PAD
    return pl.pallas_call(
        _mix_body,
        grid=(m // tm,),
        in_specs=[pl.BlockSpec((tm, d), row), pl.BlockSpec((tm, GDN_V_WIDTH), prow),
                  pl.BlockSpec((tm, GDN_V_WIDTH), prow), pl.BlockSpec((tm, hw), row),
                  pl.BlockSpec((1, LANES), fixed), pl.BlockSpec((GDN_V_WIDTH, d), fixed),
                  pl.BlockSpec((hw, d), fixed), pl.BlockSpec((1, d), fixed)],
        out_specs=[pl.BlockSpec((tm, d), row), pl.BlockSpec((tm, d), row)],
        out_shape=[jax.ShapeDtypeStruct((m, d), F32), jax.ShapeDtypeStruct((m, d), BF16)],
        compiler_params=_cp(("parallel",)),
        name="mix_out",
    )(h, gdn_o, z, mla_o, p["gdn_norm"], p["w_o_gdn"], p["w_o_mla"], p["norm2"])


def _ffn_body(h1_ref, hn_ref, wg_ref, wu_ref, wd_ref, y_ref):
    f = pl.program_id(1)

    @pl.when(f == 0)
    def _():
        y_ref[...] = h1_ref[...]

    x = hn_ref[...]
    g = _dot(x, wg_ref[...])
    u = _dot(x, wu_ref[...])
    y_ref[...] += _dot(((g * _sigmoid(g)) * u).astype(BF16), wd_ref[...])


def ffn(h1, hn, wg, wu, wd, tm, tf):
    m, d = h1.shape
    dff = wg.shape[1]
    row = lambda i, f: (i, 0)
    return pl.pallas_call(
        _ffn_body,
        grid=(m // tm, dff // tf),
        in_specs=[pl.BlockSpec((tm, d), row), pl.BlockSpec((tm, d), row),
                  pl.BlockSpec((d, tf), lambda i, f: (0, f)), pl.BlockSpec((d, tf), lambda i, f: (0, f)),
                  pl.BlockSpec((tf, d), lambda i, f: (f, 0))],
        out_specs=pl.BlockSpec((tm, d), row),
        out_shape=jax.ShapeDtypeStruct((m, d), F32),
        compiler_params=_cp(("parallel", "arbitrary")),
        name="ffn",
    )(h1, hn, wg, wu, wd)


def _head_slabs(w, width):
    k, h, _ = w.shape
    return jnp.pad(w, ((0, 0), (0, 0), (0, HEAD_PAD - width))).reshape(k, h * HEAD_PAD)


def _pad_lanes(v, lo=0):
    return jnp.pad(v.astype(F32), (lo, LANES - lo - v.shape[0])).reshape(1, LANES)


def _prep_params(norm1, w_in, conv_w, a_log, dt_bias, gdn_norm, q_a_norm, w_q_b, q_nope_norm,
                 q_pe_norm, kv_a_norm, k_pe_norm, w_uk, w_uv, k_nope_norm, w_o, norm2,
                 w_gate, w_up, w_down):
    d = w_in.shape[0]
    o = 0
    cuts = {}
    for name, wdt in (("qkv", CONV_DIM), ("z", GDN_V_WIDTH), ("a", GDN_HEADS), ("b", GDN_HEADS),
                      ("cq", Q_RANK), ("ckv", KV_RANK), ("kpe", ROPE_DIM)):
        cuts[name] = w_in[:, o:o + wdt]
        o += wdt
    zeros = lambda n: jnp.zeros((d, n), w_in.dtype)
    misc = jnp.concatenate([cuts["a"], cuts["b"], zeros(PE_LO - 2 * GDN_HEADS), cuts["kpe"],
                            zeros(MISC_W - PE_LO - ROPE_DIM)], axis=1)
    w_r = jnp.concatenate([cuts["qkv"], cuts["z"], cuts["cq"], cuts["ckv"], misc], axis=1).astype(BF16)
    wqb = w_q_b.reshape(Q_RANK, MLA_HEADS, QK_DIM)
    gain_q = jnp.concatenate([q_nope_norm, q_pe_norm]).astype(F32)
    gain_k = jnp.concatenate([k_nope_norm, k_pe_norm]).astype(F32)
    w_uk_hdr = jnp.transpose(w_uk, (1, 2, 0))
    w_abs = jnp.pad(w_uk_hdr * k_nope_norm[None, :, None], ((0, 0), (0, HEAD_PAD - NOPE_DIM), (0, 0)))
    w_o_mla = jnp.pad(w_o[GDN_V_WIDTH:].reshape(MLA_HEADS, V_DIM, d),
                      ((0, 0), (0, HEAD_PAD - V_DIM), (0, 0))).reshape(MLA_HEADS * HEAD_PAD, d)
    return dict(
        norm1=norm1, w_in=w_r, conv_w=conv_w.astype(F32),
        a_log=_pad_lanes(a_log), dt_bias=_pad_lanes(dt_bias),
        gdn_norm=gdn_norm.reshape(1, GDN_DV).astype(F32),
        q_a_norm=q_a_norm.reshape(1, Q_RANK).astype(F32), w_q_b=_head_slabs(wqb, QK_DIM).astype(BF16),
        q_gain=_pad_lanes(gain_q), kv_a_norm=kv_a_norm.reshape(1, KV_RANK).astype(F32),
        w_uk=_head_slabs(w_uk, NOPE_DIM).astype(BF16), k_gain=_pad_lanes(gain_k),
        w_uv_t=jnp.pad(jnp.transpose(w_uv, (1, 2, 0)), ((0, 0), (0, VT_ROWS - V_DIM), (0, 0))).astype(BF16),
        w_uk_t=w_uk_hdr.reshape(MLA_HEADS * NOPE_DIM, KV_RANK).astype(BF16),
        w_abs=w_abs.astype(BF16),
        w_uv_h=jnp.pad(jnp.transpose(w_uv, (1, 0, 2)), ((0, 0), (0, 0), (0, HEAD_PAD - V_DIM))).astype(BF16),
        w_o_gdn=w_o[:GDN_V_WIDTH].astype(BF16), w_o_mla=w_o_mla.astype(BF16),
        norm2=norm2.reshape(1, d).astype(F32),
        w_gate=w_gate.astype(BF16), w_up=w_up.astype(BF16), w_down=w_down.astype(BF16),
    )


def _rope_tables(pos):
    inv_freq = np.power(ROPE_THETA, -np.arange(0, ROPE_DIM, 2, dtype=np.float64) / ROPE_DIM)
    ang = np.asarray(pos, np.float64)[:, None] * inv_freq[None, :]
    cos, sin = jnp.asarray(np.cos(ang), F32), jnp.asarray(np.sin(ang), F32)
    n = len(pos)
    ones = jnp.ones((n, PE_LO), F32)
    z = lambda w: jnp.zeros((n, w), F32)
    tail = LANES - PE_LO - ROPE_DIM
    cosb = jnp.concatenate([ones, cos, cos, z(tail)], axis=1)
    sin_lo = jnp.concatenate([z(PE_LO), -sin, z(PE_HALF + tail)], axis=1)
    sin_hi = jnp.concatenate([z(PE_LO + PE_HALF), sin, z(tail)], axis=1)
    return cosb, sin_lo, sin_hi


def _ff_tile(dff):
    for t in (1408, 1024, 512, 256, 128):
        if dff % t == 0:
            return t
    return dff


def _prompt_group(x_prompt, meta_tokens, p):
    seq, d = x_prompt.shape
    tp = P0 + seq
    meta_row0 = P0 - N_META
    lead = jnp.concatenate([jnp.zeros((meta_row0, d), F32), meta_tokens.astype(F32)], axis=0)
    q, k, v, gb, pre_tail, z, cq, ckv_raw, misc = front_conv_prompt(
        x_prompt, lead, p["norm1"], p["w_in"], p["conv_w"], p["a_log"], p["dt_bias"], tm=P0,
        first_valid_row=meta_row0)
    s0 = jnp.zeros((1, GDN_HEADS, GDN_DK, GDN_DV), F32)
    gdn_o, s_fin = gdn(q, k, v, gb, s0, C=CHUNK, CB_prep=4, CB_scan=8, n_seq=1)
    tabs = _rope_tables(np.arange(tp) - meta_row0)
    ckv, kpe, qh, kh, vt = mla_prep(cq, ckv_raw, misc, tabs, p, tm=256, with_kv=True)
    mla_o = flash_prompt(qh, kh, vt, tq=512, tk=512, n_prefix=N_META)
    h1, hn2 = mix_out(x_prompt, gdn_o, z, mla_o, p, tm=512, row_block0=P0 // 512)
    y = ffn(h1, hn2, p["w_gate"], p["w_up"], p["w_down"], tm=512, tf=_ff_tile(p["w_gate"].shape[1]))
    new_conv = pre_tail[8 - (CONV_WIDTH - 1):]
    return (y, new_conv, s_fin[0], ckv[meta_row0:], kpe[meta_row0:, PE_LO:PE_LO + ROPE_DIM])


def _sample_group(x_sample, cache_conv, state_gdn, cache_ckv, cache_kpe, page_table, p):
    b, L, d = x_sample.shape
    m = b * L
    page = cache_ckv.shape[2]
    past_len = page_table.shape[1] * page
    x = x_sample.reshape(m, d)
    tm = min(m, 256)
    qkv_pre, z, cq, ckv_raw, misc = front_proj(x, p["norm1"], p["w_in"], tm=tm)
    xp = jnp.concatenate([jnp.transpose(cache_conv.astype(F32), (1, 0, 2)),
                          jnp.transpose(qkv_pre.reshape(b, L, CONV_DIM), (1, 0, 2))], axis=0)
    misc_t = jnp.transpose(misc.reshape(b, L, MISC_W), (1, 0, 2))
    q, k, v, gb = conv_sample(xp, misc_t, p["conv_w"], p["a_log"], p["dt_bias"])
    C = 16
    to_chunks = lambda t: jnp.pad(jnp.transpose(t, (1, 0, 2)), ((0, 0), (0, C - L), (0, 0))).reshape(b * C, -1)
    gdn_o, s_fin = gdn(to_chunks(q), to_chunks(k), to_chunks(v), to_chunks(gb), state_gdn.astype(F32),
                       C=C, CB_prep=4 if b % 4 == 0 else 1, CB_scan=1, n_seq=b)
    gdn_o = gdn_o.reshape(b, C, GDN_V_WIDTH)[:, :L].reshape(m, GDN_V_WIDTH)
    tabs = _rope_tables(np.tile(past_len + np.arange(L), b))
    ckv, kpe, qh = mla_prep(cq, ckv_raw, misc, tabs, p, tm=tm, with_kv=False)
    kpe32 = kpe[:, PE_LO:PE_LO + ROPE_DIM]
    qabs = q_absorb(qh, p["w_abs"])
    R = L * MLA_HEADS
    rows_lh = lambda t: jnp.transpose(t.reshape(MLA_HEADS, b, L, -1), (1, 2, 0, 3)).reshape(b, R, -1)
    qpe = rows_lh(qh[:, :, PE_LO:PE_LO + ROPE_DIM])
    new_ckv = jnp.pad(ckv.reshape(b, L, KV_RANK), ((0, 0), (0, 8 - L), (0, 0)))
    new_kpe_t = jnp.pad(jnp.swapaxes(kpe32.reshape(b, L, ROPE_DIM), 1, 2), ((0, 0), (0, 0), (0, 8 - L)))
    n_pages = page_table.shape[1]
    PG = 64 if n_pages % 64 == 0 else DECODE_SCORE_PAGES
    o_lat = decode_attn(page_table, p["w_uk_t"], rows_lh(qabs), qpe, new_ckv, new_kpe_t,
                        cache_ckv, jnp.swapaxes(cache_kpe, 2, 3), PG=PG, L=L)
    o_lat_h = jnp.transpose(o_lat.reshape(b, L, MLA_HEADS, KV_RANK), (2, 0, 1, 3)).reshape(MLA_HEADS, m, KV_RANK)
    mla_o = latent_out(o_lat_h, p["w_uv_h"])
    h1, hn2 = mix_out(x, gdn_o, z, mla_o, p, tm=tm, row_block0=0)
    y = ffn(h1, hn2, p["w_gate"], p["w_up"], p["w_down"], tm=tm, tf=_ff_tile(p["w_gate"].shape[1]))
    new_conv = qkv_pre.reshape(b, L, CONV_DIM)[:, L - (CONV_WIDTH - 1):]
    return (y.reshape(b, L, d), new_conv, s_fin, ckv.reshape(b, L, KV_RANK), kpe32.reshape(b, L, ROPE_DIM))


def kernel(x_prompt, x_sample, cache_conv, state_gdn, cache_ckv, cache_kpe, page_table, meta_tokens, norm1, w_in, conv_w, a_log, dt_bias, gdn_norm, q_a_norm, w_q_b, q_nope_norm, q_pe_norm, kv_a_norm, k_pe_norm, w_uk, w_uv, k_nope_norm, w_o, norm2, w_gate, w_up, w_down):
    assert x_prompt.shape[0] == 1 and norm1.shape[0] == 1, "one prompt sequence, one layer"
    p = _prep_params(norm1[0], w_in[0], conv_w[0], a_log[0], dt_bias[0], gdn_norm[0], q_a_norm[0],
                     w_q_b[0], q_nope_norm[0], q_pe_norm[0], kv_a_norm[0], k_pe_norm[0], w_uk[0],
                     w_uv[0], k_nope_norm[0], w_o[0], norm2[0], w_gate[0], w_up[0], w_down[0])
    y_p, conv_p, st_p, ckv_p, kpe_p = _prompt_group(x_prompt[0], meta_tokens, p)
    y_s, conv_s, st_s, ckv_s, kpe_s = _sample_group(x_sample, cache_conv[0], state_gdn[0], cache_ckv,
                                                     cache_kpe, page_table, p)
    lead = lambda t: t[None, None]
    return (y_p[None], y_s, lead(conv_p), lead(st_p), lead(ckv_p), lead(kpe_p),
            conv_s[None], st_s[None], ckv_s[None], kpe_s[None])
```

```python
import functools
import math

import jax
import jax.numpy as jnp
import numpy as np
from jax import lax
from jax.experimental import pallas as pl
from jax.experimental.pallas import tpu as pltpu

F32 = jnp.float32
BF16 = jnp.bfloat16
HIGHEST = lax.Precision.HIGHEST

N_META = 16
EPS = 1e-6
NEG_INF = -1e30
GDN_HEADS = 4
GDN_DK = 128
GDN_DV = 128
GDN_QK_WIDTH = GDN_HEADS * GDN_DK
GDN_V_WIDTH = GDN_HEADS * GDN_DV
CONV_WIDTH = 4
CONV_DIM = 2 * GDN_QK_WIDTH + GDN_V_WIDTH
CHUNK = 64
MLA_HEADS = 8
Q_RANK = 384
KV_RANK = 256
NOPE_DIM = 64
ROPE_DIM = 32
V_DIM = 64
QK_DIM = NOPE_DIM + ROPE_DIM
ROPE_THETA = 10000.0
SCORE_SCALE_LOG2 = QK_DIM ** -0.5 * math.log2(math.e)

LANES = 128
HEAD_PAD = LANES
PE_LO = NOPE_DIM
PE_HALF = ROPE_DIM // 2
MISC_W = LANES
P0 = 512
FLASH_QPARTS = 2
FLASH_AHEAD = 4
VT_ROWS = V_DIM + 16
DECODE_SCORE_PAGES = 4
VMEM_LIMIT = 56 * 1024 * 1024


def _cp(sem, vmem=None):
    return pltpu.CompilerParams(dimension_semantics=sem, vmem_limit_bytes=vmem or VMEM_LIMIT)


def _sigmoid(x):
    return 1.0 / (1.0 + jnp.exp(-x))


def _softplus(x):
    return jnp.maximum(x, 0.0) + jnp.log1p(jnp.exp(-jnp.abs(x)))


def _dot(a, b, precision=None):
    return jnp.dot(a, b, preferred_element_type=F32, precision=precision)


def _dot_nt(a, b, precision=None):
    return lax.dot_general(a, b, (((1,), (1,)), ((), ())), preferred_element_type=F32,
                           precision=precision)


W_SPLITS = (CONV_DIM, GDN_V_WIDTH, Q_RANK, KV_RANK, MISC_W)


def _front_body(x_ref, g_ref, w_ref, *outs):
    x = x_ref[...]
    hn = ((x * lax.rsqrt(jnp.mean(x * x, axis=-1, keepdims=True) + EPS)) * g_ref[...]).astype(BF16)
    off = 0
    for width, o_ref in zip(W_SPLITS, outs):
        o_ref[...] = _dot(hn, w_ref[:, off:off + width])
        off += width


def front_proj(x, norm1, w_r, tm):
    m, d = x.shape
    n_all = sum(W_SPLITS)
    row = lambda i: (i, 0)
    fixed = lambda i: (0, 0)
    return pl.pallas_call(
        _front_body,
        grid=(m // tm,),
        in_specs=[pl.BlockSpec((tm, d), row), pl.BlockSpec((1, d), fixed), pl.BlockSpec((d, n_all), fixed)],
        out_specs=[pl.BlockSpec((tm, w), row) for w in W_SPLITS],
        out_shape=[jax.ShapeDtypeStruct((m, w), F32) for w in W_SPLITS],
        compiler_params=_cp(("parallel",)),
        name="front_proj",
    )(x, norm1.reshape(1, d), w_r)


def _gates(misc, alog, dtb):
    lane = lax.broadcasted_iota(jnp.int32, misc.shape, 1)
    g = -jnp.exp(alog) * _softplus(misc + dtb)
    beta = _sigmoid(misc)
    return jnp.where(lane < GDN_HEADS, g, jnp.where(lane < 2 * GDN_HEADS, beta, 0.0))


def _conv_cols(taps, w_ref, c, q_ref, k_ref, v_ref):
    cols = slice(c * LANES, (c + 1) * LANES)
    acc = taps(0, cols) * w_ref[0:1, cols]
    for j in range(1, CONV_WIDTH):
        acc = acc + taps(j, cols) * w_ref[j:j + 1, cols]
    y = acc * _sigmoid(acc)
    if c < 2 * GDN_HEADS:
        yn = y * lax.rsqrt(jnp.sum(y * y, axis=-1, keepdims=True) + EPS)
        if c < GDN_HEADS:
            q_ref[:, cols] = yn * (GDN_DK ** -0.5)
        else:
            k_ref[:, (c - GDN_HEADS) * LANES:(c - GDN_HEADS + 1) * LANES] = yn
    else:
        v_ref[:, (c - 2 * GDN_HEADS) * LANES:(c - 2 * GDN_HEADS + 1) * LANES] = y


def _front_conv_body(lead_ref, x_ref, g_ref, w_ref, cw_ref, alog_ref, dtb_ref,
                     q_ref, k_ref, v_ref, gb_ref, tail_ref, z_ref, cq_ref, ckv_ref, misc_ref, xs,
                     *, tm, first_valid_row):
    i = pl.program_id(0)
    x = jnp.where(i == 0, lead_ref[...], x_ref[...])
    hn = ((x * lax.rsqrt(jnp.mean(x * x, axis=-1, keepdims=True) + EPS)) * g_ref[...]).astype(BF16)
    xs[0:8, :] = jnp.where(i > 0, xs[tm:tm + 8, :], 0.0)
    xs[8:tm + 8, :] = _dot(hn, w_ref[:, 0:CONV_DIM])
    off = CONV_DIM
    for width, o_ref in zip(W_SPLITS[1:], (z_ref, cq_ref, ckv_ref, misc_ref)):
        o_ref[...] = _dot(hn, w_ref[:, off:off + width])
        off += width
    tail_ref[...] = xs[tm:tm + 8, :]
    taps = lambda j, cols: xs[8 - (CONV_WIDTH - 1) + j:8 - (CONV_WIDTH - 1) + j + tm, cols]
    for c in range(CONV_DIM // LANES):
        _conv_cols(taps, cw_ref, c, q_ref, k_ref, v_ref)
    gb = _gates(misc_ref[...], alog_ref[...], dtb_ref[...])
    row = i * tm + lax.broadcasted_iota(jnp.int32, gb.shape, 0)
    gb_ref[...] = jnp.where(row >= first_valid_row, gb, 0.0)


def front_conv_prompt(x, lead, norm1, w_r, conv_w, alog, dtb, tm, first_valid_row):
    m, d = x.shape
    n_all = sum(W_SPLITS)
    row = lambda i: (i, 0)
    fixed = lambda i: (0, 0)
    n_blocks = m // tm + 1
    rows = n_blocks * tm
    widths = (GDN_QK_WIDTH, GDN_QK_WIDTH, GDN_V_WIDTH, MISC_W) + (CONV_DIM,) + W_SPLITS[1:]
    out_specs = [pl.BlockSpec((tm, w), row) for w in widths]
    out_shape = [jax.ShapeDtypeStruct((rows, w), F32) for w in widths]
    out_specs[4] = pl.BlockSpec((8, CONV_DIM), fixed)
    out_shape[4] = jax.ShapeDtypeStruct((8, CONV_DIM), F32)
    body = functools.partial(_front_conv_body, tm=tm, first_valid_row=first_valid_row)
    return pl.pallas_call(
        body,
        grid=(n_blocks,),
        in_specs=[pl.BlockSpec((tm, d), fixed), pl.BlockSpec((tm, d), lambda i: (jnp.maximum(i - 1, 0), 0)),
                  pl.BlockSpec((1, d), fixed), pl.BlockSpec((d, n_all), fixed),
                  pl.BlockSpec((CONV_WIDTH, CONV_DIM), fixed),
                  pl.BlockSpec((1, MISC_W), fixed), pl.BlockSpec((1, MISC_W), fixed)],
        out_specs=out_specs,
        out_shape=out_shape,
        scratch_shapes=[pltpu.VMEM((tm + 8, CONV_DIM), F32)],
        compiler_params=_cp(("arbitrary",)),
        name="front_conv_prompt",
    )(lead, x, norm1.reshape(1, d), w_r, conv_w, alog, dtb)


def _conv_sample_body(xp_ref, misc_ref, w_ref, alog_ref, dtb_ref, q_ref, k_ref, v_ref, gb_ref, *, L):
    for l in range(L):
        taps = lambda j, cols, l=l: xp_ref[l + j, :, cols]
        qv, kv, vv = (r.at[l] for r in (q_ref, k_ref, v_ref))
        for c in range(CONV_DIM // LANES):
            _conv_cols(taps, w_ref, c, qv, kv, vv)
        gb_ref[l] = _gates(misc_ref[l], alog_ref[...], dtb_ref[...])


def conv_sample(xp, misc, conv_w, alog, dtb):
    lp, b, _ = xp.shape
    L = lp - (CONV_WIDTH - 1)
    body = functools.partial(_conv_sample_body, L=L)
    return pl.pallas_call(
        body,
        out_shape=[jax.ShapeDtypeStruct((L, b, GDN_QK_WIDTH), F32)] * 3
        + [jax.ShapeDtypeStruct((L, b, MISC_W), F32)],
        compiler_params=pltpu.CompilerParams(vmem_limit_bytes=VMEM_LIMIT),
        name="conv_sample",
    )(xp, misc, conv_w, alog, dtb)


def _split_bf16(x):
    hi = x.astype(BF16)
    return hi, (x - hi.astype(F32)).astype(BF16)


def _rows_hi_lo(x):
    return jnp.concatenate(_split_bf16(x), axis=0)


def _gdn_prep_body(q_ref, k_ref, v_ref, gb_ref, wq_ref, ut_ref, qk_ref, kdt_ref, gl_ref, *, C, CB):
    row = lax.broadcasted_iota(jnp.int32, (C, C), 0)
    col = lax.broadcasted_iota(jnp.int32, (C, C), 1)
    incl = row >= col
    strict = row > col
    tri = incl.astype(F32)
    n_sq = int(math.log2(C)) - 1
    inst = [(cc, h) for cc in range(CB) for h in range(GDN_HEADS)]
    rows_of = lambda cc: slice(cc * C, (cc + 1) * C)
    cols_of = lambda h: slice(h * LANES, (h + 1) * LANES)
    g_cum, g_cum_t, gbs = [], [], []
    for cc in range(CB):
        gb = gb_ref[rows_of(cc), :]
        gc_all = _dot(tri, gb, HIGHEST)
        gl_ref[cc] = jnp.broadcast_to(jnp.exp(gc_all[C - 1:C, :]), (8, LANES))
        gbs.append(gb)
        g_cum.append(gc_all)
        g_cum_t.append(gc_all.T)
    ks = [k_ref[rows_of(cc), cols_of(h)] for cc, h in inst]
    both = [_dot_nt(jnp.concatenate([q_ref[rows_of(cc), cols_of(h)], k], axis=0).astype(BF16), k.astype(BF16))
            for (cc, h), k in zip(inst, ks)]
    a, r, gcs = [], [], []
    for i, (cc, h) in enumerate(inst):
        gc = g_cum[cc][:, h:h + 1]
        gr = g_cum_t[cc][h:h + 1, :]
        beta = gbs[cc][:, GDN_HEADS + h:GDN_HEADS + h + 1]
        decay = jnp.where(incl, jnp.exp(jnp.where(incl, gc - gr, 0.0)), 0.0)
        a.append(jnp.where(strict, -(beta * both[i][C:2 * C] * decay), 0.0))
        qk = both[i][0:C] * decay
        qk_ref[rows_of(cc), cols_of(h)] = jnp.concatenate(
            [qk, jnp.zeros((C, LANES - C), F32)], axis=1).astype(BF16)
        e_g = jnp.exp(gc)
        wq_ref[cc, C:2 * C, cols_of(h)] = (e_g * q_ref[rows_of(cc), cols_of(h)]).astype(BF16)
        k_dec = jnp.exp(g_cum[cc][C - 1:C, h:h + 1] - gc) * ks[i]
        kdt_ref[cc, h] = jnp.concatenate([k_dec, k_dec], axis=0).T.astype(BF16)
        r.append(jnp.concatenate([(beta * e_g) * ks[i], beta * v_ref[rows_of(cc), cols_of(h)]], axis=1))
    for step in range(n_sq + 1):
        a2 = [jnp.concatenate([x, x], axis=1).astype(BF16) for x in a]
        r = [x + _dot(y, _rows_hi_lo(x)) for x, y in zip(r, a2)]
        if step < n_sq:
            a = [_dot(y, _rows_hi_lo(x)) for x, y in zip(a, a2)]
    for i, (cc, h) in enumerate(inst):
        wq_ref[cc, 0:C, cols_of(h)] = r[i][:, 0:LANES].astype(BF16)
        ut_ref[rows_of(cc), cols_of(h)] = r[i][:, LANES:2 * LANES]


def gdn_prep(q, k, v, gb, *, C, CB):
    rows = q.shape[0]
    nc = rows // C
    blk = lambda i: (i, 0)
    body = functools.partial(_gdn_prep_body, C=C, CB=CB)
    return pl.pallas_call(
        body,
        grid=(nc // CB,),
        in_specs=[pl.BlockSpec((CB * C, GDN_QK_WIDTH), blk)] * 3 + [pl.BlockSpec((CB * C, MISC_W), blk)],
        out_specs=[pl.BlockSpec((CB, 2 * C, GDN_QK_WIDTH), lambda i: (i, 0, 0)),
                   pl.BlockSpec((CB * C, GDN_V_WIDTH), blk),
                   pl.BlockSpec((CB * C, GDN_HEADS * LANES), blk),
                   pl.BlockSpec((CB, GDN_HEADS, GDN_DK, 2 * C), lambda i: (i, 0, 0, 0)),
                   pl.BlockSpec((CB, 8, LANES), lambda i: (i, 0, 0))],
        out_shape=[jax.ShapeDtypeStruct((nc, 2 * C, GDN_QK_WIDTH), BF16),
                   jax.ShapeDtypeStruct((rows, GDN_V_WIDTH), F32),
                   jax.ShapeDtypeStruct((rows, GDN_HEADS * LANES), BF16),
                   jax.ShapeDtypeStruct((nc, GDN_HEADS, GDN_DK, 2 * C), BF16),
                   jax.ShapeDtypeStruct((nc, 8, LANES), F32)],
        compiler_params=_cp(("parallel",)),
        name="gdn_prep",
    )(q, k, v, gb)


def _gdn_scan_body(gl_ref, wq_ref, ut_ref, qk_ref, kdt_ref, s0_ref, o_ref, sfin_ref, s_scr,
                   *, C, CB, n_steps):
    sq = pl.program_id(0)
    st = pl.program_id(1)

    @pl.when(st == 0)
    def _():
        s_scr[...] = s0_ref[0]

    for cc in range(CB):
        rows = slice(cc * C, (cc + 1) * C)
        chunk = (sq * n_steps + st) * CB + cc
        heads = range(GDN_HEADS)
        cols = [slice(h * LANES, (h + 1) * LANES) for h in heads]
        ss = [s_scr[h] for h in heads]
        ws = [_dot(jnp.concatenate([wq_ref[cc, :, cols[h]], wq_ref[cc, :, cols[h]]], axis=1),
                   _rows_hi_lo(ss[h])) for h in heads]
        u2 = [_rows_hi_lo(ut_ref[rows, cols[h]] - ws[h][0:C]) for h in heads]
        for h in heads:
            s_scr[h] = gl_ref[chunk * GDN_HEADS + h] * ss[h] + _dot(kdt_ref[cc, h], u2[h])
        for h in heads:
            o_ref[rows, cols[h]] = ws[h][C:2 * C] + _dot(qk_ref[rows, h * LANES:h * LANES + C], u2[h][0:C])

    @pl.when(st == n_steps - 1)
    def _():
        sfin_ref[0] = s_scr[...]


def gdn_scan(gl, wq, ut, qk, kdt, s0, *, C, CB, n_seq):
    rows = ut.shape[0]
    nc = rows // C
    n_steps = nc // (n_seq * CB)
    blk = lambda s, t, gl: (s * n_steps + t, 0)
    blk3 = lambda s, t, gl: (s * n_steps + t, 0, 0)
    blk4 = lambda s, t, gl: (s * n_steps + t, 0, 0, 0)
    st4 = lambda s, t, gl: (s, 0, 0, 0)
    body = functools.partial(_gdn_scan_body, C=C, CB=CB, n_steps=n_steps)
    grid_spec = pltpu.PrefetchScalarGridSpec(
        num_scalar_prefetch=1,
        grid=(n_seq, n_steps),
        in_specs=[pl.BlockSpec((CB, 2 * C, GDN_QK_WIDTH), blk3), pl.BlockSpec((CB * C, GDN_V_WIDTH), blk),
                  pl.BlockSpec((CB * C, GDN_HEADS * LANES), blk),
                  pl.BlockSpec((CB, GDN_HEADS, GDN_DK, 2 * C), blk4),
                  pl.BlockSpec((1, GDN_HEADS, GDN_DK, GDN_DV), st4)],
        out_specs=[pl.BlockSpec((CB * C, GDN_V_WIDTH), blk),
                   pl.BlockSpec((1, GDN_HEADS, GDN_DK, GDN_DV), st4)],
        scratch_shapes=[pltpu.VMEM((GDN_HEADS, GDN_DK, GDN_DV), F32)],
    )
    return pl.pallas_call(
        body,
        grid_spec=grid_spec,
        out_shape=[jax.ShapeDtypeStruct((rows, GDN_V_WIDTH), F32), jax.ShapeDtypeStruct(s0.shape, F32)],
        compiler_params=_cp(("arbitrary", "arbitrary")),
        name="gdn_scan",
    )(gl, wq, ut, qk, kdt, s0)


def gdn(q, k, v, gb, s0, *, C, CB_prep, CB_scan, n_seq):
    wq, ut, qk, kdt, glv = gdn_prep(q, k, v, gb, C=C, CB=CB_prep)
    gl = glv[:, 0, :GDN_HEADS].reshape(-1)
    return gdn_scan(gl, wq, ut, qk, kdt, s0, C=C, CB=CB_scan, n_seq=n_seq)


def _rope(y, cosb, sin_lo, sin_hi):
    return (y * cosb + pltpu.roll(y, LANES - PE_HALF, 1) * sin_lo + pltpu.roll(y, PE_HALF, 1) * sin_hi)


def _slab_norm(x, gain, lane):
    x2 = x * x
    is_nope = lane < NOPE_DIM
    is_pe = (lane >= PE_LO) & (lane < PE_LO + ROPE_DIM)
    ms_n = jnp.sum(jnp.where(is_nope, x2, 0.0), axis=-1, keepdims=True) * (1.0 / NOPE_DIM)
    ms_p = jnp.sum(jnp.where(is_pe, x2, 0.0), axis=-1, keepdims=True) * (1.0 / ROPE_DIM)
    rinv = jnp.where(is_nope, lax.rsqrt(ms_n + EPS), lax.rsqrt(ms_p + EPS))
    return (x * rinv) * gain


def _mla_body(cq_ref, ckvr_ref, misc_ref, cos_ref, slo_ref, shi_ref, qan_ref, wqb_ref, qg_ref,
              kvn_ref, kg_ref, *rest, with_kv):
    if with_kv:
        wuk_ref, wuvt_ref, ckv_ref, kpe_ref, q_ref, k_ref, vt_ref = rest
    else:
        ckv_ref, kpe_ref, q_ref = rest
    cosb, slo, shi = cos_ref[...], slo_ref[...], shi_ref[...]
    lane = lax.broadcasted_iota(jnp.int32, cosb.shape, 1)
    cq = cq_ref[...]
    cqn = ((cq * lax.rsqrt(jnp.mean(cq * cq, axis=-1, keepdims=True) + EPS)) * qan_ref[...]).astype(BF16)
    c = ckvr_ref[...]
    ckv = (c * lax.rsqrt(jnp.mean(c * c, axis=-1, keepdims=True) + EPS)) * kvn_ref[...]
    ckv_ref[...] = ckv
    ckv_b = ckv.astype(BF16)
    kpe_raw = jnp.where((lane >= PE_LO) & (lane < PE_LO + ROPE_DIM), misc_ref[...], 0.0)
    kpe = _rope(_slab_norm(kpe_raw, kg_ref[...], lane), cosb, slo, shi)
    kpe = jnp.where(lane >= PE_LO, kpe, 0.0)
    kpe_ref[...] = kpe
    for h in range(MLA_HEADS):
        cols = slice(h * HEAD_PAD, (h + 1) * HEAD_PAD)
        qh = _dot(cqn, wqb_ref[:, cols])
        qh = _rope(_slab_norm(qh, qg_ref[...], lane), cosb, slo, shi)
        q_ref[h] = (qh * SCORE_SCALE_LOG2).astype(BF16)
        if with_kv:
            kh = _dot(ckv_b, wuk_ref[:, cols])
            k_ref[h] = (_slab_norm(kh, kg_ref[...], lane) + kpe).astype(BF16)
            vt = _dot_nt(wuvt_ref[h], ckv_b)
            ones_row = lax.broadcasted_iota(jnp.int32, vt.shape, 0) == V_DIM
            vt_ref[h] = jnp.where(ones_row, 1.0, vt).astype(BF16)


def mla_prep(cq, ckv_raw, misc, rope_tabs, p, tm, with_kv):
    m = cq.shape[0]
    row = lambda i: (i, 0)
    fixed = lambda i: (0, 0)
    hw = MLA_HEADS * HEAD_PAD
    in_specs = ([pl.BlockSpec((tm, Q_RANK), row), pl.BlockSpec((tm, KV_RANK), row),
                 pl.BlockSpec((tm, MISC_W), row)] + [pl.BlockSpec((tm, LANES), row)] * 3
                + [pl.BlockSpec((1, Q_RANK), fixed), pl.BlockSpec((Q_RANK, hw), fixed),
                   pl.BlockSpec((1, LANES), fixed), pl.BlockSpec((1, KV_RANK), fixed),
                   pl.BlockSpec((1, LANES), fixed)])
    args = [cq, ckv_raw, misc, *rope_tabs, p["q_a_norm"], p["w_q_b"], p["q_gain"], p["kv_a_norm"], p["k_gain"]]
    out_specs = [pl.BlockSpec((tm, KV_RANK), row), pl.BlockSpec((tm, LANES), row),
                 pl.BlockSpec((MLA_HEADS, tm, HEAD_PAD), lambda i: (0, i, 0))]
    out_shape = [jax.ShapeDtypeStruct((m, KV_RANK), F32), jax.ShapeDtypeStruct((m, LANES), F32),
                 jax.ShapeDtypeStruct((MLA_HEADS, m, HEAD_PAD), BF16)]
    if with_kv:
        in_specs += [pl.BlockSpec((KV_RANK, hw), fixed),
                     pl.BlockSpec((MLA_HEADS, VT_ROWS, KV_RANK), lambda i: (0, 0, 0))]
        args += [p["w_uk"], p["w_uv_t"]]
        out_specs += [pl.BlockSpec((MLA_HEADS, tm, HEAD_PAD), lambda i: (0, i, 0)),
                      pl.BlockSpec((MLA_HEADS, VT_ROWS, tm), lambda i: (0, 0, i))]
        out_shape += [jax.ShapeDtypeStruct((MLA_HEADS, m, HEAD_PAD), BF16),
                      jax.ShapeDtypeStruct((MLA_HEADS, VT_ROWS, m), BF16)]
    return pl.pallas_call(
        functools.partial(_mla_body, with_kv=with_kv),
        grid=(m // tm,),
        in_specs=in_specs,
        out_specs=out_specs,
        out_shape=out_shape,
        compiler_params=_cp(("parallel",)),
        name="mla_prep",
    )(*args)


def _flash_body(qmap_ref, kmap_ref, q_ref, k_ref, vt_ref, kpre_ref, vtpre_ref, o_ref, m_scr, acc_scr,
                *, tq, tk, q_block0):
    step = pl.program_id(0)
    q0 = (qmap_ref[step] + q_block0) * tq
    k0 = kmap_ref[step] * tk

    @pl.when(k0 == P0)
    def _():
        sts = [_dot_nt(kpre_ref[h], q_ref[h]) for h in range(MLA_HEADS)]
        m0s = [jnp.max(st, axis=0, keepdims=True) for st in sts]
        for h in range(MLA_HEADS):
            m_scr[h] = m0s[h]
            acc_scr[h] = _dot(vtpre_ref[h], jnp.exp2(sts[h] - m0s[h]).astype(BF16))

    def update(masked):
        tqp = tq // FLASH_QPARTS
        items = [(h, part) for h in range(MLA_HEADS) for part in range(FLASH_QPARTS)]
        scores = lambda h, part: _dot_nt(k_ref[h], q_ref[h, part * tqp:(part + 1) * tqp, :])
        pending = [scores(*it) for it in items[:FLASH_AHEAD]]
        for n, (h, part) in enumerate(items):
            st = pending.pop(0)
            if n + FLASH_AHEAD < len(items):
                pending.append(scores(*items[n + FLASH_AHEAD]))
            qc = slice(part * tqp, (part + 1) * tqp)
            if masked:
                kpos = k0 + lax.broadcasted_iota(jnp.int32, (tk, tqp), 0)
                qpos = q0 + part * tqp + lax.broadcasted_iota(jnp.int32, (tk, tqp), 1)
                st = jnp.where(kpos <= qpos, st, NEG_INF)
            m_prev = m_scr[h, :, qc]
            m_new = jnp.maximum(m_prev, jnp.max(st, axis=0, keepdims=True))
            e = jnp.exp2(st - m_new).astype(BF16)
            acc_scr[h, :, qc] = jnp.exp2(m_prev - m_new) * acc_scr[h, :, qc] + _dot(vt_ref[h], e)
            m_scr[h, :, qc] = m_new

    needs_mask = k0 + tk - 1 > q0
    pl.when(needs_mask)(lambda: update(True))
    pl.when(jnp.logical_not(needs_mask))(lambda: update(False))

    @pl.when(k0 + tk >= q0 + tq)
    def _():
        for h in range(MLA_HEADS):
            a = acc_scr[h]
            o = (a[0:V_DIM] * (1.0 / a[V_DIM:V_DIM + 1, :])).T
            o_ref[:, h * HEAD_PAD:(h + 1) * HEAD_PAD] = jnp.concatenate(
                [o, jnp.zeros((tq, HEAD_PAD - V_DIM), F32)], axis=1).astype(BF16)


def flash_prompt(q, k, vt, *, tq, tk, n_prefix):
    tp = q.shape[1]
    q_block0 = P0 // tq
    nq = (tp - P0) // tq
    qs, ks = [], []
    for qi in range(nq):
        last_k = ((qi + q_block0 + 1) * tq - 1) // tk
        for ki in range(P0 // tk, last_k + 1):
            qs.append(qi)
            ks.append(ki)
    qmap = jnp.asarray(qs, jnp.int32)
    kmap = jnp.asarray(ks, jnp.int32)
    k_pre = k[:, P0 - n_prefix:P0]
    vt_pre = vt[:, :, P0 - n_prefix:P0]
    body = functools.partial(_flash_body, tq=tq, tk=tk, q_block0=q_block0)
    whole = lambda s, qm, km: (0, 0, 0)
    grid_spec = pltpu.PrefetchScalarGridSpec(
        num_scalar_prefetch=2,
        grid=(len(qs),),
        in_specs=[pl.BlockSpec((MLA_HEADS, tq, HEAD_PAD), lambda s, qm, km: (0, qm[s] + q_block0, 0)),
                  pl.BlockSpec((MLA_HEADS, tk, HEAD_PAD), lambda s, qm, km: (0, km[s], 0)),
                  pl.BlockSpec((MLA_HEADS, VT_ROWS, tk), lambda s, qm, km: (0, 0, km[s])),
                  pl.BlockSpec((MLA_HEADS, n_prefix, HEAD_PAD), whole),
                  pl.BlockSpec((MLA_HEADS, VT_ROWS, n_prefix), whole)],
        out_specs=pl.BlockSpec((tq, MLA_HEADS * HEAD_PAD), lambda s, qm, km: (qm[s], 0)),
        scratch_shapes=[pltpu.VMEM((MLA_HEADS, 1, tq), F32), pltpu.VMEM((MLA_HEADS, VT_ROWS, tq), F32)],
    )
    return pl.pallas_call(
        body,
        grid_spec=grid_spec,
        out_shape=jax.ShapeDtypeStruct((tp - P0, MLA_HEADS * HEAD_PAD), BF16),
        compiler_params=_cp(("arbitrary",)),
        name="flash_prompt",
    )(qmap, kmap, q, k, vt, k_pre, vt_pre)


def _qabs_body(q_ref, w_ref, o_ref):
    o_ref[0] = _dot(q_ref[0], w_ref[0]).astype(BF16)


def q_absorb(q, w_abs):
    h, m, _ = q.shape
    blk = lambda i: (i, 0, 0)
    return pl.pallas_call(
        _qabs_body,
        grid=(h,),
        in_specs=[pl.BlockSpec((1, m, HEAD_PAD), blk), pl.BlockSpec((1, HEAD_PAD, KV_RANK), blk)],
        out_specs=pl.BlockSpec((1, m, KV_RANK), blk),
        out_shape=jax.ShapeDtypeStruct((h, m, KV_RANK), BF16),
        compiler_params=_cp(("parallel",)),
        name="q_absorb",
    )(q, w_abs)


def _decode_scores(waug, qpe, ckv_b, kpe_t, n_lq):
    kp = _dot_nt(waug, ckv_b)
    n_k = MLA_HEADS * NOPE_DIM
    kproj = kp[:n_k]
    ms = jnp.sum((kproj * kproj).reshape(MLA_HEADS, NOPE_DIM, kproj.shape[-1]), axis=1) * (1.0 / NOPE_DIM)
    rinv = lax.rsqrt(ms + EPS)
    s_nope = kp[n_k:]
    s_pe = _dot(qpe, kpe_t)
    rinv_rows = jnp.concatenate([rinv] * n_lq, axis=0)
    return s_nope * rinv_rows + s_pe


def _page_copies(pt_ref, ckv_hbm, kpet_hbm, ckv_buf, kpe_buf, sem, step, slot, *, PG, page):
    out = []
    for i in range(PG):
        pid = pt_ref[step * PG + i]
        out.append(pltpu.make_async_copy(ckv_hbm.at[0, pid], ckv_buf.at[slot, pl.ds(i * page, page)],
                                         sem.at[0, slot]))
        out.append(pltpu.make_async_copy(kpet_hbm.at[0, pid], kpe_buf.at[slot, :, pl.ds(i * page, page)],
                                         sem.at[1, slot]))
    return out


def _decode_body(pt_ref, wukt_ref, qabs_ref, qpe_ref, nckv_ref, nkpet_ref, ckv_hbm, kpet_hbm, o_ref,
                 ckv_buf, kpe_buf, sem, waug, ckvb, m_scr, l_scr, acc_scr, *, PG, L, n_groups, n_steps, page):
    s = pl.program_id(0)
    j = lax.rem(s, n_groups)
    slot = lax.rem(s, 2)
    n_k = MLA_HEADS * NOPE_DIM
    R = MLA_HEADS * L
    copies = functools.partial(_page_copies, pt_ref, ckv_hbm, kpet_hbm, ckv_buf, kpe_buf, sem, PG=PG, page=page)

    nxt = lax.rem(s + 1, n_steps)

    @pl.when(s == 0)
    def _():
        for c in copies(0, 0):
            c.start()

    @pl.when(j == 0)
    def _():
        waug[0:n_k, :] = wukt_ref[...]
        waug[n_k:n_k + R, :] = qabs_ref[0]
        c_new = nckv_ref[0].astype(BF16)
        sc = _decode_scores(waug[...], qpe_ref[0], c_new, nkpet_ref[0].astype(BF16), L)
        r_l = lax.shift_right_logical(lax.broadcasted_iota(jnp.int32, sc.shape, 0),
                                      int(math.log2(MLA_HEADS)))
        t = lax.broadcasted_iota(jnp.int32, sc.shape, 1)
        sc = jnp.where((t <= r_l) & (t < L), sc, NEG_INF)
        m0 = jnp.max(sc, axis=-1, keepdims=True)
        e0 = jnp.exp2(sc - m0)
        m_scr[...] = m0
        l_scr[...] = jnp.sum(e0, axis=-1, keepdims=True)
        acc_scr[...] = _dot(e0.astype(BF16), c_new)

    for c in copies(s, slot):
        c.wait()

    w = waug[...]
    qpe = qpe_ref[0]

    def scores(p):
        tok = pl.ds(p * DECODE_SCORE_PAGES * page, DECODE_SCORE_PAGES * page)
        c2 = ckv_buf[slot, tok, :].astype(BF16)
        ckvb[p] = c2
        return _decode_scores(w, qpe, c2, kpe_buf[slot, :, tok].astype(BF16), L)

    ms, ls, accs = [m_scr[...]], [l_scr[...]], [acc_scr[...]]
    pending = [scores(0)]
    next_copies = copies(nxt, 1 - slot)
    n_groups_step = PG // DECODE_SCORE_PAGES
    per_group = -(-len(next_copies) // max(n_groups_step // 4, 1))
    for p in range(n_groups_step):
        sc = pending.pop(0)
        if p + 1 < PG // DECODE_SCORE_PAGES:
            pending.append(scores(p + 1))
        for c in next_copies[p * per_group:(p + 1) * per_group]:
            c.start()
        m_p = jnp.max(sc, axis=-1, keepdims=True)
        e = jnp.exp2(sc - m_p)
        ms.append(m_p)
        ls.append(jnp.sum(e, axis=-1, keepdims=True))
        accs.append(_dot(e.astype(BF16), ckvb[p]))
    m = functools.reduce(jnp.maximum, ms)
    fs = [jnp.exp2(x - m) for x in ms]
    l = functools.reduce(lambda x, y: x + y, [f * x for f, x in zip(fs, ls)])
    acc = functools.reduce(lambda x, y: x + y, [f * x for f, x in zip(fs, accs)])
    m_scr[...], l_scr[...], acc_scr[...] = m, l, acc

    @pl.when(j == n_groups - 1)
    def _():
        o_ref[0] = acc / l

    @pl.when(s == n_steps - 1)
    def _():
        for c in copies(nxt, 1 - slot):
            c.wait()


def decode_attn(page_table, wuk_t, qabs, qpe, new_ckv, new_kpe_t, cache_ckv, cache_kpe_t, *, PG, L):
    b, n_pages = page_table.shape
    page = cache_ckv.shape[2]
    R = MLA_HEADS * L
    n_k = MLA_HEADS * NOPE_DIM
    n_groups = n_pages // PG
    per_b = lambda s, pt: (s // n_groups, 0, 0)
    fixed = lambda s, pt: (0, 0)
    body = functools.partial(_decode_body, PG=PG, L=L, n_groups=n_groups, n_steps=b * n_groups, page=page)
    grid_spec = pltpu.PrefetchScalarGridSpec(
        num_scalar_prefetch=1,
        grid=(b * n_groups,),
        in_specs=[pl.BlockSpec((n_k, KV_RANK), fixed), pl.BlockSpec((1, R, KV_RANK), per_b),
                  pl.BlockSpec((1, R, ROPE_DIM), per_b), pl.BlockSpec((1, 8, KV_RANK), per_b),
                  pl.BlockSpec((1, ROPE_DIM, 8), per_b),
                  pl.BlockSpec(memory_space=pl.ANY), pl.BlockSpec(memory_space=pl.ANY)],
        out_specs=pl.BlockSpec((1, R, KV_RANK), per_b),
        scratch_shapes=[pltpu.VMEM((2, PG * page, KV_RANK), F32), pltpu.VMEM((2, ROPE_DIM, PG * page), F32),
                        pltpu.SemaphoreType.DMA((2, 2)),
                        pltpu.VMEM((n_k + R, KV_RANK), BF16), pltpu.VMEM((PG // DECODE_SCORE_PAGES, DECODE_SCORE_PAGES * page, KV_RANK), BF16),
                        pltpu.VMEM((R, 1), F32), pltpu.VMEM((R, 1), F32), pltpu.VMEM((R, KV_RANK), F32)],
    )
    return pl.pallas_call(
        body,
        grid_spec=grid_spec,
        out_shape=jax.ShapeDtypeStruct((b, R, KV_RANK), F32),
        compiler_params=_cp(("arbitrary",)),
        name="decode_attn",
    )(page_table.reshape(-1), wuk_t, qabs, qpe, new_ckv, new_kpe_t, cache_ckv, cache_kpe_t)


def _latent_out_body(x_ref, w_ref, o_ref):
    o_ref[...] = _dot(x_ref[0].astype(BF16), w_ref[0]).astype(BF16)


def latent_out(o_lat, w_uv_h):
    h, m, _ = o_lat.shape
    return pl.pallas_call(
        _latent_out_body,
        grid=(h,),
        in_specs=[pl.BlockSpec((1, m, KV_RANK), lambda i: (i, 0, 0)),
                  pl.BlockSpec((1, KV_RANK, HEAD_PAD), lambda i: (i, 0, 0))],
        out_specs=pl.BlockSpec((m, HEAD_PAD), lambda i: (0, i)),
        out_shape=jax.ShapeDtypeStruct((m, h * HEAD_PAD), BF16),
        compiler_params=_cp(("parallel",)),
        name="latent_out",
    )(o_lat, w_uv_h)


def _back_body(h_ref, o_ref, z_ref, mla_ref, gg_ref, wog_ref, wom_ref, n2_ref, wg_ref, wu_ref, wd_ref, y_ref):
    h1 = h_ref[...] + _dot(mla_ref[...], wom_ref[...])
    for h in range(GDN_HEADS):
        cols = slice(h * LANES, (h + 1) * LANES)
        o = o_ref[:, cols]
        z = z_ref[:, cols]
        y = ((o * lax.rsqrt(jnp.mean(o * o, axis=-1, keepdims=True) + EPS)) * gg_ref[...]) * (z * _sigmoid(z))
        h1 = h1 + _dot(y.astype(BF16), wog_ref[cols, :])
    hn = ((h1 * lax.rsqrt(jnp.mean(h1 * h1, axis=-1, keepdims=True) + EPS)) * n2_ref[...]).astype(BF16)
    g = _dot(hn, wg_ref[...])
    u = _dot(hn, wu_ref[...])
    y_ref[...] = h1 + _dot(((g * _sigmoid(g)) * u).astype(BF16), wd_ref[...])


def layer_back(h, gdn_o, z, mla_o, p, tm, row_block0):
    m, d = h.shape
    dff = p["w_gate"].shape[1]
    row = lambda i: (i, 0)
    prow = lambda i: (i + row_block0, 0)
    fixed = lambda i: (0, 0)
    hw = MLA_HEADS * HEAD_PAD
    return pl.pallas_call(
        _back_body,
        grid=(m // tm,),
        in_specs=[pl.BlockSpec((tm, d), row), pl.BlockSpec((tm, GDN_V_WIDTH), prow),
                  pl.BlockSpec((tm, GDN_V_WIDTH), prow), pl.BlockSpec((tm, hw), row),
                  pl.BlockSpec((1, LANES), fixed), pl.BlockSpec((GDN_V_WIDTH, d), fixed),
                  pl.BlockSpec((hw, d), fixed), pl.BlockSpec((1, d), fixed),
                  pl.BlockSpec((d, dff), fixed), pl.BlockSpec((d, dff), fixed), pl.BlockSpec((dff, d), fixed)],
        out_specs=pl.BlockSpec((tm, d), row),
        out_shape=jax.ShapeDtypeStruct((m, d), F32),
        compiler_params=_cp(("parallel",)),
        name="layer_back",
    )(h, gdn_o, z, mla_o, p["gdn_norm"], p["w_o_gdn"], p["w_o_mla"], p["norm2"],
      p["w_gate"], p["w_up"], p["w_down"])


def _head_slabs(w, width):
    k, h, _ = w.shape
    return jnp.pad(w, ((0, 0), (0, 0), (0, HEAD_PAD - width))).reshape(k, h * HEAD_PAD)


def _pad_lanes(v, lo=0):
    return jnp.pad(v.astype(F32), (lo, LANES - lo - v.shape[0])).reshape(1, LANES)


def _prep_params(norm1, w_in, conv_w, a_log, dt_bias, gdn_norm, q_a_norm, w_q_b, q_nope_norm,
                 q_pe_norm, kv_a_norm, k_pe_norm, w_uk, w_uv, k_nope_norm, w_o, norm2,
                 w_gate, w_up, w_down):
    d = w_in.shape[0]
    o = 0
    cuts = {}
    for name, wdt in (("qkv", CONV_DIM), ("z", GDN_V_WIDTH), ("a", GDN_HEADS), ("b", GDN_HEADS),
                      ("cq", Q_RANK), ("ckv", KV_RANK), ("kpe", ROPE_DIM)):
        cuts[name] = w_in[:, o:o + wdt]
        o += wdt
    zeros = lambda n: jnp.zeros((d, n), w_in.dtype)
    misc = jnp.concatenate([cuts["a"], cuts["b"], zeros(PE_LO - 2 * GDN_HEADS), cuts["kpe"],
                            zeros(MISC_W - PE_LO - ROPE_DIM)], axis=1)
    w_r = jnp.concatenate([cuts["qkv"], cuts["z"], cuts["cq"], cuts["ckv"], misc], axis=1).astype(BF16)
    wqb = w_q_b.reshape(Q_RANK, MLA_HEADS, QK_DIM)
    gain_q = jnp.concatenate([q_nope_norm, q_pe_norm]).astype(F32)
    gain_k = jnp.concatenate([k_nope_norm, k_pe_norm]).astype(F32)
    w_uk_hdr = jnp.transpose(w_uk, (1, 2, 0))
    w_abs = jnp.pad(w_uk_hdr * k_nope_norm[None, :, None], ((0, 0), (0, HEAD_PAD - NOPE_DIM), (0, 0)))
    w_o_mla = jnp.pad(w_o[GDN_V_WIDTH:].reshape(MLA_HEADS, V_DIM, d),
                      ((0, 0), (0, HEAD_PAD - V_DIM), (0, 0))).reshape(MLA_HEADS * HEAD_PAD, d)
    return dict(
        norm1=norm1, w_in=w_r, conv_w=conv_w.astype(F32),
        a_log=_pad_lanes(a_log), dt_bias=_pad_lanes(dt_bias),
        gdn_norm=gdn_norm.reshape(1, GDN_DV).astype(F32),
        q_a_norm=q_a_norm.reshape(1, Q_RANK).astype(F32), w_q_b=_head_slabs(wqb, QK_DIM).astype(BF16),
        q_gain=_pad_lanes(gain_q), kv_a_norm=kv_a_norm.reshape(1, KV_RANK).astype(F32),
        w_uk=_head_slabs(w_uk, NOPE_DIM).astype(BF16), k_gain=_pad_lanes(gain_k),
        w_uv_t=jnp.pad(jnp.transpose(w_uv, (1, 2, 0)), ((0, 0), (0, VT_ROWS - V_DIM), (0, 0))).astype(BF16),
        w_uk_t=w_uk_hdr.reshape(MLA_HEADS * NOPE_DIM, KV_RANK).astype(BF16),
        w_abs=w_abs.astype(BF16),
        w_uv_h=jnp.pad(jnp.transpose(w_uv, (1, 0, 2)), ((0, 0), (0, 0), (0, HEAD_PAD - V_DIM))).astype(BF16),
        w_o_gdn=w_o[:GDN_V_WIDTH].astype(BF16), w_o_mla=w_o_mla.astype(BF16),
        norm2=norm2.reshape(1, d).astype(F32),
        w_gate=w_gate.astype(BF16), w_up=w_up.astype(BF16), w_down=w_down.astype(BF16),
    )


def _rope_tables(pos):
    inv_freq = np.power(ROPE_THETA, -np.arange(0, ROPE_DIM, 2, dtype=np.float64) / ROPE_DIM)
    ang = np.asarray(pos, np.float64)[:, None] * inv_freq[None, :]
    cos, sin = jnp.asarray(np.cos(ang), F32), jnp.asarray(np.sin(ang), F32)
    n = len(pos)
    ones = jnp.ones((n, PE_LO), F32)
    z = lambda w: jnp.zeros((n, w), F32)
    tail = LANES - PE_LO - ROPE_DIM
    cosb = jnp.concatenate([ones, cos, cos, z(tail)], axis=1)
    sin_lo = jnp.concatenate([z(PE_LO), -sin, z(PE_HALF + tail)], axis=1)
    sin_hi = jnp.concatenate([z(PE_LO + PE_HALF), sin, z(tail)], axis=1)
    return cosb, sin_lo, sin_hi


def _prompt_group(x_prompt, meta_tokens, p):
    seq, d = x_prompt.shape
    tp = P0 + seq
    meta_row0 = P0 - N_META
    lead = jnp.concatenate([jnp.zeros((meta_row0, d), F32), meta_tokens.astype(F32)], axis=0)
    q, k, v, gb, pre_tail, z, cq, ckv_raw, misc = front_conv_prompt(
        x_prompt, lead, p["norm1"], p["w_in"], p["conv_w"], p["a_log"], p["dt_bias"], tm=P0,
        first_valid_row=meta_row0)
    s0 = jnp.zeros((1, GDN_HEADS, GDN_DK, GDN_DV), F32)
    gdn_o, s_fin = gdn(q, k, v, gb, s0, C=CHUNK, CB_prep=4, CB_scan=8, n_seq=1)
    tabs = _rope_tables(np.arange(tp) - meta_row0)
    ckv, kpe, qh, kh, vt = mla_prep(cq, ckv_raw, misc, tabs, p, tm=256, with_kv=True)
    mla_o = flash_prompt(qh, kh, vt, tq=512, tk=512, n_prefix=N_META)
    y = layer_back(x_prompt, gdn_o, z, mla_o, p, tm=256, row_block0=P0 // 256)
    new_conv = pre_tail[8 - (CONV_WIDTH - 1):]
    return (y, new_conv, s_fin[0], ckv[meta_row0:], kpe[meta_row0:, PE_LO:PE_LO + ROPE_DIM])


def _sample_group(x_sample, cache_conv, state_gdn, cache_ckv, cache_kpe, page_table, p):
    b, L, d = x_sample.shape
    m = b * L
    page = cache_ckv.shape[2]
    past_len = page_table.shape[1] * page
    x = x_sample.reshape(m, d)
    tm = min(m, 256)
    qkv_pre, z, cq, ckv_raw, misc = front_proj(x, p["norm1"], p["w_in"], tm=tm)
    xp = jnp.concatenate([jnp.transpose(cache_conv.astype(F32), (1, 0, 2)),
                          jnp.transpose(qkv_pre.reshape(b, L, CONV_DIM), (1, 0, 2))], axis=0)
    misc_t = jnp.transpose(misc.reshape(b, L, MISC_W), (1, 0, 2))
    q, k, v, gb = conv_sample(xp, misc_t, p["conv_w"], p["a_log"], p["dt_bias"])
    C = 16
    to_chunks = lambda t: jnp.pad(jnp.transpose(t, (1, 0, 2)), ((0, 0), (0, C - L), (0, 0))).reshape(b * C, -1)
    gdn_o, s_fin = gdn(to_chunks(q), to_chunks(k), to_chunks(v), to_chunks(gb), state_gdn.astype(F32),
                       C=C, CB_prep=4 if b % 4 == 0 else 1, CB_scan=1, n_seq=b)
    gdn_o = gdn_o.reshape(b, C, GDN_V_WIDTH)[:, :L].reshape(m, GDN_V_WIDTH)
    tabs = _rope_tables(np.tile(past_len + np.arange(L), b))
    ckv, kpe, qh = mla_prep(cq, ckv_raw, misc, tabs, p, tm=tm, with_kv=False)
    kpe32 = kpe[:, PE_LO:PE_LO + ROPE_DIM]
    qabs = q_absorb(qh, p["w_abs"])
    R = L * MLA_HEADS
    rows_lh = lambda t: jnp.transpose(t.reshape(MLA_HEADS, b, L, -1), (1, 2, 0, 3)).reshape(b, R, -1)
    qpe = rows_lh(qh[:, :, PE_LO:PE_LO + ROPE_DIM])
    new_ckv = jnp.pad(ckv.reshape(b, L, KV_RANK), ((0, 0), (0, 8 - L), (0, 0)))
    new_kpe_t = jnp.pad(jnp.swapaxes(kpe32.reshape(b, L, ROPE_DIM), 1, 2), ((0, 0), (0, 0), (0, 8 - L)))
    n_pages = page_table.shape[1]
    PG = 64 if n_pages % 64 == 0 else DECODE_SCORE_PAGES
    o_lat = decode_attn(page_table, p["w_uk_t"], rows_lh(qabs), qpe, new_ckv, new_kpe_t,
                        cache_ckv, jnp.swapaxes(cache_kpe, 2, 3), PG=PG, L=L)
    o_lat_h = jnp.transpose(o_lat.reshape(b, L, MLA_HEADS, KV_RANK), (2, 0, 1, 3)).reshape(MLA_HEADS, m, KV_RANK)
    mla_o = latent_out(o_lat_h, p["w_uv_h"])
    y = layer_back(x, gdn_o, z, mla_o, p, tm=tm, row_block0=0)
    new_conv = qkv_pre.reshape(b, L, CONV_DIM)[:, L - (CONV_WIDTH - 1):]
    return (y.reshape(b, L, d), new_conv, s_fin, ckv.reshape(b, L, KV_RANK), kpe32.reshape(b, L, ROPE_DIM))


def kernel(x_prompt, x_sample, cache_conv, state_gdn, cache_ckv, cache_kpe, page_table, meta_tokens, norm1, w_in, conv_w, a_log, dt_bias, gdn_norm, q_a_norm, w_q_b, q_nope_norm, q_pe_norm, kv_a_norm, k_pe_norm, w_uk, w_uv, k_nope_norm, w_o, norm2, w_gate, w_up, w_down):
    assert x_prompt.shape[0] == 1 and norm1.shape[0] == 1, "one prompt sequence, one layer"
    p = _prep_params(norm1[0], w_in[0], conv_w[0], a_log[0], dt_bias[0], gdn_norm[0], q_a_norm[0],
                     w_q_b[0], q_nope_norm[0], q_pe_norm[0], kv_a_norm[0], k_pe_norm[0], w_uk[0],
                     w_uv[0], k_nope_norm[0], w_o[0], norm2[0], w_gate[0], w_up[0], w_down[0])
    y_p, conv_p, st_p, ckv_p, kpe_p = _prompt_group(x_prompt[0], meta_tokens, p)
    y_s, conv_s, st_s, ckv_s, kpe_s = _sample_group(x_sample, cache_conv[0], state_gdn[0], cache_ckv,
                                                     cache_kpe, page_table, p)
    lead = lambda t: t[None, None]
    return (y_p[None], y_s, lead(conv_p), lead(st_p), lead(ckv_p), lead(kpe_p),
            conv_s[None], st_s[None], ckv_s[None], kpe_s[None])
```

```python
import functools
import math

import jax
import jax.numpy as jnp
import numpy as np
from jax import lax
from jax.experimental import pallas as pl
from jax.experimental.pallas import tpu as pltpu

F32 = jnp.float32
BF16 = jnp.bfloat16
HIGHEST = lax.Precision.HIGHEST

N_META = 16
EPS = 1e-6
NEG_INF = -1e30
GDN_HEADS = 4
GDN_DK = 128
GDN_DV = 128
GDN_QK_WIDTH = GDN_HEADS * GDN_DK
GDN_V_WIDTH = GDN_HEADS * GDN_DV
CONV_WIDTH = 4
CONV_DIM = 2 * GDN_QK_WIDTH + GDN_V_WIDTH
CHUNK = 64
MLA_HEADS = 8
Q_RANK = 384
KV_RANK = 256
NOPE_DIM = 64
ROPE_DIM = 32
V_DIM = 64
QK_DIM = NOPE_DIM + ROPE_DIM
ROPE_THETA = 10000.0
SCORE_SCALE_LOG2 = QK_DIM ** -0.5 * math.log2(math.e)

LANES = 128
HEAD_PAD = LANES
PE_LO = NOPE_DIM
PE_HALF = ROPE_DIM // 2
MISC_W = LANES
P0 = 512
FLASH_QPARTS = 2
FLASH_KPARTS = 2
FLASH_AHEAD = 4
VT_ROWS = V_DIM + 16
DECODE_SCORE_PAGES = 4
VMEM_LIMIT = 56 * 1024 * 1024


def _cp(sem, vmem=None):
    return pltpu.CompilerParams(dimension_semantics=sem, vmem_limit_bytes=vmem or VMEM_LIMIT)


def _sigmoid(x):
    return 1.0 / (1.0 + jnp.exp(-x))


def _softplus(x):
    return jnp.maximum(x, 0.0) + jnp.log1p(jnp.exp(-jnp.abs(x)))


def _dot(a, b, precision=None):
    return jnp.dot(a, b, preferred_element_type=F32, precision=precision)


def _dot_nt(a, b, precision=None):
    return lax.dot_general(a, b, (((1,), (1,)), ((), ())), preferred_element_type=F32,
                           precision=precision)


W_SPLITS = (CONV_DIM, GDN_V_WIDTH, Q_RANK, KV_RANK, MISC_W)


def _front_body(x_ref, g_ref, w_ref, *outs):
    x = x_ref[...]
    hn = ((x * lax.rsqrt(jnp.mean(x * x, axis=-1, keepdims=True) + EPS)) * g_ref[...]).astype(BF16)
    off = 0
    for width, o_ref in zip(W_SPLITS, outs):
        o_ref[...] = _dot(hn, w_ref[:, off:off + width])
        off += width


def front_proj(x, norm1, w_r, tm):
    m, d = x.shape
    n_all = sum(W_SPLITS)
    row = lambda i: (i, 0)
    fixed = lambda i: (0, 0)
    return pl.pallas_call(
        _front_body,
        grid=(m // tm,),
        in_specs=[pl.BlockSpec((tm, d), row), pl.BlockSpec((1, d), fixed), pl.BlockSpec((d, n_all), fixed)],
        out_specs=[pl.BlockSpec((tm, w), row) for w in W_SPLITS],
        out_shape=[jax.ShapeDtypeStruct((m, w), F32) for w in W_SPLITS],
        compiler_params=_cp(("parallel",)),
        name="front_proj",
    )(x, norm1.reshape(1, d), w_r)


def _gates(misc, alog, dtb):
    lane = lax.broadcasted_iota(jnp.int32, misc.shape, 1)
    g = -jnp.exp(alog) * _softplus(misc + dtb)
    beta = _sigmoid(misc)
    return jnp.where(lane < GDN_HEADS, g, jnp.where(lane < 2 * GDN_HEADS, beta, 0.0))


def _conv_cols(taps, w_ref, c, q_ref, k_ref, v_ref):
    cols = slice(c * LANES, (c + 1) * LANES)
    acc = taps(0, cols) * w_ref[0:1, cols]
    for j in range(1, CONV_WIDTH):
        acc = acc + taps(j, cols) * w_ref[j:j + 1, cols]
    y = acc * _sigmoid(acc)
    if c < 2 * GDN_HEADS:
        yn = y * lax.rsqrt(jnp.sum(y * y, axis=-1, keepdims=True) + EPS)
        if c < GDN_HEADS:
            q_ref[:, cols] = yn * (GDN_DK ** -0.5)
        else:
            k_ref[:, (c - GDN_HEADS) * LANES:(c - GDN_HEADS + 1) * LANES] = yn
    else:
        v_ref[:, (c - 2 * GDN_HEADS) * LANES:(c - 2 * GDN_HEADS + 1) * LANES] = y


def _front_conv_body(lead_ref, x_ref, g_ref, w_ref, cw_ref, alog_ref, dtb_ref,
                     q_ref, k_ref, v_ref, gb_ref, tail_ref, z_ref, cq_ref, ckv_ref, misc_ref, xs,
                     *, tm, first_valid_row):
    i = pl.program_id(0)
    x = jnp.where(i == 0, lead_ref[...], x_ref[...])
    hn = ((x * lax.rsqrt(jnp.mean(x * x, axis=-1, keepdims=True) + EPS)) * g_ref[...]).astype(BF16)
    xs[0:8, :] = jnp.where(i > 0, xs[tm:tm + 8, :], 0.0)
    xs[8:tm + 8, :] = _dot(hn, w_ref[:, 0:CONV_DIM])
    off = CONV_DIM
    for width, o_ref in zip(W_SPLITS[1:], (z_ref, cq_ref, ckv_ref, misc_ref)):
        o_ref[...] = _dot(hn, w_ref[:, off:off + width])
        off += width
    tail_ref[...] = xs[tm:tm + 8, :]
    taps = lambda j, cols: xs[8 - (CONV_WIDTH - 1) + j:8 - (CONV_WIDTH - 1) + j + tm, cols]
    for c in range(CONV_DIM // LANES):
        _conv_cols(taps, cw_ref, c, q_ref, k_ref, v_ref)
    gb = _gates(misc_ref[...], alog_ref[...], dtb_ref[...])
    row = i * tm + lax.broadcasted_iota(jnp.int32, gb.shape, 0)
    gb_ref[...] = jnp.where(row >= first_valid_row, gb, 0.0)


def front_conv_prompt(x, lead, norm1, w_r, conv_w, alog, dtb, tm, first_valid_row):
    m, d = x.shape
    n_all = sum(W_SPLITS)
    row = lambda i: (i, 0)
    fixed = lambda i: (0, 0)
    n_blocks = m // tm + 1
    rows = n_blocks * tm
    widths = (GDN_QK_WIDTH, GDN_QK_WIDTH, GDN_V_WIDTH, MISC_W) + (CONV_DIM,) + W_SPLITS[1:]
    out_specs = [pl.BlockSpec((tm, w), row) for w in widths]
    out_shape = [jax.ShapeDtypeStruct((rows, w), F32) for w in widths]
    out_specs[4] = pl.BlockSpec((8, CONV_DIM), fixed)
    out_shape[4] = jax.ShapeDtypeStruct((8, CONV_DIM), F32)
    body = functools.partial(_front_conv_body, tm=tm, first_valid_row=first_valid_row)
    return pl.pallas_call(
        body,
        grid=(n_blocks,),
        in_specs=[pl.BlockSpec((tm, d), fixed), pl.BlockSpec((tm, d), lambda i: (jnp.maximum(i - 1, 0), 0)),
                  pl.BlockSpec((1, d), fixed), pl.BlockSpec((d, n_all), fixed),
                  pl.BlockSpec((CONV_WIDTH, CONV_DIM), fixed),
                  pl.BlockSpec((1, MISC_W), fixed), pl.BlockSpec((1, MISC_W), fixed)],
        out_specs=out_specs,
        out_shape=out_shape,
        scratch_shapes=[pltpu.VMEM((tm + 8, CONV_DIM), F32)],
        compiler_params=_cp(("arbitrary",)),
        name="front_conv_prompt",
    )(lead, x, norm1.reshape(1, d), w_r, conv_w, alog, dtb)


def _conv_sample_body(xp_ref, misc_ref, w_ref, alog_ref, dtb_ref, q_ref, k_ref, v_ref, gb_ref, *, L):
    for l in range(L):
        taps = lambda j, cols, l=l: xp_ref[l + j, :, cols]
        qv, kv, vv = (r.at[l] for r in (q_ref, k_ref, v_ref))
        for c in range(CONV_DIM // LANES):
            _conv_cols(taps, w_ref, c, qv, kv, vv)
        gb_ref[l] = _gates(misc_ref[l], alog_ref[...], dtb_ref[...])


def conv_sample(xp, misc, conv_w, alog, dtb):
    lp, b, _ = xp.shape
    L = lp - (CONV_WIDTH - 1)
    body = functools.partial(_conv_sample_body, L=L)
    return pl.pallas_call(
        body,
        out_shape=[jax.ShapeDtypeStruct((L, b, GDN_QK_WIDTH), F32)] * 3
        + [jax.ShapeDtypeStruct((L, b, MISC_W), F32)],
        compiler_params=pltpu.CompilerParams(vmem_limit_bytes=VMEM_LIMIT),
        name="conv_sample",
    )(xp, misc, conv_w, alog, dtb)


def _split_bf16(x):
    hi = x.astype(BF16)
    return hi, (x - hi.astype(F32)).astype(BF16)


def _rows_hi_lo(x):
    return jnp.concatenate(_split_bf16(x), axis=0)


def _gdn_prep_body(q_ref, k_ref, v_ref, gb_ref, wq_ref, ut_ref, qk_ref, kdt_ref, gl_ref, *, C, CB):
    row = lax.broadcasted_iota(jnp.int32, (C, C), 0)
    col = lax.broadcasted_iota(jnp.int32, (C, C), 1)
    incl = row >= col
    strict = row > col
    tri = incl.astype(F32)
    n_sq = int(math.log2(C)) - 1
    inst = [(cc, h) for cc in range(CB) for h in range(GDN_HEADS)]
    rows_of = lambda cc: slice(cc * C, (cc + 1) * C)
    cols_of = lambda h: slice(h * LANES, (h + 1) * LANES)
    g_cum, g_cum_t, gbs = [], [], []
    for cc in range(CB):
        gb = gb_ref[rows_of(cc), :]
        gc_all = _dot(tri, gb, HIGHEST)
        gl_ref[cc] = jnp.broadcast_to(jnp.exp(gc_all[C - 1:C, :]), (8, LANES))
        gbs.append(gb)
        g_cum.append(gc_all)
        g_cum_t.append(gc_all.T)
    ks = [k_ref[rows_of(cc), cols_of(h)] for cc, h in inst]
    both = [_dot_nt(jnp.concatenate([q_ref[rows_of(cc), cols_of(h)], k], axis=0).astype(BF16), k.astype(BF16))
            for (cc, h), k in zip(inst, ks)]
    a, r, gcs = [], [], []
    for i, (cc, h) in enumerate(inst):
        gc = g_cum[cc][:, h:h + 1]
        gr = g_cum_t[cc][h:h + 1, :]
        beta = gbs[cc][:, GDN_HEADS + h:GDN_HEADS + h + 1]
        decay = jnp.where(incl, jnp.exp(jnp.where(incl, gc - gr, 0.0)), 0.0)
        a.append(jnp.where(strict, -(beta * both[i][C:2 * C] * decay), 0.0))
        qk = both[i][0:C] * decay
        qk_ref[rows_of(cc), cols_of(h)] = jnp.concatenate(
            [qk, jnp.zeros((C, LANES - C), F32)], axis=1).astype(BF16)
        e_g = jnp.exp(gc)
        wq_ref[cc, C:2 * C, cols_of(h)] = (e_g * q_ref[rows_of(cc), cols_of(h)]).astype(BF16)
        k_dec = jnp.exp(g_cum[cc][C - 1:C, h:h + 1] - gc) * ks[i]
        kdt_ref[cc, h] = jnp.concatenate([k_dec, k_dec], axis=0).T.astype(BF16)
        r.append(jnp.concatenate([(beta * e_g) * ks[i], beta * v_ref[rows_of(cc), cols_of(h)]], axis=1))
    for step in range(n_sq + 1):
        a2 = [jnp.concatenate([x, x], axis=1).astype(BF16) for x in a]
        r = [x + _dot(y, _rows_hi_lo(x)) for x, y in zip(r, a2)]
        if step < n_sq:
            a = [_dot(y, _rows_hi_lo(x)) for x, y in zip(a, a2)]
    for i, (cc, h) in enumerate(inst):
        wq_ref[cc, 0:C, cols_of(h)] = r[i][:, 0:LANES].astype(BF16)
        ut_ref[rows_of(cc), cols_of(h)] = r[i][:, LANES:2 * LANES]


def gdn_prep(q, k, v, gb, *, C, CB):
    rows = q.shape[0]
    nc = rows // C
    blk = lambda i: (i, 0)
    body = functools.partial(_gdn_prep_body, C=C, CB=CB)
    return pl.pallas_call(
        body,
        grid=(nc // CB,),
        in_specs=[pl.BlockSpec((CB * C, GDN_QK_WIDTH), blk)] * 3 + [pl.BlockSpec((CB * C, MISC_W), blk)],
        out_specs=[pl.BlockSpec((CB, 2 * C, GDN_QK_WIDTH), lambda i: (i, 0, 0)),
                   pl.BlockSpec((CB * C, GDN_V_WIDTH), blk),
                   pl.BlockSpec((CB * C, GDN_HEADS * LANES), blk),
                   pl.BlockSpec((CB, GDN_HEADS, GDN_DK, 2 * C), lambda i: (i, 0, 0, 0)),
                   pl.BlockSpec((CB, 8, LANES), lambda i: (i, 0, 0))],
        out_shape=[jax.ShapeDtypeStruct((nc, 2 * C, GDN_QK_WIDTH), BF16),
                   jax.ShapeDtypeStruct((rows, GDN_V_WIDTH), F32),
                   jax.ShapeDtypeStruct((rows, GDN_HEADS * LANES), BF16),
                   jax.ShapeDtypeStruct((nc, GDN_HEADS, GDN_DK, 2 * C), BF16),
                   jax.ShapeDtypeStruct((nc, 8, LANES), F32)],
        compiler_params=_cp(("parallel",)),
        name="gdn_prep",
    )(q, k, v, gb)


def _gdn_scan_body(gl_ref, wq_ref, ut_ref, qk_ref, kdt_ref, s0_ref, o_ref, sfin_ref, s_scr,
                   *, C, CB, SB, n_steps):
    sq = pl.program_id(0)
    st = pl.program_id(1)

    @pl.when(st == 0)
    def _():
        s_scr[...] = s0_ref[...]

    cols = [slice(h * LANES, (h + 1) * LANES) for h in range(GDN_HEADS)]
    for cc in range(CB):
        inst = [(sb, h) for sb in range(SB) for h in range(GDN_HEADS)]
        blk = [sb * CB + cc for sb, _ in inst]
        rows = [slice(b * C, (b + 1) * C) for b in blk]
        chunk = [((sq * SB + sb) * n_steps + st) * CB + cc for sb, _ in inst]
        ss = [s_scr[sb, h] for sb, h in inst]
        ws = [_dot(jnp.concatenate([wq_ref[blk[i], :, cols[h]]] * 2, axis=1), _rows_hi_lo(ss[i]))
              for i, (_, h) in enumerate(inst)]
        u2 = [_rows_hi_lo(ut_ref[rows[i], cols[h]] - ws[i][0:C]) for i, (_, h) in enumerate(inst)]
        for i, (sb, h) in enumerate(inst):
            s_scr[sb, h] = gl_ref[chunk[i] * GDN_HEADS + h] * ss[i] + _dot(kdt_ref[blk[i], h], u2[i])
        for i, (_, h) in enumerate(inst):
            o_ref[rows[i], cols[h]] = ws[i][C:2 * C] + _dot(qk_ref[rows[i], h * LANES:h * LANES + C], u2[i][0:C])

    @pl.when(st == n_steps - 1)
    def _():
        sfin_ref[...] = s_scr[...]


def gdn_scan(gl, wq, ut, qk, kdt, s0, *, C, CB, SB, n_seq):
    rows = ut.shape[0]
    nc = rows // C
    n_steps = nc // (n_seq * CB)
    assert SB == 1 or n_steps == 1
    nb = SB * CB
    blk = lambda s, t, gl: (s * n_steps + t, 0)
    blk3 = lambda s, t, gl: (s * n_steps + t, 0, 0)
    blk4 = lambda s, t, gl: (s * n_steps + t, 0, 0, 0)
    st4 = lambda s, t, gl: (s, 0, 0, 0)
    body = functools.partial(_gdn_scan_body, C=C, CB=CB, SB=SB, n_steps=n_steps)
    grid_spec = pltpu.PrefetchScalarGridSpec(
        num_scalar_prefetch=1,
        grid=(n_seq // SB, n_steps),
        in_specs=[pl.BlockSpec((nb, 2 * C, GDN_QK_WIDTH), blk3), pl.BlockSpec((nb * C, GDN_V_WIDTH), blk),
                  pl.BlockSpec((nb * C, GDN_HEADS * LANES), blk),
                  pl.BlockSpec((nb, GDN_HEADS, GDN_DK, 2 * C), blk4),
                  pl.BlockSpec((SB, GDN_HEADS, GDN_DK, GDN_DV), st4)],
        out_specs=[pl.BlockSpec((nb * C, GDN_V_WIDTH), blk),
                   pl.BlockSpec((SB, GDN_HEADS, GDN_DK, GDN_DV), st4)],
        scratch_shapes=[pltpu.VMEM((SB, GDN_HEADS, GDN_DK, GDN_DV), F32)],
    )
    return pl.pallas_call(
        body,
        grid_spec=grid_spec,
        out_shape=[jax.ShapeDtypeStruct((rows, GDN_V_WIDTH), F32), jax.ShapeDtypeStruct(s0.shape, F32)],
        compiler_params=_cp(("arbitrary", "arbitrary")),
        name="gdn_scan",
    )(gl, wq, ut, qk, kdt, s0)


def gdn(q, k, v, gb, s0, *, C, CB_prep, CB_scan, SB_scan, n_seq):
    wq, ut, qk, kdt, glv = gdn_prep(q, k, v, gb, C=C, CB=CB_prep)
    gl = glv[:, 0, :GDN_HEADS].reshape(-1)
    return gdn_scan(gl, wq, ut, qk, kdt, s0, C=C, CB=CB_scan, SB=SB_scan, n_seq=n_seq)


def _rope(y, cosb, sin_lo, sin_hi):
    return (y * cosb + pltpu.roll(y, LANES - PE_HALF, 1) * sin_lo + pltpu.roll(y, PE_HALF, 1) * sin_hi)


def _slab_norm(x, gain, lane, parts="np"):
    x2 = x * x
    is_nope = lane < NOPE_DIM
    is_pe = (lane >= PE_LO) & (lane < PE_LO + ROPE_DIM)
    if parts == "n":
        rinv = lax.rsqrt(jnp.sum(x2, axis=-1, keepdims=True) * (1.0 / NOPE_DIM) + EPS)
    elif parts == "p":
        rinv = lax.rsqrt(jnp.sum(x2, axis=-1, keepdims=True) * (1.0 / ROPE_DIM) + EPS)
    else:
        ms_n = jnp.sum(jnp.where(is_nope, x2, 0.0), axis=-1, keepdims=True) * (1.0 / NOPE_DIM)
        ms_p = jnp.sum(jnp.where(is_pe, x2, 0.0), axis=-1, keepdims=True) * (1.0 / ROPE_DIM)
        rinv = jnp.where(is_nope, lax.rsqrt(ms_n + EPS), lax.rsqrt(ms_p + EPS))
    return (x * rinv) * gain


def _mla_body(cq_ref, ckvr_ref, misc_ref, cos_ref, slo_ref, shi_ref, qan_ref, wqb_ref, qg_ref,
              kvn_ref, kg_ref, *rest, with_kv):
    if with_kv:
        wuk_ref, wuvt_ref, ckv_ref, kpe_ref, q_ref, k_ref, vt_ref = rest
    else:
        ckv_ref, kpe_ref, q_ref = rest
    cosb, slo, shi = cos_ref[...], slo_ref[...], shi_ref[...]
    lane = lax.broadcasted_iota(jnp.int32, cosb.shape, 1)
    cq = cq_ref[...]
    cqn = ((cq * lax.rsqrt(jnp.mean(cq * cq, axis=-1, keepdims=True) + EPS)) * qan_ref[...]).astype(BF16)
    c = ckvr_ref[...]
    ckv = (c * lax.rsqrt(jnp.mean(c * c, axis=-1, keepdims=True) + EPS)) * kvn_ref[...]
    ckv_ref[...] = ckv
    ckv_b = ckv.astype(BF16)
    kpe_raw = jnp.where((lane >= PE_LO) & (lane < PE_LO + ROPE_DIM), misc_ref[...], 0.0)
    kpe = _rope(_slab_norm(kpe_raw, kg_ref[...], lane, "p"), cosb, slo, shi)
    kpe = jnp.where(lane >= PE_LO, kpe, 0.0)
    kpe_ref[...] = kpe
    for h in range(MLA_HEADS):
        cols = slice(h * HEAD_PAD, (h + 1) * HEAD_PAD)
        qh = _dot(cqn, wqb_ref[:, cols])
        qh = _rope(_slab_norm(qh, qg_ref[...], lane), cosb, slo, shi)
        q_ref[h] = (qh * SCORE_SCALE_LOG2).astype(BF16)
        if with_kv:
            kh = _dot(ckv_b, wuk_ref[:, cols])
            k_ref[h] = (_slab_norm(kh, kg_ref[...], lane, "n") + kpe).astype(BF16)
            vt = _dot_nt(wuvt_ref[h], ckv_b)
            ones_row = lax.broadcasted_iota(jnp.int32, vt.shape, 0) == V_DIM
            vt_ref[h] = jnp.where(ones_row, 1.0, vt).astype(BF16)


def mla_prep(cq, ckv_raw, misc, rope_tabs, p, tm, with_kv):
    m = cq.shape[0]
    row = lambda i: (i, 0)
    fixed = lambda i: (0, 0)
    hw = MLA_HEADS * HEAD_PAD
    in_specs = ([pl.BlockSpec((tm, Q_RANK), row), pl.BlockSpec((tm, KV_RANK), row),
                 pl.BlockSpec((tm, MISC_W), row)] + [pl.BlockSpec((tm, LANES), row)] * 3
                + [pl.BlockSpec((1, Q_RANK), fixed), pl.BlockSpec((Q_RANK, hw), fixed),
                   pl.BlockSpec((1, LANES), fixed), pl.BlockSpec((1, KV_RANK), fixed),
                   pl.BlockSpec((1, LANES), fixed)])
    args = [cq, ckv_raw, misc, *rope_tabs, p["q_a_norm"], p["w_q_b"], p["q_gain"], p["kv_a_norm"], p["k_gain"]]
    out_specs = [pl.BlockSpec((tm, KV_RANK), row), pl.BlockSpec((tm, LANES), row),
                 pl.BlockSpec((MLA_HEADS, tm, HEAD_PAD), lambda i: (0, i, 0))]
    out_shape = [jax.ShapeDtypeStruct((m, KV_RANK), F32), jax.ShapeDtypeStruct((m, LANES), F32),
                 jax.ShapeDtypeStruct((MLA_HEADS, m, HEAD_PAD), BF16)]
    if with_kv:
        in_specs += [pl.BlockSpec((KV_RANK, hw), fixed),
                     pl.BlockSpec((MLA_HEADS, VT_ROWS, KV_RANK), lambda i: (0, 0, 0))]
        args += [p["w_uk"], p["w_uv_t"]]
        out_specs += [pl.BlockSpec((MLA_HEADS, tm, HEAD_PAD), lambda i: (0, i, 0)),
                      pl.BlockSpec((MLA_HEADS, VT_ROWS, tm), lambda i: (0, 0, i))]
        out_shape += [jax.ShapeDtypeStruct((MLA_HEADS, m, HEAD_PAD), BF16),
                      jax.ShapeDtypeStruct((MLA_HEADS, VT_ROWS, m), BF16)]
    return pl.pallas_call(
        functools.partial(_mla_body, with_kv=with_kv),
        grid=(m // tm,),
        in_specs=in_specs,
        out_specs=out_specs,
        out_shape=out_shape,
        compiler_params=_cp(("parallel",)),
        name="mla_prep",
    )(*args)


def _flash_body(qmap_ref, kmap_ref, q_ref, k_ref, vt_ref, kpre_ref, vtpre_ref, o_ref, m_scr, acc_scr,
                *, tq, tk, q_block0):
    step = pl.program_id(0)
    q0 = (qmap_ref[step] + q_block0) * tq
    k0 = kmap_ref[step] * tk

    @pl.when(k0 == P0)
    def _():
        sts = [_dot_nt(kpre_ref[h], q_ref[h]) for h in range(MLA_HEADS)]
        m0s = [jnp.max(st, axis=0, keepdims=True) for st in sts]
        for h in range(MLA_HEADS):
            m_scr[h] = m0s[h]
            acc_scr[h] = _dot(vtpre_ref[h], jnp.exp2(sts[h] - m0s[h]).astype(BF16))

    def update(masked):
        tqp = tq // FLASH_QPARTS
        tkp = tk // FLASH_KPARTS
        items = [(h, kp, qp) for h in range(MLA_HEADS) for kp in range(FLASH_KPARTS)
                 for qp in range(FLASH_QPARTS)]
        scores = lambda h, kp, qp: _dot_nt(k_ref[h, kp * tkp:(kp + 1) * tkp, :],
                                           q_ref[h, qp * tqp:(qp + 1) * tqp, :])
        pending = [scores(*it) for it in items[:FLASH_AHEAD]]
        for n, (h, kp, qp) in enumerate(items):
            st = pending.pop(0)
            if n + FLASH_AHEAD < len(items):
                pending.append(scores(*items[n + FLASH_AHEAD]))
            qc = slice(qp * tqp, (qp + 1) * tqp)
            if masked:
                kpos = k0 + kp * tkp + lax.broadcasted_iota(jnp.int32, (tkp, tqp), 0)
                qpos = q0 + qp * tqp + lax.broadcasted_iota(jnp.int32, (tkp, tqp), 1)
                st = jnp.where(kpos <= qpos, st, NEG_INF)
            m_prev = m_scr[h, :, qc]
            m_new = jnp.maximum(m_prev, jnp.max(st, axis=0, keepdims=True))
            e = jnp.exp2(st - m_new).astype(BF16)
            acc_scr[h, :, qc] = (jnp.exp2(m_prev - m_new) * acc_scr[h, :, qc]
                                 + _dot(vt_ref[h, :, kp * tkp:(kp + 1) * tkp], e))
            m_scr[h, :, qc] = m_new

    needs_mask = k0 + tk - 1 > q0
    pl.when(needs_mask)(lambda: update(True))
    pl.when(jnp.logical_not(needs_mask))(lambda: update(False))

    @pl.when(k0 + tk >= q0 + tq)
    def _():
        for h in range(MLA_HEADS):
            a = acc_scr[h]
            o = (a[0:V_DIM] * (1.0 / a[V_DIM:V_DIM + 1, :])).T
            o_ref[:, h * HEAD_PAD:(h + 1) * HEAD_PAD] = jnp.concatenate(
                [o, jnp.zeros((tq, HEAD_PAD - V_DIM), F32)], axis=1).astype(BF16)


def flash_prompt(q, k, vt, *, tq, tk, n_prefix):
    tp = q.shape[1]
    q_block0 = P0 // tq
    nq = (tp - P0) // tq
    qs, ks = [], []
    for qi in range(nq):
        last_k = ((qi + q_block0 + 1) * tq - 1) // tk
        for ki in range(P0 // tk, last_k + 1):
            qs.append(qi)
            ks.append(ki)
    qmap = jnp.asarray(qs, jnp.int32)
    kmap = jnp.asarray(ks, jnp.int32)
    k_pre = k[:, P0 - n_prefix:P0]
    vt_pre = vt[:, :, P0 - n_prefix:P0]
    body = functools.partial(_flash_body, tq=tq, tk=tk, q_block0=q_block0)
    whole = lambda s, qm, km: (0, 0, 0)
    grid_spec = pltpu.PrefetchScalarGridSpec(
        num_scalar_prefetch=2,
        grid=(len(qs),),
        in_specs=[pl.BlockSpec((MLA_HEADS, tq, HEAD_PAD), lambda s, qm, km: (0, qm[s] + q_block0, 0)),
                  pl.BlockSpec((MLA_HEADS, tk, HEAD_PAD), lambda s, qm, km: (0, km[s], 0)),
                  pl.BlockSpec((MLA_HEADS, VT_ROWS, tk), lambda s, qm, km: (0, 0, km[s])),
                  pl.BlockSpec((MLA_HEADS, n_prefix, HEAD_PAD), whole),
                  pl.BlockSpec((MLA_HEADS, VT_ROWS, n_prefix), whole)],
        out_specs=pl.BlockSpec((tq, MLA_HEADS * HEAD_PAD), lambda s, qm, km: (qm[s], 0)),
        scratch_shapes=[pltpu.VMEM((MLA_HEADS, 1, tq), F32), pltpu.VMEM((MLA_HEADS, VT_ROWS, tq), F32)],
    )
    return pl.pallas_call(
        body,
        grid_spec=grid_spec,
        out_shape=jax.ShapeDtypeStruct((tp - P0, MLA_HEADS * HEAD_PAD), BF16),
        compiler_params=_cp(("arbitrary",)),
        name="flash_prompt",
    )(qmap, kmap, q, k, vt, k_pre, vt_pre)


def _qabs_body(q_ref, w_ref, o_ref):
    o_ref[0] = _dot(q_ref[0], w_ref[0]).astype(BF16)


def q_absorb(q, w_abs):
    h, m, _ = q.shape
    blk = lambda i: (i, 0, 0)
    return pl.pallas_call(
        _qabs_body,
        grid=(h,),
        in_specs=[pl.BlockSpec((1, m, HEAD_PAD), blk), pl.BlockSpec((1, HEAD_PAD, KV_RANK), blk)],
        out_specs=pl.BlockSpec((1, m, KV_RANK), blk),
        out_shape=jax.ShapeDtypeStruct((h, m, KV_RANK), BF16),
        compiler_params=_cp(("parallel",)),
        name="q_absorb",
    )(q, w_abs)


def _decode_scores(waug, qpe, ckv_b, kpe_t, n_lq):
    kp = _dot_nt(waug, ckv_b)
    n_k = MLA_HEADS * NOPE_DIM
    kproj = kp[:n_k]
    ms = jnp.sum((kproj * kproj).reshape(MLA_HEADS, NOPE_DIM, kproj.shape[-1]), axis=1) * (1.0 / NOPE_DIM)
    rinv = lax.rsqrt(ms + EPS)
    s_nope = kp[n_k:]
    s_pe = _dot(qpe, kpe_t)
    rinv_rows = jnp.concatenate([rinv] * n_lq, axis=0)
    return s_nope * rinv_rows + s_pe


def _page_copies(pt_ref, ckv_hbm, kpet_hbm, ckv_buf, kpe_buf, sem, step, slot, *, PG, page):
    out = []
    for i in range(PG):
        pid = pt_ref[step * PG + i]
        out.append(pltpu.make_async_copy(ckv_hbm.at[0, pid], ckv_buf.at[slot, pl.ds(i * page, page)],
                                         sem.at[0, slot]))
        out.append(pltpu.make_async_copy(kpet_hbm.at[0, pid], kpe_buf.at[slot, :, pl.ds(i * page, page)],
                                         sem.at[1, slot]))
    return out


def _decode_body(pt_ref, wukt_ref, qabs_ref, qpe_ref, nckv_ref, nkpet_ref, ckv_hbm, kpet_hbm, o_ref,
                 ckv_buf, kpe_buf, sem, waug, ckvb, m_scr, l_scr, acc_scr, *, PG, L, n_groups, n_steps, page):
    s = pl.program_id(0)
    j = lax.rem(s, n_groups)
    slot = lax.rem(s, 2)
    n_k = MLA_HEADS * NOPE_DIM
    R = MLA_HEADS * L
    copies = functools.partial(_page_copies, pt_ref, ckv_hbm, kpet_hbm, ckv_buf, kpe_buf, sem, PG=PG, page=page)

    nxt = lax.rem(s + 1, n_steps)

    @pl.when(s == 0)
    def _():
        for c in copies(0, 0):
            c.start()

    @pl.when(j == 0)
    def _():
        waug[0:n_k, :] = wukt_ref[...]
        waug[n_k:n_k + R, :] = qabs_ref[0]
        c_new = nckv_ref[0].astype(BF16)
        sc = _decode_scores(waug[...], qpe_ref[0], c_new, nkpet_ref[0].astype(BF16), L)
        r_l = lax.shift_right_logical(lax.broadcasted_iota(jnp.int32, sc.shape, 0),
                                      int(math.log2(MLA_HEADS)))
        t = lax.broadcasted_iota(jnp.int32, sc.shape, 1)
        sc = jnp.where((t <= r_l) & (t < L), sc, NEG_INF)
        m0 = jnp.max(sc, axis=-1, keepdims=True)
        e0 = jnp.exp2(sc - m0)
        m_scr[...] = m0
        l_scr[...] = jnp.sum(e0, axis=-1, keepdims=True)
        acc_scr[...] = _dot(e0.astype(BF16), c_new)

    for c in copies(s, slot):
        c.wait()

    w = waug[...]
    qpe = qpe_ref[0]

    def scores(p):
        tok = pl.ds(p * DECODE_SCORE_PAGES * page, DECODE_SCORE_PAGES * page)
        c2 = ckv_buf[slot, tok, :].astype(BF16)
        ckvb[p] = c2
        return _decode_scores(w, qpe, c2, kpe_buf[slot, :, tok].astype(BF16), L)

    ms, ls, accs = [m_scr[...]], [l_scr[...]], [acc_scr[...]]
    pending = [scores(0)]
    next_copies = copies(nxt, 1 - slot)
    n_groups_step = PG // DECODE_SCORE_PAGES
    per_group = -(-len(next_copies) // max(n_groups_step // 4, 1))
    for p in range(n_groups_step):
        sc = pending.pop(0)
        if p + 1 < PG // DECODE_SCORE_PAGES:
            pending.append(scores(p + 1))
        for c in next_copies[p * per_group:(p + 1) * per_group]:
            c.start()
        m_p = jnp.max(sc, axis=-1, keepdims=True)
        e = jnp.exp2(sc - m_p)
        ms.append(m_p)
        ls.append(jnp.sum(e, axis=-1, keepdims=True))
        accs.append(_dot(e.astype(BF16), ckvb[p]))
    m = functools.reduce(jnp.maximum, ms)
    fs = [jnp.exp2(x - m) for x in ms]
    l = functools.reduce(lambda x, y: x + y, [f * x for f, x in zip(fs, ls)])
    acc = functools.reduce(lambda x, y: x + y, [f * x for f, x in zip(fs, accs)])
    m_scr[...], l_scr[...], acc_scr[...] = m, l, acc

    @pl.when(j == n_groups - 1)
    def _():
        o_ref[0] = acc / l

    @pl.when(s == n_steps - 1)
    def _():
        for c in copies(nxt, 1 - slot):
            c.wait()


def decode_attn(page_table, wuk_t, qabs, qpe, new_ckv, new_kpe_t, cache_ckv, cache_kpe_t, *, PG, L):
    b, n_pages = page_table.shape
    page = cache_ckv.shape[2]
    R = MLA_HEADS * L
    n_k = MLA_HEADS * NOPE_DIM
    n_groups = n_pages // PG
    per_b = lambda s, pt: (s // n_groups, 0, 0)
    fixed = lambda s, pt: (0, 0)
    body = functools.partial(_decode_body, PG=PG, L=L, n_groups=n_groups, n_steps=b * n_groups, page=page)
    grid_spec = pltpu.PrefetchScalarGridSpec(
        num_scalar_prefetch=1,
        grid=(b * n_groups,),
        in_specs=[pl.BlockSpec((n_k, KV_RANK), fixed), pl.BlockSpec((1, R, KV_RANK), per_b),
                  pl.BlockSpec((1, R, ROPE_DIM), per_b), pl.BlockSpec((1, 8, KV_RANK), per_b),
                  pl.BlockSpec((1, ROPE_DIM, 8), per_b),
                  pl.BlockSpec(memory_space=pl.ANY), pl.BlockSpec(memory_space=pl.ANY)],
        out_specs=pl.BlockSpec((1, R, KV_RANK), per_b),
        scratch_shapes=[pltpu.VMEM((2, PG * page, KV_RANK), F32), pltpu.VMEM((2, ROPE_DIM, PG * page), F32),
                        pltpu.SemaphoreType.DMA((2, 2)),
                        pltpu.VMEM((n_k + R, KV_RANK), BF16), pltpu.VMEM((PG // DECODE_SCORE_PAGES, DECODE_SCORE_PAGES * page, KV_RANK), BF16),
                        pltpu.VMEM((R, 1), F32), pltpu.VMEM((R, 1), F32), pltpu.VMEM((R, KV_RANK), F32)],
    )
    return pl.pallas_call(
        body,
        grid_spec=grid_spec,
        out_shape=jax.ShapeDtypeStruct((b, R, KV_RANK), F32),
        compiler_params=_cp(("arbitrary",)),
        name="decode_attn",
    )(page_table.reshape(-1), wuk_t, qabs, qpe, new_ckv, new_kpe_t, cache_ckv, cache_kpe_t)


def _latent_out_body(x_ref, w_ref, o_ref):
    o_ref[...] = _dot(x_ref[0].astype(BF16), w_ref[0]).astype(BF16)


def latent_out(o_lat, w_uv_h):
    h, m, _ = o_lat.shape
    return pl.pallas_call(
        _latent_out_body,
        grid=(h,),
        in_specs=[pl.BlockSpec((1, m, KV_RANK), lambda i: (i, 0, 0)),
                  pl.BlockSpec((1, KV_RANK, HEAD_PAD), lambda i: (i, 0, 0))],
        out_specs=pl.BlockSpec((m, HEAD_PAD), lambda i: (0, i)),
        out_shape=jax.ShapeDtypeStruct((m, h * HEAD_PAD), BF16),
        compiler_params=_cp(("parallel",)),
        name="latent_out",
    )(o_lat, w_uv_h)


def _back_body(h_ref, o_ref, z_ref, mla_ref, gg_ref, wog_ref, wom_ref, n2_ref, wg_ref, wu_ref, wd_ref, y_ref):
    h1 = h_ref[...] + _dot(mla_ref[...], wom_ref[...])
    for h in range(GDN_HEADS):
        cols = slice(h * LANES, (h + 1) * LANES)
        o = o_ref[:, cols]
        z = z_ref[:, cols]
        y = ((o * lax.rsqrt(jnp.mean(o * o, axis=-1, keepdims=True) + EPS)) * gg_ref[...]) * (z * _sigmoid(z))
        h1 = h1 + _dot(y.astype(BF16), wog_ref[cols, :])
    hn = ((h1 * lax.rsqrt(jnp.mean(h1 * h1, axis=-1, keepdims=True) + EPS)) * n2_ref[...]).astype(BF16)
    g = _dot(hn, wg_ref[...])
    u = _dot(hn, wu_ref[...])
    y_ref[...] = h1 + _dot(((g * _sigmoid(g)) * u).astype(BF16), wd_ref[...])


def layer_back(h, gdn_o, z, mla_o, p, tm, row_block0):
    m, d = h.shape
    dff = p["w_gate"].shape[1]
    row = lambda i: (i, 0)
    prow = lambda i: (i + row_block0, 0)
    fixed = lambda i: (0, 0)
    hw = MLA_HEADS * HEAD_PAD
    return pl.pallas_call(
        _back_body,
        grid=(m // tm,),
        in_specs=[pl.BlockSpec((tm, d), row), pl.BlockSpec((tm, GDN_V_WIDTH), prow),
                  pl.BlockSpec((tm, GDN_V_WIDTH), prow), pl.BlockSpec((tm, hw), row),
                  pl.BlockSpec((1, LANES), fixed), pl.BlockSpec((GDN_V_WIDTH, d), fixed),
                  pl.BlockSpec((hw, d), fixed), pl.BlockSpec((1, d), fixed),
                  pl.BlockSpec((d, dff), fixed), pl.BlockSpec((d, dff), fixed), pl.BlockSpec((dff, d), fixed)],
        out_specs=pl.BlockSpec((tm, d), row),
        out_shape=jax.ShapeDtypeStruct((m, d), F32),
        compiler_params=_cp(("parallel",)),
        name="layer_back",
    )(h, gdn_o, z, mla_o, p["gdn_norm"], p["w_o_gdn"], p["w_o_mla"], p["norm2"],
      p["w_gate"], p["w_up"], p["w_down"])


def _head_slabs(w, width):
    k, h, _ = w.shape
    return jnp.pad(w, ((0, 0), (0, 0), (0, HEAD_PAD - width))).reshape(k, h * HEAD_PAD)


def _pad_lanes(v, lo=0):
    return jnp.pad(v.astype(F32), (lo, LANES - lo - v.shape[0])).reshape(1, LANES)


def _prep_params(norm1, w_in, conv_w, a_log, dt_bias, gdn_norm, q_a_norm, w_q_b, q_nope_norm,
                 q_pe_norm, kv_a_norm, k_pe_norm, w_uk, w_uv, k_nope_norm, w_o, norm2,
                 w_gate, w_up, w_down):
    d = w_in.shape[0]
    o = 0
    cuts = {}
    for name, wdt in (("qkv", CONV_DIM), ("z", GDN_V_WIDTH), ("a", GDN_HEADS), ("b", GDN_HEADS),
                      ("cq", Q_RANK), ("ckv", KV_RANK), ("kpe", ROPE_DIM)):
        cuts[name] = w_in[:, o:o + wdt]
        o += wdt
    zeros = lambda n: jnp.zeros((d, n), w_in.dtype)
    misc = jnp.concatenate([cuts["a"], cuts["b"], zeros(PE_LO - 2 * GDN_HEADS), cuts["kpe"],
                            zeros(MISC_W - PE_LO - ROPE_DIM)], axis=1)
    w_r = jnp.concatenate([cuts["qkv"], cuts["z"], cuts["cq"], cuts["ckv"], misc], axis=1).astype(BF16)
    wqb = w_q_b.reshape(Q_RANK, MLA_HEADS, QK_DIM)
    gain_q = jnp.concatenate([q_nope_norm, q_pe_norm]).astype(F32)
    gain_k = jnp.concatenate([k_nope_norm, k_pe_norm]).astype(F32)
    w_uk_hdr = jnp.transpose(w_uk, (1, 2, 0))
    w_abs = jnp.pad(w_uk_hdr * k_nope_norm[None, :, None], ((0, 0), (0, HEAD_PAD - NOPE_DIM), (0, 0)))
    w_o_mla = jnp.pad(w_o[GDN_V_WIDTH:].reshape(MLA_HEADS, V_DIM, d),
                      ((0, 0), (0, HEAD_PAD - V_DIM), (0, 0))).reshape(MLA_HEADS * HEAD_PAD, d)
    return dict(
        norm1=norm1, w_in=w_r, conv_w=conv_w.astype(F32),
        a_log=_pad_lanes(a_log), dt_bias=_pad_lanes(dt_bias),
        gdn_norm=gdn_norm.reshape(1, GDN_DV).astype(F32),
        q_a_norm=q_a_norm.reshape(1, Q_RANK).astype(F32), w_q_b=_head_slabs(wqb, QK_DIM).astype(BF16),
        q_gain=_pad_lanes(gain_q), kv_a_norm=kv_a_norm.reshape(1, KV_RANK).astype(F32),
        w_uk=_head_slabs(w_uk, NOPE_DIM).astype(BF16), k_gain=_pad_lanes(gain_k),
        w_uv_t=jnp.pad(jnp.transpose(w_uv, (1, 2, 0)), ((0, 0), (0, VT_ROWS - V_DIM), (0, 0))).astype(BF16),
        w_uk_t=w_uk_hdr.reshape(MLA_HEADS * NOPE_DIM, KV_RANK).astype(BF16),
        w_abs=w_abs.astype(BF16),
        w_uv_h=jnp.pad(jnp.transpose(w_uv, (1, 0, 2)), ((0, 0), (0, 0), (0, HEAD_PAD - V_DIM))).astype(BF16),
        w_o_gdn=w_o[:GDN_V_WIDTH].astype(BF16), w_o_mla=w_o_mla.astype(BF16),
        norm2=norm2.reshape(1, d).astype(F32),
        w_gate=w_gate.astype(BF16), w_up=w_up.astype(BF16), w_down=w_down.astype(BF16),
    )


def _rope_tables(pos):
    inv_freq = np.power(ROPE_THETA, -np.arange(0, ROPE_DIM, 2, dtype=np.float64) / ROPE_DIM)
    ang = np.asarray(pos, np.float64)[:, None] * inv_freq[None, :]
    cos, sin = jnp.asarray(np.cos(ang), F32), jnp.asarray(np.sin(ang), F32)
    n = len(pos)
    ones = jnp.ones((n, PE_LO), F32)
    z = lambda w: jnp.zeros((n, w), F32)
    tail = LANES - PE_LO - ROPE_DIM
    cosb = jnp.concatenate([ones, cos, cos, z(tail)], axis=1)
    sin_lo = jnp.concatenate([z(PE_LO), -sin, z(PE_HALF + tail)], axis=1)
    sin_hi = jnp.concatenate([z(PE_LO + PE_HALF), sin, z(tail)], axis=1)
    return cosb, sin_lo, sin_hi


def _prompt_group(x_prompt, meta_tokens, p):
    seq, d = x_prompt.shape
    tp = P0 + seq
    meta_row0 = P0 - N_META
    lead = jnp.concatenate([jnp.zeros((meta_row0, d), F32), meta_tokens.astype(F32)], axis=0)
    q, k, v, gb, pre_tail, z, cq, ckv_raw, misc = front_conv_prompt(
        x_prompt, lead, p["norm1"], p["w_in"], p["conv_w"], p["a_log"], p["dt_bias"], tm=P0,
        first_valid_row=meta_row0)
    s0 = jnp.zeros((1, GDN_HEADS, GDN_DK, GDN_DV), F32)
    gdn_o, s_fin = gdn(q, k, v, gb, s0, C=CHUNK, CB_prep=4, CB_scan=8, SB_scan=1, n_seq=1)
    tabs = _rope_tables(np.arange(tp) - meta_row0)
    ckv, kpe, qh, kh, vt = mla_prep(cq, ckv_raw, misc, tabs, p, tm=256, with_kv=True)
    mla_o = flash_prompt(qh, kh, vt, tq=512, tk=512, n_prefix=N_META)
    y = layer_back(x_prompt, gdn_o, z, mla_o, p, tm=256, row_block0=P0 // 256)
    new_conv = pre_tail[8 - (CONV_WIDTH - 1):]
    return (y, new_conv, s_fin[0], ckv[meta_row0:], kpe[meta_row0:, PE_LO:PE_LO + ROPE_DIM])


def _sample_group(x_sample, cache_conv, state_gdn, cache_ckv, cache_kpe, page_table, p):
    b, L, d = x_sample.shape
    m = b * L
    page = cache_ckv.shape[2]
    past_len = page_table.shape[1] * page
    x = x_sample.reshape(m, d)
    tm = min(m, 256)
    qkv_pre, z, cq, ckv_raw, misc = front_proj(x, p["norm1"], p["w_in"], tm=tm)
    xp = jnp.concatenate([jnp.transpose(cache_conv.astype(F32), (1, 0, 2)),
                          jnp.transpose(qkv_pre.reshape(b, L, CONV_DIM), (1, 0, 2))], axis=0)
    misc_t = jnp.transpose(misc.reshape(b, L, MISC_W), (1, 0, 2))
    q, k, v, gb = conv_sample(xp, misc_t, p["conv_w"], p["a_log"], p["dt_bias"])
    C = 16
    to_chunks = lambda t: jnp.pad(jnp.transpose(t, (1, 0, 2)), ((0, 0), (0, C - L), (0, 0))).reshape(b * C, -1)
    gdn_o, s_fin = gdn(to_chunks(q), to_chunks(k), to_chunks(v), to_chunks(gb), state_gdn.astype(F32),
                       C=C, CB_prep=4 if b % 4 == 0 else 1, CB_scan=1, SB_scan=4 if b % 4 == 0 else 1, n_seq=b)
    gdn_o = gdn_o.reshape(b, C, GDN_V_WIDTH)[:, :L].reshape(m, GDN_V_WIDTH)
    tabs = _rope_tables(np.tile(past_len + np.arange(L), b))
    ckv, kpe, qh = mla_prep(cq, ckv_raw, misc, tabs, p, tm=tm, with_kv=False)
    kpe32 = kpe[:, PE_LO:PE_LO + ROPE_DIM]
    qabs = q_absorb(qh, p["w_abs"])
    R = L * MLA_HEADS
    rows_lh = lambda t: jnp.transpose(t.reshape(MLA_HEADS, b, L, -1), (1, 2, 0, 3)).reshape(b, R, -1)
    qpe = rows_lh(qh[:, :, PE_LO:PE_LO + ROPE_DIM])
    new_ckv = jnp.pad(ckv.reshape(b, L, KV_RANK), ((0, 0), (0, 8 - L), (0, 0)))
    new_kpe_t = jnp.pad(jnp.swapaxes(kpe32.reshape(b, L, ROPE_DIM), 1, 2), ((0, 0), (0, 0), (0, 8 - L)))
    n_pages = page_table.shape[1]
    PG = 64 if n_pages % 64 == 0 else DECODE_SCORE_PAGES
    o_lat = decode_attn(page_table, p["w_uk_t"], rows_lh(qabs), qpe, new_ckv, new_kpe_t,
                        cache_ckv, jnp.swapaxes(cache_kpe, 2, 3), PG=PG, L=L)
    o_lat_h = jnp.transpose(o_lat.reshape(b, L, MLA_HEADS, KV_RANK), (2, 0, 1, 3)).reshape(MLA_HEADS, m, KV_RANK)
    mla_o = latent_out(o_lat_h, p["w_uv_h"])
    y = layer_back(x, gdn_o, z, mla_o, p, tm=tm, row_block0=0)
    new_conv = qkv_pre.reshape(b, L, CONV_DIM)[:, L - (CONV_WIDTH - 1):]
    return (y.reshape(b, L, d), new_conv, s_fin, ckv.reshape(b, L, KV_RANK), kpe32.reshape(b, L, ROPE_DIM))


def kernel(x_prompt, x_sample, cache_conv, state_gdn, cache_ckv, cache_kpe, page_table, meta_tokens, norm1, w_in, conv_w, a_log, dt_bias, gdn_norm, q_a_norm, w_q_b, q_nope_norm, q_pe_norm, kv_a_norm, k_pe_norm, w_uk, w_uv, k_nope_norm, w_o, norm2, w_gate, w_up, w_down):
    assert x_prompt.shape[0] == 1 and norm1.shape[0] == 1, "one prompt sequence, one layer"
    p = _prep_params(norm1[0], w_in[0], conv_w[0], a_log[0], dt_bias[0], gdn_norm[0], q_a_norm[0],
                     w_q_b[0], q_nope_norm[0], q_pe_norm[0], kv_a_norm[0], k_pe_norm[0], w_uk[0],
                     w_uv[0], k_nope_norm[0], w_o[0], norm2[0], w_gate[0], w_up[0], w_down[0])
    y_p, conv_p, st_p, ckv_p, kpe_p = _prompt_group(x_prompt[0], meta_tokens, p)
    y_s, conv_s, st_s, ckv_s, kpe_s = _sample_group(x_sample, cache_conv[0], state_gdn[0], cache_ckv,
                                                     cache_kpe, page_table, p)
    lead = lambda t: t[None, None]
    return (y_p[None], y_s, lead(conv_p), lead(st_p), lead(ckv_p), lead(kpe_p),
            conv_s[None], st_s[None], ckv_s[None], kpe_s[None])
```

```python
import functools
import math

import jax
import jax.numpy as jnp
import numpy as np
from jax import lax
from jax.experimental import pallas as pl
from jax.experimental.pallas import tpu as pltpu

F32 = jnp.float32
BF16 = jnp.bfloat16
HIGHEST = lax.Precision.HIGHEST

N_META = 16
EPS = 1e-6
NEG_INF = -1e30
GDN_HEADS = 4
GDN_DK = 128
GDN_DV = 128
GDN_QK_WIDTH = GDN_HEADS * GDN_DK
GDN_V_WIDTH = GDN_HEADS * GDN_DV
CONV_WIDTH = 4
CONV_DIM = 2 * GDN_QK_WIDTH + GDN_V_WIDTH
CHUNK = 64
MLA_HEADS = 8
Q_RANK = 384
KV_RANK = 256
NOPE_DIM = 64
ROPE_DIM = 32
V_DIM = 64
QK_DIM = NOPE_DIM + ROPE_DIM
ROPE_THETA = 10000.0
SCORE_SCALE_LOG2 = QK_DIM ** -0.5 * math.log2(math.e)

LANES = 128
HEAD_PAD = LANES
PE_LO = NOPE_DIM
PE_HALF = ROPE_DIM // 2
MISC_W = LANES
P0 = 512
FLASH_QPARTS = 2
FLASH_KPARTS = 2
FLASH_AHEAD = 4
VT_ROWS = V_DIM + 16
DECODE_SCORE_PAGES = 4
VMEM_LIMIT = 56 * 1024 * 1024


def _cp(sem, vmem=None):
    return pltpu.CompilerParams(dimension_semantics=sem, vmem_limit_bytes=vmem or VMEM_LIMIT)


def _sigmoid(x):
    return 1.0 / (1.0 + jnp.exp(-x))


def _softplus(x):
    return jnp.maximum(x, 0.0) + jnp.log1p(jnp.exp(-jnp.abs(x)))


def _dot(a, b, precision=None):
    return jnp.dot(a, b, preferred_element_type=F32, precision=precision)


def _dot_nt(a, b, precision=None):
    return lax.dot_general(a, b, (((1,), (1,)), ((), ())), preferred_element_type=F32,
                           precision=precision)


W_SPLITS = (CONV_DIM, GDN_V_WIDTH, Q_RANK, KV_RANK, MISC_W)


def _front_body(x_ref, g_ref, w_ref, *outs):
    x = x_ref[...]
    hn = ((x * lax.rsqrt(jnp.mean(x * x, axis=-1, keepdims=True) + EPS)) * g_ref[...]).astype(BF16)
    off = 0
    for width, o_ref in zip(W_SPLITS, outs):
        o_ref[...] = _dot(hn, w_ref[:, off:off + width])
        off += width


def front_proj(x, norm1, w_r, tm):
    m, d = x.shape
    n_all = sum(W_SPLITS)
    row = lambda i: (i, 0)
    fixed = lambda i: (0, 0)
    return pl.pallas_call(
        _front_body,
        grid=(m // tm,),
        in_specs=[pl.BlockSpec((tm, d), row), pl.BlockSpec((1, d), fixed), pl.BlockSpec((d, n_all), fixed)],
        out_specs=[pl.BlockSpec((tm, w), row) for w in W_SPLITS],
        out_shape=[jax.ShapeDtypeStruct((m, w), F32) for w in W_SPLITS],
        compiler_params=_cp(("parallel",)),
        name="front_proj",
    )(x, norm1.reshape(1, d), w_r)


def _gates(misc, alog, dtb):
    lane = lax.broadcasted_iota(jnp.int32, misc.shape, 1)
    g = -jnp.exp(alog) * _softplus(misc + dtb)
    beta = _sigmoid(misc)
    return jnp.where(lane < GDN_HEADS, g, jnp.where(lane < 2 * GDN_HEADS, beta, 0.0))


def _conv_cols(taps, w_ref, c, q_ref, k_ref, v_ref):
    cols = slice(c * LANES, (c + 1) * LANES)
    acc = taps(0, cols) * w_ref[0:1, cols]
    for j in range(1, CONV_WIDTH):
        acc = acc + taps(j, cols) * w_ref[j:j + 1, cols]
    y = acc * _sigmoid(acc)
    if c < 2 * GDN_HEADS:
        yn = y * lax.rsqrt(jnp.sum(y * y, axis=-1, keepdims=True) + EPS)
        if c < GDN_HEADS:
            q_ref[:, cols] = yn * (GDN_DK ** -0.5)
        else:
            k_ref[:, (c - GDN_HEADS) * LANES:(c - GDN_HEADS + 1) * LANES] = yn
    else:
        v_ref[:, (c - 2 * GDN_HEADS) * LANES:(c - 2 * GDN_HEADS + 1) * LANES] = y


def _front_conv_body(lead_ref, x_ref, g_ref, w_ref, cw_ref, alog_ref, dtb_ref,
                     q_ref, k_ref, v_ref, gb_ref, tail_ref, z_ref, cq_ref, ckv_ref, misc_ref, xs,
                     *, tm, first_valid_row):
    i = pl.program_id(0)
    x = jnp.where(i == 0, lead_ref[...], x_ref[...])
    hn = ((x * lax.rsqrt(jnp.mean(x * x, axis=-1, keepdims=True) + EPS)) * g_ref[...]).astype(BF16)
    xs[0:8, :] = jnp.where(i > 0, xs[tm:tm + 8, :], 0.0)
    xs[8:tm + 8, :] = _dot(hn, w_ref[:, 0:CONV_DIM])
    off = CONV_DIM
    for width, o_ref in zip(W_SPLITS[1:], (z_ref, cq_ref, ckv_ref, misc_ref)):
        o_ref[...] = _dot(hn, w_ref[:, off:off + width])
        off += width
    tail_ref[...] = xs[tm:tm + 8, :]
    taps = lambda j, cols: xs[8 - (CONV_WIDTH - 1) + j:8 - (CONV_WIDTH - 1) + j + tm, cols]
    for c in range(CONV_DIM // LANES):
        _conv_cols(taps, cw_ref, c, q_ref, k_ref, v_ref)
    gb = _gates(misc_ref[...], alog_ref[...], dtb_ref[...])
    row = i * tm + lax.broadcasted_iota(jnp.int32, gb.shape, 0)
    gb_ref[...] = jnp.where(row >= first_valid_row, gb, 0.0)


def front_conv_prompt(x, lead, norm1, w_r, conv_w, alog, dtb, tm, first_valid_row):
    m, d = x.shape
    n_all = sum(W_SPLITS)
    row = lambda i: (i, 0)
    fixed = lambda i: (0, 0)
    n_blocks = m // tm + 1
    rows = n_blocks * tm
    widths = (GDN_QK_WIDTH, GDN_QK_WIDTH, GDN_V_WIDTH, MISC_W) + (CONV_DIM,) + W_SPLITS[1:]
    out_specs = [pl.BlockSpec((tm, w), row) for w in widths]
    out_shape = [jax.ShapeDtypeStruct((rows, w), F32) for w in widths]
    out_specs[4] = pl.BlockSpec((8, CONV_DIM), fixed)
    out_shape[4] = jax.ShapeDtypeStruct((8, CONV_DIM), F32)
    body = functools.partial(_front_conv_body, tm=tm, first_valid_row=first_valid_row)
    return pl.pallas_call(
        body,
        grid=(n_blocks,),
        in_specs=[pl.BlockSpec((tm, d), fixed), pl.BlockSpec((tm, d), lambda i: (jnp.maximum(i - 1, 0), 0)),
                  pl.BlockSpec((1, d), fixed), pl.BlockSpec((d, n_all), fixed),
                  pl.BlockSpec((CONV_WIDTH, CONV_DIM), fixed),
                  pl.BlockSpec((1, MISC_W), fixed), pl.BlockSpec((1, MISC_W), fixed)],
        out_specs=out_specs,
        out_shape=out_shape,
        scratch_shapes=[pltpu.VMEM((tm + 8, CONV_DIM), F32)],
        compiler_params=_cp(("arbitrary",)),
        name="front_conv_prompt",
    )(lead, x, norm1.reshape(1, d), w_r, conv_w, alog, dtb)


def _conv_sample_body(xp_ref, misc_ref, w_ref, alog_ref, dtb_ref, q_ref, k_ref, v_ref, gb_ref, *, L):
    for l in range(L):
        taps = lambda j, cols, l=l: xp_ref[l + j, :, cols]
        qv, kv, vv = (r.at[l] for r in (q_ref, k_ref, v_ref))
        for c in range(CONV_DIM // LANES):
            _conv_cols(taps, w_ref, c, qv, kv, vv)
        gb_ref[l] = _gates(misc_ref[l], alog_ref[...], dtb_ref[...])


def conv_sample(xp, misc, conv_w, alog, dtb):
    lp, b, _ = xp.shape
    L = lp - (CONV_WIDTH - 1)
    body = functools.partial(_conv_sample_body, L=L)
    return pl.pallas_call(
        body,
        out_shape=[jax.ShapeDtypeStruct((L, b, GDN_QK_WIDTH), F32)] * 3
        + [jax.ShapeDtypeStruct((L, b, MISC_W), F32)],
        compiler_params=pltpu.CompilerParams(vmem_limit_bytes=VMEM_LIMIT),
        name="conv_sample",
    )(xp, misc, conv_w, alog, dtb)


def _split_bf16(x):
    hi = x.astype(BF16)
    return hi, (x - hi.astype(F32)).astype(BF16)


def _rows_hi_lo(x):
    return jnp.concatenate(_split_bf16(x), axis=0)


def _gdn_prep_body(q_ref, k_ref, v_ref, gb_ref, wq_ref, ut_ref, qk_ref, kdt_ref, gl_ref, *, C, CB):
    row = lax.broadcasted_iota(jnp.int32, (C, C), 0)
    col = lax.broadcasted_iota(jnp.int32, (C, C), 1)
    incl = row >= col
    strict = row > col
    tri = incl.astype(F32)
    n_sq = int(math.log2(C)) - 1
    inst = [(cc, h) for cc in range(CB) for h in range(GDN_HEADS)]
    rows_of = lambda cc: slice(cc * C, (cc + 1) * C)
    cols_of = lambda h: slice(h * LANES, (h + 1) * LANES)
    g_cum, g_cum_t, gbs = [], [], []
    for cc in range(CB):
        gb = gb_ref[rows_of(cc), :]
        gc_all = _dot(tri, gb, HIGHEST)
        gl_ref[cc] = jnp.broadcast_to(jnp.exp(gc_all[C - 1:C, :]), (8, LANES))
        gbs.append(gb)
        g_cum.append(gc_all)
        g_cum_t.append(gc_all.T)
    ks = [k_ref[rows_of(cc), cols_of(h)] for cc, h in inst]
    both = [_dot_nt(jnp.concatenate([q_ref[rows_of(cc), cols_of(h)], k], axis=0).astype(BF16), k.astype(BF16))
            for (cc, h), k in zip(inst, ks)]
    a, r, gcs = [], [], []
    for i, (cc, h) in enumerate(inst):
        gc = g_cum[cc][:, h:h + 1]
        gr = g_cum_t[cc][h:h + 1, :]
        beta = gbs[cc][:, GDN_HEADS + h:GDN_HEADS + h + 1]
        decay = jnp.where(incl, jnp.exp(jnp.where(incl, gc - gr, 0.0)), 0.0)
        a.append(jnp.where(strict, -(beta * both[i][C:2 * C] * decay), 0.0))
        qk = both[i][0:C] * decay
        qk_ref[rows_of(cc), cols_of(h)] = jnp.concatenate(
            [qk, jnp.zeros((C, LANES - C), F32)], axis=1).astype(BF16)
        e_g = jnp.exp(gc)
        wq_ref[cc, C:2 * C, cols_of(h)] = (e_g * q_ref[rows_of(cc), cols_of(h)]).astype(BF16)
        k_dec = jnp.exp(g_cum[cc][C - 1:C, h:h + 1] - gc) * ks[i]
        kdt_ref[cc, h] = jnp.concatenate([k_dec, k_dec], axis=0).T.astype(BF16)
        r.append(jnp.concatenate([(beta * e_g) * ks[i], beta * v_ref[rows_of(cc), cols_of(h)]], axis=1))
    for step in range(n_sq + 1):
        ab = [x.astype(BF16) for x in a]
        r = [x + _dot(jnp.concatenate([y, y], axis=1), _rows_hi_lo(x)) for x, y in zip(r, ab)]
        if step < n_sq:
            a = [_dot(y, y) for y in ab]
    for i, (cc, h) in enumerate(inst):
        wq_ref[cc, 0:C, cols_of(h)] = r[i][:, 0:LANES].astype(BF16)
        ut_ref[rows_of(cc), cols_of(h)] = r[i][:, LANES:2 * LANES]


def gdn_prep(q, k, v, gb, *, C, CB):
    rows = q.shape[0]
    nc = rows // C
    blk = lambda i: (i, 0)
    body = functools.partial(_gdn_prep_body, C=C, CB=CB)
    return pl.pallas_call(
        body,
        grid=(nc // CB,),
        in_specs=[pl.BlockSpec((CB * C, GDN_QK_WIDTH), blk)] * 3 + [pl.BlockSpec((CB * C, MISC_W), blk)],
        out_specs=[pl.BlockSpec((CB, 2 * C, GDN_QK_WIDTH), lambda i: (i, 0, 0)),
                   pl.BlockSpec((CB * C, GDN_V_WIDTH), blk),
                   pl.BlockSpec((CB * C, GDN_HEADS * LANES), blk),
                   pl.BlockSpec((CB, GDN_HEADS, GDN_DK, 2 * C), lambda i: (i, 0, 0, 0)),
                   pl.BlockSpec((CB, 8, LANES), lambda i: (i, 0, 0))],
        out_shape=[jax.ShapeDtypeStruct((nc, 2 * C, GDN_QK_WIDTH), BF16),
                   jax.ShapeDtypeStruct((rows, GDN_V_WIDTH), F32),
                   jax.ShapeDtypeStruct((rows, GDN_HEADS * LANES), BF16),
                   jax.ShapeDtypeStruct((nc, GDN_HEADS, GDN_DK, 2 * C), BF16),
                   jax.ShapeDtypeStruct((nc, 8, LANES), F32)],
        compiler_params=_cp(("parallel",)),
        name="gdn_prep",
    )(q, k, v, gb)


def _gdn_scan_body(gl_ref, wq_ref, ut_ref, qk_ref, kdt_ref, s0_ref, o_ref, sfin_ref, s_scr,
                   *, C, CB, SB, n_steps):
    sq = pl.program_id(0)
    st = pl.program_id(1)

    @pl.when(st == 0)
    def _():
        s_scr[...] = s0_ref[...]

    cols = [slice(h * LANES, (h + 1) * LANES) for h in range(GDN_HEADS)]
    for cc in range(CB):
        inst = [(sb, h) for sb in range(SB) for h in range(GDN_HEADS)]
        blk = [sb * CB + cc for sb, _ in inst]
        rows = [slice(b * C, (b + 1) * C) for b in blk]
        chunk = [((sq * SB + sb) * n_steps + st) * CB + cc for sb, _ in inst]
        ss = [s_scr[sb, h] for sb, h in inst]
        ws = [_dot(jnp.concatenate([wq_ref[blk[i], :, cols[h]]] * 2, axis=1), _rows_hi_lo(ss[i]))
              for i, (_, h) in enumerate(inst)]
        u2 = [_rows_hi_lo(ut_ref[rows[i], cols[h]] - ws[i][0:C]) for i, (_, h) in enumerate(inst)]
        for i, (sb, h) in enumerate(inst):
            s_scr[sb, h] = gl_ref[chunk[i] * GDN_HEADS + h] * ss[i] + _dot(kdt_ref[blk[i], h], u2[i])
        for i, (_, h) in enumerate(inst):
            o_ref[rows[i], cols[h]] = ws[i][C:2 * C] + _dot(qk_ref[rows[i], h * LANES:h * LANES + C], u2[i][0:C])

    @pl.when(st == n_steps - 1)
    def _():
        sfin_ref[...] = s_scr[...]


def gdn_scan(gl, wq, ut, qk, kdt, s0, *, C, CB, SB, n_seq):
    rows = ut.shape[0]
    nc = rows // C
    n_steps = nc // (n_seq * CB)
    assert SB == 1 or n_steps == 1
    nb = SB * CB
    blk = lambda s, t, gl: (s * n_steps + t, 0)
    blk3 = lambda s, t, gl: (s * n_steps + t, 0, 0)
    blk4 = lambda s, t, gl: (s * n_steps + t, 0, 0, 0)
    st4 = lambda s, t, gl: (s, 0, 0, 0)
    body = functools.partial(_gdn_scan_body, C=C, CB=CB, SB=SB, n_steps=n_steps)
    grid_spec = pltpu.PrefetchScalarGridSpec(
        num_scalar_prefetch=1,
        grid=(n_seq // SB, n_steps),
        in_specs=[pl.BlockSpec((nb, 2 * C, GDN_QK_WIDTH), blk3), pl.BlockSpec((nb * C, GDN_V_WIDTH), blk),
                  pl.BlockSpec((nb * C, GDN_HEADS * LANES), blk),
                  pl.BlockSpec((nb, GDN_HEADS, GDN_DK, 2 * C), blk4),
                  pl.BlockSpec((SB, GDN_HEADS, GDN_DK, GDN_DV), st4)],
        out_specs=[pl.BlockSpec((nb * C, GDN_V_WIDTH), blk),
                   pl.BlockSpec((SB, GDN_HEADS, GDN_DK, GDN_DV), st4)],
        scratch_shapes=[pltpu.VMEM((SB, GDN_HEADS, GDN_DK, GDN_DV), F32)],
    )
    return pl.pallas_call(
        body,
        grid_spec=grid_spec,
        out_shape=[jax.ShapeDtypeStruct((rows, GDN_V_WIDTH), F32), jax.ShapeDtypeStruct(s0.shape, F32)],
        compiler_params=_cp(("arbitrary", "arbitrary")),
        name="gdn_scan",
    )(gl, wq, ut, qk, kdt, s0)


def gdn(q, k, v, gb, s0, *, C, CB_prep, CB_scan, SB_scan, n_seq):
    wq, ut, qk, kdt, glv = gdn_prep(q, k, v, gb, C=C, CB=CB_prep)
    gl = glv[:, 0, :GDN_HEADS].reshape(-1)
    return gdn_scan(gl, wq, ut, qk, kdt, s0, C=C, CB=CB_scan, SB=SB_scan, n_seq=n_seq)


def _rope(y, cosb, sin_lo, sin_hi):
    return (y * cosb + pltpu.roll(y, LANES - PE_HALF, 1) * sin_lo + pltpu.roll(y, PE_HALF, 1) * sin_hi)


def _slab_norm(x, gain, lane, parts="np"):
    x2 = x * x
    is_nope = lane < NOPE_DIM
    is_pe = (lane >= PE_LO) & (lane < PE_LO + ROPE_DIM)
    if parts == "n":
        rinv = lax.rsqrt(jnp.sum(x2, axis=-1, keepdims=True) * (1.0 / NOPE_DIM) + EPS)
    elif parts == "p":
        rinv = lax.rsqrt(jnp.sum(x2, axis=-1, keepdims=True) * (1.0 / ROPE_DIM) + EPS)
    else:
        ms_n = jnp.sum(jnp.where(is_nope, x2, 0.0), axis=-1, keepdims=True) * (1.0 / NOPE_DIM)
        ms_p = jnp.sum(jnp.where(is_pe, x2, 0.0), axis=-1, keepdims=True) * (1.0 / ROPE_DIM)
        rinv = jnp.where(is_nope, lax.rsqrt(ms_n + EPS), lax.rsqrt(ms_p + EPS))
    return (x * rinv) * gain


def _mla_body(cq_ref, ckvr_ref, misc_ref, cos_ref, slo_ref, shi_ref, qan_ref, wqb_ref, qg_ref,
              kvn_ref, kg_ref, *rest, with_kv):
    if with_kv:
        wuk_ref, wuvt_ref, ckv_ref, kpe_ref, q_ref, k_ref, vt_ref = rest
    else:
        ckv_ref, kpe_ref, q_ref = rest
    cosb, slo, shi = cos_ref[...], slo_ref[...], shi_ref[...]
    lane = lax.broadcasted_iota(jnp.int32, cosb.shape, 1)
    cq = cq_ref[...]
    cqn = ((cq * lax.rsqrt(jnp.mean(cq * cq, axis=-1, keepdims=True) + EPS)) * qan_ref[...]).astype(BF16)
    c = ckvr_ref[...]
    ckv = (c * lax.rsqrt(jnp.mean(c * c, axis=-1, keepdims=True) + EPS)) * kvn_ref[...]
    ckv_ref[...] = ckv
    ckv_b = ckv.astype(BF16)
    kpe_raw = jnp.where((lane >= PE_LO) & (lane < PE_LO + ROPE_DIM), misc_ref[...], 0.0)
    kpe = _rope(_slab_norm(kpe_raw, kg_ref[...], lane, "p"), cosb, slo, shi)
    kpe = jnp.where(lane >= PE_LO, kpe, 0.0)
    kpe_ref[...] = kpe
    for h in range(MLA_HEADS):
        cols = slice(h * HEAD_PAD, (h + 1) * HEAD_PAD)
        qh = _dot(cqn, wqb_ref[:, cols])
        qh = _rope(_slab_norm(qh, qg_ref[...], lane), cosb, slo, shi)
        q_ref[h] = (qh * SCORE_SCALE_LOG2).astype(BF16)
        if with_kv:
            kh = _dot(ckv_b, wuk_ref[:, cols])
            k_ref[h] = (_slab_norm(kh, kg_ref[...], lane, "n") + kpe).astype(BF16)
            vt = _dot_nt(wuvt_ref[h], ckv_b)
            ones_row = lax.broadcasted_iota(jnp.int32, vt.shape, 0) == V_DIM
            vt_ref[h] = jnp.where(ones_row, 1.0, vt).astype(BF16)


def mla_prep(cq, ckv_raw, misc, rope_tabs, p, tm, with_kv):
    m = cq.shape[0]
    row = lambda i: (i, 0)
    fixed = lambda i: (0, 0)
    hw = MLA_HEADS * HEAD_PAD
    in_specs = ([pl.BlockSpec((tm, Q_RANK), row), pl.BlockSpec((tm, KV_RANK), row),
                 pl.BlockSpec((tm, MISC_W), row)] + [pl.BlockSpec((tm, LANES), row)] * 3
                + [pl.BlockSpec((1, Q_RANK), fixed), pl.BlockSpec((Q_RANK, hw), fixed),
                   pl.BlockSpec((1, LANES), fixed), pl.BlockSpec((1, KV_RANK), fixed),
                   pl.BlockSpec((1, LANES), fixed)])
    args = [cq, ckv_raw, misc, *rope_tabs, p["q_a_norm"], p["w_q_b"], p["q_gain"], p["kv_a_norm"], p["k_gain"]]
    out_specs = [pl.BlockSpec((tm, KV_RANK), row), pl.BlockSpec((tm, LANES), row),
                 pl.BlockSpec((MLA_HEADS, tm, HEAD_PAD), lambda i: (0, i, 0))]
    out_shape = [jax.ShapeDtypeStruct((m, KV_RANK), F32), jax.ShapeDtypeStruct((m, LANES), F32),
                 jax.ShapeDtypeStruct((MLA_HEADS, m, HEAD_PAD), BF16)]
    if with_kv:
        in_specs += [pl.BlockSpec((KV_RANK, hw), fixed),
                     pl.BlockSpec((MLA_HEADS, VT_ROWS, KV_RANK), lambda i: (0, 0, 0))]
        args += [p["w_uk"], p["w_uv_t"]]
        out_specs += [pl.BlockSpec((MLA_HEADS, tm, HEAD_PAD), lambda i: (0, i, 0)),
                      pl.BlockSpec((MLA_HEADS, VT_ROWS, tm), lambda i: (0, 0, i))]
        out_shape += [jax.ShapeDtypeStruct((MLA_HEADS, m, HEAD_PAD), BF16),
                      jax.ShapeDtypeStruct((MLA_HEADS, VT_ROWS, m), BF16)]
    return pl.pallas_call(
        functools.partial(_mla_body, with_kv=with_kv),
        grid=(m // tm,),
        in_specs=in_specs,
        out_specs=out_specs,
        out_shape=out_shape,
        compiler_params=_cp(("parallel",)),
        name="mla_prep",
    )(*args)


def _flash_body(qmap_ref, kmap_ref, q_ref, k_ref, vt_ref, kpre_ref, vtpre_ref, o_ref, m_scr, acc_scr,
                *, tq, tk, q_block0):
    step = pl.program_id(0)
    q0 = (qmap_ref[step] + q_block0) * tq
    k0 = kmap_ref[step] * tk

    @pl.when(k0 == P0)
    def _():
        sts = [_dot_nt(kpre_ref[h], q_ref[h]) for h in range(MLA_HEADS)]
        m0s = [jnp.max(st, axis=0, keepdims=True) for st in sts]
        for h in range(MLA_HEADS):
            m_scr[h] = m0s[h]
            acc_scr[h] = _dot(vtpre_ref[h], jnp.exp2(sts[h] - m0s[h]).astype(BF16))

    def update(masked):
        tqp = tq // FLASH_QPARTS
        tkp = tk // FLASH_KPARTS
        items = [(h, kp, qp) for h in range(MLA_HEADS) for kp in range(FLASH_KPARTS)
                 for qp in range(FLASH_QPARTS)]
        scores = lambda h, kp, qp: _dot_nt(k_ref[h, kp * tkp:(kp + 1) * tkp, :],
                                           q_ref[h, qp * tqp:(qp + 1) * tqp, :])
        pending = [scores(*it) for it in items[:FLASH_AHEAD]]
        for n, (h, kp, qp) in enumerate(items):
            st = pending.pop(0)
            if n + FLASH_AHEAD < len(items):
                pending.append(scores(*items[n + FLASH_AHEAD]))
            qc = slice(qp * tqp, (qp + 1) * tqp)
            if masked:
                kpos = k0 + kp * tkp + lax.broadcasted_iota(jnp.int32, (tkp, tqp), 0)
                qpos = q0 + qp * tqp + lax.broadcasted_iota(jnp.int32, (tkp, tqp), 1)
                st = jnp.where(kpos <= qpos, st, NEG_INF)
            m_prev = m_scr[h, :, qc]
            m_new = jnp.maximum(m_prev, jnp.max(st, axis=0, keepdims=True))
            e = jnp.exp2(st - m_new).astype(BF16)
            acc_scr[h, :, qc] = (jnp.exp2(m_prev - m_new) * acc_scr[h, :, qc]
                                 + _dot(vt_ref[h, :, kp * tkp:(kp + 1) * tkp], e))
            m_scr[h, :, qc] = m_new

    needs_mask = k0 + tk - 1 > q0
    pl.when(needs_mask)(lambda: update(True))
    pl.when(jnp.logical_not(needs_mask))(lambda: update(False))

    @pl.when(k0 + tk >= q0 + tq)
    def _():
        for h in range(MLA_HEADS):
            a = acc_scr[h]
            o = (a[0:V_DIM] * (1.0 / a[V_DIM:V_DIM + 1, :])).T
            o_ref[:, h * HEAD_PAD:(h + 1) * HEAD_PAD] = jnp.concatenate(
                [o, jnp.zeros((tq, HEAD_PAD - V_DIM), F32)], axis=1).astype(BF16)


def flash_prompt(q, k, vt, *, tq, tk, n_prefix):
    tp = q.shape[1]
    q_block0 = P0 // tq
    nq = (tp - P0) // tq
    qs, ks = [], []
    for qi in range(nq):
        last_k = ((qi + q_block0 + 1) * tq - 1) // tk
        for ki in range(P0 // tk, last_k + 1):
            qs.append(qi)
            ks.append(ki)
    qmap = jnp.asarray(qs, jnp.int32)
    kmap = jnp.asarray(ks, jnp.int32)
    k_pre = k[:, P0 - n_prefix:P0]
    vt_pre = vt[:, :, P0 - n_prefix:P0]
    body = functools.partial(_flash_body, tq=tq, tk=tk, q_block0=q_block0)
    whole = lambda s, qm, km: (0, 0, 0)
    grid_spec = pltpu.PrefetchScalarGridSpec(
        num_scalar_prefetch=2,
        grid=(len(qs),),
        in_specs=[pl.BlockSpec((MLA_HEADS, tq, HEAD_PAD), lambda s, qm, km: (0, qm[s] + q_block0, 0)),
                  pl.BlockSpec((MLA_HEADS, tk, HEAD_PAD), lambda s, qm, km: (0, km[s], 0)),
                  pl.BlockSpec((MLA_HEADS, VT_ROWS, tk), lambda s, qm, km: (0, 0, km[s])),
                  pl.BlockSpec((MLA_HEADS, n_prefix, HEAD_PAD), whole),
                  pl.BlockSpec((MLA_HEADS, VT_ROWS, n_prefix), whole)],
        out_specs=pl.BlockSpec((tq, MLA_HEADS * HEAD_PAD), lambda s, qm, km: (qm[s], 0)),
        scratch_shapes=[pltpu.VMEM((MLA_HEADS, 1, tq), F32), pltpu.VMEM((MLA_HEADS, VT_ROWS, tq), F32)],
    )
    return pl.pallas_call(
        body,
        grid_spec=grid_spec,
        out_shape=jax.ShapeDtypeStruct((tp - P0, MLA_HEADS * HEAD_PAD), BF16),
        compiler_params=_cp(("arbitrary",)),
        name="flash_prompt",
    )(qmap, kmap, q, k, vt, k_pre, vt_pre)


def _qabs_body(q_ref, w_ref, o_ref):
    o_ref[0] = _dot(q_ref[0], w_ref[0]).astype(BF16)


def q_absorb(q, w_abs):
    h, m, _ = q.shape
    blk = lambda i: (i, 0, 0)
    return pl.pallas_call(
        _qabs_body,
        grid=(h,),
        in_specs=[pl.BlockSpec((1, m, HEAD_PAD), blk), pl.BlockSpec((1, HEAD_PAD, KV_RANK), blk)],
        out_specs=pl.BlockSpec((1, m, KV_RANK), blk),
        out_shape=jax.ShapeDtypeStruct((h, m, KV_RANK), BF16),
        compiler_params=_cp(("parallel",)),
        name="q_absorb",
    )(q, w_abs)


def _decode_scores(waug, qpe, ckv_b, kpe_t, n_lq):
    kp = _dot_nt(waug, ckv_b)
    n_k = MLA_HEADS * NOPE_DIM
    kproj = kp[:n_k]
    ms = jnp.sum((kproj * kproj).reshape(MLA_HEADS, NOPE_DIM, kproj.shape[-1]), axis=1) * (1.0 / NOPE_DIM)
    rinv = lax.rsqrt(ms + EPS)
    s_nope = kp[n_k:]
    s_pe = _dot(qpe, kpe_t)
    rinv_rows = jnp.concatenate([rinv] * n_lq, axis=0)
    return s_nope * rinv_rows + s_pe


def _page_copies(pt_ref, ckv_hbm, kpet_hbm, ckv_buf, kpe_buf, sem, step, slot, *, PG, page):
    out = []
    for i in range(PG):
        pid = pt_ref[step * PG + i]
        out.append(pltpu.make_async_copy(ckv_hbm.at[0, pid], ckv_buf.at[slot, pl.ds(i * page, page)],
                                         sem.at[0, slot]))
        out.append(pltpu.make_async_copy(kpet_hbm.at[0, pid], kpe_buf.at[slot, :, pl.ds(i * page, page)],
                                         sem.at[1, slot]))
    return out


def _decode_body(pt_ref, wukt_ref, qabs_ref, qpe_ref, nckv_ref, nkpet_ref, ckv_hbm, kpet_hbm, o_ref,
                 ckv_buf, kpe_buf, sem, waug, ckvb, m_scr, l_scr, acc_scr, *, PG, L, n_groups, n_steps, page):
    s = pl.program_id(0)
    j = lax.rem(s, n_groups)
    slot = lax.rem(s, 2)
    n_k = MLA_HEADS * NOPE_DIM
    R = MLA_HEADS * L
    copies = functools.partial(_page_copies, pt_ref, ckv_hbm, kpet_hbm, ckv_buf, kpe_buf, sem, PG=PG, page=page)

    nxt = lax.rem(s + 1, n_steps)

    @pl.when(s == 0)
    def _():
        for n, c in enumerate(copies(0, 0)):
            c.start(priority=(n // 2) % 2)

    @pl.when(j == 0)
    def _():
        waug[0:n_k, :] = wukt_ref[...]
        waug[n_k:n_k + R, :] = qabs_ref[0]
        c_new = nckv_ref[0].astype(BF16)
        sc = _decode_scores(waug[...], qpe_ref[0], c_new, nkpet_ref[0].astype(BF16), L)
        r_l = lax.shift_right_logical(lax.broadcasted_iota(jnp.int32, sc.shape, 0),
                                      int(math.log2(MLA_HEADS)))
        t = lax.broadcasted_iota(jnp.int32, sc.shape, 1)
        sc = jnp.where((t <= r_l) & (t < L), sc, NEG_INF)
        m0 = jnp.max(sc, axis=-1, keepdims=True)
        e0 = jnp.exp2(sc - m0)
        m_scr[...] = m0
        l_scr[...] = jnp.sum(e0, axis=-1, keepdims=True)
        acc_scr[...] = _dot(e0.astype(BF16), c_new)

    for c in copies(s, slot):
        c.wait()

    w = waug[...]
    qpe = qpe_ref[0]

    def scores(p):
        tok = pl.ds(p * DECODE_SCORE_PAGES * page, DECODE_SCORE_PAGES * page)
        c2 = ckv_buf[slot, tok, :].astype(BF16)
        ckvb[p] = c2
        return _decode_scores(w, qpe, c2, kpe_buf[slot, :, tok].astype(BF16), L)

    ms, ls, accs = [m_scr[...]], [l_scr[...]], [acc_scr[...]]
    pending = [scores(0)]
    next_copies = copies(nxt, 1 - slot)
    n_groups_step = PG // DECODE_SCORE_PAGES
    per_group = -(-len(next_copies) // max(n_groups_step // 4, 1))
    for p in range(n_groups_step):
        sc = pending.pop(0)
        if p + 1 < PG // DECODE_SCORE_PAGES:
            pending.append(scores(p + 1))
        for n, c in enumerate(next_copies[p * per_group:(p + 1) * per_group]):
            c.start(priority=(n // 2) % 2)
        m_p = jnp.max(sc, axis=-1, keepdims=True)
        e = jnp.exp2(sc - m_p)
        ms.append(m_p)
        ls.append(jnp.sum(e, axis=-1, keepdims=True))
        accs.append(_dot(e.astype(BF16), ckvb[p]))
    m = functools.reduce(jnp.maximum, ms)
    fs = [jnp.exp2(x - m) for x in ms]
    l = functools.reduce(lambda x, y: x + y, [f * x for f, x in zip(fs, ls)])
    acc = functools.reduce(lambda x, y: x + y, [f * x for f, x in zip(fs, accs)])
    m_scr[...], l_scr[...], acc_scr[...] = m, l, acc

    @pl.when(j == n_groups - 1)
    def _():
        o_ref[0] = acc / l

    @pl.when(s == n_steps - 1)
    def _():
        for c in copies(nxt, 1 - slot):
            c.wait()


def decode_attn(page_table, wuk_t, qabs, qpe, new_ckv, new_kpe_t, cache_ckv, cache_kpe_t, *, PG, L):
    b, n_pages = page_table.shape
    page = cache_ckv.shape[2]
    R = MLA_HEADS * L
    n_k = MLA_HEADS * NOPE_DIM
    n_groups = n_pages // PG
    per_b = lambda s, pt: (s // n_groups, 0, 0)
    fixed = lambda s, pt: (0, 0)
    body = functools.partial(_decode_body, PG=PG, L=L, n_groups=n_groups, n_steps=b * n_groups, page=page)
    grid_spec = pltpu.PrefetchScalarGridSpec(
        num_scalar_prefetch=1,
        grid=(b * n_groups,),
        in_specs=[pl.BlockSpec((n_k, KV_RANK), fixed), pl.BlockSpec((1, R, KV_RANK), per_b),
                  pl.BlockSpec((1, R, ROPE_DIM), per_b), pl.BlockSpec((1, 8, KV_RANK), per_b),
                  pl.BlockSpec((1, ROPE_DIM, 8), per_b),
                  pl.BlockSpec(memory_space=pl.ANY), pl.BlockSpec(memory_space=pl.ANY)],
        out_specs=pl.BlockSpec((1, R, KV_RANK), per_b),
        scratch_shapes=[pltpu.VMEM((2, PG * page, KV_RANK), F32), pltpu.VMEM((2, ROPE_DIM, PG * page), F32),
                        pltpu.SemaphoreType.DMA((2, 2)),
                        pltpu.VMEM((n_k + R, KV_RANK), BF16), pltpu.VMEM((PG // DECODE_SCORE_PAGES, DECODE_SCORE_PAGES * page, KV_RANK), BF16),
                        pltpu.VMEM((R, 1), F32), pltpu.VMEM((R, 1), F32), pltpu.VMEM((R, KV_RANK), F32)],
    )
    return pl.pallas_call(
        body,
        grid_spec=grid_spec,
        out_shape=jax.ShapeDtypeStruct((b, R, KV_RANK), F32),
        compiler_params=_cp(("arbitrary",)),
        name="decode_attn",
    )(page_table.reshape(-1), wuk_t, qabs, qpe, new_ckv, new_kpe_t, cache_ckv, cache_kpe_t)


def _latent_out_body(x_ref, w_ref, o_ref):
    o_ref[...] = _dot(x_ref[0].astype(BF16), w_ref[0]).astype(BF16)


def latent_out(o_lat, w_uv_h):
    h, m, _ = o_lat.shape
    return pl.pallas_call(
        _latent_out_body,
        grid=(h,),
        in_specs=[pl.BlockSpec((1, m, KV_RANK), lambda i: (i, 0, 0)),
                  pl.BlockSpec((1, KV_RANK, HEAD_PAD), lambda i: (i, 0, 0))],
        out_specs=pl.BlockSpec((m, HEAD_PAD), lambda i: (0, i)),
        out_shape=jax.ShapeDtypeStruct((m, h * HEAD_PAD), BF16),
        compiler_params=_cp(("parallel",)),
        name="latent_out",
    )(o_lat, w_uv_h)


def _back_body(h_ref, o_ref, z_ref, mla_ref, gg_ref, wog_ref, wom_ref, n2_ref, wg_ref, wu_ref, wd_ref, y_ref):
    h1 = h_ref[...] + _dot(mla_ref[...], wom_ref[...])
    for h in range(GDN_HEADS):
        cols = slice(h * LANES, (h + 1) * LANES)
        o = o_ref[:, cols]
        z = z_ref[:, cols]
        y = ((o * lax.rsqrt(jnp.mean(o * o, axis=-1, keepdims=True) + EPS)) * gg_ref[...]) * (z * _sigmoid(z))
        h1 = h1 + _dot(y.astype(BF16), wog_ref[cols, :])
    hn = ((h1 * lax.rsqrt(jnp.mean(h1 * h1, axis=-1, keepdims=True) + EPS)) * n2_ref[...]).astype(BF16)
    g = _dot(hn, wg_ref[...])
    u = _dot(hn, wu_ref[...])
    y_ref[...] = h1 + _dot(((g * _sigmoid(g)) * u).astype(BF16), wd_ref[...])


def layer_back(h, gdn_o, z, mla_o, p, tm, row_block0):
    m, d = h.shape
    dff = p["w_gate"].shape[1]
    row = lambda i: (i, 0)
    prow = lambda i: (i + row_block0, 0)
    fixed = lambda i: (0, 0)
    hw = MLA_HEADS * HEAD_PAD
    return pl.pallas_call(
        _back_body,
        grid=(m // tm,),
        in_specs=[pl.BlockSpec((tm, d), row), pl.BlockSpec((tm, GDN_V_WIDTH), prow),
                  pl.BlockSpec((tm, GDN_V_WIDTH), prow), pl.BlockSpec((tm, hw), row),
                  pl.BlockSpec((1, LANES), fixed), pl.BlockSpec((GDN_V_WIDTH, d), fixed),
                  pl.BlockSpec((hw, d), fixed), pl.BlockSpec((1, d), fixed),
                  pl.BlockSpec((d, dff), fixed), pl.BlockSpec((d, dff), fixed), pl.BlockSpec((dff, d), fixed)],
        out_specs=pl.BlockSpec((tm, d), row),
        out_shape=jax.ShapeDtypeStruct((m, d), F32),
        compiler_params=_cp(("parallel",)),
        name="layer_back",
    )(h, gdn_o, z, mla_o, p["gdn_norm"], p["w_o_gdn"], p["w_o_mla"], p["norm2"],
      p["w_gate"], p["w_up"], p["w_down"])


def _head_slabs(w, width):
    k, h, _ = w.shape
    return jnp.pad(w, ((0, 0), (0, 0), (0, HEAD_PAD - width))).reshape(k, h * HEAD_PAD)


def _pad_lanes(v, lo=0):
    return jnp.pad(v.astype(F32), (lo, LANES - lo - v.shape[0])).reshape(1, LANES)


def _prep_params(norm1, w_in, conv_w, a_log, dt_bias, gdn_norm, q_a_norm, w_q_b, q_nope_norm,
                 q_pe_norm, kv_a_norm, k_pe_norm, w_uk, w_uv, k_nope_norm, w_o, norm2,
                 w_gate, w_up, w_down):
    d = w_in.shape[0]
    o = 0
    cuts = {}
    for name, wdt in (("qkv", CONV_DIM), ("z", GDN_V_WIDTH), ("a", GDN_HEADS), ("b", GDN_HEADS),
                      ("cq", Q_RANK), ("ckv", KV_RANK), ("kpe", ROPE_DIM)):
        cuts[name] = w_in[:, o:o + wdt]
        o += wdt
    zeros = lambda n: jnp.zeros((d, n), w_in.dtype)
    misc = jnp.concatenate([cuts["a"], cuts["b"], zeros(PE_LO - 2 * GDN_HEADS), cuts["kpe"],
                            zeros(MISC_W - PE_LO - ROPE_DIM)], axis=1)
    w_r = jnp.concatenate([cuts["qkv"], cuts["z"], cuts["cq"], cuts["ckv"], misc], axis=1).astype(BF16)
    wqb = w_q_b.reshape(Q_RANK, MLA_HEADS, QK_DIM)
    gain_q = jnp.concatenate([q_nope_norm, q_pe_norm]).astype(F32)
    gain_k = jnp.concatenate([k_nope_norm, k_pe_norm]).astype(F32)
    w_uk_hdr = jnp.transpose(w_uk, (1, 2, 0))
    w_abs = jnp.pad(w_uk_hdr * k_nope_norm[None, :, None], ((0, 0), (0, HEAD_PAD - NOPE_DIM), (0, 0)))
    w_o_mla = jnp.pad(w_o[GDN_V_WIDTH:].reshape(MLA_HEADS, V_DIM, d),
                      ((0, 0), (0, HEAD_PAD - V_DIM), (0, 0))).reshape(MLA_HEADS * HEAD_PAD, d)
    return dict(
        norm1=norm1, w_in=w_r, conv_w=conv_w.astype(F32),
        a_log=_pad_lanes(a_log), dt_bias=_pad_lanes(dt_bias),
        gdn_norm=gdn_norm.reshape(1, GDN_DV).astype(F32),
        q_a_norm=q_a_norm.reshape(1, Q_RANK).astype(F32), w_q_b=_head_slabs(wqb, QK_DIM).astype(BF16),
        q_gain=_pad_lanes(gain_q), kv_a_norm=kv_a_norm.reshape(1, KV_RANK).astype(F32),
        w_uk=_head_slabs(w_uk, NOPE_DIM).astype(BF16), k_gain=_pad_lanes(gain_k),
        w_uv_t=jnp.pad(jnp.transpose(w_uv, (1, 2, 0)), ((0, 0), (0, VT_ROWS - V_DIM), (0, 0))).astype(BF16),
        w_uk_t=w_uk_hdr.reshape(MLA_HEADS * NOPE_DIM, KV_RANK).astype(BF16),
        w_abs=w_abs.astype(BF16),
        w_uv_h=jnp.pad(jnp.transpose(w_uv, (1, 0, 2)), ((0, 0), (0, 0), (0, HEAD_PAD - V_DIM))).astype(BF16),
        w_o_gdn=w_o[:GDN_V_WIDTH].astype(BF16), w_o_mla=w_o_mla.astype(BF16),
        norm2=norm2.reshape(1, d).astype(F32),
        w_gate=w_gate.astype(BF16), w_up=w_up.astype(BF16), w_down=w_down.astype(BF16),
    )


def _rope_tables(pos):
    inv_freq = np.power(ROPE_THETA, -np.arange(0, ROPE_DIM, 2, dtype=np.float64) / ROPE_DIM)
    ang = np.asarray(pos, np.float64)[:, None] * inv_freq[None, :]
    cos, sin = jnp.asarray(np.cos(ang), F32), jnp.asarray(np.sin(ang), F32)
    n = len(pos)
    ones = jnp.ones((n, PE_LO), F32)
    z = lambda w: jnp.zeros((n, w), F32)
    tail = LANES - PE_LO - ROPE_DIM
    cosb = jnp.concatenate([ones, cos, cos, z(tail)], axis=1)
    sin_lo = jnp.concatenate([z(PE_LO), -sin, z(PE_HALF + tail)], axis=1)
    sin_hi = jnp.concatenate([z(PE_LO + PE_HALF), sin, z(tail)], axis=1)
    return cosb, sin_lo, sin_hi


def _prompt_group(x_prompt, meta_tokens, p):
    seq, d = x_prompt.shape
    tp = P0 + seq
    meta_row0 = P0 - N_META
    lead = jnp.concatenate([jnp.zeros((meta_row0, d), F32), meta_tokens.astype(F32)], axis=0)
    q, k, v, gb, pre_tail, z, cq, ckv_raw, misc = front_conv_prompt(
        x_prompt, lead, p["norm1"], p["w_in"], p["conv_w"], p["a_log"], p["dt_bias"], tm=P0,
        first_valid_row=meta_row0)
    s0 = jnp.zeros((1, GDN_HEADS, GDN_DK, GDN_DV), F32)
    gdn_o, s_fin = gdn(q, k, v, gb, s0, C=CHUNK, CB_prep=4, CB_scan=8, SB_scan=1, n_seq=1)
    tabs = _rope_tables(np.arange(tp) - meta_row0)
    ckv, kpe, qh, kh, vt = mla_prep(cq, ckv_raw, misc, tabs, p, tm=256, with_kv=True)
    mla_o = flash_prompt(qh, kh, vt, tq=512, tk=512, n_prefix=N_META)
    y = layer_back(x_prompt, gdn_o, z, mla_o, p, tm=256, row_block0=P0 // 256)
    new_conv = pre_tail[8 - (CONV_WIDTH - 1):]
    return (y, new_conv, s_fin[0], ckv[meta_row0:], kpe[meta_row0:, PE_LO:PE_LO + ROPE_DIM])


def _sample_group(x_sample, cache_conv, state_gdn, cache_ckv, cache_kpe, page_table, p):
    b, L, d = x_sample.shape
    m = b * L
    page = cache_ckv.shape[2]
    past_len = page_table.shape[1] * page
    x = x_sample.reshape(m, d)
    tm = min(m, 256)
    qkv_pre, z, cq, ckv_raw, misc = front_proj(x, p["norm1"], p["w_in"], tm=tm)
    xp = jnp.concatenate([jnp.transpose(cache_conv.astype(F32), (1, 0, 2)),
                          jnp.transpose(qkv_pre.reshape(b, L, CONV_DIM), (1, 0, 2))], axis=0)
    misc_t = jnp.transpose(misc.reshape(b, L, MISC_W), (1, 0, 2))
    q, k, v, gb = conv_sample(xp, misc_t, p["conv_w"], p["a_log"], p["dt_bias"])
    C = 16
    to_chunks = lambda t: jnp.pad(jnp.transpose(t, (1, 0, 2)), ((0, 0), (0, C - L), (0, 0))).reshape(b * C, -1)
    gdn_o, s_fin = gdn(to_chunks(q), to_chunks(k), to_chunks(v), to_chunks(gb), state_gdn.astype(F32),
                       C=C, CB_prep=8 if b % 8 == 0 else 1, CB_scan=1, SB_scan=4 if b % 4 == 0 else 1, n_seq=b)
    gdn_o = gdn_o.reshape(b, C, GDN_V_WIDTH)[:, :L].reshape(m, GDN_V_WIDTH)
    tabs = _rope_tables(np.tile(past_len + np.arange(L), b))
    ckv, kpe, qh = mla_prep(cq, ckv_raw, misc, tabs, p, tm=tm, with_kv=False)
    kpe32 = kpe[:, PE_LO:PE_LO + ROPE_DIM]
    qabs = q_absorb(qh, p["w_abs"])
    R = L * MLA_HEADS
    rows_lh = lambda t: jnp.transpose(t.reshape(MLA_HEADS, b, L, -1), (1, 2, 0, 3)).reshape(b, R, -1)
    qpe = rows_lh(qh[:, :, PE_LO:PE_LO + ROPE_DIM])
    new_ckv = jnp.pad(ckv.reshape(b, L, KV_RANK), ((0, 0), (0, 8 - L), (0, 0)))
    new_kpe_t = jnp.pad(jnp.swapaxes(kpe32.reshape(b, L, ROPE_DIM), 1, 2), ((0, 0), (0, 0), (0, 8 - L)))
    n_pages = page_table.shape[1]
    PG = 64 if n_pages % 64 == 0 else DECODE_SCORE_PAGES
    o_lat = decode_attn(page_table, p["w_uk_t"], rows_lh(qabs), qpe, new_ckv, new_kpe_t,
                        cache_ckv, jnp.swapaxes(cache_kpe, 2, 3), PG=PG, L=L)
    o_lat_h = jnp.transpose(o_lat.reshape(b, L, MLA_HEADS, KV_RANK), (2, 0, 1, 3)).reshape(MLA_HEADS, m, KV_RANK)
    mla_o = latent_out(o_lat_h, p["w_uv_h"])
    y = layer_back(x, gdn_o, z, mla_o, p, tm=tm, row_block0=0)
    new_conv = qkv_pre.reshape(b, L, CONV_DIM)[:, L - (CONV_WIDTH - 1):]
    return (y.reshape(b, L, d), new_conv, s_fin, ckv.reshape(b, L, KV_RANK), kpe32.reshape(b, L, ROPE_DIM))


def kernel(x_prompt, x_sample, cache_conv, state_gdn, cache_ckv, cache_kpe, page_table, meta_tokens, norm1, w_in, conv_w, a_log, dt_bias, gdn_norm, q_a_norm, w_q_b, q_nope_norm, q_pe_norm, kv_a_norm, k_pe_norm, w_uk, w_uv, k_nope_norm, w_o, norm2, w_gate, w_up, w_down):
    assert x_prompt.shape[0] == 1 and norm1.shape[0] == 1, "one prompt sequence, one layer"
    p = _prep_params(norm1[0], w_in[0], conv_w[0], a_log[0], dt_bias[0], gdn_norm[0], q_a_norm[0],
                     w_q_b[0], q_nope_norm[0], q_pe_norm[0], kv_a_norm[0], k_pe_norm[0], w_uk[0],
                     w_uv[0], k_nope_norm[0], w_o[0], norm2[0], w_gate[0], w_up[0], w_down[0])
    y_p, conv_p, st_p, ckv_p, kpe_p = _prompt_group(x_prompt[0], meta_tokens, p)
    y_s, conv_s, st_s, ckv_s, kpe_s = _sample_group(x_sample, cache_conv[0], state_gdn[0], cache_ckv,
                                                     cache_kpe, page_table, p)
    lead = lambda t: t[None, None]
    return (y_p[None], y_s, lead(conv_p), lead(st_p), lead(ckv_p), lead(kpe_p),
            conv_s[None], st_s[None], ckv_s[None], kpe_s[None])
```

```python
import functools
import math

import jax
import jax.numpy as jnp
import numpy as np
from jax import lax
from jax.experimental import pallas as pl
from jax.experimental.pallas import tpu as pltpu

F32 = jnp.float32
BF16 = jnp.bfloat16
HIGHEST = lax.Precision.HIGHEST

N_META = 16
EPS = 1e-6
NEG_INF = -1e30
GDN_HEADS = 4
GDN_DK = 128
GDN_DV = 128
GDN_QK_WIDTH = GDN_HEADS * GDN_DK
GDN_V_WIDTH = GDN_HEADS * GDN_DV
CONV_WIDTH = 4
CONV_DIM = 2 * GDN_QK_WIDTH + GDN_V_WIDTH
CHUNK = 64
MLA_HEADS = 8
Q_RANK = 384
KV_RANK = 256
NOPE_DIM = 64
ROPE_DIM = 32
V_DIM = 64
QK_DIM = NOPE_DIM + ROPE_DIM
ROPE_THETA = 10000.0
SCORE_SCALE_LOG2 = QK_DIM ** -0.5 * math.log2(math.e)

LANES = 128
SUBLANES = 8
BF16_ROWS = 16
MXU_WIDTH = 256
VMEM_BYTES = 64 * 1024 * 1024
VMEM_LIMIT = VMEM_BYTES - 8 * 1024 * 1024

HEAD_PAD = LANES
PE_LO = NOPE_DIM
PE_HALF = ROPE_DIM // 2
MISC_W = LANES
VT_ROWS = V_DIM + BF16_ROWS

P0 = 512
ROW_TILE = MXU_WIDTH
FLASH_TQ = 512
FLASH_TK = 512
FLASH_QPARTS = FLASH_TQ // MXU_WIDTH
FLASH_KPARTS = FLASH_TK // MXU_WIDTH
FLASH_AHEAD = 4
GDN_PREP_CHUNKS = 4
GDN_SCAN_CHUNKS = 8
SAMPLE_CHUNK = BF16_ROWS
SAMPLE_PREP_CHUNKS = 8
SAMPLE_SCAN_SEQS = 4
DECODE_STEP_PAGES = 128
DECODE_SCORE_PAGES = 4


def _cp(sem, vmem=None):
    return pltpu.CompilerParams(dimension_semantics=sem, vmem_limit_bytes=vmem or VMEM_LIMIT)


def _sigmoid(x):
    return 1.0 / (1.0 + jnp.exp(-x))


def _softplus(x):
    return jnp.maximum(x, 0.0) + jnp.log1p(jnp.exp(-jnp.abs(x)))


def _dot(a, b, precision=None):
    return jnp.dot(a, b, preferred_element_type=F32, precision=precision)


def _dot_nt(a, b, precision=None):
    return lax.dot_general(a, b, (((1,), (1,)), ((), ())), preferred_element_type=F32,
                           precision=precision)


W_SPLITS = (CONV_DIM, GDN_V_WIDTH, Q_RANK, KV_RANK, MISC_W)


def _front_body(x_ref, g_ref, w_ref, *outs):
    x = x_ref[...]
    hn = ((x * lax.rsqrt(jnp.mean(x * x, axis=-1, keepdims=True) + EPS)) * g_ref[...]).astype(BF16)
    off = 0
    for width, o_ref in zip(W_SPLITS, outs):
        o_ref[...] = _dot(hn, w_ref[:, off:off + width])
        off += width


def front_proj(x, norm1, w_r, tm):
    m, d = x.shape
    n_all = sum(W_SPLITS)
    row = lambda i: (i, 0)
    fixed = lambda i: (0, 0)
    return pl.pallas_call(
        _front_body,
        grid=(m // tm,),
        in_specs=[pl.BlockSpec((tm, d), row), pl.BlockSpec((1, d), fixed), pl.BlockSpec((d, n_all), fixed)],
        out_specs=[pl.BlockSpec((tm, w), row) for w in W_SPLITS],
        out_shape=[jax.ShapeDtypeStruct((m, w), F32) for w in W_SPLITS],
        compiler_params=_cp(("parallel",)),
        name="front_proj",
    )(x, norm1.reshape(1, d), w_r)


def _gates(misc, alog, dtb):
    lane = lax.broadcasted_iota(jnp.int32, misc.shape, 1)
    g = -jnp.exp(alog) * _softplus(misc + dtb)
    beta = _sigmoid(misc)
    return jnp.where(lane < GDN_HEADS, g, jnp.where(lane < 2 * GDN_HEADS, beta, 0.0))


def _conv_cols(taps, w_ref, c, q_ref, k_ref, v_ref):
    cols = slice(c * LANES, (c + 1) * LANES)
    acc = taps(0, cols) * w_ref[0:1, cols]
    for j in range(1, CONV_WIDTH):
        acc = acc + taps(j, cols) * w_ref[j:j + 1, cols]
    y = acc * _sigmoid(acc)
    if c < 2 * GDN_HEADS:
        yn = y * lax.rsqrt(jnp.sum(y * y, axis=-1, keepdims=True) + EPS)
        if c < GDN_HEADS:
            q_ref[:, cols] = yn * (GDN_DK ** -0.5)
        else:
            k_ref[:, (c - GDN_HEADS) * LANES:(c - GDN_HEADS + 1) * LANES] = yn
    else:
        v_ref[:, (c - 2 * GDN_HEADS) * LANES:(c - 2 * GDN_HEADS + 1) * LANES] = y


def _front_conv_body(lead_ref, x_ref, g_ref, w_ref, cw_ref, alog_ref, dtb_ref,
                     q_ref, k_ref, v_ref, gb_ref, tail_ref, z_ref, cq_ref, ckv_ref, misc_ref, xs,
                     *, tm, first_valid_row):
    i = pl.program_id(0)
    x = jnp.where(i == 0, lead_ref[...], x_ref[...])
    hn = ((x * lax.rsqrt(jnp.mean(x * x, axis=-1, keepdims=True) + EPS)) * g_ref[...]).astype(BF16)
    halo = SUBLANES
    xs[0:halo, :] = jnp.where(i > 0, xs[tm:tm + halo, :], 0.0)
    xs[halo:tm + halo, :] = _dot(hn, w_ref[:, 0:CONV_DIM])
    off = CONV_DIM
    for width, o_ref in zip(W_SPLITS[1:], (z_ref, cq_ref, ckv_ref, misc_ref)):
        o_ref[...] = _dot(hn, w_ref[:, off:off + width])
        off += width
    tail_ref[...] = xs[tm:tm + halo, :]
    taps = lambda j, cols: xs[halo - (CONV_WIDTH - 1) + j:halo - (CONV_WIDTH - 1) + j + tm, cols]
    for c in range(CONV_DIM // LANES):
        _conv_cols(taps, cw_ref, c, q_ref, k_ref, v_ref)
    gb = _gates(misc_ref[...], alog_ref[...], dtb_ref[...])
    row = i * tm + lax.broadcasted_iota(jnp.int32, gb.shape, 0)
    gb_ref[...] = jnp.where(row >= first_valid_row, gb, 0.0)


def front_conv_prompt(x, lead, norm1, w_r, conv_w, alog, dtb, tm, first_valid_row):
    m, d = x.shape
    n_all = sum(W_SPLITS)
    row = lambda i: (i, 0)
    fixed = lambda i: (0, 0)
    n_blocks = m // tm + 1
    rows = n_blocks * tm
    widths = (GDN_QK_WIDTH, GDN_QK_WIDTH, GDN_V_WIDTH, MISC_W) + (CONV_DIM,) + W_SPLITS[1:]
    out_specs = [pl.BlockSpec((tm, w), row) for w in widths]
    out_shape = [jax.ShapeDtypeStruct((rows, w), F32) for w in widths]
    out_specs[4] = pl.BlockSpec((SUBLANES, CONV_DIM), fixed)
    out_shape[4] = jax.ShapeDtypeStruct((SUBLANES, CONV_DIM), F32)
    body = functools.partial(_front_conv_body, tm=tm, first_valid_row=first_valid_row)
    return pl.pallas_call(
        body,
        grid=(n_blocks,),
        in_specs=[pl.BlockSpec((tm, d), fixed), pl.BlockSpec((tm, d), lambda i: (jnp.maximum(i - 1, 0), 0)),
                  pl.BlockSpec((1, d), fixed), pl.BlockSpec((d, n_all), fixed),
                  pl.BlockSpec((CONV_WIDTH, CONV_DIM), fixed),
                  pl.BlockSpec((1, MISC_W), fixed), pl.BlockSpec((1, MISC_W), fixed)],
        out_specs=out_specs,
        out_shape=out_shape,
        scratch_shapes=[pltpu.VMEM((tm + SUBLANES, CONV_DIM), F32)],
        compiler_params=_cp(("arbitrary",)),
        name="front_conv_prompt",
    )(lead, x, norm1.reshape(1, d), w_r, conv_w, alog, dtb)


def _conv_sample_body(xp_ref, misc_ref, w_ref, alog_ref, dtb_ref, q_ref, k_ref, v_ref, gb_ref, *, L):
    for l in range(L):
        taps = lambda j, cols, l=l: xp_ref[l + j, :, cols]
        qv, kv, vv = (r.at[l] for r in (q_ref, k_ref, v_ref))
        for c in range(CONV_DIM // LANES):
            _conv_cols(taps, w_ref, c, qv, kv, vv)
        gb_ref[l] = _gates(misc_ref[l], alog_ref[...], dtb_ref[...])


def conv_sample(xp, misc, conv_w, alog, dtb):
    lp, b, _ = xp.shape
    L = lp - (CONV_WIDTH - 1)
    body = functools.partial(_conv_sample_body, L=L)
    return pl.pallas_call(
        body,
        out_shape=[jax.ShapeDtypeStruct((L, b, GDN_QK_WIDTH), F32)] * 3
        + [jax.ShapeDtypeStruct((L, b, MISC_W), F32)],
        compiler_params=pltpu.CompilerParams(vmem_limit_bytes=VMEM_LIMIT),
        name="conv_sample",
    )(xp, misc, conv_w, alog, dtb)


def _split_bf16(x):
    hi = x.astype(BF16)
    return hi, (x - hi.astype(F32)).astype(BF16)


def _rows_hi_lo(x):
    return jnp.concatenate(_split_bf16(x), axis=0)


def _gdn_prep_body(q_ref, k_ref, v_ref, gb_ref, wq_ref, ut_ref, qk_ref, kdt_ref, gl_ref, *, C, CB):
    row = lax.broadcasted_iota(jnp.int32, (C, C), 0)
    col = lax.broadcasted_iota(jnp.int32, (C, C), 1)
    incl = row >= col
    strict = row > col
    tri = incl.astype(F32)
    n_sq = int(math.log2(C)) - 1
    inst = [(cc, h) for cc in range(CB) for h in range(GDN_HEADS)]
    rows_of = lambda cc: slice(cc * C, (cc + 1) * C)
    cols_of = lambda h: slice(h * LANES, (h + 1) * LANES)
    g_cum, g_cum_t, gbs = [], [], []
    for cc in range(CB):
        gb = gb_ref[rows_of(cc), :]
        gc_all = _dot(tri, gb, HIGHEST)
        gl_ref[cc] = jnp.broadcast_to(jnp.exp(gc_all[C - 1:C, :]), (SUBLANES, LANES))
        gbs.append(gb)
        g_cum.append(gc_all)
        g_cum_t.append(gc_all.T)
    ks = [k_ref[rows_of(cc), cols_of(h)] for cc, h in inst]
    both = [_dot_nt(jnp.concatenate([q_ref[rows_of(cc), cols_of(h)], k], axis=0).astype(BF16), k.astype(BF16))
            for (cc, h), k in zip(inst, ks)]
    a, r, gcs = [], [], []
    for i, (cc, h) in enumerate(inst):
        gc = g_cum[cc][:, h:h + 1]
        gr = g_cum_t[cc][h:h + 1, :]
        beta = gbs[cc][:, GDN_HEADS + h:GDN_HEADS + h + 1]
        decay = jnp.where(incl, jnp.exp(jnp.where(incl, gc - gr, 0.0)), 0.0)
        a.append(jnp.where(strict, -(beta * both[i][C:2 * C] * decay), 0.0))
        qk = both[i][0:C] * decay
        qk_ref[rows_of(cc), cols_of(h)] = jnp.concatenate(
            [qk, jnp.zeros((C, LANES - C), F32)], axis=1).astype(BF16)
        e_g = jnp.exp(gc)
        wq_ref[cc, C:2 * C, cols_of(h)] = (e_g * q_ref[rows_of(cc), cols_of(h)]).astype(BF16)
        k_dec = jnp.exp(g_cum[cc][C - 1:C, h:h + 1] - gc) * ks[i]
        kdt_ref[cc, h] = jnp.concatenate([k_dec, k_dec], axis=0).T.astype(BF16)
        r.append(jnp.concatenate([(beta * e_g) * ks[i], beta * v_ref[rows_of(cc), cols_of(h)]], axis=1))
    for step in range(n_sq + 1):
        ab = [x.astype(BF16) for x in a]
        r = [x + _dot(jnp.concatenate([y, y], axis=1), _rows_hi_lo(x)) for x, y in zip(r, ab)]
        if step < n_sq:
            a = [_dot(y, y) for y in ab]
    for i, (cc, h) in enumerate(inst):
        wq_ref[cc, 0:C, cols_of(h)] = r[i][:, 0:LANES].astype(BF16)
        ut_ref[rows_of(cc), cols_of(h)] = r[i][:, LANES:2 * LANES]


def gdn_prep(q, k, v, gb, *, C, CB):
    rows = q.shape[0]
    nc = rows // C
    blk = lambda i: (i, 0)
    body = functools.partial(_gdn_prep_body, C=C, CB=CB)
    return pl.pallas_call(
        body,
        grid=(nc // CB,),
        in_specs=[pl.BlockSpec((CB * C, GDN_QK_WIDTH), blk)] * 3 + [pl.BlockSpec((CB * C, MISC_W), blk)],
        out_specs=[pl.BlockSpec((CB, 2 * C, GDN_QK_WIDTH), lambda i: (i, 0, 0)),
                   pl.BlockSpec((CB * C, GDN_V_WIDTH), blk),
                   pl.BlockSpec((CB * C, GDN_HEADS * LANES), blk),
                   pl.BlockSpec((CB, GDN_HEADS, GDN_DK, 2 * C), lambda i: (i, 0, 0, 0)),
                   pl.BlockSpec((CB, SUBLANES, LANES), lambda i: (i, 0, 0))],
        out_shape=[jax.ShapeDtypeStruct((nc, 2 * C, GDN_QK_WIDTH), BF16),
                   jax.ShapeDtypeStruct((rows, GDN_V_WIDTH), F32),
                   jax.ShapeDtypeStruct((rows, GDN_HEADS * LANES), BF16),
                   jax.ShapeDtypeStruct((nc, GDN_HEADS, GDN_DK, 2 * C), BF16),
                   jax.ShapeDtypeStruct((nc, SUBLANES, LANES), F32)],
        compiler_params=_cp(("parallel",)),
        name="gdn_prep",
    )(q, k, v, gb)


def _gdn_scan_body(gl_ref, wq_ref, ut_ref, qk_ref, kdt_ref, s0_ref, o_ref, sfin_ref, s_scr,
                   *, C, CB, SB, n_steps):
    sq = pl.program_id(0)
    st = pl.program_id(1)

    @pl.when(st == 0)
    def _():
        s_scr[...] = s0_ref[...]

    cols = [slice(h * LANES, (h + 1) * LANES) for h in range(GDN_HEADS)]
    for cc in range(CB):
        inst = [(sb, h) for sb in range(SB) for h in range(GDN_HEADS)]
        blk = [sb * CB + cc for sb, _ in inst]
        rows = [slice(b * C, (b + 1) * C) for b in blk]
        chunk = [((sq * SB + sb) * n_steps + st) * CB + cc for sb, _ in inst]
        ss = [s_scr[sb, h] for sb, h in inst]
        ws = [_dot(jnp.concatenate([wq_ref[blk[i], :, cols[h]]] * 2, axis=1), _rows_hi_lo(ss[i]))
              for i, (_, h) in enumerate(inst)]
        u2 = [_rows_hi_lo(ut_ref[rows[i], cols[h]] - ws[i][0:C]) for i, (_, h) in enumerate(inst)]
        for i, (sb, h) in enumerate(inst):
            s_scr[sb, h] = gl_ref[chunk[i] * GDN_HEADS + h] * ss[i] + _dot(kdt_ref[blk[i], h], u2[i])
        for i, (_, h) in enumerate(inst):
            o_ref[rows[i], cols[h]] = ws[i][C:2 * C] + _dot(qk_ref[rows[i], h * LANES:h * LANES + C], u2[i][0:C])

    @pl.when(st == n_steps - 1)
    def _():
        sfin_ref[...] = s_scr[...]


def gdn_scan(gl, wq, ut, qk, kdt, s0, *, C, CB, SB, n_seq):
    rows = ut.shape[0]
    nc = rows // C
    n_steps = nc // (n_seq * CB)
    assert SB == 1 or n_steps == 1
    nb = SB * CB
    blk = lambda s, t, gl: (s * n_steps + t, 0)
    blk3 = lambda s, t, gl: (s * n_steps + t, 0, 0)
    blk4 = lambda s, t, gl: (s * n_steps + t, 0, 0, 0)
    st4 = lambda s, t, gl: (s, 0, 0, 0)
    body = functools.partial(_gdn_scan_body, C=C, CB=CB, SB=SB, n_steps=n_steps)
    grid_spec = pltpu.PrefetchScalarGridSpec(
        num_scalar_prefetch=1,
        grid=(n_seq // SB, n_steps),
        in_specs=[pl.BlockSpec((nb, 2 * C, GDN_QK_WIDTH), blk3), pl.BlockSpec((nb * C, GDN_V_WIDTH), blk),
                  pl.BlockSpec((nb * C, GDN_HEADS * LANES), blk),
                  pl.BlockSpec((nb, GDN_HEADS, GDN_DK, 2 * C), blk4),
                  pl.BlockSpec((SB, GDN_HEADS, GDN_DK, GDN_DV), st4)],
        out_specs=[pl.BlockSpec((nb * C, GDN_V_WIDTH), blk),
                   pl.BlockSpec((SB, GDN_HEADS, GDN_DK, GDN_DV), st4)],
        scratch_shapes=[pltpu.VMEM((SB, GDN_HEADS, GDN_DK, GDN_DV), F32)],
    )
    return pl.pallas_call(
        body,
        grid_spec=grid_spec,
        out_shape=[jax.ShapeDtypeStruct((rows, GDN_V_WIDTH), F32), jax.ShapeDtypeStruct(s0.shape, F32)],
        compiler_params=_cp(("arbitrary", "arbitrary")),
        name="gdn_scan",
    )(gl, wq, ut, qk, kdt, s0)


def gdn(q, k, v, gb, s0, *, C, CB_prep, CB_scan, SB_scan, n_seq):
    wq, ut, qk, kdt, glv = gdn_prep(q, k, v, gb, C=C, CB=CB_prep)
    gl = glv[:, 0, :GDN_HEADS].reshape(-1)
    return gdn_scan(gl, wq, ut, qk, kdt, s0, C=C, CB=CB_scan, SB=SB_scan, n_seq=n_seq)


def _rope(y, cosb, sin_lo, sin_hi):
    return (y * cosb + pltpu.roll(y, LANES - PE_HALF, 1) * sin_lo + pltpu.roll(y, PE_HALF, 1) * sin_hi)


def _slab_norm(x, gain, lane, parts="np"):
    x2 = x * x
    is_nope = lane < NOPE_DIM
    is_pe = (lane >= PE_LO) & (lane < PE_LO + ROPE_DIM)
    if parts == "n":
        rinv = lax.rsqrt(jnp.sum(x2, axis=-1, keepdims=True) * (1.0 / NOPE_DIM) + EPS)
    elif parts == "p":
        rinv = lax.rsqrt(jnp.sum(x2, axis=-1, keepdims=True) * (1.0 / ROPE_DIM) + EPS)
    else:
        ms_n = jnp.sum(jnp.where(is_nope, x2, 0.0), axis=-1, keepdims=True) * (1.0 / NOPE_DIM)
        ms_p = jnp.sum(jnp.where(is_pe, x2, 0.0), axis=-1, keepdims=True) * (1.0 / ROPE_DIM)
        rinv = jnp.where(is_nope, lax.rsqrt(ms_n + EPS), lax.rsqrt(ms_p + EPS))
    return (x * rinv) * gain


def _mla_body(cq_ref, ckvr_ref, misc_ref, cos_ref, slo_ref, shi_ref, qan_ref, wqb_ref, qg_ref,
              kvn_ref, kg_ref, *rest, with_kv):
    if with_kv:
        wuk_ref, wuvt_ref, ckv_ref, kpe_ref, q_ref, k_ref, vt_ref = rest
    else:
        ckv_ref, kpe_ref, q_ref = rest
    cosb, slo, shi = cos_ref[...], slo_ref[...], shi_ref[...]
    lane = lax.broadcasted_iota(jnp.int32, cosb.shape, 1)
    cq = cq_ref[...]
    cqn = ((cq * lax.rsqrt(jnp.mean(cq * cq, axis=-1, keepdims=True) + EPS)) * qan_ref[...]).astype(BF16)
    c = ckvr_ref[...]
    ckv = (c * lax.rsqrt(jnp.mean(c * c, axis=-1, keepdims=True) + EPS)) * kvn_ref[...]
    ckv_ref[...] = ckv
    ckv_b = ckv.astype(BF16)
    kpe_raw = jnp.where((lane >= PE_LO) & (lane < PE_LO + ROPE_DIM), misc_ref[...], 0.0)
    kpe = _rope(_slab_norm(kpe_raw, kg_ref[...], lane, "p"), cosb, slo, shi)
    kpe = jnp.where(lane >= PE_LO, kpe, 0.0)
    kpe_ref[...] = kpe
    for h in range(MLA_HEADS):
        cols = slice(h * HEAD_PAD, (h + 1) * HEAD_PAD)
        qh = _dot(cqn, wqb_ref[:, cols])
        qh = _rope(_slab_norm(qh, qg_ref[...], lane), cosb, slo, shi)
        q_ref[h] = (qh * SCORE_SCALE_LOG2).astype(BF16)
        if with_kv:
            kh = _dot(ckv_b, wuk_ref[:, cols])
            k_ref[h] = (_slab_norm(kh, kg_ref[...], lane, "n") + kpe).astype(BF16)
            vt = _dot_nt(wuvt_ref[h], ckv_b)
            ones_row = lax.broadcasted_iota(jnp.int32, vt.shape, 0) == V_DIM
            vt_ref[h] = jnp.where(ones_row, 1.0, vt).astype(BF16)


def mla_prep(cq, ckv_raw, misc, rope_tabs, p, tm, with_kv):
    m = cq.shape[0]
    row = lambda i: (i, 0)
    fixed = lambda i: (0, 0)
    hw = MLA_HEADS * HEAD_PAD
    in_specs = ([pl.BlockSpec((tm, Q_RANK), row), pl.BlockSpec((tm, KV_RANK), row),
                 pl.BlockSpec((tm, MISC_W), row)] + [pl.BlockSpec((tm, LANES), row)] * 3
                + [pl.BlockSpec((1, Q_RANK), fixed), pl.BlockSpec((Q_RANK, hw), fixed),
                   pl.BlockSpec((1, LANES), fixed), pl.BlockSpec((1, KV_RANK), fixed),
                   pl.BlockSpec((1, LANES), fixed)])
    args = [cq, ckv_raw, misc, *rope_tabs, p["q_a_norm"], p["w_q_b"], p["q_gain"], p["kv_a_norm"], p["k_gain"]]
    out_specs = [pl.BlockSpec((tm, KV_RANK), row), pl.BlockSpec((tm, LANES), row),
                 pl.BlockSpec((MLA_HEADS, tm, HEAD_PAD), lambda i: (0, i, 0))]
    out_shape = [jax.ShapeDtypeStruct((m, KV_RANK), F32), jax.ShapeDtypeStruct((m, LANES), F32),
                 jax.ShapeDtypeStruct((MLA_HEADS, m, HEAD_PAD), BF16)]
    if with_kv:
        in_specs += [pl.BlockSpec((KV_RANK, hw), fixed),
                     pl.BlockSpec((MLA_HEADS, VT_ROWS, KV_RANK), lambda i: (0, 0, 0))]
        args += [p["w_uk"], p["w_uv_t"]]
        out_specs += [pl.BlockSpec((MLA_HEADS, tm, HEAD_PAD), lambda i: (0, i, 0)),
                      pl.BlockSpec((MLA_HEADS, VT_ROWS, tm), lambda i: (0, 0, i))]
        out_shape += [jax.ShapeDtypeStruct((MLA_HEADS, m, HEAD_PAD), BF16),
                      jax.ShapeDtypeStruct((MLA_HEADS, VT_ROWS, m), BF16)]
    return pl.pallas_call(
        functools.partial(_mla_body, with_kv=with_kv),
        grid=(m // tm,),
        in_specs=in_specs,
        out_specs=out_specs,
        out_shape=out_shape,
        compiler_params=_cp(("parallel",)),
        name="mla_prep",
    )(*args)


def _flash_body(qmap_ref, kmap_ref, q_ref, k_ref, vt_ref, kpre_ref, vtpre_ref, o_ref, m_scr, acc_scr,
                *, tq, tk, q_block0):
    step = pl.program_id(0)
    q0 = (qmap_ref[step] + q_block0) * tq
    k0 = kmap_ref[step] * tk

    @pl.when(k0 == P0)
    def _():
        sts = [_dot_nt(kpre_ref[h], q_ref[h]) for h in range(MLA_HEADS)]
        m0s = [jnp.max(st, axis=0, keepdims=True) for st in sts]
        for h in range(MLA_HEADS):
            m_scr[h] = m0s[h]
            acc_scr[h] = _dot(vtpre_ref[h], jnp.exp2(sts[h] - m0s[h]).astype(BF16))

    def update(masked):
        tqp = tq // FLASH_QPARTS
        tkp = tk // FLASH_KPARTS
        items = [(h, kp, qp) for h in range(MLA_HEADS) for kp in range(FLASH_KPARTS)
                 for qp in range(FLASH_QPARTS)]
        scores = lambda h, kp, qp: _dot_nt(k_ref[h, kp * tkp:(kp + 1) * tkp, :],
                                           q_ref[h, qp * tqp:(qp + 1) * tqp, :])
        pending = [scores(*it) for it in items[:FLASH_AHEAD]]
        for n, (h, kp, qp) in enumerate(items):
            st = pending.pop(0)
            if n + FLASH_AHEAD < len(items):
                pending.append(scores(*items[n + FLASH_AHEAD]))
            qc = slice(qp * tqp, (qp + 1) * tqp)
            if masked:
                kpos = k0 + kp * tkp + lax.broadcasted_iota(jnp.int32, (tkp, tqp), 0)
                qpos = q0 + qp * tqp + lax.broadcasted_iota(jnp.int32, (tkp, tqp), 1)
                st = jnp.where(kpos <= qpos, st, NEG_INF)
            m_prev = m_scr[h, :, qc]
            m_new = jnp.maximum(m_prev, jnp.max(st, axis=0, keepdims=True))
            e = jnp.exp2(st - m_new).astype(BF16)
            acc_scr[h, :, qc] = (jnp.exp2(m_prev - m_new) * acc_scr[h, :, qc]
                                 + _dot(vt_ref[h, :, kp * tkp:(kp + 1) * tkp], e))
            m_scr[h, :, qc] = m_new

    needs_mask = k0 + tk - 1 > q0
    pl.when(needs_mask)(lambda: update(True))
    pl.when(jnp.logical_not(needs_mask))(lambda: update(False))

    @pl.when(k0 + tk >= q0 + tq)
    def _():
        for h in range(MLA_HEADS):
            a = acc_scr[h]
            o = (a[0:V_DIM] * (1.0 / a[V_DIM:V_DIM + 1, :])).T
            o_ref[:, h * HEAD_PAD:(h + 1) * HEAD_PAD] = jnp.concatenate(
                [o, jnp.zeros((tq, HEAD_PAD - V_DIM), F32)], axis=1).astype(BF16)


def flash_prompt(q, k, vt, *, tq, tk, n_prefix):
    tp = q.shape[1]
    q_block0 = P0 // tq
    nq = (tp - P0) // tq
    qs, ks = [], []
    for qi in range(nq):
        last_k = ((qi + q_block0 + 1) * tq - 1) // tk
        for ki in range(P0 // tk, last_k + 1):
            qs.append(qi)
            ks.append(ki)
    qmap = jnp.asarray(qs, jnp.int32)
    kmap = jnp.asarray(ks, jnp.int32)
    k_pre = k[:, P0 - n_prefix:P0]
    vt_pre = vt[:, :, P0 - n_prefix:P0]
    body = functools.partial(_flash_body, tq=tq, tk=tk, q_block0=q_block0)
    whole = lambda s, qm, km: (0, 0, 0)
    grid_spec = pltpu.PrefetchScalarGridSpec(
        num_scalar_prefetch=2,
        grid=(len(qs),),
        in_specs=[pl.BlockSpec((MLA_HEADS, tq, HEAD_PAD), lambda s, qm, km: (0, qm[s] + q_block0, 0)),
                  pl.BlockSpec((MLA_HEADS, tk, HEAD_PAD), lambda s, qm, km: (0, km[s], 0)),
                  pl.BlockSpec((MLA_HEADS, VT_ROWS, tk), lambda s, qm, km: (0, 0, km[s])),
                  pl.BlockSpec((MLA_HEADS, n_prefix, HEAD_PAD), whole),
                  pl.BlockSpec((MLA_HEADS, VT_ROWS, n_prefix), whole)],
        out_specs=pl.BlockSpec((tq, MLA_HEADS * HEAD_PAD), lambda s, qm, km: (qm[s], 0)),
        scratch_shapes=[pltpu.VMEM((MLA_HEADS, 1, tq), F32), pltpu.VMEM((MLA_HEADS, VT_ROWS, tq), F32)],
    )
    return pl.pallas_call(
        body,
        grid_spec=grid_spec,
        out_shape=jax.ShapeDtypeStruct((tp - P0, MLA_HEADS * HEAD_PAD), BF16),
        compiler_params=_cp(("arbitrary",)),
        name="flash_prompt",
    )(qmap, kmap, q, k, vt, k_pre, vt_pre)


def _qabs_body(q_ref, w_ref, o_ref):
    o_ref[0] = _dot(q_ref[0], w_ref[0]).astype(BF16)


def q_absorb(q, w_abs):
    h, m, _ = q.shape
    blk = lambda i: (i, 0, 0)
    return pl.pallas_call(
        _qabs_body,
        grid=(h,),
        in_specs=[pl.BlockSpec((1, m, HEAD_PAD), blk), pl.BlockSpec((1, HEAD_PAD, KV_RANK), blk)],
        out_specs=pl.BlockSpec((1, m, KV_RANK), blk),
        out_shape=jax.ShapeDtypeStruct((h, m, KV_RANK), BF16),
        compiler_params=_cp(("parallel",)),
        name="q_absorb",
    )(q, w_abs)


def _decode_scores(waug, qpe, ckv_b, kpe_t, n_lq):
    kp = _dot_nt(waug, ckv_b)
    n_k = MLA_HEADS * NOPE_DIM
    kproj = kp[:n_k]
    ms = jnp.sum((kproj * kproj).reshape(MLA_HEADS, NOPE_DIM, kproj.shape[-1]), axis=1) * (1.0 / NOPE_DIM)
    rinv = lax.rsqrt(ms + EPS)
    s_nope = kp[n_k:]
    s_pe = _dot(qpe, kpe_t)
    rinv_rows = jnp.concatenate([rinv] * n_lq, axis=0)
    return s_nope * rinv_rows + s_pe


def _page_copies(pt_ref, ckv_hbm, kpet_hbm, ckv_buf, kpe_buf, sem, step, slot, *, PG, page):
    out = []
    for i in range(PG):
        pid = pt_ref[step * PG + i]
        out.append(pltpu.make_async_copy(ckv_hbm.at[0, pid], ckv_buf.at[slot, pl.ds(i * page, page)],
                                         sem.at[0, slot]))
        out.append(pltpu.make_async_copy(kpet_hbm.at[0, pid], kpe_buf.at[slot, :, pl.ds(i * page, page)],
                                         sem.at[1, slot]))
    return out


def _decode_body(pt_ref, wukt_ref, qabs_ref, qpe_ref, nckv_ref, nkpet_ref, ckv_hbm, kpet_hbm, o_ref,
                 ckv_buf, kpe_buf, sem, waug, ckvb, m_scr, l_scr, acc_scr, *, PG, L, n_groups, n_steps, page):
    s = pl.program_id(0)
    j = lax.rem(s, n_groups)
    slot = lax.rem(s, 2)
    n_k = MLA_HEADS * NOPE_DIM
    R = MLA_HEADS * L
    copies = functools.partial(_page_copies, pt_ref, ckv_hbm, kpet_hbm, ckv_buf, kpe_buf, sem, PG=PG, page=page)

    nxt = lax.rem(s + 1, n_steps)

    @pl.when(s == 0)
    def _():
        for n, c in enumerate(copies(0, 0)):
            c.start(priority=(n // 2) % 2)

    @pl.when(j == 0)
    def _():
        waug[0:n_k, :] = wukt_ref[...]
        waug[n_k:n_k + R, :] = qabs_ref[0]
        c_new = nckv_ref[0].astype(BF16)
        sc = _decode_scores(waug[...], qpe_ref[0], c_new, nkpet_ref[0].astype(BF16), L)
        r_l = lax.shift_right_logical(lax.broadcasted_iota(jnp.int32, sc.shape, 0),
                                      int(math.log2(MLA_HEADS)))
        t = lax.broadcasted_iota(jnp.int32, sc.shape, 1)
        sc = jnp.where((t <= r_l) & (t < L), sc, NEG_INF)
        m0 = jnp.max(sc, axis=-1, keepdims=True)
        e0 = jnp.exp2(sc - m0)
        m_scr[...] = m0
        l_scr[...] = jnp.sum(e0, axis=-1, keepdims=True)
        acc_scr[...] = _dot(e0.astype(BF16), c_new)

    for c in copies(s, slot):
        c.wait()

    w = waug[...]
    qpe = qpe_ref[0]

    def scores(p):
        tok = pl.ds(p * DECODE_SCORE_PAGES * page, DECODE_SCORE_PAGES * page)
        c2 = ckv_buf[slot, tok, :].astype(BF16)
        ckvb[p] = c2
        return _decode_scores(w, qpe, c2, kpe_buf[slot, :, tok].astype(BF16), L)

    ms, ls, accs = [m_scr[...]], [l_scr[...]], [acc_scr[...]]
    pending = [scores(0)]
    next_copies = copies(nxt, 1 - slot)
    n_groups_step = PG // DECODE_SCORE_PAGES
    per_group = -(-len(next_copies) // max(n_groups_step // 4, 1))
    for p in range(n_groups_step):
        sc = pending.pop(0)
        if p + 1 < PG // DECODE_SCORE_PAGES:
            pending.append(scores(p + 1))
        for n, c in enumerate(next_copies[p * per_group:(p + 1) * per_group]):
            c.start(priority=(n // 2) % 2)
        m_p = jnp.max(sc, axis=-1, keepdims=True)
        e = jnp.exp2(sc - m_p)
        ms.append(m_p)
        ls.append(jnp.sum(e, axis=-1, keepdims=True))
        accs.append(_dot(e.astype(BF16), ckvb[p]))
    m = functools.reduce(jnp.maximum, ms)
    fs = [jnp.exp2(x - m) for x in ms]
    l = functools.reduce(lambda x, y: x + y, [f * x for f, x in zip(fs, ls)])
    acc = functools.reduce(lambda x, y: x + y, [f * x for f, x in zip(fs, accs)])
    m_scr[...], l_scr[...], acc_scr[...] = m, l, acc

    @pl.when(j == n_groups - 1)
    def _():
        o_ref[0] = acc / l

    @pl.when(s == n_steps - 1)
    def _():
        for c in copies(nxt, 1 - slot):
            c.wait()


def decode_attn(page_table, wuk_t, qabs, qpe, new_ckv, new_kpe_t, cache_ckv, cache_kpe_t, *, PG, L):
    b, n_pages = page_table.shape
    page = cache_ckv.shape[2]
    R = MLA_HEADS * L
    n_k = MLA_HEADS * NOPE_DIM
    n_groups = n_pages // PG
    per_b = lambda s, pt: (s // n_groups, 0, 0)
    fixed = lambda s, pt: (0, 0)
    body = functools.partial(_decode_body, PG=PG, L=L, n_groups=n_groups, n_steps=b * n_groups, page=page)
    grid_spec = pltpu.PrefetchScalarGridSpec(
        num_scalar_prefetch=1,
        grid=(b * n_groups,),
        in_specs=[pl.BlockSpec((n_k, KV_RANK), fixed), pl.BlockSpec((1, R, KV_RANK), per_b),
                  pl.BlockSpec((1, R, ROPE_DIM), per_b), pl.BlockSpec((1, SUBLANES, KV_RANK), per_b),
                  pl.BlockSpec((1, ROPE_DIM, SUBLANES), per_b),
                  pl.BlockSpec(memory_space=pl.ANY), pl.BlockSpec(memory_space=pl.ANY)],
        out_specs=pl.BlockSpec((1, R, KV_RANK), per_b),
        scratch_shapes=[pltpu.VMEM((2, PG * page, KV_RANK), F32), pltpu.VMEM((2, ROPE_DIM, PG * page), F32),
                        pltpu.SemaphoreType.DMA((2, 2)),
                        pltpu.VMEM((n_k + R, KV_RANK), BF16), pltpu.VMEM((PG // DECODE_SCORE_PAGES, DECODE_SCORE_PAGES * page, KV_RANK), BF16),
                        pltpu.VMEM((R, 1), F32), pltpu.VMEM((R, 1), F32), pltpu.VMEM((R, KV_RANK), F32)],
    )
    return pl.pallas_call(
        body,
        grid_spec=grid_spec,
        out_shape=jax.ShapeDtypeStruct((b, R, KV_RANK), F32),
        compiler_params=_cp(("arbitrary",)),
        name="decode_attn",
    )(page_table.reshape(-1), wuk_t, qabs, qpe, new_ckv, new_kpe_t, cache_ckv, cache_kpe_t)


def _latent_out_body(x_ref, w_ref, o_ref):
    o_ref[...] = _dot(x_ref[0].astype(BF16), w_ref[0]).astype(BF16)


def latent_out(o_lat, w_uv_h):
    h, m, _ = o_lat.shape
    return pl.pallas_call(
        _latent_out_body,
        grid=(h,),
        in_specs=[pl.BlockSpec((1, m, KV_RANK), lambda i: (i, 0, 0)),
                  pl.BlockSpec((1, KV_RANK, HEAD_PAD), lambda i: (i, 0, 0))],
        out_specs=pl.BlockSpec((m, HEAD_PAD), lambda i: (0, i)),
        out_shape=jax.ShapeDtypeStruct((m, h * HEAD_PAD), BF16),
        compiler_params=_cp(("parallel",)),
        name="latent_out",
    )(o_lat, w_uv_h)


def _back_body(h_ref, o_ref, z_ref, mla_ref, gg_ref, wog_ref, wom_ref, n2_ref, wg_ref, wu_ref, wd_ref, y_ref):
    h1 = h_ref[...] + _dot(mla_ref[...], wom_ref[...])
    for h in range(GDN_HEADS):
        cols = slice(h * LANES, (h + 1) * LANES)
        o = o_ref[:, cols]
        z = z_ref[:, cols]
        y = ((o * lax.rsqrt(jnp.mean(o * o, axis=-1, keepdims=True) + EPS)) * gg_ref[...]) * (z * _sigmoid(z))
        h1 = h1 + _dot(y.astype(BF16), wog_ref[cols, :])
    hn = ((h1 * lax.rsqrt(jnp.mean(h1 * h1, axis=-1, keepdims=True) + EPS)) * n2_ref[...]).astype(BF16)
    g = _dot(hn, wg_ref[...])
    u = _dot(hn, wu_ref[...])
    y_ref[...] = h1 + _dot(((g * _sigmoid(g)) * u).astype(BF16), wd_ref[...])


def layer_back(h, gdn_o, z, mla_o, p, tm, row_block0):
    m, d = h.shape
    dff = p["w_gate"].shape[1]
    row = lambda i: (i, 0)
    prow = lambda i: (i + row_block0, 0)
    fixed = lambda i: (0, 0)
    hw = MLA_HEADS * HEAD_PAD
    return pl.pallas_call(
        _back_body,
        grid=(m // tm,),
        in_specs=[pl.BlockSpec((tm, d), row), pl.BlockSpec((tm, GDN_V_WIDTH), prow),
                  pl.BlockSpec((tm, GDN_V_WIDTH), prow), pl.BlockSpec((tm, hw), row),
                  pl.BlockSpec((1, LANES), fixed), pl.BlockSpec((GDN_V_WIDTH, d), fixed),
                  pl.BlockSpec((hw, d), fixed), pl.BlockSpec((1, d), fixed),
                  pl.BlockSpec((d, dff), fixed), pl.BlockSpec((d, dff), fixed), pl.BlockSpec((dff, d), fixed)],
        out_specs=pl.BlockSpec((tm, d), row),
        out_shape=jax.ShapeDtypeStruct((m, d), F32),
        compiler_params=_cp(("parallel",)),
        name="layer_back",
    )(h, gdn_o, z, mla_o, p["gdn_norm"], p["w_o_gdn"], p["w_o_mla"], p["norm2"],
      p["w_gate"], p["w_up"], p["w_down"])


def _head_slabs(w, width):
    k, h, _ = w.shape
    return jnp.pad(w, ((0, 0), (0, 0), (0, HEAD_PAD - width))).reshape(k, h * HEAD_PAD)


def _pad_lanes(v, lo=0):
    return jnp.pad(v.astype(F32), (lo, LANES - lo - v.shape[0])).reshape(1, LANES)


def _prep_params(norm1, w_in, conv_w, a_log, dt_bias, gdn_norm, q_a_norm, w_q_b, q_nope_norm,
                 q_pe_norm, kv_a_norm, k_pe_norm, w_uk, w_uv, k_nope_norm, w_o, norm2,
                 w_gate, w_up, w_down):
    d = w_in.shape[0]
    o = 0
    cuts = {}
    for name, wdt in (("qkv", CONV_DIM), ("z", GDN_V_WIDTH), ("a", GDN_HEADS), ("b", GDN_HEADS),
                      ("cq", Q_RANK), ("ckv", KV_RANK), ("kpe", ROPE_DIM)):
        cuts[name] = w_in[:, o:o + wdt]
        o += wdt
    zeros = lambda n: jnp.zeros((d, n), w_in.dtype)
    misc = jnp.concatenate([cuts["a"], cuts["b"], zeros(PE_LO - 2 * GDN_HEADS), cuts["kpe"],
                            zeros(MISC_W - PE_LO - ROPE_DIM)], axis=1)
    w_r = jnp.concatenate([cuts["qkv"], cuts["z"], cuts["cq"], cuts["ckv"], misc], axis=1).astype(BF16)
    wqb = w_q_b.reshape(Q_RANK, MLA_HEADS, QK_DIM)
    gain_q = jnp.concatenate([q_nope_norm, q_pe_norm]).astype(F32)
    gain_k = jnp.concatenate([k_nope_norm, k_pe_norm]).astype(F32)
    w_uk_hdr = jnp.transpose(w_uk, (1, 2, 0))
    w_abs = jnp.pad(w_uk_hdr * k_nope_norm[None, :, None], ((0, 0), (0, HEAD_PAD - NOPE_DIM), (0, 0)))
    w_o_mla = jnp.pad(w_o[GDN_V_WIDTH:].reshape(MLA_HEADS, V_DIM, d),
                      ((0, 0), (0, HEAD_PAD - V_DIM), (0, 0))).reshape(MLA_HEADS * HEAD_PAD, d)
    return dict(
        norm1=norm1, w_in=w_r, conv_w=conv_w.astype(F32),
        a_log=_pad_lanes(a_log), dt_bias=_pad_lanes(dt_bias),
        gdn_norm=gdn_norm.reshape(1, GDN_DV).astype(F32),
        q_a_norm=q_a_norm.reshape(1, Q_RANK).astype(F32), w_q_b=_head_slabs(wqb, QK_DIM).astype(BF16),
        q_gain=_pad_lanes(gain_q), kv_a_norm=kv_a_norm.reshape(1, KV_RANK).astype(F32),
        w_uk=_head_slabs(w_uk, NOPE_DIM).astype(BF16), k_gain=_pad_lanes(gain_k),
        w_uv_t=jnp.pad(jnp.transpose(w_uv, (1, 2, 0)), ((0, 0), (0, VT_ROWS - V_DIM), (0, 0))).astype(BF16),
        w_uk_t=w_uk_hdr.reshape(MLA_HEADS * NOPE_DIM, KV_RANK).astype(BF16),
        w_abs=w_abs.astype(BF16),
        w_uv_h=jnp.pad(jnp.transpose(w_uv, (1, 0, 2)), ((0, 0), (0, 0), (0, HEAD_PAD - V_DIM))).astype(BF16),
        w_o_gdn=w_o[:GDN_V_WIDTH].astype(BF16), w_o_mla=w_o_mla.astype(BF16),
        norm2=norm2.reshape(1, d).astype(F32),
        w_gate=w_gate.astype(BF16), w_up=w_up.astype(BF16), w_down=w_down.astype(BF16),
    )


def _rope_tables(pos):
    inv_freq = np.power(ROPE_THETA, -np.arange(0, ROPE_DIM, 2, dtype=np.float64) / ROPE_DIM)
    ang = np.asarray(pos, np.float64)[:, None] * inv_freq[None, :]
    cos, sin = jnp.asarray(np.cos(ang), F32), jnp.asarray(np.sin(ang), F32)
    n = len(pos)
    ones = jnp.ones((n, PE_LO), F32)
    z = lambda w: jnp.zeros((n, w), F32)
    tail = LANES - PE_LO - ROPE_DIM
    cosb = jnp.concatenate([ones, cos, cos, z(tail)], axis=1)
    sin_lo = jnp.concatenate([z(PE_LO), -sin, z(PE_HALF + tail)], axis=1)
    sin_hi = jnp.concatenate([z(PE_LO + PE_HALF), sin, z(tail)], axis=1)
    return cosb, sin_lo, sin_hi


def _prompt_group(x_prompt, meta_tokens, p):
    seq, d = x_prompt.shape
    tp = P0 + seq
    meta_row0 = P0 - N_META
    lead = jnp.concatenate([jnp.zeros((meta_row0, d), F32), meta_tokens.astype(F32)], axis=0)
    q, k, v, gb, pre_tail, z, cq, ckv_raw, misc = front_conv_prompt(
        x_prompt, lead, p["norm1"], p["w_in"], p["conv_w"], p["a_log"], p["dt_bias"], tm=P0,
        first_valid_row=meta_row0)
    s0 = jnp.zeros((1, GDN_HEADS, GDN_DK, GDN_DV), F32)
    gdn_o, s_fin = gdn(q, k, v, gb, s0, C=CHUNK, CB_prep=GDN_PREP_CHUNKS, CB_scan=GDN_SCAN_CHUNKS,
                       SB_scan=1, n_seq=1)
    tabs = _rope_tables(np.arange(tp) - meta_row0)
    ckv, kpe, qh, kh, vt = mla_prep(cq, ckv_raw, misc, tabs, p, tm=ROW_TILE, with_kv=True)
    mla_o = flash_prompt(qh, kh, vt, tq=FLASH_TQ, tk=FLASH_TK, n_prefix=N_META)
    y = layer_back(x_prompt, gdn_o, z, mla_o, p, tm=ROW_TILE, row_block0=P0 // ROW_TILE)
    new_conv = pre_tail[SUBLANES - (CONV_WIDTH - 1):]
    return (y, new_conv, s_fin[0], ckv[meta_row0:], kpe[meta_row0:, PE_LO:PE_LO + ROPE_DIM])


def _sample_group(x_sample, cache_conv, state_gdn, cache_ckv, cache_kpe, page_table, p):
    b, L, d = x_sample.shape
    m = b * L
    page = cache_ckv.shape[2]
    past_len = page_table.shape[1] * page
    x = x_sample.reshape(m, d)
    tm = min(m, ROW_TILE)
    qkv_pre, z, cq, ckv_raw, misc = front_proj(x, p["norm1"], p["w_in"], tm=tm)
    xp = jnp.concatenate([jnp.transpose(cache_conv.astype(F32), (1, 0, 2)),
                          jnp.transpose(qkv_pre.reshape(b, L, CONV_DIM), (1, 0, 2))], axis=0)
    misc_t = jnp.transpose(misc.reshape(b, L, MISC_W), (1, 0, 2))
    q, k, v, gb = conv_sample(xp, misc_t, p["conv_w"], p["a_log"], p["dt_bias"])
    C = SAMPLE_CHUNK
    to_chunks = lambda t: jnp.pad(jnp.transpose(t, (1, 0, 2)), ((0, 0), (0, C - L), (0, 0))).reshape(b * C, -1)
    gdn_o, s_fin = gdn(to_chunks(q), to_chunks(k), to_chunks(v), to_chunks(gb), state_gdn.astype(F32),
                       C=C, CB_prep=SAMPLE_PREP_CHUNKS if b % SAMPLE_PREP_CHUNKS == 0 else 1, CB_scan=1,
                       SB_scan=SAMPLE_SCAN_SEQS if b % SAMPLE_SCAN_SEQS == 0 else 1, n_seq=b)
    gdn_o = gdn_o.reshape(b, C, GDN_V_WIDTH)[:, :L].reshape(m, GDN_V_WIDTH)
    tabs = _rope_tables(np.tile(past_len + np.arange(L), b))
    ckv, kpe, qh = mla_prep(cq, ckv_raw, misc, tabs, p, tm=tm, with_kv=False)
    kpe32 = kpe[:, PE_LO:PE_LO + ROPE_DIM]
    qabs = q_absorb(qh, p["w_abs"])
    R = L * MLA_HEADS
    rows_lh = lambda t: jnp.transpose(t.reshape(MLA_HEADS, b, L, -1), (1, 2, 0, 3)).reshape(b, R, -1)
    qpe = rows_lh(qh[:, :, PE_LO:PE_LO + ROPE_DIM])
    new_ckv = jnp.pad(ckv.reshape(b, L, KV_RANK), ((0, 0), (0, SUBLANES - L), (0, 0)))
    new_kpe_t = jnp.pad(jnp.swapaxes(kpe32.reshape(b, L, ROPE_DIM), 1, 2), ((0, 0), (0, 0), (0, SUBLANES - L)))
    n_pages = page_table.shape[1]
    PG = DECODE_STEP_PAGES if n_pages % DECODE_STEP_PAGES == 0 else DECODE_SCORE_PAGES
    o_lat = decode_attn(page_table, p["w_uk_t"], rows_lh(qabs), qpe, new_ckv, new_kpe_t,
                        cache_ckv, jnp.swapaxes(cache_kpe, 2, 3), PG=PG, L=L)
    o_lat_h = jnp.transpose(o_lat.reshape(b, L, MLA_HEADS, KV_RANK), (2, 0, 1, 3)).reshape(MLA_HEADS, m, KV_RANK)
    mla_o = latent_out(o_lat_h, p["w_uv_h"])
    y = layer_back(x, gdn_o, z, mla_o, p, tm=tm, row_block0=0)
    new_conv = qkv_pre.reshape(b, L, CONV_DIM)[:, L - (CONV_WIDTH - 1):]
    return (y.reshape(b, L, d), new_conv, s_fin, ckv.reshape(b, L, KV_RANK), kpe32.reshape(b, L, ROPE_DIM))


def kernel(x_prompt, x_sample, cache_conv, state_gdn, cache_ckv, cache_kpe, page_table, meta_tokens, norm1, w_in, conv_w, a_log, dt_bias, gdn_norm, q_a_norm, w_q_b, q_nope_norm, q_pe_norm, kv_a_norm, k_pe_norm, w_uk, w_uv, k_nope_norm, w_o, norm2, w_gate, w_up, w_down):
    assert x_prompt.shape[0] == 1 and norm1.shape[0] == 1, "one prompt sequence, one layer"
    p = _prep_params(norm1[0], w_in[0], conv_w[0], a_log[0], dt_bias[0], gdn_norm[0], q_a_norm[0],
                     w_q_b[0], q_nope_norm[0], q_pe_norm[0], kv_a_norm[0], k_pe_norm[0], w_uk[0],
                     w_uv[0], k_nope_norm[0], w_o[0], norm2[0], w_gate[0], w_up[0], w_down[0])
    y_p, conv_p, st_p, ckv_p, kpe_p = _prompt_group(x_prompt[0], meta_tokens, p)
    y_s, conv_s, st_s, ckv_s, kpe_s = _sample_group(x_sample, cache_conv[0], state_gdn[0], cache_ckv,
                                                     cache_kpe, page_table, p)
    lead = lambda t: t[None, None]
    return (y_p[None], y_s, lead(conv_p), lead(st_p), lead(ckv_p), lead(kpe_p),
            conv_s[None], st_s[None], ckv_s[None], kpe_s[None])
```

```python
import functools
import math

import jax
import jax.numpy as jnp
import numpy as np
from jax import lax
from jax.experimental import pallas as pl
from jax.experimental.pallas import tpu as pltpu

F32 = jnp.float32
BF16 = jnp.bfloat16
HIGHEST = lax.Precision.HIGHEST

N_META = 16
EPS = 1e-6
NEG_INF = -1e30
GDN_HEADS = 4
GDN_DK = 128
GDN_DV = 128
GDN_QK_WIDTH = GDN_HEADS * GDN_DK
GDN_V_WIDTH = GDN_HEADS * GDN_DV
CONV_WIDTH = 4
CONV_DIM = 2 * GDN_QK_WIDTH + GDN_V_WIDTH
CHUNK = 64
MLA_HEADS = 8
Q_RANK = 384
KV_RANK = 256
NOPE_DIM = 64
ROPE_DIM = 32
V_DIM = 64
QK_DIM = NOPE_DIM + ROPE_DIM
ROPE_THETA = 10000.0
SCORE_SCALE_LOG2 = QK_DIM ** -0.5 * math.log2(math.e)

LANES = 128
SUBLANES = 8
BF16_ROWS = 16
MXU_WIDTH = 256
VMEM_BYTES = 64 * 1024 * 1024
VMEM_LIMIT = VMEM_BYTES - 8 * 1024 * 1024

HEAD_PAD = LANES
PE_LO = NOPE_DIM
PE_HALF = ROPE_DIM // 2
MISC_W = LANES
VT_ROWS = V_DIM + BF16_ROWS

P0 = 512
ROW_TILE = MXU_WIDTH
FLASH_TQ = 512
FLASH_TK = 512
FLASH_QPARTS = FLASH_TQ // MXU_WIDTH
FLASH_KPARTS = FLASH_TK // MXU_WIDTH
FLASH_AHEAD = 4
GDN_PREP_CHUNKS = 4
GDN_SCAN_CHUNKS = 8
GDN_SPLIT_STEPS = 3
SAMPLE_CHUNK = BF16_ROWS
SAMPLE_PREP_CHUNKS = 8
SAMPLE_SCAN_SEQS = 4
DECODE_STEP_PAGES = 128
DECODE_SCORE_PAGES = 4


def _cp(sem, vmem=None):
    return pltpu.CompilerParams(dimension_semantics=sem, vmem_limit_bytes=vmem or VMEM_LIMIT)


def _sigmoid(x):
    return 1.0 / (1.0 + jnp.exp(-x))


def _softplus(x):
    return jnp.maximum(x, 0.0) + jnp.log1p(jnp.exp(-jnp.abs(x)))


def _dot(a, b, precision=None):
    return jnp.dot(a, b, preferred_element_type=F32, precision=precision)


def _dot_nt(a, b, precision=None):
    return lax.dot_general(a, b, (((1,), (1,)), ((), ())), preferred_element_type=F32,
                           precision=precision)


W_SPLITS = (CONV_DIM, GDN_V_WIDTH, Q_RANK, KV_RANK, MISC_W)


def _front_body(x_ref, g_ref, w_ref, *outs):
    x = x_ref[...]
    hn = ((x * lax.rsqrt(jnp.mean(x * x, axis=-1, keepdims=True) + EPS)) * g_ref[...]).astype(BF16)
    off = 0
    for width, o_ref in zip(W_SPLITS, outs):
        o_ref[...] = _dot(hn, w_ref[:, off:off + width])
        off += width


def front_proj(x, norm1, w_r, tm):
    m, d = x.shape
    n_all = sum(W_SPLITS)
    row = lambda i: (i, 0)
    fixed = lambda i: (0, 0)
    return pl.pallas_call(
        _front_body,
        grid=(m // tm,),
        in_specs=[pl.BlockSpec((tm, d), row), pl.BlockSpec((1, d), fixed), pl.BlockSpec((d, n_all), fixed)],
        out_specs=[pl.BlockSpec((tm, w), row) for w in W_SPLITS],
        out_shape=[jax.ShapeDtypeStruct((m, w), F32) for w in W_SPLITS],
        compiler_params=_cp(("parallel",)),
        name="front_proj",
    )(x, norm1.reshape(1, d), w_r)


def _gates(misc, alog, dtb):
    lane = lax.broadcasted_iota(jnp.int32, misc.shape, 1)
    g = -jnp.exp(alog) * _softplus(misc + dtb)
    beta = _sigmoid(misc)
    return jnp.where(lane < GDN_HEADS, g, jnp.where(lane < 2 * GDN_HEADS, beta, 0.0))


def _conv_cols(conv, c, q_ref, k_ref, v_ref):
    cols = slice(c * LANES, (c + 1) * LANES)
    acc = conv(cols)
    y = acc * _sigmoid(acc)
    if c < 2 * GDN_HEADS:
        yn = y * lax.rsqrt(jnp.sum(y * y, axis=-1, keepdims=True) + EPS)
        if c < GDN_HEADS:
            q_ref[:, cols] = yn * (GDN_DK ** -0.5)
        else:
            k_ref[:, (c - GDN_HEADS) * LANES:(c - GDN_HEADS + 1) * LANES] = yn
    else:
        v_ref[:, (c - 2 * GDN_HEADS) * LANES:(c - 2 * GDN_HEADS + 1) * LANES] = y


def _front_conv_body(lead_ref, x_ref, g_ref, w_ref, cw_ref, alog_ref, dtb_ref,
                     q_ref, k_ref, v_ref, gb_ref, tail_ref, z_ref, cq_ref, ckv_ref, misc_ref, xs,
                     *, tm, first_valid_row):
    i = pl.program_id(0)
    x = jnp.where(i == 0, lead_ref[...], x_ref[...])
    hn = ((x * lax.rsqrt(jnp.mean(x * x, axis=-1, keepdims=True) + EPS)) * g_ref[...]).astype(BF16)
    halo = SUBLANES
    xs[0:halo, :] = jnp.where(i > 0, xs[tm:tm + halo, :], 0.0)
    xs[halo:tm + halo, :] = _dot(hn, w_ref[:, 0:CONV_DIM])
    off = CONV_DIM
    for width, o_ref in zip(W_SPLITS[1:], (z_ref, cq_ref, ckv_ref, misc_ref)):
        o_ref[...] = _dot(hn, w_ref[:, off:off + width])
        off += width
    tail_ref[...] = xs[tm:tm + halo, :]

    def conv(cols):
        x_ext = xs[:, cols]
        t = x_ext * cw_ref[0:1, cols]
        for j in range(1, CONV_WIDTH):
            t = x_ext * cw_ref[j:j + 1, cols] + pltpu.roll(t, 1, 0)
        return t[halo:, :]

    for c in range(CONV_DIM // LANES):
        _conv_cols(conv, c, q_ref, k_ref, v_ref)
    gb = _gates(misc_ref[...], alog_ref[...], dtb_ref[...])
    row = i * tm + lax.broadcasted_iota(jnp.int32, gb.shape, 0)
    gb_ref[...] = jnp.where(row >= first_valid_row, gb, 0.0)


def front_conv_prompt(x, lead, norm1, w_r, conv_w, alog, dtb, tm, first_valid_row):
    m, d = x.shape
    n_all = sum(W_SPLITS)
    row = lambda i: (i, 0)
    fixed = lambda i: (0, 0)
    n_blocks = m // tm + 1
    rows = n_blocks * tm
    widths = (GDN_QK_WIDTH, GDN_QK_WIDTH, GDN_V_WIDTH, MISC_W) + (CONV_DIM,) + W_SPLITS[1:]
    out_specs = [pl.BlockSpec((tm, w), row) for w in widths]
    out_shape = [jax.ShapeDtypeStruct((rows, w), F32) for w in widths]
    out_specs[4] = pl.BlockSpec((SUBLANES, CONV_DIM), fixed)
    out_shape[4] = jax.ShapeDtypeStruct((SUBLANES, CONV_DIM), F32)
    body = functools.partial(_front_conv_body, tm=tm, first_valid_row=first_valid_row)
    return pl.pallas_call(
        body,
        grid=(n_blocks,),
        in_specs=[pl.BlockSpec((tm, d), fixed), pl.BlockSpec((tm, d), lambda i: (jnp.maximum(i - 1, 0), 0)),
                  pl.BlockSpec((1, d), fixed), pl.BlockSpec((d, n_all), fixed),
                  pl.BlockSpec((CONV_WIDTH, CONV_DIM), fixed),
                  pl.BlockSpec((1, MISC_W), fixed), pl.BlockSpec((1, MISC_W), fixed)],
        out_specs=out_specs,
        out_shape=out_shape,
        scratch_shapes=[pltpu.VMEM((tm + SUBLANES, CONV_DIM), F32)],
        compiler_params=_cp(("arbitrary",)),
        name="front_conv_prompt",
    )(lead, x, norm1.reshape(1, d), w_r, conv_w, alog, dtb)


def _conv_sample_body(xp_ref, misc_ref, w_ref, alog_ref, dtb_ref, q_ref, k_ref, v_ref, gb_ref, *, L):
    for l in range(L):
        conv = lambda cols, l=l: sum(xp_ref[l + j, :, cols] * w_ref[j:j + 1, cols] for j in range(CONV_WIDTH))
        qv, kv, vv = (r.at[l] for r in (q_ref, k_ref, v_ref))
        for c in range(CONV_DIM // LANES):
            _conv_cols(conv, c, qv, kv, vv)
        gb_ref[l] = _gates(misc_ref[l], alog_ref[...], dtb_ref[...])


def conv_sample(xp, misc, conv_w, alog, dtb):
    lp, b, _ = xp.shape
    L = lp - (CONV_WIDTH - 1)
    body = functools.partial(_conv_sample_body, L=L)
    return pl.pallas_call(
        body,
        out_shape=[jax.ShapeDtypeStruct((L, b, GDN_QK_WIDTH), F32)] * 3
        + [jax.ShapeDtypeStruct((L, b, MISC_W), F32)],
        compiler_params=pltpu.CompilerParams(vmem_limit_bytes=VMEM_LIMIT),
        name="conv_sample",
    )(xp, misc, conv_w, alog, dtb)


def _split_bf16(x):
    hi = x.astype(BF16)
    return hi, (x - hi.astype(F32)).astype(BF16)


def _rows_hi_lo(x):
    return jnp.concatenate(_split_bf16(x), axis=0)


def _gdn_prep_body(q_ref, k_ref, v_ref, gb_ref, wq_ref, ut_ref, qk_ref, kdt_ref, gl_ref, *, C, CB):
    row = lax.broadcasted_iota(jnp.int32, (C, C), 0)
    col = lax.broadcasted_iota(jnp.int32, (C, C), 1)
    incl = row >= col
    strict = row > col
    tri = incl.astype(F32)
    n_sq = int(math.log2(C)) - 1
    inst = [(cc, h) for cc in range(CB) for h in range(GDN_HEADS)]
    rows_of = lambda cc: slice(cc * C, (cc + 1) * C)
    cols_of = lambda h: slice(h * LANES, (h + 1) * LANES)
    g_cum, g_cum_t, gbs = [], [], []
    for cc in range(CB):
        gb = gb_ref[rows_of(cc), :]
        gc_all = _dot(tri, gb, HIGHEST)
        gl_ref[cc] = jnp.broadcast_to(jnp.exp(gc_all[C - 1:C, :]), (SUBLANES, LANES))
        gbs.append(gb)
        g_cum.append(gc_all)
        g_cum_t.append(gc_all.T)
    ks = [k_ref[rows_of(cc), cols_of(h)] for cc, h in inst]
    both = [_dot_nt(jnp.concatenate([q_ref[rows_of(cc), cols_of(h)], k], axis=0).astype(BF16), k.astype(BF16))
            for (cc, h), k in zip(inst, ks)]
    a, r, gcs = [], [], []
    for i, (cc, h) in enumerate(inst):
        gc = g_cum[cc][:, h:h + 1]
        gr = g_cum_t[cc][h:h + 1, :]
        beta = gbs[cc][:, GDN_HEADS + h:GDN_HEADS + h + 1]
        decay = jnp.where(incl, jnp.exp(jnp.where(incl, gc - gr, 0.0)), 0.0)
        a.append(jnp.where(strict, -(beta * both[i][C:2 * C] * decay), 0.0))
        qk = both[i][0:C] * decay
        qk_ref[rows_of(cc), cols_of(h)] = jnp.concatenate(
            [qk, jnp.zeros((C, LANES - C), F32)], axis=1).astype(BF16)
        e_g = jnp.exp(gc)
        wq_ref[cc, C:2 * C, cols_of(h)] = (e_g * q_ref[rows_of(cc), cols_of(h)]).astype(BF16)
        k_dec = jnp.exp(g_cum[cc][C - 1:C, h:h + 1] - gc) * ks[i]
        kdt_ref[cc, h] = jnp.concatenate([k_dec, k_dec], axis=0).T.astype(BF16)
        r.append(jnp.concatenate([(beta * e_g) * ks[i], beta * v_ref[rows_of(cc), cols_of(h)]], axis=1))
    for step in range(n_sq + 1):
        ab = [x.astype(BF16) for x in a]
        if step < GDN_SPLIT_STEPS:
            r = [x + _dot(jnp.concatenate([y, y], axis=1), _rows_hi_lo(x)) for x, y in zip(r, ab)]
        else:
            r = [x + _dot(y, x.astype(BF16)) for x, y in zip(r, ab)]
        if step < n_sq:
            a = [_dot(y, y) for y in ab]
    for i, (cc, h) in enumerate(inst):
        wq_ref[cc, 0:C, cols_of(h)] = r[i][:, 0:LANES].astype(BF16)
        ut_ref[rows_of(cc), cols_of(h)] = r[i][:, LANES:2 * LANES]


def gdn_prep(q, k, v, gb, *, C, CB):
    rows = q.shape[0]
    nc = rows // C
    blk = lambda i: (i, 0)
    body = functools.partial(_gdn_prep_body, C=C, CB=CB)
    return pl.pallas_call(
        body,
        grid=(nc // CB,),
        in_specs=[pl.BlockSpec((CB * C, GDN_QK_WIDTH), blk)] * 3 + [pl.BlockSpec((CB * C, MISC_W), blk)],
        out_specs=[pl.BlockSpec((CB, 2 * C, GDN_QK_WIDTH), lambda i: (i, 0, 0)),
                   pl.BlockSpec((CB * C, GDN_V_WIDTH), blk),
                   pl.BlockSpec((CB * C, GDN_HEADS * LANES), blk),
                   pl.BlockSpec((CB, GDN_HEADS, GDN_DK, 2 * C), lambda i: (i, 0, 0, 0)),
                   pl.BlockSpec((CB, SUBLANES, LANES), lambda i: (i, 0, 0))],
        out_shape=[jax.ShapeDtypeStruct((nc, 2 * C, GDN_QK_WIDTH), BF16),
                   jax.ShapeDtypeStruct((rows, GDN_V_WIDTH), F32),
                   jax.ShapeDtypeStruct((rows, GDN_HEADS * LANES), BF16),
                   jax.ShapeDtypeStruct((nc, GDN_HEADS, GDN_DK, 2 * C), BF16),
                   jax.ShapeDtypeStruct((nc, SUBLANES, LANES), F32)],
        compiler_params=_cp(("parallel",)),
        name="gdn_prep",
    )(q, k, v, gb)


def _gdn_scan_body(gl_ref, wq_ref, ut_ref, qk_ref, kdt_ref, s0_ref, o_ref, sfin_ref, s_scr,
                   *, C, CB, SB, n_steps):
    sq = pl.program_id(0)
    st = pl.program_id(1)

    @pl.when(st == 0)
    def _():
        s_scr[...] = s0_ref[...]

    cols = [slice(h * LANES, (h + 1) * LANES) for h in range(GDN_HEADS)]
    for cc in range(CB):
        inst = [(sb, h) for sb in range(SB) for h in range(GDN_HEADS)]
        blk = [sb * CB + cc for sb, _ in inst]
        rows = [slice(b * C, (b + 1) * C) for b in blk]
        chunk = [((sq * SB + sb) * n_steps + st) * CB + cc for sb, _ in inst]
        ss = [s_scr[sb, h] for sb, h in inst]
        ws = [_dot(jnp.concatenate([wq_ref[blk[i], :, cols[h]]] * 2, axis=1), _rows_hi_lo(ss[i]))
              for i, (_, h) in enumerate(inst)]
        u2 = [_rows_hi_lo(ut_ref[rows[i], cols[h]] - ws[i][0:C]) for i, (_, h) in enumerate(inst)]
        for i, (sb, h) in enumerate(inst):
            s_scr[sb, h] = gl_ref[chunk[i] * GDN_HEADS + h] * ss[i] + _dot(kdt_ref[blk[i], h], u2[i])
        for i, (_, h) in enumerate(inst):
            o_ref[rows[i], cols[h]] = ws[i][C:2 * C] + _dot(qk_ref[rows[i], h * LANES:h * LANES + C], u2[i][0:C])

    @pl.when(st == n_steps - 1)
    def _():
        sfin_ref[...] = s_scr[...]


def gdn_scan(gl, wq, ut, qk, kdt, s0, *, C, CB, SB, n_seq):
    rows = ut.shape[0]
    nc = rows // C
    n_steps = nc // (n_seq * CB)
    assert SB == 1 or n_steps == 1
    nb = SB * CB
    blk = lambda s, t, gl: (s * n_steps + t, 0)
    blk3 = lambda s, t, gl: (s * n_steps + t, 0, 0)
    blk4 = lambda s, t, gl: (s * n_steps + t, 0, 0, 0)
    st4 = lambda s, t, gl: (s, 0, 0, 0)
    body = functools.partial(_gdn_scan_body, C=C, CB=CB, SB=SB, n_steps=n_steps)
    grid_spec = pltpu.PrefetchScalarGridSpec(
        num_scalar_prefetch=1,
        grid=(n_seq // SB, n_steps),
        in_specs=[pl.BlockSpec((nb, 2 * C, GDN_QK_WIDTH), blk3), pl.BlockSpec((nb * C, GDN_V_WIDTH), blk),
                  pl.BlockSpec((nb * C, GDN_HEADS * LANES), blk),
                  pl.BlockSpec((nb, GDN_HEADS, GDN_DK, 2 * C), blk4),
                  pl.BlockSpec((SB, GDN_HEADS, GDN_DK, GDN_DV), st4)],
        out_specs=[pl.BlockSpec((nb * C, GDN_V_WIDTH), blk),
                   pl.BlockSpec((SB, GDN_HEADS, GDN_DK, GDN_DV), st4)],
        scratch_shapes=[pltpu.VMEM((SB, GDN_HEADS, GDN_DK, GDN_DV), F32)],
    )
    return pl.pallas_call(
        body,
        grid_spec=grid_spec,
        out_shape=[jax.ShapeDtypeStruct((rows, GDN_V_WIDTH), F32), jax.ShapeDtypeStruct(s0.shape, F32)],
        compiler_params=_cp(("arbitrary", "arbitrary")),
        name="gdn_scan",
    )(gl, wq, ut, qk, kdt, s0)


def gdn(q, k, v, gb, s0, *, C, CB_prep, CB_scan, SB_scan, n_seq):
    wq, ut, qk, kdt, glv = gdn_prep(q, k, v, gb, C=C, CB=CB_prep)
    gl = glv[:, 0, :GDN_HEADS].reshape(-1)
    return gdn_scan(gl, wq, ut, qk, kdt, s0, C=C, CB=CB_scan, SB=SB_scan, n_seq=n_seq)


def _rope(y, cosb, sin_lo, sin_hi):
    return (y * cosb + pltpu.roll(y, LANES - PE_HALF, 1) * sin_lo + pltpu.roll(y, PE_HALF, 1) * sin_hi)


def _slab_norm(x, gain, lane, parts="np"):
    x2 = x * x
    is_nope = lane < NOPE_DIM
    is_pe = (lane >= PE_LO) & (lane < PE_LO + ROPE_DIM)
    if parts == "n":
        rinv = lax.rsqrt(jnp.sum(x2, axis=-1, keepdims=True) * (1.0 / NOPE_DIM) + EPS)
    elif parts == "p":
        rinv = lax.rsqrt(jnp.sum(x2, axis=-1, keepdims=True) * (1.0 / ROPE_DIM) + EPS)
    else:
        ms_n = jnp.sum(jnp.where(is_nope, x2, 0.0), axis=-1, keepdims=True) * (1.0 / NOPE_DIM)
        ms_p = jnp.sum(jnp.where(is_pe, x2, 0.0), axis=-1, keepdims=True) * (1.0 / ROPE_DIM)
        rinv = jnp.where(is_nope, lax.rsqrt(ms_n + EPS), lax.rsqrt(ms_p + EPS))
    return (x * rinv) * gain


def _mla_body(cq_ref, ckvr_ref, misc_ref, cos_ref, slo_ref, shi_ref, qan_ref, wqb_ref, qg_ref,
              kvn_ref, kg_ref, *rest, with_kv):
    if with_kv:
        wuk_ref, wuvt_ref, ckv_ref, kpe_ref, q_ref, k_ref, vt_ref = rest
    else:
        ckv_ref, kpe_ref, q_ref = rest
    cosb, slo, shi = cos_ref[...], slo_ref[...], shi_ref[...]
    lane = lax.broadcasted_iota(jnp.int32, cosb.shape, 1)
    cq = cq_ref[...]
    cqn = ((cq * lax.rsqrt(jnp.mean(cq * cq, axis=-1, keepdims=True) + EPS)) * qan_ref[...]).astype(BF16)
    c = ckvr_ref[...]
    ckv = (c * lax.rsqrt(jnp.mean(c * c, axis=-1, keepdims=True) + EPS)) * kvn_ref[...]
    ckv_ref[...] = ckv
    ckv_b = ckv.astype(BF16)
    kpe_raw = jnp.where((lane >= PE_LO) & (lane < PE_LO + ROPE_DIM), misc_ref[...], 0.0)
    kpe = _rope(_slab_norm(kpe_raw, kg_ref[...], lane, "p"), cosb, slo, shi)
    kpe = jnp.where(lane >= PE_LO, kpe, 0.0)
    kpe_ref[...] = kpe
    for h in range(MLA_HEADS):
        cols = slice(h * HEAD_PAD, (h + 1) * HEAD_PAD)
        qh = _dot(cqn, wqb_ref[:, cols])
        qh = _rope(_slab_norm(qh, qg_ref[...], lane), cosb, slo, shi)
        q_ref[h] = (qh * SCORE_SCALE_LOG2).astype(BF16)
        if with_kv:
            kh = _dot(ckv_b, wuk_ref[:, cols])
            k_ref[h] = (_slab_norm(kh, kg_ref[...], lane, "n") + kpe).astype(BF16)
            vt = _dot_nt(wuvt_ref[h], ckv_b)
            ones_row = lax.broadcasted_iota(jnp.int32, vt.shape, 0) == V_DIM
            vt_ref[h] = jnp.where(ones_row, 1.0, vt).astype(BF16)


def mla_prep(cq, ckv_raw, misc, rope_tabs, p, tm, with_kv):
    m = cq.shape[0]
    row = lambda i: (i, 0)
    fixed = lambda i: (0, 0)
    hw = MLA_HEADS * HEAD_PAD
    in_specs = ([pl.BlockSpec((tm, Q_RANK), row), pl.BlockSpec((tm, KV_RANK), row),
                 pl.BlockSpec((tm, MISC_W), row)] + [pl.BlockSpec((tm, LANES), row)] * 3
                + [pl.BlockSpec((1, Q_RANK), fixed), pl.BlockSpec((Q_RANK, hw), fixed),
                   pl.BlockSpec((1, LANES), fixed), pl.BlockSpec((1, KV_RANK), fixed),
                   pl.BlockSpec((1, LANES), fixed)])
    args = [cq, ckv_raw, misc, *rope_tabs, p["q_a_norm"], p["w_q_b"], p["q_gain"], p["kv_a_norm"], p["k_gain"]]
    out_specs = [pl.BlockSpec((tm, KV_RANK), row), pl.BlockSpec((tm, LANES), row),
                 pl.BlockSpec((MLA_HEADS, tm, HEAD_PAD), lambda i: (0, i, 0))]
    out_shape = [jax.ShapeDtypeStruct((m, KV_RANK), F32), jax.ShapeDtypeStruct((m, LANES), F32),
                 jax.ShapeDtypeStruct((MLA_HEADS, m, HEAD_PAD), BF16)]
    if with_kv:
        in_specs += [pl.BlockSpec((KV_RANK, hw), fixed),
                     pl.BlockSpec((MLA_HEADS, VT_ROWS, KV_RANK), lambda i: (0, 0, 0))]
        args += [p["w_uk"], p["w_uv_t"]]
        out_specs += [pl.BlockSpec((MLA_HEADS, tm, HEAD_PAD), lambda i: (0, i, 0)),
                      pl.BlockSpec((MLA_HEADS, VT_ROWS, tm), lambda i: (0, 0, i))]
        out_shape += [jax.ShapeDtypeStruct((MLA_HEADS, m, HEAD_PAD), BF16),
                      jax.ShapeDtypeStruct((MLA_HEADS, VT_ROWS, m), BF16)]
    return pl.pallas_call(
        functools.partial(_mla_body, with_kv=with_kv),
        grid=(m // tm,),
        in_specs=in_specs,
        out_specs=out_specs,
        out_shape=out_shape,
        compiler_params=_cp(("parallel",)),
        name="mla_prep",
    )(*args)


def _flash_body(qmap_ref, kmap_ref, q_ref, k_ref, vt_ref, kpre_ref, vtpre_ref, o_ref, m_scr, acc_scr,
                *, tq, tk, q_block0):
    step = pl.program_id(0)
    q0 = (qmap_ref[step] + q_block0) * tq
    k0 = kmap_ref[step] * tk

    @pl.when(k0 == P0)
    def _():
        sts = [_dot_nt(kpre_ref[h], q_ref[h]) for h in range(MLA_HEADS)]
        m0s = [jnp.max(st, axis=0, keepdims=True) for st in sts]
        for h in range(MLA_HEADS):
            m_scr[h] = m0s[h]
            acc_scr[h] = _dot(vtpre_ref[h], jnp.exp2(sts[h] - m0s[h]).astype(BF16))

    def update(masked):
        tqp = tq // FLASH_QPARTS
        tkp = tk // FLASH_KPARTS
        items = [(h, kp, qp) for h in range(MLA_HEADS) for kp in range(FLASH_KPARTS)
                 for qp in range(FLASH_QPARTS)]
        scores = lambda h, kp, qp: _dot_nt(k_ref[h, kp * tkp:(kp + 1) * tkp, :],
                                           q_ref[h, qp * tqp:(qp + 1) * tqp, :])
        pending = [scores(*it) for it in items[:FLASH_AHEAD]]
        for n, (h, kp, qp) in enumerate(items):
            st = pending.pop(0)
            if n + FLASH_AHEAD < len(items):
                pending.append(scores(*items[n + FLASH_AHEAD]))
            qc = slice(qp * tqp, (qp + 1) * tqp)
            if masked:
                kpos = k0 + kp * tkp + lax.broadcasted_iota(jnp.int32, (tkp, tqp), 0)
                qpos = q0 + qp * tqp + lax.broadcasted_iota(jnp.int32, (tkp, tqp), 1)
                st = jnp.where(kpos <= qpos, st, NEG_INF)
            m_prev = m_scr[h, :, qc]
            m_new = jnp.maximum(m_prev, jnp.max(st, axis=0, keepdims=True))
            e = jnp.exp2(st - m_new).astype(BF16)
            acc_scr[h, :, qc] = (jnp.exp2(m_prev - m_new) * acc_scr[h, :, qc]
                                 + _dot(vt_ref[h, :, kp * tkp:(kp + 1) * tkp], e))
            m_scr[h, :, qc] = m_new

    needs_mask = k0 + tk - 1 > q0
    pl.when(needs_mask)(lambda: update(True))
    pl.when(jnp.logical_not(needs_mask))(lambda: update(False))

    @pl.when(k0 + tk >= q0 + tq)
    def _():
        for h in range(MLA_HEADS):
            a = acc_scr[h]
            o = (a[0:V_DIM] * (1.0 / a[V_DIM:V_DIM + 1, :])).T
            o_ref[:, h * HEAD_PAD:(h + 1) * HEAD_PAD] = jnp.concatenate(
                [o, jnp.zeros((tq, HEAD_PAD - V_DIM), F32)], axis=1).astype(BF16)


def flash_prompt(q, k, vt, *, tq, tk, n_prefix):
    tp = q.shape[1]
    q_block0 = P0 // tq
    nq = (tp - P0) // tq
    qs, ks = [], []
    for qi in range(nq):
        last_k = ((qi + q_block0 + 1) * tq - 1) // tk
        for ki in range(P0 // tk, last_k + 1):
            qs.append(qi)
            ks.append(ki)
    qmap = jnp.asarray(qs, jnp.int32)
    kmap = jnp.asarray(ks, jnp.int32)
    k_pre = k[:, P0 - n_prefix:P0]
    vt_pre = vt[:, :, P0 - n_prefix:P0]
    body = functools.partial(_flash_body, tq=tq, tk=tk, q_block0=q_block0)
    whole = lambda s, qm, km: (0, 0, 0)
    grid_spec = pltpu.PrefetchScalarGridSpec(
        num_scalar_prefetch=2,
        grid=(len(qs),),
        in_specs=[pl.BlockSpec((MLA_HEADS, tq, HEAD_PAD), lambda s, qm, km: (0, qm[s] + q_block0, 0)),
                  pl.BlockSpec((MLA_HEADS, tk, HEAD_PAD), lambda s, qm, km: (0, km[s], 0)),
                  pl.BlockSpec((MLA_HEADS, VT_ROWS, tk), lambda s, qm, km: (0, 0, km[s])),
                  pl.BlockSpec((MLA_HEADS, n_prefix, HEAD_PAD), whole),
                  pl.BlockSpec((MLA_HEADS, VT_ROWS, n_prefix), whole)],
        out_specs=pl.BlockSpec((tq, MLA_HEADS * HEAD_PAD), lambda s, qm, km: (qm[s], 0)),
        scratch_shapes=[pltpu.VMEM((MLA_HEADS, 1, tq), F32), pltpu.VMEM((MLA_HEADS, VT_ROWS, tq), F32)],
    )
    return pl.pallas_call(
        body,
        grid_spec=grid_spec,
        out_shape=jax.ShapeDtypeStruct((tp - P0, MLA_HEADS * HEAD_PAD), BF16),
        compiler_params=_cp(("arbitrary",)),
        name="flash_prompt",
    )(qmap, kmap, q, k, vt, k_pre, vt_pre)


def _qabs_body(q_ref, w_ref, o_ref):
    o_ref[0] = _dot(q_ref[0], w_ref[0]).astype(BF16)


def q_absorb(q, w_abs):
    h, m, _ = q.shape
    blk = lambda i: (i, 0, 0)
    return pl.pallas_call(
        _qabs_body,
        grid=(h,),
        in_specs=[pl.BlockSpec((1, m, HEAD_PAD), blk), pl.BlockSpec((1, HEAD_PAD, KV_RANK), blk)],
        out_specs=pl.BlockSpec((1, m, KV_RANK), blk),
        out_shape=jax.ShapeDtypeStruct((h, m, KV_RANK), BF16),
        compiler_params=_cp(("parallel",)),
        name="q_absorb",
    )(q, w_abs)


def _decode_scores(waug, qpe, ckv_b, kpe_t, n_lq):
    kp = _dot_nt(waug, ckv_b)
    n_k = MLA_HEADS * NOPE_DIM
    kproj = kp[:n_k]
    ms = jnp.sum((kproj * kproj).reshape(MLA_HEADS, NOPE_DIM, kproj.shape[-1]), axis=1) * (1.0 / NOPE_DIM)
    rinv = lax.rsqrt(ms + EPS)
    s_nope = kp[n_k:]
    s_pe = _dot(qpe, kpe_t)
    rinv_rows = jnp.concatenate([rinv] * n_lq, axis=0)
    return s_nope * rinv_rows + s_pe


def _page_copies(pt_ref, ckv_hbm, kpet_hbm, ckv_buf, kpe_buf, sem, step, slot, *, PG, page):
    out = []
    for i in range(PG):
        pid = pt_ref[step * PG + i]
        out.append(pltpu.make_async_copy(ckv_hbm.at[0, pid], ckv_buf.at[slot, pl.ds(i * page, page)],
                                         sem.at[0, slot]))
        out.append(pltpu.make_async_copy(kpet_hbm.at[0, pid], kpe_buf.at[slot, :, pl.ds(i * page, page)],
                                         sem.at[1, slot]))
    return out


def _decode_body(pt_ref, wukt_ref, qabs_ref, qpe_ref, nckv_ref, nkpet_ref, ckv_hbm, kpet_hbm, o_ref,
                 ckv_buf, kpe_buf, sem, waug, ckvb, m_scr, l_scr, acc_scr, *, PG, L, n_groups, n_steps, page):
    s = pl.program_id(0)
    j = lax.rem(s, n_groups)
    slot = lax.rem(s, 2)
    n_k = MLA_HEADS * NOPE_DIM
    R = MLA_HEADS * L
    copies = functools.partial(_page_copies, pt_ref, ckv_hbm, kpet_hbm, ckv_buf, kpe_buf, sem, PG=PG, page=page)

    nxt = lax.rem(s + 1, n_steps)

    @pl.when(s == 0)
    def _():
        for n, c in enumerate(copies(0, 0)):
            c.start(priority=(n // 2) % 2)

    @pl.when(j == 0)
    def _():
        waug[0:n_k, :] = wukt_ref[...]
        waug[n_k:n_k + R, :] = qabs_ref[0]
        c_new = nckv_ref[0].astype(BF16)
        sc = _decode_scores(waug[...], qpe_ref[0], c_new, nkpet_ref[0].astype(BF16), L)
        r_l = lax.shift_right_logical(lax.broadcasted_iota(jnp.int32, sc.shape, 0),
                                      int(math.log2(MLA_HEADS)))
        t = lax.broadcasted_iota(jnp.int32, sc.shape, 1)
        sc = jnp.where((t <= r_l) & (t < L), sc, NEG_INF)
        m0 = jnp.max(sc, axis=-1, keepdims=True)
        e0 = jnp.exp2(sc - m0)
        m_scr[...] = m0
        l_scr[...] = jnp.sum(e0, axis=-1, keepdims=True)
        acc_scr[...] = _dot(e0.astype(BF16), c_new)

    for c in copies(s, slot):
        c.wait()

    w = waug[...]
    qpe = qpe_ref[0]

    def scores(p):
        tok = pl.ds(p * DECODE_SCORE_PAGES * page, DECODE_SCORE_PAGES * page)
        c2 = ckv_buf[slot, tok, :].astype(BF16)
        ckvb[p] = c2
        return _decode_scores(w, qpe, c2, kpe_buf[slot, :, tok].astype(BF16), L)

    ms, ls, accs = [m_scr[...]], [l_scr[...]], [acc_scr[...]]
    pending = [scores(0)]
    next_copies = copies(nxt, 1 - slot)
    n_groups_step = PG // DECODE_SCORE_PAGES
    per_group = -(-len(next_copies) // max(n_groups_step // 4, 1))
    for p in range(n_groups_step):
        sc = pending.pop(0)
        if p + 1 < PG // DECODE_SCORE_PAGES:
            pending.append(scores(p + 1))
        for n, c in enumerate(next_copies[p * per_group:(p + 1) * per_group]):
            c.start(priority=(n // 2) % 2)
        m_p = jnp.max(sc, axis=-1, keepdims=True)
        e = jnp.exp2(sc - m_p)
        ms.append(m_p)
        ls.append(jnp.sum(e, axis=-1, keepdims=True))
        accs.append(_dot(e.astype(BF16), ckvb[p]))
    m = functools.reduce(jnp.maximum, ms)
    fs = [jnp.exp2(x - m) for x in ms]
    l = functools.reduce(lambda x, y: x + y, [f * x for f, x in zip(fs, ls)])
    acc = functools.reduce(lambda x, y: x + y, [f * x for f, x in zip(fs, accs)])
    m_scr[...], l_scr[...], acc_scr[...] = m, l, acc

    @pl.when(j == n_groups - 1)
    def _():
        o_ref[0] = acc / l

    @pl.when(s == n_steps - 1)
    def _():
        for c in copies(nxt, 1 - slot):
            c.wait()


def decode_attn(page_table, wuk_t, qabs, qpe, new_ckv, new_kpe_t, cache_ckv, cache_kpe_t, *, PG, L):
    b, n_pages = page_table.shape
    page = cache_ckv.shape[2]
    R = MLA_HEADS * L
    n_k = MLA_HEADS * NOPE_DIM
    n_groups = n_pages // PG
    per_b = lambda s, pt: (s // n_groups, 0, 0)
    fixed = lambda s, pt: (0, 0)
    body = functools.partial(_decode_body, PG=PG, L=L, n_groups=n_groups, n_steps=b * n_groups, page=page)
    grid_spec = pltpu.PrefetchScalarGridSpec(
        num_scalar_prefetch=1,
        grid=(b * n_groups,),
        in_specs=[pl.BlockSpec((n_k, KV_RANK), fixed), pl.BlockSpec((1, R, KV_RANK), per_b),
                  pl.BlockSpec((1, R, ROPE_DIM), per_b), pl.BlockSpec((1, SUBLANES, KV_RANK), per_b),
                  pl.BlockSpec((1, ROPE_DIM, SUBLANES), per_b),
                  pl.BlockSpec(memory_space=pl.ANY), pl.BlockSpec(memory_space=pl.ANY)],
        out_specs=pl.BlockSpec((1, R, KV_RANK), per_b),
        scratch_shapes=[pltpu.VMEM((2, PG * page, KV_RANK), F32), pltpu.VMEM((2, ROPE_DIM, PG * page), F32),
                        pltpu.SemaphoreType.DMA((2, 2)),
                        pltpu.VMEM((n_k + R, KV_RANK), BF16), pltpu.VMEM((PG // DECODE_SCORE_PAGES, DECODE_SCORE_PAGES * page, KV_RANK), BF16),
                        pltpu.VMEM((R, 1), F32), pltpu.VMEM((R, 1), F32), pltpu.VMEM((R, KV_RANK), F32)],
    )
    return pl.pallas_call(
        body,
        grid_spec=grid_spec,
        out_shape=jax.ShapeDtypeStruct((b, R, KV_RANK), F32),
        compiler_params=_cp(("arbitrary",)),
        name="decode_attn",
    )(page_table.reshape(-1), wuk_t, qabs, qpe, new_ckv, new_kpe_t, cache_ckv, cache_kpe_t)


def _latent_out_body(x_ref, w_ref, o_ref):
    o_ref[...] = _dot(x_ref[0].astype(BF16), w_ref[0]).astype(BF16)


def latent_out(o_lat, w_uv_h):
    h, m, _ = o_lat.shape
    return pl.pallas_call(
        _latent_out_body,
        grid=(h,),
        in_specs=[pl.BlockSpec((1, m, KV_RANK), lambda i: (i, 0, 0)),
                  pl.BlockSpec((1, KV_RANK, HEAD_PAD), lambda i: (i, 0, 0))],
        out_specs=pl.BlockSpec((m, HEAD_PAD), lambda i: (0, i)),
        out_shape=jax.ShapeDtypeStruct((m, h * HEAD_PAD), BF16),
        compiler_params=_cp(("parallel",)),
        name="latent_out",
    )(o_lat, w_uv_h)


def _back_body(h_ref, o_ref, z_ref, mla_ref, gg_ref, wog_ref, wom_ref, n2_ref, wg_ref, wu_ref, wd_ref, y_ref):
    h1 = h_ref[...] + _dot(mla_ref[...], wom_ref[...])
    for h in range(GDN_HEADS):
        cols = slice(h * LANES, (h + 1) * LANES)
        o = o_ref[:, cols]
        z = z_ref[:, cols]
        y = ((o * lax.rsqrt(jnp.mean(o * o, axis=-1, keepdims=True) + EPS)) * gg_ref[...]) * (z * _sigmoid(z))
        h1 = h1 + _dot(y.astype(BF16), wog_ref[cols, :])
    hn = ((h1 * lax.rsqrt(jnp.mean(h1 * h1, axis=-1, keepdims=True) + EPS)) * n2_ref[...]).astype(BF16)
    g = _dot(hn, wg_ref[...])
    u = _dot(hn, wu_ref[...])
    y_ref[...] = h1 + _dot(((g * _sigmoid(g)) * u).astype(BF16), wd_ref[...])


def layer_back(h, gdn_o, z, mla_o, p, tm, row_block0):
    m, d = h.shape
    dff = p["w_gate"].shape[1]
    row = lambda i: (i, 0)
    prow = lambda i: (i + row_block0, 0)
    fixed = lambda i: (0, 0)
    hw = MLA_HEADS * HEAD_PAD
    return pl.pallas_call(
        _back_body,
        grid=(m // tm,),
        in_specs=[pl.BlockSpec((tm, d), row), pl.BlockSpec((tm, GDN_V_WIDTH), prow),
                  pl.BlockSpec((tm, GDN_V_WIDTH), prow), pl.BlockSpec((tm, hw), row),
                  pl.BlockSpec((1, LANES), fixed), pl.BlockSpec((GDN_V_WIDTH, d), fixed),
                  pl.BlockSpec((hw, d), fixed), pl.BlockSpec((1, d), fixed),
                  pl.BlockSpec((d, dff), fixed), pl.BlockSpec((d, dff), fixed), pl.BlockSpec((dff, d), fixed)],
        out_specs=pl.BlockSpec((tm, d), row),
        out_shape=jax.ShapeDtypeStruct((m, d), F32),
        compiler_params=_cp(("parallel",)),
        name="layer_back",
    )(h, gdn_o, z, mla_o, p["gdn_norm"], p["w_o_gdn"], p["w_o_mla"], p["norm2"],
      p["w_gate"], p["w_up"], p["w_down"])


def _head_slabs(w, width):
    k, h, _ = w.shape
    return jnp.pad(w, ((0, 0), (0, 0), (0, HEAD_PAD - width))).reshape(k, h * HEAD_PAD)


def _pad_lanes(v, lo=0):
    return jnp.pad(v.astype(F32), (lo, LANES - lo - v.shape[0])).reshape(1, LANES)


def _prep_params(norm1, w_in, conv_w, a_log, dt_bias, gdn_norm, q_a_norm, w_q_b, q_nope_norm,
                 q_pe_norm, kv_a_norm, k_pe_norm, w_uk, w_uv, k_nope_norm, w_o, norm2,
                 w_gate, w_up, w_down):
    d = w_in.shape[0]
    o = 0
    cuts = {}
    for name, wdt in (("qkv", CONV_DIM), ("z", GDN_V_WIDTH), ("a", GDN_HEADS), ("b", GDN_HEADS),
                      ("cq", Q_RANK), ("ckv", KV_RANK), ("kpe", ROPE_DIM)):
        cuts[name] = w_in[:, o:o + wdt]
        o += wdt
    zeros = lambda n: jnp.zeros((d, n), w_in.dtype)
    misc = jnp.concatenate([cuts["a"], cuts["b"], zeros(PE_LO - 2 * GDN_HEADS), cuts["kpe"],
                            zeros(MISC_W - PE_LO - ROPE_DIM)], axis=1)
    w_r = jnp.concatenate([cuts["qkv"], cuts["z"], cuts["cq"], cuts["ckv"], misc], axis=1).astype(BF16)
    wqb = w_q_b.reshape(Q_RANK, MLA_HEADS, QK_DIM)
    gain_q = jnp.concatenate([q_nope_norm, q_pe_norm]).astype(F32)
    gain_k = jnp.concatenate([k_nope_norm, k_pe_norm]).astype(F32)
    w_uk_hdr = jnp.transpose(w_uk, (1, 2, 0))
    w_abs = jnp.pad(w_uk_hdr * k_nope_norm[None, :, None], ((0, 0), (0, HEAD_PAD - NOPE_DIM), (0, 0)))
    w_o_mla = jnp.pad(w_o[GDN_V_WIDTH:].reshape(MLA_HEADS, V_DIM, d),
                      ((0, 0), (0, HEAD_PAD - V_DIM), (0, 0))).reshape(MLA_HEADS * HEAD_PAD, d)
    return dict(
        norm1=norm1, w_in=w_r, conv_w=conv_w.astype(F32),
        a_log=_pad_lanes(a_log), dt_bias=_pad_lanes(dt_bias),
        gdn_norm=gdn_norm.reshape(1, GDN_DV).astype(F32),
        q_a_norm=q_a_norm.reshape(1, Q_RANK).astype(F32), w_q_b=_head_slabs(wqb, QK_DIM).astype(BF16),
        q_gain=_pad_lanes(gain_q), kv_a_norm=kv_a_norm.reshape(1, KV_RANK).astype(F32),
        w_uk=_head_slabs(w_uk, NOPE_DIM).astype(BF16), k_gain=_pad_lanes(gain_k),
        w_uv_t=jnp.pad(jnp.transpose(w_uv, (1, 2, 0)), ((0, 0), (0, VT_ROWS - V_DIM), (0, 0))).astype(BF16),
        w_uk_t=w_uk_hdr.reshape(MLA_HEADS * NOPE_DIM, KV_RANK).astype(BF16),
        w_abs=w_abs.astype(BF16),
        w_uv_h=jnp.pad(jnp.transpose(w_uv, (1, 0, 2)), ((0, 0), (0, 0), (0, HEAD_PAD - V_DIM))).astype(BF16),
        w_o_gdn=w_o[:GDN_V_WIDTH].astype(BF16), w_o_mla=w_o_mla.astype(BF16),
        norm2=norm2.reshape(1, d).astype(F32),
        w_gate=w_gate.astype(BF16), w_up=w_up.astype(BF16), w_down=w_down.astype(BF16),
    )


def _rope_tables(pos):
    inv_freq = np.power(ROPE_THETA, -np.arange(0, ROPE_DIM, 2, dtype=np.float64) / ROPE_DIM)
    ang = np.asarray(pos, np.float64)[:, None] * inv_freq[None, :]
    cos, sin = jnp.asarray(np.cos(ang), F32), jnp.asarray(np.sin(ang), F32)
    n = len(pos)
    ones = jnp.ones((n, PE_LO), F32)
    z = lambda w: jnp.zeros((n, w), F32)
    tail = LANES - PE_LO - ROPE_DIM
    cosb = jnp.concatenate([ones, cos, cos, z(tail)], axis=1)
    sin_lo = jnp.concatenate([z(PE_LO), -sin, z(PE_HALF + tail)], axis=1)
    sin_hi = jnp.concatenate([z(PE_LO + PE_HALF), sin, z(tail)], axis=1)
    return cosb, sin_lo, sin_hi


def _prompt_group(x_prompt, meta_tokens, p):
    seq, d = x_prompt.shape
    tp = P0 + seq
    meta_row0 = P0 - N_META
    lead = jnp.concatenate([jnp.zeros((meta_row0, d), F32), meta_tokens.astype(F32)], axis=0)
    q, k, v, gb, pre_tail, z, cq, ckv_raw, misc = front_conv_prompt(
        x_prompt, lead, p["norm1"], p["w_in"], p["conv_w"], p["a_log"], p["dt_bias"], tm=P0,
        first_valid_row=meta_row0)
    s0 = jnp.zeros((1, GDN_HEADS, GDN_DK, GDN_DV), F32)
    gdn_o, s_fin = gdn(q, k, v, gb, s0, C=CHUNK, CB_prep=GDN_PREP_CHUNKS, CB_scan=GDN_SCAN_CHUNKS,
                       SB_scan=1, n_seq=1)
    tabs = _rope_tables(np.arange(tp) - meta_row0)
    ckv, kpe, qh, kh, vt = mla_prep(cq, ckv_raw, misc, tabs, p, tm=ROW_TILE, with_kv=True)
    mla_o = flash_prompt(qh, kh, vt, tq=FLASH_TQ, tk=FLASH_TK, n_prefix=N_META)
    y = layer_back(x_prompt, gdn_o, z, mla_o, p, tm=ROW_TILE, row_block0=P0 // ROW_TILE)
    new_conv = pre_tail[SUBLANES - (CONV_WIDTH - 1):]
    return (y, new_conv, s_fin[0], ckv[meta_row0:], kpe[meta_row0:, PE_LO:PE_LO + ROPE_DIM])


def _sample_group(x_sample, cache_conv, state_gdn, cache_ckv, cache_kpe, page_table, p):
    b, L, d = x_sample.shape
    m = b * L
    page = cache_ckv.shape[2]
    past_len = page_table.shape[1] * page
    x = x_sample.reshape(m, d)
    tm = min(m, ROW_TILE)
    qkv_pre, z, cq, ckv_raw, misc = front_proj(x, p["norm1"], p["w_in"], tm=tm)
    xp = jnp.concatenate([jnp.transpose(cache_conv.astype(F32), (1, 0, 2)),
                          jnp.transpose(qkv_pre.reshape(b, L, CONV_DIM), (1, 0, 2))], axis=0)
    misc_t = jnp.transpose(misc.reshape(b, L, MISC_W), (1, 0, 2))
    q, k, v, gb = conv_sample(xp, misc_t, p["conv_w"], p["a_log"], p["dt_bias"])
    C = SAMPLE_CHUNK
    to_chunks = lambda t: jnp.pad(jnp.transpose(t, (1, 0, 2)), ((0, 0), (0, C - L), (0, 0))).reshape(b * C, -1)
    gdn_o, s_fin = gdn(to_chunks(q), to_chunks(k), to_chunks(v), to_chunks(gb), state_gdn.astype(F32),
                       C=C, CB_prep=SAMPLE_PREP_CHUNKS if b % SAMPLE_PREP_CHUNKS == 0 else 1, CB_scan=1,
                       SB_scan=SAMPLE_SCAN_SEQS if b % SAMPLE_SCAN_SEQS == 0 else 1, n_seq=b)
    gdn_o = gdn_o.reshape(b, C, GDN_V_WIDTH)[:, :L].reshape(m, GDN_V_WIDTH)
    tabs = _rope_tables(np.tile(past_len + np.arange(L), b))
    ckv, kpe, qh = mla_prep(cq, ckv_raw, misc, tabs, p, tm=tm, with_kv=False)
    kpe32 = kpe[:, PE_LO:PE_LO + ROPE_DIM]
    qabs = q_absorb(qh, p["w_abs"])
    R = L * MLA_HEADS
    rows_lh = lambda t: jnp.transpose(t.reshape(MLA_HEADS, b, L, -1), (1, 2, 0, 3)).reshape(b, R, -1)
    qpe = rows_lh(qh[:, :, PE_LO:PE_LO + ROPE_DIM])
    new_ckv = jnp.pad(ckv.reshape(b, L, KV_RANK), ((0, 0), (0, SUBLANES - L), (0, 0)))
    new_kpe_t = jnp.pad(jnp.swapaxes(kpe32.reshape(b, L, ROPE_DIM), 1, 2), ((0, 0), (0, 0), (0, SUBLANES - L)))
    n_pages = page_table.shape[1]
    PG = DECODE_STEP_PAGES if n_pages % DECODE_STEP_PAGES == 0 else DECODE_SCORE_PAGES
    o_lat = decode_attn(page_table, p["w_uk_t"], rows_lh(qabs), qpe, new_ckv, new_kpe_t,
                        cache_ckv, jnp.swapaxes(cache_kpe, 2, 3), PG=PG, L=L)
    o_lat_h = jnp.transpose(o_lat.reshape(b, L, MLA_HEADS, KV_RANK), (2, 0, 1, 3)).reshape(MLA_HEADS, m, KV_RANK)
    mla_o = latent_out(o_lat_h, p["w_uv_h"])
    y = layer_back(x, gdn_o, z, mla_o, p, tm=tm, row_block0=0)
    new_conv = qkv_pre.reshape(b, L, CONV_DIM)[:, L - (CONV_WIDTH - 1):]
    return (y.reshape(b, L, d), new_conv, s_fin, ckv.reshape(b, L, KV_RANK), kpe32.reshape(b, L, ROPE_DIM))


def kernel(x_prompt, x_sample, cache_conv, state_gdn, cache_ckv, cache_kpe, page_table, meta_tokens, norm1, w_in, conv_w, a_log, dt_bias, gdn_norm, q_a_norm, w_q_b, q_nope_norm, q_pe_norm, kv_a_norm, k_pe_norm, w_uk, w_uv, k_nope_norm, w_o, norm2, w_gate, w_up, w_down):
    assert x_prompt.shape[0] == 1 and norm1.shape[0] == 1, "one prompt sequence, one layer"
    p = _prep_params(norm1[0], w_in[0], conv_w[0], a_log[0], dt_bias[0], gdn_norm[0], q_a_norm[0],
                     w_q_b[0], q_nope_norm[0], q_pe_norm[0], kv_a_norm[0], k_pe_norm[0], w_uk[0],
                     w_uv[0], k_nope_norm[0], w_o[0], norm2[0], w_gate[0], w_up[0], w_down[0])
    y_p, conv_p, st_p, ckv_p, kpe_p = _prompt_group(x_prompt[0], meta_tokens, p)
    y_s, conv_s, st_s, ckv_s, kpe_s = _sample_group(x_sample, cache_conv[0], state_gdn[0], cache_ckv,
                                                     cache_kpe, page_table, p)
    lead = lambda t: t[None, None]
    return (y_p[None], y_s, lead(conv_p), lead(st_p), lead(ckv_p), lead(kpe_p),
            conv_s[None], st_s[None], ckv_s[None], kpe_s[None])
```

```python
import functools
import math

import jax
import jax.numpy as jnp
import numpy as np
from jax import lax
from jax.experimental import pallas as pl
from jax.experimental.pallas import tpu as pltpu

F32 = jnp.float32
BF16 = jnp.bfloat16
HIGHEST = lax.Precision.HIGHEST

N_META = 16
EPS = 1e-6
NEG_INF = -1e30
GDN_HEADS = 4
GDN_DK = 128
GDN_DV = 128
GDN_QK_WIDTH = GDN_HEADS * GDN_DK
GDN_V_WIDTH = GDN_HEADS * GDN_DV
CONV_WIDTH = 4
CONV_DIM = 2 * GDN_QK_WIDTH + GDN_V_WIDTH
CHUNK = 64
MLA_HEADS = 8
Q_RANK = 384
KV_RANK = 256
NOPE_DIM = 64
ROPE_DIM = 32
V_DIM = 64
QK_DIM = NOPE_DIM + ROPE_DIM
ROPE_THETA = 10000.0
SCORE_SCALE_LOG2 = QK_DIM ** -0.5 * math.log2(math.e)

LANES = 128
SUBLANES = 8
BF16_ROWS = 16
MXU_WIDTH = 256
VMEM_BYTES = 64 * 1024 * 1024
VMEM_LIMIT = VMEM_BYTES - 8 * 1024 * 1024

HEAD_PAD = LANES
PE_LO = NOPE_DIM
PE_HALF = ROPE_DIM // 2
MISC_W = LANES
VT_ROWS = V_DIM + BF16_ROWS

P0 = 512
ROW_TILE = MXU_WIDTH
FLASH_TQ = 512
FLASH_TK = 512
FLASH_QPARTS = FLASH_TQ // MXU_WIDTH
FLASH_KPARTS = FLASH_TK // MXU_WIDTH
FLASH_AHEAD = 5
GDN_PREP_CHUNKS = 4
GDN_SCAN_CHUNKS = 8
GDN_SPLIT_STEPS = 3
SAMPLE_CHUNK = BF16_ROWS
SAMPLE_PREP_CHUNKS = 8
SAMPLE_SCAN_SEQS = 4
DECODE_STEP_PAGES = 128
DECODE_SCORE_PAGES = 4
DECODE_AHEAD = 2


def _cp(sem, vmem=None):
    return pltpu.CompilerParams(dimension_semantics=sem, vmem_limit_bytes=vmem or VMEM_LIMIT)


def _sigmoid(x):
    return 1.0 / (1.0 + jnp.exp(-x))


def _softplus(x):
    return jnp.maximum(x, 0.0) + jnp.log1p(jnp.exp(-jnp.abs(x)))


def _dot(a, b, precision=None):
    return jnp.dot(a, b, preferred_element_type=F32, precision=precision)


def _dot_nt(a, b, precision=None):
    return lax.dot_general(a, b, (((1,), (1,)), ((), ())), preferred_element_type=F32,
                           precision=precision)


W_SPLITS = (CONV_DIM, GDN_V_WIDTH, Q_RANK, KV_RANK, MISC_W)


def _front_body(x_ref, g_ref, w_ref, *outs):
    x = x_ref[...]
    hn = ((x * lax.rsqrt(jnp.mean(x * x, axis=-1, keepdims=True) + EPS)) * g_ref[...]).astype(BF16)
    off = 0
    for width, o_ref in zip(W_SPLITS, outs):
        o_ref[...] = _dot(hn, w_ref[:, off:off + width])
        off += width


def front_proj(x, norm1, w_r, tm):
    m, d = x.shape
    n_all = sum(W_SPLITS)
    row = lambda i: (i, 0)
    fixed = lambda i: (0, 0)
    return pl.pallas_call(
        _front_body,
        grid=(m // tm,),
        in_specs=[pl.BlockSpec((tm, d), row), pl.BlockSpec((1, d), fixed), pl.BlockSpec((d, n_all), fixed)],
        out_specs=[pl.BlockSpec((tm, w), row) for w in W_SPLITS],
        out_shape=[jax.ShapeDtypeStruct((m, w), F32) for w in W_SPLITS],
        compiler_params=_cp(("parallel",)),
        name="front_proj",
    )(x, norm1.reshape(1, d), w_r)


def _gates(misc, alog, dtb):
    lane = lax.broadcasted_iota(jnp.int32, misc.shape, 1)
    g = -jnp.exp(alog) * _softplus(misc + dtb)
    beta = _sigmoid(misc)
    return jnp.where(lane < GDN_HEADS, g, jnp.where(lane < 2 * GDN_HEADS, beta, 0.0))


def _conv_cols(conv, c, q_ref, k_ref, v_ref):
    cols = slice(c * LANES, (c + 1) * LANES)
    acc = conv(cols)
    y = acc * _sigmoid(acc)
    if c < 2 * GDN_HEADS:
        yn = y * lax.rsqrt(jnp.sum(y * y, axis=-1, keepdims=True) + EPS)
        if c < GDN_HEADS:
            q_ref[:, cols] = yn * (GDN_DK ** -0.5)
        else:
            k_ref[:, (c - GDN_HEADS) * LANES:(c - GDN_HEADS + 1) * LANES] = yn
    else:
        v_ref[:, (c - 2 * GDN_HEADS) * LANES:(c - 2 * GDN_HEADS + 1) * LANES] = y


def _front_conv_body(lead_ref, x_ref, g_ref, w_ref, cw_ref, alog_ref, dtb_ref,
                     q_ref, k_ref, v_ref, gb_ref, tail_ref, z_ref, cq_ref, ckv_ref, misc_ref, xs,
                     *, tm, first_valid_row):
    i = pl.program_id(0)
    x = jnp.where(i == 0, lead_ref[...], x_ref[...])
    hn = ((x * lax.rsqrt(jnp.mean(x * x, axis=-1, keepdims=True) + EPS)) * g_ref[...]).astype(BF16)
    halo = SUBLANES
    xs[0:halo, :] = jnp.where(i > 0, xs[tm:tm + halo, :], 0.0)
    xs[halo:tm + halo, :] = _dot(hn, w_ref[:, 0:CONV_DIM])
    off = CONV_DIM
    for width, o_ref in zip(W_SPLITS[1:], (z_ref, cq_ref, ckv_ref, misc_ref)):
        o_ref[...] = _dot(hn, w_ref[:, off:off + width])
        off += width
    tail_ref[...] = xs[tm:tm + halo, :]

    def conv(cols):
        x_ext = xs[:, cols]
        t = x_ext * cw_ref[0:1, cols]
        for j in range(1, CONV_WIDTH):
            t = x_ext * cw_ref[j:j + 1, cols] + pltpu.roll(t, 1, 0)
        return t[halo:, :]

    for c in range(CONV_DIM // LANES):
        _conv_cols(conv, c, q_ref, k_ref, v_ref)
    gb = _gates(misc_ref[...], alog_ref[...], dtb_ref[...])
    row = i * tm + lax.broadcasted_iota(jnp.int32, gb.shape, 0)
    gb_ref[...] = jnp.where(row >= first_valid_row, gb, 0.0)


def front_conv_prompt(x, lead, norm1, w_r, conv_w, alog, dtb, tm, first_valid_row):
    m, d = x.shape
    n_all = sum(W_SPLITS)
    row = lambda i: (i, 0)
    fixed = lambda i: (0, 0)
    n_blocks = m // tm + 1
    rows = n_blocks * tm
    widths = (GDN_QK_WIDTH, GDN_QK_WIDTH, GDN_V_WIDTH, MISC_W) + (CONV_DIM,) + W_SPLITS[1:]
    out_specs = [pl.BlockSpec((tm, w), row) for w in widths]
    out_shape = [jax.ShapeDtypeStruct((rows, w), F32) for w in widths]
    out_specs[4] = pl.BlockSpec((SUBLANES, CONV_DIM), fixed)
    out_shape[4] = jax.ShapeDtypeStruct((SUBLANES, CONV_DIM), F32)
    body = functools.partial(_front_conv_body, tm=tm, first_valid_row=first_valid_row)
    return pl.pallas_call(
        body,
        grid=(n_blocks,),
        in_specs=[pl.BlockSpec((tm, d), fixed), pl.BlockSpec((tm, d), lambda i: (jnp.maximum(i - 1, 0), 0)),
                  pl.BlockSpec((1, d), fixed), pl.BlockSpec((d, n_all), fixed),
                  pl.BlockSpec((CONV_WIDTH, CONV_DIM), fixed),
                  pl.BlockSpec((1, MISC_W), fixed), pl.BlockSpec((1, MISC_W), fixed)],
        out_specs=out_specs,
        out_shape=out_shape,
        scratch_shapes=[pltpu.VMEM((tm + SUBLANES, CONV_DIM), F32)],
        compiler_params=_cp(("arbitrary",)),
        name="front_conv_prompt",
    )(lead, x, norm1.reshape(1, d), w_r, conv_w, alog, dtb)


def _conv_sample_body(xp_ref, misc_ref, w_ref, alog_ref, dtb_ref, q_ref, k_ref, v_ref, gb_ref, *, L):
    for l in range(L):
        conv = lambda cols, l=l: sum(xp_ref[l + j, :, cols] * w_ref[j:j + 1, cols] for j in range(CONV_WIDTH))
        qv, kv, vv = (r.at[l] for r in (q_ref, k_ref, v_ref))
        for c in range(CONV_DIM // LANES):
            _conv_cols(conv, c, qv, kv, vv)
        gb_ref[l] = _gates(misc_ref[l], alog_ref[...], dtb_ref[...])


def conv_sample(xp, misc, conv_w, alog, dtb):
    lp, b, _ = xp.shape
    L = lp - (CONV_WIDTH - 1)
    body = functools.partial(_conv_sample_body, L=L)
    return pl.pallas_call(
        body,
        out_shape=[jax.ShapeDtypeStruct((L, b, GDN_QK_WIDTH), F32)] * 3
        + [jax.ShapeDtypeStruct((L, b, MISC_W), F32)],
        compiler_params=pltpu.CompilerParams(vmem_limit_bytes=VMEM_LIMIT),
        name="conv_sample",
    )(xp, misc, conv_w, alog, dtb)


def _split_bf16(x):
    hi = x.astype(BF16)
    return hi, (x - hi.astype(F32)).astype(BF16)


def _rows_hi_lo(x):
    return jnp.concatenate(_split_bf16(x), axis=0)


def _gdn_prep_body(q_ref, k_ref, v_ref, gb_ref, wq_ref, ut_ref, qk_ref, kdt_ref, gl_ref, *, C, CB):
    row = lax.broadcasted_iota(jnp.int32, (C, C), 0)
    col = lax.broadcasted_iota(jnp.int32, (C, C), 1)
    incl = row >= col
    strict = row > col
    tri = incl.astype(F32)
    n_sq = int(math.log2(C)) - 1
    inst = [(cc, h) for cc in range(CB) for h in range(GDN_HEADS)]
    rows_of = lambda cc: slice(cc * C, (cc + 1) * C)
    cols_of = lambda h: slice(h * LANES, (h + 1) * LANES)
    g_cum, g_cum_t, gbs = [], [], []
    for cc in range(CB):
        gb = gb_ref[rows_of(cc), :]
        gc_all = _dot(tri, gb, HIGHEST)
        gl_ref[cc] = jnp.broadcast_to(jnp.exp(gc_all[C - 1:C, :]), (SUBLANES, LANES))
        gbs.append(gb)
        g_cum.append(gc_all)
        g_cum_t.append(gc_all.T)
    ks = [k_ref[rows_of(cc), cols_of(h)] for cc, h in inst]
    both = [_dot_nt(jnp.concatenate([q_ref[rows_of(cc), cols_of(h)], k], axis=0).astype(BF16), k.astype(BF16))
            for (cc, h), k in zip(inst, ks)]
    a, r, gcs = [], [], []
    for i, (cc, h) in enumerate(inst):
        gc = g_cum[cc][:, h:h + 1]
        gr = g_cum_t[cc][h:h + 1, :]
        beta = gbs[cc][:, GDN_HEADS + h:GDN_HEADS + h + 1]
        decay = jnp.where(incl, jnp.exp(jnp.where(incl, gc - gr, 0.0)), 0.0)
        a.append(jnp.where(strict, -(beta * both[i][C:2 * C] * decay), 0.0))
        qk = both[i][0:C] * decay
        qk_ref[rows_of(cc), cols_of(h)] = jnp.concatenate(
            [qk, jnp.zeros((C, LANES - C), F32)], axis=1).astype(BF16)
        e_g = jnp.exp(gc)
        wq_ref[cc, C:2 * C, cols_of(h)] = (e_g * q_ref[rows_of(cc), cols_of(h)]).astype(BF16)
        k_dec = jnp.exp(g_cum[cc][C - 1:C, h:h + 1] - gc) * ks[i]
        kdt_ref[cc, h] = jnp.concatenate([k_dec, k_dec], axis=0).T.astype(BF16)
        r.append(jnp.concatenate([(beta * e_g) * ks[i], beta * v_ref[rows_of(cc), cols_of(h)]], axis=1))
    for step in range(n_sq + 1):
        ab = [x.astype(BF16) for x in a]
        if step < GDN_SPLIT_STEPS:
            r = [x + _dot(jnp.concatenate([y, y], axis=1), _rows_hi_lo(x)) for x, y in zip(r, ab)]
        else:
            r = [x + _dot(y, x.astype(BF16)) for x, y in zip(r, ab)]
        if step < n_sq:
            a = [_dot(y, y) for y in ab]
    for i, (cc, h) in enumerate(inst):
        wq_ref[cc, 0:C, cols_of(h)] = r[i][:, 0:LANES].astype(BF16)
        ut_ref[rows_of(cc), cols_of(h)] = r[i][:, LANES:2 * LANES]


def gdn_prep(q, k, v, gb, *, C, CB):
    rows = q.shape[0]
    nc = rows // C
    blk = lambda i: (i, 0)
    body = functools.partial(_gdn_prep_body, C=C, CB=CB)
    return pl.pallas_call(
        body,
        grid=(nc // CB,),
        in_specs=[pl.BlockSpec((CB * C, GDN_QK_WIDTH), blk)] * 3 + [pl.BlockSpec((CB * C, MISC_W), blk)],
        out_specs=[pl.BlockSpec((CB, 2 * C, GDN_QK_WIDTH), lambda i: (i, 0, 0)),
                   pl.BlockSpec((CB * C, GDN_V_WIDTH), blk),
                   pl.BlockSpec((CB * C, GDN_HEADS * LANES), blk),
                   pl.BlockSpec((CB, GDN_HEADS, GDN_DK, 2 * C), lambda i: (i, 0, 0, 0)),
                   pl.BlockSpec((CB, SUBLANES, LANES), lambda i: (i, 0, 0))],
        out_shape=[jax.ShapeDtypeStruct((nc, 2 * C, GDN_QK_WIDTH), BF16),
                   jax.ShapeDtypeStruct((rows, GDN_V_WIDTH), F32),
                   jax.ShapeDtypeStruct((rows, GDN_HEADS * LANES), BF16),
                   jax.ShapeDtypeStruct((nc, GDN_HEADS, GDN_DK, 2 * C), BF16),
                   jax.ShapeDtypeStruct((nc, SUBLANES, LANES), F32)],
        compiler_params=_cp(("parallel",)),
        name="gdn_prep",
    )(q, k, v, gb)


def _gdn_scan_body(gl_ref, wq_ref, ut_ref, qk_ref, kdt_ref, s0_ref, o_ref, sfin_ref, s_scr,
                   *, C, CB, SB, n_steps):
    sq = pl.program_id(0)
    st = pl.program_id(1)

    @pl.when(st == 0)
    def _():
        s_scr[...] = s0_ref[...]

    cols = [slice(h * LANES, (h + 1) * LANES) for h in range(GDN_HEADS)]
    for cc in range(CB):
        inst = [(sb, h) for sb in range(SB) for h in range(GDN_HEADS)]
        blk = [sb * CB + cc for sb, _ in inst]
        rows = [slice(b * C, (b + 1) * C) for b in blk]
        chunk = [((sq * SB + sb) * n_steps + st) * CB + cc for sb, _ in inst]
        ss = [s_scr[sb, h] for sb, h in inst]
        ws = [_dot(jnp.concatenate([wq_ref[blk[i], :, cols[h]]] * 2, axis=1), _rows_hi_lo(ss[i]))
              for i, (_, h) in enumerate(inst)]
        u2 = [_rows_hi_lo(ut_ref[rows[i], cols[h]] - ws[i][0:C]) for i, (_, h) in enumerate(inst)]
        for i, (sb, h) in enumerate(inst):
            s_scr[sb, h] = gl_ref[chunk[i] * GDN_HEADS + h] * ss[i] + _dot(kdt_ref[blk[i], h], u2[i])
        for i, (_, h) in enumerate(inst):
            o_ref[rows[i], cols[h]] = ws[i][C:2 * C] + _dot(qk_ref[rows[i], h * LANES:h * LANES + C], u2[i][0:C])

    @pl.when(st == n_steps - 1)
    def _():
        sfin_ref[...] = s_scr[...]


def gdn_scan(gl, wq, ut, qk, kdt, s0, *, C, CB, SB, n_seq):
    rows = ut.shape[0]
    nc = rows // C
    n_steps = nc // (n_seq * CB)
    assert SB == 1 or n_steps == 1
    nb = SB * CB
    blk = lambda s, t, gl: (s * n_steps + t, 0)
    blk3 = lambda s, t, gl: (s * n_steps + t, 0, 0)
    blk4 = lambda s, t, gl: (s * n_steps + t, 0, 0, 0)
    st4 = lambda s, t, gl: (s, 0, 0, 0)
    body = functools.partial(_gdn_scan_body, C=C, CB=CB, SB=SB, n_steps=n_steps)
    grid_spec = pltpu.PrefetchScalarGridSpec(
        num_scalar_prefetch=1,
        grid=(n_seq // SB, n_steps),
        in_specs=[pl.BlockSpec((nb, 2 * C, GDN_QK_WIDTH), blk3), pl.BlockSpec((nb * C, GDN_V_WIDTH), blk),
                  pl.BlockSpec((nb * C, GDN_HEADS * LANES), blk),
                  pl.BlockSpec((nb, GDN_HEADS, GDN_DK, 2 * C), blk4),
                  pl.BlockSpec((SB, GDN_HEADS, GDN_DK, GDN_DV), st4)],
        out_specs=[pl.BlockSpec((nb * C, GDN_V_WIDTH), blk),
                   pl.BlockSpec((SB, GDN_HEADS, GDN_DK, GDN_DV), st4)],
        scratch_shapes=[pltpu.VMEM((SB, GDN_HEADS, GDN_DK, GDN_DV), F32)],
    )
    return pl.pallas_call(
        body,
        grid_spec=grid_spec,
        out_shape=[jax.ShapeDtypeStruct((rows, GDN_V_WIDTH), F32), jax.ShapeDtypeStruct(s0.shape, F32)],
        compiler_params=_cp(("arbitrary", "arbitrary")),
        name="gdn_scan",
    )(gl, wq, ut, qk, kdt, s0)


def gdn(q, k, v, gb, s0, *, C, CB_prep, CB_scan, SB_scan, n_seq):
    wq, ut, qk, kdt, glv = gdn_prep(q, k, v, gb, C=C, CB=CB_prep)
    gl = glv[:, 0, :GDN_HEADS].reshape(-1)
    return gdn_scan(gl, wq, ut, qk, kdt, s0, C=C, CB=CB_scan, SB=SB_scan, n_seq=n_seq)


def _rope(y, cosb, sin_lo, sin_hi):
    return (y * cosb + pltpu.roll(y, LANES - PE_HALF, 1) * sin_lo + pltpu.roll(y, PE_HALF, 1) * sin_hi)


def _slab_norm(x, gain, lane, parts="np"):
    x2 = x * x
    is_nope = lane < NOPE_DIM
    is_pe = (lane >= PE_LO) & (lane < PE_LO + ROPE_DIM)
    if parts == "n":
        rinv = lax.rsqrt(jnp.sum(x2, axis=-1, keepdims=True) * (1.0 / NOPE_DIM) + EPS)
    elif parts == "p":
        rinv = lax.rsqrt(jnp.sum(x2, axis=-1, keepdims=True) * (1.0 / ROPE_DIM) + EPS)
    else:
        ms_n = jnp.sum(jnp.where(is_nope, x2, 0.0), axis=-1, keepdims=True) * (1.0 / NOPE_DIM)
        ms_p = jnp.sum(jnp.where(is_pe, x2, 0.0), axis=-1, keepdims=True) * (1.0 / ROPE_DIM)
        rinv = jnp.where(is_nope, lax.rsqrt(ms_n + EPS), lax.rsqrt(ms_p + EPS))
    return (x * rinv) * gain


def _mla_body(cq_ref, ckvr_ref, misc_ref, cos_ref, slo_ref, shi_ref, qan_ref, wqb_ref, qg_ref,
              kvn_ref, kg_ref, *rest, with_kv):
    if with_kv:
        wuk_ref, wuvt_ref, ckv_ref, kpe_ref, q_ref, k_ref, vt_ref = rest
    else:
        ckv_ref, kpe_ref, q_ref = rest
    cosb, slo, shi = cos_ref[...], slo_ref[...], shi_ref[...]
    lane = lax.broadcasted_iota(jnp.int32, cosb.shape, 1)
    cq = cq_ref[...]
    cqn = ((cq * lax.rsqrt(jnp.mean(cq * cq, axis=-1, keepdims=True) + EPS)) * qan_ref[...]).astype(BF16)
    c = ckvr_ref[...]
    ckv = (c * lax.rsqrt(jnp.mean(c * c, axis=-1, keepdims=True) + EPS)) * kvn_ref[...]
    ckv_ref[...] = ckv
    ckv_b = ckv.astype(BF16)
    kpe_raw = jnp.where((lane >= PE_LO) & (lane < PE_LO + ROPE_DIM), misc_ref[...], 0.0)
    kpe = _rope(_slab_norm(kpe_raw, kg_ref[...], lane, "p"), cosb, slo, shi)
    kpe = jnp.where(lane >= PE_LO, kpe, 0.0)
    kpe_ref[...] = kpe
    for h in range(MLA_HEADS):
        cols = slice(h * HEAD_PAD, (h + 1) * HEAD_PAD)
        qh = _dot(cqn, wqb_ref[:, cols])
        qh = _rope(_slab_norm(qh, qg_ref[...], lane), cosb, slo, shi)
        q_ref[h] = (qh * SCORE_SCALE_LOG2).astype(BF16)
        if with_kv:
            kh = _dot(ckv_b, wuk_ref[:, cols])
            k_ref[h] = (_slab_norm(kh, kg_ref[...], lane, "n") + kpe).astype(BF16)
            vt = _dot_nt(wuvt_ref[h], ckv_b)
            ones_row = lax.broadcasted_iota(jnp.int32, vt.shape, 0) == V_DIM
            vt_ref[h] = jnp.where(ones_row, 1.0, vt).astype(BF16)


def mla_prep(cq, ckv_raw, misc, rope_tabs, p, tm, with_kv):
    m = cq.shape[0]
    row = lambda i: (i, 0)
    fixed = lambda i: (0, 0)
    hw = MLA_HEADS * HEAD_PAD
    in_specs = ([pl.BlockSpec((tm, Q_RANK), row), pl.BlockSpec((tm, KV_RANK), row),
                 pl.BlockSpec((tm, MISC_W), row)] + [pl.BlockSpec((tm, LANES), row)] * 3
                + [pl.BlockSpec((1, Q_RANK), fixed), pl.BlockSpec((Q_RANK, hw), fixed),
                   pl.BlockSpec((1, LANES), fixed), pl.BlockSpec((1, KV_RANK), fixed),
                   pl.BlockSpec((1, LANES), fixed)])
    args = [cq, ckv_raw, misc, *rope_tabs, p["q_a_norm"], p["w_q_b"], p["q_gain"], p["kv_a_norm"], p["k_gain"]]
    out_specs = [pl.BlockSpec((tm, KV_RANK), row), pl.BlockSpec((tm, LANES), row),
                 pl.BlockSpec((MLA_HEADS, tm, HEAD_PAD), lambda i: (0, i, 0))]
    out_shape = [jax.ShapeDtypeStruct((m, KV_RANK), F32), jax.ShapeDtypeStruct((m, LANES), F32),
                 jax.ShapeDtypeStruct((MLA_HEADS, m, HEAD_PAD), BF16)]
    if with_kv:
        in_specs += [pl.BlockSpec((KV_RANK, hw), fixed),
                     pl.BlockSpec((MLA_HEADS, VT_ROWS, KV_RANK), lambda i: (0, 0, 0))]
        args += [p["w_uk"], p["w_uv_t"]]
        out_specs += [pl.BlockSpec((MLA_HEADS, tm, HEAD_PAD), lambda i: (0, i, 0)),
                      pl.BlockSpec((MLA_HEADS, VT_ROWS, tm), lambda i: (0, 0, i))]
        out_shape += [jax.ShapeDtypeStruct((MLA_HEADS, m, HEAD_PAD), BF16),
                      jax.ShapeDtypeStruct((MLA_HEADS, VT_ROWS, m), BF16)]
    return pl.pallas_call(
        functools.partial(_mla_body, with_kv=with_kv),
        grid=(m // tm,),
        in_specs=in_specs,
        out_specs=out_specs,
        out_shape=out_shape,
        compiler_params=_cp(("parallel",)),
        name="mla_prep",
    )(*args)


def _flash_body(qmap_ref, kmap_ref, q_ref, k_ref, vt_ref, kpre_ref, vtpre_ref, o_ref, m_scr, acc_scr,
                *, tq, tk, q_block0):
    step = pl.program_id(0)
    q0 = (qmap_ref[step] + q_block0) * tq
    k0 = kmap_ref[step] * tk

    @pl.when(k0 == P0)
    def _():
        sts = [_dot_nt(kpre_ref[h], q_ref[h]) for h in range(MLA_HEADS)]
        m0s = [jnp.max(st, axis=0, keepdims=True) for st in sts]
        for h in range(MLA_HEADS):
            m_scr[h] = m0s[h]
            acc_scr[h] = _dot(vtpre_ref[h], jnp.exp2(sts[h] - m0s[h]).astype(BF16))

    def update(masked):
        tqp = tq // FLASH_QPARTS
        tkp = tk // FLASH_KPARTS
        items = [(h, kp, qp) for h in range(MLA_HEADS) for kp in range(FLASH_KPARTS)
                 for qp in range(FLASH_QPARTS)]
        scores = lambda h, kp, qp: _dot_nt(k_ref[h, kp * tkp:(kp + 1) * tkp, :],
                                           q_ref[h, qp * tqp:(qp + 1) * tqp, :])
        pending = [scores(*it) for it in items[:FLASH_AHEAD]]
        for n, (h, kp, qp) in enumerate(items):
            st = pending.pop(0)
            if n + FLASH_AHEAD < len(items):
                pending.append(scores(*items[n + FLASH_AHEAD]))
            qc = slice(qp * tqp, (qp + 1) * tqp)
            if masked:
                kpos = k0 + kp * tkp + lax.broadcasted_iota(jnp.int32, (tkp, tqp), 0)
                qpos = q0 + qp * tqp + lax.broadcasted_iota(jnp.int32, (tkp, tqp), 1)
                st = jnp.where(kpos <= qpos, st, NEG_INF)
            m_prev = m_scr[h, :, qc]
            m_new = jnp.maximum(m_prev, jnp.max(st, axis=0, keepdims=True))
            e = jnp.exp2(st - m_new).astype(BF16)
            acc_scr[h, :, qc] = (jnp.exp2(m_prev - m_new) * acc_scr[h, :, qc]
                                 + _dot(vt_ref[h, :, kp * tkp:(kp + 1) * tkp], e))
            m_scr[h, :, qc] = m_new

    needs_mask = k0 + tk - 1 > q0
    pl.when(needs_mask)(lambda: update(True))
    pl.when(jnp.logical_not(needs_mask))(lambda: update(False))

    @pl.when(k0 + tk >= q0 + tq)
    def _():
        for h in range(MLA_HEADS):
            a = acc_scr[h]
            o = (a[0:V_DIM] * (1.0 / a[V_DIM:V_DIM + 1, :])).T
            o_ref[:, h * HEAD_PAD:(h + 1) * HEAD_PAD] = jnp.concatenate(
                [o, jnp.zeros((tq, HEAD_PAD - V_DIM), F32)], axis=1).astype(BF16)


def flash_prompt(q, k, vt, *, tq, tk, n_prefix):
    tp = q.shape[1]
    q_block0 = P0 // tq
    nq = (tp - P0) // tq
    qs, ks = [], []
    for qi in range(nq):
        last_k = ((qi + q_block0 + 1) * tq - 1) // tk
        for ki in range(P0 // tk, last_k + 1):
            qs.append(qi)
            ks.append(ki)
    qmap = jnp.asarray(qs, jnp.int32)
    kmap = jnp.asarray(ks, jnp.int32)
    k_pre = k[:, P0 - n_prefix:P0]
    vt_pre = vt[:, :, P0 - n_prefix:P0]
    body = functools.partial(_flash_body, tq=tq, tk=tk, q_block0=q_block0)
    whole = lambda s, qm, km: (0, 0, 0)
    grid_spec = pltpu.PrefetchScalarGridSpec(
        num_scalar_prefetch=2,
        grid=(len(qs),),
        in_specs=[pl.BlockSpec((MLA_HEADS, tq, HEAD_PAD), lambda s, qm, km: (0, qm[s] + q_block0, 0)),
                  pl.BlockSpec((MLA_HEADS, tk, HEAD_PAD), lambda s, qm, km: (0, km[s], 0)),
                  pl.BlockSpec((MLA_HEADS, VT_ROWS, tk), lambda s, qm, km: (0, 0, km[s])),
                  pl.BlockSpec((MLA_HEADS, n_prefix, HEAD_PAD), whole),
                  pl.BlockSpec((MLA_HEADS, VT_ROWS, n_prefix), whole)],
        out_specs=pl.BlockSpec((tq, MLA_HEADS * HEAD_PAD), lambda s, qm, km: (qm[s], 0)),
        scratch_shapes=[pltpu.VMEM((MLA_HEADS, 1, tq), F32), pltpu.VMEM((MLA_HEADS, VT_ROWS, tq), F32)],
    )
    return pl.pallas_call(
        body,
        grid_spec=grid_spec,
        out_shape=jax.ShapeDtypeStruct((tp - P0, MLA_HEADS * HEAD_PAD), BF16),
        compiler_params=_cp(("arbitrary",)),
        name="flash_prompt",
    )(qmap, kmap, q, k, vt, k_pre, vt_pre)


def _qabs_body(q_ref, w_ref, o_ref):
    o_ref[0] = _dot(q_ref[0], w_ref[0]).astype(BF16)


def q_absorb(q, w_abs):
    h, m, _ = q.shape
    blk = lambda i: (i, 0, 0)
    return pl.pallas_call(
        _qabs_body,
        grid=(h,),
        in_specs=[pl.BlockSpec((1, m, HEAD_PAD), blk), pl.BlockSpec((1, HEAD_PAD, KV_RANK), blk)],
        out_specs=pl.BlockSpec((1, m, KV_RANK), blk),
        out_shape=jax.ShapeDtypeStruct((h, m, KV_RANK), BF16),
        compiler_params=_cp(("parallel",)),
        name="q_absorb",
    )(q, w_abs)


def _decode_scores(waug, qpe, ckv_b, kpe_t, n_lq):
    kp = _dot_nt(waug, ckv_b)
    n_k = MLA_HEADS * NOPE_DIM
    kproj = kp[:n_k]
    ms = jnp.sum((kproj * kproj).reshape(MLA_HEADS, NOPE_DIM, kproj.shape[-1]), axis=1) * (1.0 / NOPE_DIM)
    rinv = lax.rsqrt(ms + EPS)
    s_nope = kp[n_k:]
    s_pe = _dot(qpe, kpe_t)
    rinv_rows = jnp.concatenate([rinv] * n_lq, axis=0)
    return s_nope * rinv_rows + s_pe


def _page_copies(pt_ref, ckv_hbm, kpet_hbm, ckv_buf, kpe_buf, sem, step, slot, *, PG, page):
    out = []
    for i in range(PG):
        pid = pt_ref[step * PG + i]
        out.append(pltpu.make_async_copy(ckv_hbm.at[0, pid], ckv_buf.at[slot, pl.ds(i * page, page)],
                                         sem.at[0, slot]))
        out.append(pltpu.make_async_copy(kpet_hbm.at[0, pid], kpe_buf.at[slot, :, pl.ds(i * page, page)],
                                         sem.at[1, slot]))
    return out


def _decode_body(pt_ref, wukt_ref, qabs_ref, qpe_ref, nckv_ref, nkpet_ref, ckv_hbm, kpet_hbm, o_ref,
                 ckv_buf, kpe_buf, sem, waug, ckvb, m_scr, l_scr, acc_scr, *, PG, L, n_groups, n_steps, page):
    s = pl.program_id(0)
    j = lax.rem(s, n_groups)
    slot = lax.rem(s, 2)
    n_k = MLA_HEADS * NOPE_DIM
    R = MLA_HEADS * L
    copies = functools.partial(_page_copies, pt_ref, ckv_hbm, kpet_hbm, ckv_buf, kpe_buf, sem, PG=PG, page=page)

    nxt = lax.rem(s + 1, n_steps)

    @pl.when(s == 0)
    def _():
        for n, c in enumerate(copies(0, 0)):
            c.start(priority=(n // 2) % 2)

    @pl.when(j == 0)
    def _():
        waug[0:n_k, :] = wukt_ref[...]
        waug[n_k:n_k + R, :] = qabs_ref[0]
        c_new = nckv_ref[0].astype(BF16)
        sc = _decode_scores(waug[...], qpe_ref[0], c_new, nkpet_ref[0].astype(BF16), L)
        r_l = lax.shift_right_logical(lax.broadcasted_iota(jnp.int32, sc.shape, 0),
                                      int(math.log2(MLA_HEADS)))
        t = lax.broadcasted_iota(jnp.int32, sc.shape, 1)
        sc = jnp.where((t <= r_l) & (t < L), sc, NEG_INF)
        m0 = jnp.max(sc, axis=-1, keepdims=True)
        e0 = jnp.exp2(sc - m0)
        m_scr[...] = m0
        l_scr[...] = jnp.sum(e0, axis=-1, keepdims=True)
        acc_scr[...] = _dot(e0.astype(BF16), c_new)

    for c in copies(s, slot):
        c.wait()

    w = waug[...]
    qpe = qpe_ref[0]

    def scores(p):
        tok = pl.ds(p * DECODE_SCORE_PAGES * page, DECODE_SCORE_PAGES * page)
        c2 = ckv_buf[slot, tok, :].astype(BF16)
        ckvb[p] = c2
        return _decode_scores(w, qpe, c2, kpe_buf[slot, :, tok].astype(BF16), L)

    ms, ls, accs = [m_scr[...]], [l_scr[...]], [acc_scr[...]]
    n_groups_step = PG // DECODE_SCORE_PAGES
    pending = [scores(p) for p in range(min(DECODE_AHEAD, n_groups_step))]
    next_copies = copies(nxt, 1 - slot)
    per_group = -(-len(next_copies) // max(n_groups_step // 4, 1))
    for p in range(n_groups_step):
        sc = pending.pop(0)
        if p + DECODE_AHEAD < n_groups_step:
            pending.append(scores(p + DECODE_AHEAD))
        for n, c in enumerate(next_copies[p * per_group:(p + 1) * per_group]):
            c.start(priority=(n // 2) % 2)
        m_p = jnp.max(sc, axis=-1, keepdims=True)
        e = jnp.exp2(sc - m_p)
        ms.append(m_p)
        ls.append(jnp.sum(e, axis=-1, keepdims=True))
        accs.append(_dot(e.astype(BF16), ckvb[p]))
    m = functools.reduce(jnp.maximum, ms)
    fs = [jnp.exp2(x - m) for x in ms]
    l = functools.reduce(lambda x, y: x + y, [f * x for f, x in zip(fs, ls)])
    acc = functools.reduce(lambda x, y: x + y, [f * x for f, x in zip(fs, accs)])
    m_scr[...], l_scr[...], acc_scr[...] = m, l, acc

    @pl.when(j == n_groups - 1)
    def _():
        o_ref[0] = acc / l

    @pl.when(s == n_steps - 1)
    def _():
        for c in copies(nxt, 1 - slot):
            c.wait()


def decode_attn(page_table, wuk_t, qabs, qpe, new_ckv, new_kpe_t, cache_ckv, cache_kpe_t, *, PG, L):
    b, n_pages = page_table.shape
    page = cache_ckv.shape[2]
    R = MLA_HEADS * L
    n_k = MLA_HEADS * NOPE_DIM
    n_groups = n_pages // PG
    per_b = lambda s, pt: (s // n_groups, 0, 0)
    fixed = lambda s, pt: (0, 0)
    body = functools.partial(_decode_body, PG=PG, L=L, n_groups=n_groups, n_steps=b * n_groups, page=page)
    grid_spec = pltpu.PrefetchScalarGridSpec(
        num_scalar_prefetch=1,
        grid=(b * n_groups,),
        in_specs=[pl.BlockSpec((n_k, KV_RANK), fixed), pl.BlockSpec((1, R, KV_RANK), per_b),
                  pl.BlockSpec((1, R, ROPE_DIM), per_b), pl.BlockSpec((1, SUBLANES, KV_RANK), per_b),
                  pl.BlockSpec((1, ROPE_DIM, SUBLANES), per_b),
                  pl.BlockSpec(memory_space=pl.ANY), pl.BlockSpec(memory_space=pl.ANY)],
        out_specs=pl.BlockSpec((1, R, KV_RANK), per_b),
        scratch_shapes=[pltpu.VMEM((2, PG * page, KV_RANK), F32), pltpu.VMEM((2, ROPE_DIM, PG * page), F32),
                        pltpu.SemaphoreType.DMA((2, 2)),
                        pltpu.VMEM((n_k + R, KV_RANK), BF16), pltpu.VMEM((PG // DECODE_SCORE_PAGES, DECODE_SCORE_PAGES * page, KV_RANK), BF16),
                        pltpu.VMEM((R, 1), F32), pltpu.VMEM((R, 1), F32), pltpu.VMEM((R, KV_RANK), F32)],
    )
    return pl.pallas_call(
        body,
        grid_spec=grid_spec,
        out_shape=jax.ShapeDtypeStruct((b, R, KV_RANK), F32),
        compiler_params=_cp(("arbitrary",)),
        name="decode_attn",
    )(page_table.reshape(-1), wuk_t, qabs, qpe, new_ckv, new_kpe_t, cache_ckv, cache_kpe_t)


def _latent_out_body(x_ref, w_ref, o_ref):
    o_ref[...] = _dot(x_ref[0].astype(BF16), w_ref[0]).astype(BF16)


def latent_out(o_lat, w_uv_h):
    h, m, _ = o_lat.shape
    return pl.pallas_call(
        _latent_out_body,
        grid=(h,),
        in_specs=[pl.BlockSpec((1, m, KV_RANK), lambda i: (i, 0, 0)),
                  pl.BlockSpec((1, KV_RANK, HEAD_PAD), lambda i: (i, 0, 0))],
        out_specs=pl.BlockSpec((m, HEAD_PAD), lambda i: (0, i)),
        out_shape=jax.ShapeDtypeStruct((m, h * HEAD_PAD), BF16),
        compiler_params=_cp(("parallel",)),
        name="latent_out",
    )(o_lat, w_uv_h)


def _back_body(h_ref, o_ref, z_ref, mla_ref, gg_ref, wog_ref, wom_ref, n2_ref, wg_ref, wu_ref, wd_ref, y_ref):
    h1 = h_ref[...] + _dot(mla_ref[...], wom_ref[...])
    for h in range(GDN_HEADS):
        cols = slice(h * LANES, (h + 1) * LANES)
        o = o_ref[:, cols]
        z = z_ref[:, cols]
        y = ((o * lax.rsqrt(jnp.mean(o * o, axis=-1, keepdims=True) + EPS)) * gg_ref[...]) * (z * _sigmoid(z))
        h1 = h1 + _dot(y.astype(BF16), wog_ref[cols, :])
    hn = ((h1 * lax.rsqrt(jnp.mean(h1 * h1, axis=-1, keepdims=True) + EPS)) * n2_ref[...]).astype(BF16)
    g = _dot(hn, wg_ref[...])
    u = _dot(hn, wu_ref[...])
    y_ref[...] = h1 + _dot(((g * _sigmoid(g)) * u).astype(BF16), wd_ref[...])


def layer_back(h, gdn_o, z, mla_o, p, tm, row_block0):
    m, d = h.shape
    dff = p["w_gate"].shape[1]
    row = lambda i: (i, 0)
    prow = lambda i: (i + row_block0, 0)
    fixed = lambda i: (0, 0)
    hw = MLA_HEADS * HEAD_PAD
    return pl.pallas_call(
        _back_body,
        grid=(m // tm,),
        in_specs=[pl.BlockSpec((tm, d), row), pl.BlockSpec((tm, GDN_V_WIDTH), prow),
                  pl.BlockSpec((tm, GDN_V_WIDTH), prow), pl.BlockSpec((tm, hw), row),
                  pl.BlockSpec((1, LANES), fixed), pl.BlockSpec((GDN_V_WIDTH, d), fixed),
                  pl.BlockSpec((hw, d), fixed), pl.BlockSpec((1, d), fixed),
                  pl.BlockSpec((d, dff), fixed), pl.BlockSpec((d, dff), fixed), pl.BlockSpec((dff, d), fixed)],
        out_specs=pl.BlockSpec((tm, d), row),
        out_shape=jax.ShapeDtypeStruct((m, d), F32),
        compiler_params=_cp(("parallel",)),
        name="layer_back",
    )(h, gdn_o, z, mla_o, p["gdn_norm"], p["w_o_gdn"], p["w_o_mla"], p["norm2"],
      p["w_gate"], p["w_up"], p["w_down"])


def _head_slabs(w, width):
    k, h, _ = w.shape
    return jnp.pad(w, ((0, 0), (0, 0), (0, HEAD_PAD - width))).reshape(k, h * HEAD_PAD)


def _pad_lanes(v, lo=0):
    return jnp.pad(v.astype(F32), (lo, LANES - lo - v.shape[0])).reshape(1, LANES)


def _prep_params(norm1, w_in, conv_w, a_log, dt_bias, gdn_norm, q_a_norm, w_q_b, q_nope_norm,
                 q_pe_norm, kv_a_norm, k_pe_norm, w_uk, w_uv, k_nope_norm, w_o, norm2,
                 w_gate, w_up, w_down):
    d = w_in.shape[0]
    o = 0
    cuts = {}
    for name, wdt in (("qkv", CONV_DIM), ("z", GDN_V_WIDTH), ("a", GDN_HEADS), ("b", GDN_HEADS),
                      ("cq", Q_RANK), ("ckv", KV_RANK), ("kpe", ROPE_DIM)):
        cuts[name] = w_in[:, o:o + wdt]
        o += wdt
    zeros = lambda n: jnp.zeros((d, n), w_in.dtype)
    misc = jnp.concatenate([cuts["a"], cuts["b"], zeros(PE_LO - 2 * GDN_HEADS), cuts["kpe"],
                            zeros(MISC_W - PE_LO - ROPE_DIM)], axis=1)
    w_r = jnp.concatenate([cuts["qkv"], cuts["z"], cuts["cq"], cuts["ckv"], misc], axis=1).astype(BF16)
    wqb = w_q_b.reshape(Q_RANK, MLA_HEADS, QK_DIM)
    gain_q = jnp.concatenate([q_nope_norm, q_pe_norm]).astype(F32)
    gain_k = jnp.concatenate([k_nope_norm, k_pe_norm]).astype(F32)
    w_uk_hdr = jnp.transpose(w_uk, (1, 2, 0))
    w_abs = jnp.pad(w_uk_hdr * k_nope_norm[None, :, None], ((0, 0), (0, HEAD_PAD - NOPE_DIM), (0, 0)))
    w_o_mla = jnp.pad(w_o[GDN_V_WIDTH:].reshape(MLA_HEADS, V_DIM, d),
                      ((0, 0), (0, HEAD_PAD - V_DIM), (0, 0))).reshape(MLA_HEADS * HEAD_PAD, d)
    return dict(
        norm1=norm1, w_in=w_r, conv_w=conv_w.astype(F32),
        a_log=_pad_lanes(a_log), dt_bias=_pad_lanes(dt_bias),
        gdn_norm=gdn_norm.reshape(1, GDN_DV).astype(F32),
        q_a_norm=q_a_norm.reshape(1, Q_RANK).astype(F32), w_q_b=_head_slabs(wqb, QK_DIM).astype(BF16),
        q_gain=_pad_lanes(gain_q), kv_a_norm=kv_a_norm.reshape(1, KV_RANK).astype(F32),
        w_uk=_head_slabs(w_uk, NOPE_DIM).astype(BF16), k_gain=_pad_lanes(gain_k),
        w_uv_t=jnp.pad(jnp.transpose(w_uv, (1, 2, 0)), ((0, 0), (0, VT_ROWS - V_DIM), (0, 0))).astype(BF16),
        w_uk_t=w_uk_hdr.reshape(MLA_HEADS * NOPE_DIM, KV_RANK).astype(BF16),
        w_abs=w_abs.astype(BF16),
        w_uv_h=jnp.pad(jnp.transpose(w_uv, (1, 0, 2)), ((0, 0), (0, 0), (0, HEAD_PAD - V_DIM))).astype(BF16),
        w_o_gdn=w_o[:GDN_V_WIDTH].astype(BF16), w_o_mla=w_o_mla.astype(BF16),
        norm2=norm2.reshape(1, d).astype(F32),
        w_gate=w_gate.astype(BF16), w_up=w_up.astype(BF16), w_down=w_down.astype(BF16),
    )


def _rope_tables(pos):
    inv_freq = np.power(ROPE_THETA, -np.arange(0, ROPE_DIM, 2, dtype=np.float64) / ROPE_DIM)
    ang = np.asarray(pos, np.float64)[:, None] * inv_freq[None, :]
    cos, sin = jnp.asarray(np.cos(ang), F32), jnp.asarray(np.sin(ang), F32)
    n = len(pos)
    ones = jnp.ones((n, PE_LO), F32)
    z = lambda w: jnp.zeros((n, w), F32)
    tail = LANES - PE_LO - ROPE_DIM
    cosb = jnp.concatenate([ones, cos, cos, z(tail)], axis=1)
    sin_lo = jnp.concatenate([z(PE_LO), -sin, z(PE_HALF + tail)], axis=1)
    sin_hi = jnp.concatenate([z(PE_LO + PE_HALF), sin, z(tail)], axis=1)
    return cosb, sin_lo, sin_hi


def _prompt_group(x_prompt, meta_tokens, p):
    seq, d = x_prompt.shape
    tp = P0 + seq
    meta_row0 = P0 - N_META
    lead = jnp.concatenate([jnp.zeros((meta_row0, d), F32), meta_tokens.astype(F32)], axis=0)
    q, k, v, gb, pre_tail, z, cq, ckv_raw, misc = front_conv_prompt(
        x_prompt, lead, p["norm1"], p["w_in"], p["conv_w"], p["a_log"], p["dt_bias"], tm=P0,
        first_valid_row=meta_row0)
    s0 = jnp.zeros((1, GDN_HEADS, GDN_DK, GDN_DV), F32)
    gdn_o, s_fin = gdn(q, k, v, gb, s0, C=CHUNK, CB_prep=GDN_PREP_CHUNKS, CB_scan=GDN_SCAN_CHUNKS,
                       SB_scan=1, n_seq=1)
    tabs = _rope_tables(np.arange(tp) - meta_row0)
    ckv, kpe, qh, kh, vt = mla_prep(cq, ckv_raw, misc, tabs, p, tm=ROW_TILE, with_kv=True)
    mla_o = flash_prompt(qh, kh, vt, tq=FLASH_TQ, tk=FLASH_TK, n_prefix=N_META)
    y = layer_back(x_prompt, gdn_o, z, mla_o, p, tm=ROW_TILE, row_block0=P0 // ROW_TILE)
    new_conv = pre_tail[SUBLANES - (CONV_WIDTH - 1):]
    return (y, new_conv, s_fin[0], ckv[meta_row0:], kpe[meta_row0:, PE_LO:PE_LO + ROPE_DIM])


def _sample_group(x_sample, cache_conv, state_gdn, cache_ckv, cache_kpe, page_table, p):
    b, L, d = x_sample.shape
    m = b * L
    page = cache_ckv.shape[2]
    past_len = page_table.shape[1] * page
    x = x_sample.reshape(m, d)
    tm = min(m, ROW_TILE)
    qkv_pre, z, cq, ckv_raw, misc = front_proj(x, p["norm1"], p["w_in"], tm=tm)
    xp = jnp.concatenate([jnp.transpose(cache_conv.astype(F32), (1, 0, 2)),
                          jnp.transpose(qkv_pre.reshape(b, L, CONV_DIM), (1, 0, 2))], axis=0)
    misc_t = jnp.transpose(misc.reshape(b, L, MISC_W), (1, 0, 2))
    q, k, v, gb = conv_sample(xp, misc_t, p["conv_w"], p["a_log"], p["dt_bias"])
    C = SAMPLE_CHUNK
    to_chunks = lambda t: jnp.pad(jnp.transpose(t, (1, 0, 2)), ((0, 0), (0, C - L), (0, 0))).reshape(b * C, -1)
    gdn_o, s_fin = gdn(to_chunks(q), to_chunks(k), to_chunks(v), to_chunks(gb), state_gdn.astype(F32),
                       C=C, CB_prep=SAMPLE_PREP_CHUNKS if b % SAMPLE_PREP_CHUNKS == 0 else 1, CB_scan=1,
                       SB_scan=SAMPLE_SCAN_SEQS if b % SAMPLE_SCAN_SEQS == 0 else 1, n_seq=b)
    gdn_o = gdn_o.reshape(b, C, GDN_V_WIDTH)[:, :L].reshape(m, GDN_V_WIDTH)
    tabs = _rope_tables(np.tile(past_len + np.arange(L), b))
    ckv, kpe, qh = mla_prep(cq, ckv_raw, misc, tabs, p, tm=tm, with_kv=False)
    kpe32 = kpe[:, PE_LO:PE_LO + ROPE_DIM]
    qabs = q_absorb(qh, p["w_abs"])
    R = L * MLA_HEADS
    rows_lh = lambda t: jnp.transpose(t.reshape(MLA_HEADS, b, L, -1), (1, 2, 0, 3)).reshape(b, R, -1)
    qpe = rows_lh(qh[:, :, PE_LO:PE_LO + ROPE_DIM])
    new_ckv = jnp.pad(ckv.reshape(b, L, KV_RANK), ((0, 0), (0, SUBLANES - L), (0, 0)))
    new_kpe_t = jnp.pad(jnp.swapaxes(kpe32.reshape(b, L, ROPE_DIM), 1, 2), ((0, 0), (0, 0), (0, SUBLANES - L)))
    n_pages = page_table.shape[1]
    PG = DECODE_STEP_PAGES if n_pages % DECODE_STEP_PAGES == 0 else DECODE_SCORE_PAGES
    o_lat = decode_attn(page_table, p["w_uk_t"], rows_lh(qabs), qpe, new_ckv, new_kpe_t,
                        cache_ckv, jnp.swapaxes(cache_kpe, 2, 3), PG=PG, L=L)
    o_lat_h = jnp.transpose(o_lat.reshape(b, L, MLA_HEADS, KV_RANK), (2, 0, 1, 3)).reshape(MLA_HEADS, m, KV_RANK)
    mla_o = latent_out(o_lat_h, p["w_uv_h"])
    y = layer_back(x, gdn_o, z, mla_o, p, tm=tm, row_block0=0)
    new_conv = qkv_pre.reshape(b, L, CONV_DIM)[:, L - (CONV_WIDTH - 1):]
    return (y.reshape(b, L, d), new_conv, s_fin, ckv.reshape(b, L, KV_RANK), kpe32.reshape(b, L, ROPE_DIM))


def kernel(x_prompt, x_sample, cache_conv, state_gdn, cache_ckv, cache_kpe, page_table, meta_tokens, norm1, w_in, conv_w, a_log, dt_bias, gdn_norm, q_a_norm, w_q_b, q_nope_norm, q_pe_norm, kv_a_norm, k_pe_norm, w_uk, w_uv, k_nope_norm, w_o, norm2, w_gate, w_up, w_down):
    assert x_prompt.shape[0] == 1 and norm1.shape[0] == 1, "one prompt sequence, one layer"
    p = _prep_params(norm1[0], w_in[0], conv_w[0], a_log[0], dt_bias[0], gdn_norm[0], q_a_norm[0],
                     w_q_b[0], q_nope_norm[0], q_pe_norm[0], kv_a_norm[0], k_pe_norm[0], w_uk[0],
                     w_uv[0], k_nope_norm[0], w_o[0], norm2[0], w_gate[0], w_up[0], w_down[0])
    y_p, conv_p, st_p, ckv_p, kpe_p = _prompt_group(x_prompt[0], meta_tokens, p)
    y_s, conv_s, st_s, ckv_s, kpe_s = _sample_group(x_sample, cache_conv[0], state_gdn[0], cache_ckv,
                                                     cache_kpe, page_table, p)
    lead = lambda t: t[None, None]
    return (y_p[None], y_s, lead(conv_p), lead(st_p), lead(ckv_p), lead(kpe_p),
            conv_s[None], st_s[None], ckv_s[None], kpe_s[None])
```

```python
import functools
import math

import jax
import jax.numpy as jnp
import numpy as np
from jax import lax
from jax.experimental import pallas as pl
from jax.experimental.pallas import tpu as pltpu

F32 = jnp.float32
BF16 = jnp.bfloat16
HIGHEST = lax.Precision.HIGHEST

N_META = 16
EPS = 1e-6
NEG_INF = -1e30
GDN_HEADS = 4
GDN_DK = 128
GDN_DV = 128
GDN_QK_WIDTH = GDN_HEADS * GDN_DK
GDN_V_WIDTH = GDN_HEADS * GDN_DV
CONV_WIDTH = 4
CONV_DIM = 2 * GDN_QK_WIDTH + GDN_V_WIDTH
CHUNK = 64
MLA_HEADS = 8
Q_RANK = 384
KV_RANK = 256
NOPE_DIM = 64
ROPE_DIM = 32
V_DIM = 64
QK_DIM = NOPE_DIM + ROPE_DIM
ROPE_THETA = 10000.0
SCORE_SCALE_LOG2 = QK_DIM ** -0.5 * math.log2(math.e)

LANES = 128
SUBLANES = 8
BF16_ROWS = 16
MXU_WIDTH = 256
VMEM_BYTES = 64 * 1024 * 1024
VMEM_LIMIT = VMEM_BYTES - 8 * 1024 * 1024

HEAD_PAD = LANES
PE_LO = NOPE_DIM
PE_HALF = ROPE_DIM // 2
MISC_W = LANES
VT_ROWS = V_DIM + BF16_ROWS

P0 = 512
ROW_TILE = MXU_WIDTH
BACK_TILE = 2 * MXU_WIDTH
FLASH_TQ = 512
FLASH_TK = 512
FLASH_QPARTS = FLASH_TQ // MXU_WIDTH
FLASH_KPARTS = FLASH_TK // MXU_WIDTH
FLASH_AHEAD = 5
GDN_PREP_CHUNKS = 4
GDN_SCAN_CHUNKS = 8
GDN_SPLIT_STEPS = 3
SAMPLE_CHUNK = BF16_ROWS
SAMPLE_PREP_CHUNKS = 8
SAMPLE_SCAN_SEQS = 4
DECODE_STEP_PAGES = 128
DECODE_SCORE_PAGES = 4
DECODE_AHEAD = 2


def _cp(sem, vmem=None):
    return pltpu.CompilerParams(dimension_semantics=sem, vmem_limit_bytes=vmem or VMEM_LIMIT)


def _sigmoid(x):
    return 1.0 / (1.0 + jnp.exp(-x))


def _softplus(x):
    return jnp.maximum(x, 0.0) + jnp.log1p(jnp.exp(-jnp.abs(x)))


def _dot(a, b, precision=None):
    return jnp.dot(a, b, preferred_element_type=F32, precision=precision)


def _dot_nt(a, b, precision=None):
    return lax.dot_general(a, b, (((1,), (1,)), ((), ())), preferred_element_type=F32,
                           precision=precision)


W_SPLITS = (CONV_DIM, GDN_V_WIDTH, Q_RANK, KV_RANK, MISC_W)


def _front_body(x_ref, g_ref, w_ref, *outs):
    x = x_ref[...]
    hn = ((x * lax.rsqrt(jnp.mean(x * x, axis=-1, keepdims=True) + EPS)) * g_ref[...]).astype(BF16)
    off = 0
    for width, o_ref in zip(W_SPLITS, outs):
        o_ref[...] = _dot(hn, w_ref[:, off:off + width])
        off += width


def front_proj(x, norm1, w_r, tm):
    m, d = x.shape
    n_all = sum(W_SPLITS)
    row = lambda i: (i, 0)
    fixed = lambda i: (0, 0)
    return pl.pallas_call(
        _front_body,
        grid=(m // tm,),
        in_specs=[pl.BlockSpec((tm, d), row), pl.BlockSpec((1, d), fixed), pl.BlockSpec((d, n_all), fixed)],
        out_specs=[pl.BlockSpec((tm, w), row) for w in W_SPLITS],
        out_shape=[jax.ShapeDtypeStruct((m, w), F32) for w in W_SPLITS],
        compiler_params=_cp(("parallel",)),
        name="front_proj",
    )(x, norm1.reshape(1, d), w_r)


def _gates(misc, alog, dtb):
    lane = lax.broadcasted_iota(jnp.int32, misc.shape, 1)
    g = -jnp.exp(alog) * _softplus(misc + dtb)
    beta = _sigmoid(misc)
    return jnp.where(lane < GDN_HEADS, g, jnp.where(lane < 2 * GDN_HEADS, beta, 0.0))


def _conv_cols(conv, c, q_ref, k_ref, v_ref):
    cols = slice(c * LANES, (c + 1) * LANES)
    acc = conv(cols)
    y = acc * _sigmoid(acc)
    if c < 2 * GDN_HEADS:
        yn = y * lax.rsqrt(jnp.sum(y * y, axis=-1, keepdims=True) + EPS)
        if c < GDN_HEADS:
            q_ref[:, cols] = yn * (GDN_DK ** -0.5)
        else:
            k_ref[:, (c - GDN_HEADS) * LANES:(c - GDN_HEADS + 1) * LANES] = yn
    else:
        v_ref[:, (c - 2 * GDN_HEADS) * LANES:(c - 2 * GDN_HEADS + 1) * LANES] = y


def _front_conv_body(lead_ref, x_ref, g_ref, w_ref, cw_ref, alog_ref, dtb_ref,
                     q_ref, k_ref, v_ref, gb_ref, tail_ref, z_ref, cq_ref, ckv_ref, misc_ref, xs,
                     *, tm, first_valid_row):
    i = pl.program_id(0)
    x = jnp.where(i == 0, lead_ref[...], x_ref[...])
    hn = ((x * lax.rsqrt(jnp.mean(x * x, axis=-1, keepdims=True) + EPS)) * g_ref[...]).astype(BF16)
    halo = SUBLANES
    xs[0:halo, :] = jnp.where(i > 0, xs[tm:tm + halo, :], 0.0)
    xs[halo:tm + halo, :] = _dot(hn, w_ref[:, 0:CONV_DIM])
    off = CONV_DIM
    for width, o_ref in zip(W_SPLITS[1:], (z_ref, cq_ref, ckv_ref, misc_ref)):
        o_ref[...] = _dot(hn, w_ref[:, off:off + width])
        off += width
    tail_ref[...] = xs[tm:tm + halo, :]

    def conv(cols):
        x_ext = xs[:, cols]
        t = x_ext * cw_ref[0:1, cols]
        for j in range(1, CONV_WIDTH):
            t = x_ext * cw_ref[j:j + 1, cols] + pltpu.roll(t, 1, 0)
        return t[halo:, :]

    for c in range(CONV_DIM // LANES):
        _conv_cols(conv, c, q_ref, k_ref, v_ref)
    gb = _gates(misc_ref[...], alog_ref[...], dtb_ref[...])
    row = i * tm + lax.broadcasted_iota(jnp.int32, gb.shape, 0)
    gb_ref[...] = jnp.where(row >= first_valid_row, gb, 0.0)


def front_conv_prompt(x, lead, norm1, w_r, conv_w, alog, dtb, tm, first_valid_row):
    m, d = x.shape
    n_all = sum(W_SPLITS)
    row = lambda i: (i, 0)
    fixed = lambda i: (0, 0)
    n_blocks = m // tm + 1
    rows = n_blocks * tm
    widths = (GDN_QK_WIDTH, GDN_QK_WIDTH, GDN_V_WIDTH, MISC_W) + (CONV_DIM,) + W_SPLITS[1:]
    out_specs = [pl.BlockSpec((tm, w), row) for w in widths]
    out_shape = [jax.ShapeDtypeStruct((rows, w), F32) for w in widths]
    out_specs[4] = pl.BlockSpec((SUBLANES, CONV_DIM), fixed)
    out_shape[4] = jax.ShapeDtypeStruct((SUBLANES, CONV_DIM), F32)
    body = functools.partial(_front_conv_body, tm=tm, first_valid_row=first_valid_row)
    return pl.pallas_call(
        body,
        grid=(n_blocks,),
        in_specs=[pl.BlockSpec((tm, d), fixed), pl.BlockSpec((tm, d), lambda i: (jnp.maximum(i - 1, 0), 0)),
                  pl.BlockSpec((1, d), fixed), pl.BlockSpec((d, n_all), fixed),
                  pl.BlockSpec((CONV_WIDTH, CONV_DIM), fixed),
                  pl.BlockSpec((1, MISC_W), fixed), pl.BlockSpec((1, MISC_W), fixed)],
        out_specs=out_specs,
        out_shape=out_shape,
        scratch_shapes=[pltpu.VMEM((tm + SUBLANES, CONV_DIM), F32)],
        compiler_params=_cp(("arbitrary",)),
        name="front_conv_prompt",
    )(lead, x, norm1.reshape(1, d), w_r, conv_w, alog, dtb)


def _conv_sample_body(xp_ref, misc_ref, w_ref, alog_ref, dtb_ref, q_ref, k_ref, v_ref, gb_ref, *, L):
    for l in range(L):
        conv = lambda cols, l=l: sum(xp_ref[l + j, :, cols] * w_ref[j:j + 1, cols] for j in range(CONV_WIDTH))
        qv, kv, vv = (r.at[l] for r in (q_ref, k_ref, v_ref))
        for c in range(CONV_DIM // LANES):
            _conv_cols(conv, c, qv, kv, vv)
        gb_ref[l] = _gates(misc_ref[l], alog_ref[...], dtb_ref[...])


def conv_sample(xp, misc, conv_w, alog, dtb):
    lp, b, _ = xp.shape
    L = lp - (CONV_WIDTH - 1)
    body = functools.partial(_conv_sample_body, L=L)
    return pl.pallas_call(
        body,
        out_shape=[jax.ShapeDtypeStruct((L, b, GDN_QK_WIDTH), F32)] * 3
        + [jax.ShapeDtypeStruct((L, b, MISC_W), F32)],
        compiler_params=pltpu.CompilerParams(vmem_limit_bytes=VMEM_LIMIT),
        name="conv_sample",
    )(xp, misc, conv_w, alog, dtb)


def _split_bf16(x):
    hi = x.astype(BF16)
    return hi, (x - hi.astype(F32)).astype(BF16)


def _rows_hi_lo(x):
    return jnp.concatenate(_split_bf16(x), axis=0)


def _gdn_prep_body(q_ref, k_ref, v_ref, gb_ref, wq_ref, ut_ref, qk_ref, kdt_ref, gl_ref, *, C, CB):
    row = lax.broadcasted_iota(jnp.int32, (C, C), 0)
    col = lax.broadcasted_iota(jnp.int32, (C, C), 1)
    incl = row >= col
    strict = row > col
    tri = incl.astype(F32)
    n_sq = int(math.log2(C)) - 1
    inst = [(cc, h) for cc in range(CB) for h in range(GDN_HEADS)]
    rows_of = lambda cc: slice(cc * C, (cc + 1) * C)
    cols_of = lambda h: slice(h * LANES, (h + 1) * LANES)
    g_cum, g_cum_t, gbs = [], [], []
    for cc in range(CB):
        gb = gb_ref[rows_of(cc), :]
        gc_all = _dot(tri, gb, HIGHEST)
        gl_ref[cc] = jnp.broadcast_to(jnp.exp(gc_all[C - 1:C, :]), (SUBLANES, LANES))
        gbs.append(gb)
        g_cum.append(gc_all)
        g_cum_t.append(gc_all.T)
    ks = [k_ref[rows_of(cc), cols_of(h)] for cc, h in inst]
    both = [_dot_nt(jnp.concatenate([q_ref[rows_of(cc), cols_of(h)], k], axis=0).astype(BF16), k.astype(BF16))
            for (cc, h), k in zip(inst, ks)]
    a, r, gcs = [], [], []
    for i, (cc, h) in enumerate(inst):
        gc = g_cum[cc][:, h:h + 1]
        gr = g_cum_t[cc][h:h + 1, :]
        beta = gbs[cc][:, GDN_HEADS + h:GDN_HEADS + h + 1]
        decay = jnp.where(incl, jnp.exp(jnp.where(incl, gc - gr, 0.0)), 0.0)
        a.append(jnp.where(strict, -(beta * both[i][C:2 * C] * decay), 0.0))
        qk = both[i][0:C] * decay
        qk_ref[rows_of(cc), cols_of(h)] = jnp.concatenate(
            [qk, jnp.zeros((C, LANES - C), F32)], axis=1).astype(BF16)
        e_g = jnp.exp(gc)
        wq_ref[cc, C:2 * C, cols_of(h)] = (e_g * q_ref[rows_of(cc), cols_of(h)]).astype(BF16)
        k_dec = jnp.exp(g_cum[cc][C - 1:C, h:h + 1] - gc) * ks[i]
        kdt_ref[cc, h] = jnp.concatenate([k_dec, k_dec], axis=0).T.astype(BF16)
        r.append(jnp.concatenate([(beta * e_g) * ks[i], beta * v_ref[rows_of(cc), cols_of(h)]], axis=1))
    for step in range(n_sq + 1):
        ab = [x.astype(BF16) for x in a]
        if step < GDN_SPLIT_STEPS:
            r = [x + _dot(jnp.concatenate([y, y], axis=1), _rows_hi_lo(x)) for x, y in zip(r, ab)]
        else:
            r = [x + _dot(y, x.astype(BF16)) for x, y in zip(r, ab)]
        if step < n_sq:
            a = [_dot(y, y) for y in ab]
    for i, (cc, h) in enumerate(inst):
        wq_ref[cc, 0:C, cols_of(h)] = r[i][:, 0:LANES].astype(BF16)
        ut_ref[rows_of(cc), cols_of(h)] = r[i][:, LANES:2 * LANES]


def gdn_prep(q, k, v, gb, *, C, CB):
    rows = q.shape[0]
    nc = rows // C
    blk = lambda i: (i, 0)
    body = functools.partial(_gdn_prep_body, C=C, CB=CB)
    return pl.pallas_call(
        body,
        grid=(nc // CB,),
        in_specs=[pl.BlockSpec((CB * C, GDN_QK_WIDTH), blk)] * 3 + [pl.BlockSpec((CB * C, MISC_W), blk)],
        out_specs=[pl.BlockSpec((CB, 2 * C, GDN_QK_WIDTH), lambda i: (i, 0, 0)),
                   pl.BlockSpec((CB * C, GDN_V_WIDTH), blk),
                   pl.BlockSpec((CB * C, GDN_HEADS * LANES), blk),
                   pl.BlockSpec((CB, GDN_HEADS, GDN_DK, 2 * C), lambda i: (i, 0, 0, 0)),
                   pl.BlockSpec((CB, SUBLANES, LANES), lambda i: (i, 0, 0))],
        out_shape=[jax.ShapeDtypeStruct((nc, 2 * C, GDN_QK_WIDTH), BF16),
                   jax.ShapeDtypeStruct((rows, GDN_V_WIDTH), F32),
                   jax.ShapeDtypeStruct((rows, GDN_HEADS * LANES), BF16),
                   jax.ShapeDtypeStruct((nc, GDN_HEADS, GDN_DK, 2 * C), BF16),
                   jax.ShapeDtypeStruct((nc, SUBLANES, LANES), F32)],
        compiler_params=_cp(("parallel",)),
        name="gdn_prep",
    )(q, k, v, gb)


def _gdn_scan_body(gl_ref, wq_ref, ut_ref, qk_ref, kdt_ref, s0_ref, o_ref, sfin_ref, s_scr,
                   *, C, CB, SB, n_steps):
    sq = pl.program_id(0)
    st = pl.program_id(1)

    @pl.when(st == 0)
    def _():
        s_scr[...] = s0_ref[...]

    cols = [slice(h * LANES, (h + 1) * LANES) for h in range(GDN_HEADS)]
    for cc in range(CB):
        inst = [(sb, h) for sb in range(SB) for h in range(GDN_HEADS)]
        blk = [sb * CB + cc for sb, _ in inst]
        rows = [slice(b * C, (b + 1) * C) for b in blk]
        chunk = [((sq * SB + sb) * n_steps + st) * CB + cc for sb, _ in inst]
        ss = [s_scr[sb, h] for sb, h in inst]
        ws = [_dot(jnp.concatenate([wq_ref[blk[i], :, cols[h]]] * 2, axis=1), _rows_hi_lo(ss[i]))
              for i, (_, h) in enumerate(inst)]
        u2 = [_rows_hi_lo(ut_ref[rows[i], cols[h]] - ws[i][0:C]) for i, (_, h) in enumerate(inst)]
        for i, (sb, h) in enumerate(inst):
            s_scr[sb, h] = gl_ref[chunk[i] * GDN_HEADS + h] * ss[i] + _dot(kdt_ref[blk[i], h], u2[i])
        for i, (_, h) in enumerate(inst):
            o_ref[rows[i], cols[h]] = ws[i][C:2 * C] + _dot(qk_ref[rows[i], h * LANES:h * LANES + C], u2[i][0:C])

    @pl.when(st == n_steps - 1)
    def _():
        sfin_ref[...] = s_scr[...]


def gdn_scan(gl, wq, ut, qk, kdt, s0, *, C, CB, SB, n_seq):
    rows = ut.shape[0]
    nc = rows // C
    n_steps = nc // (n_seq * CB)
    assert SB == 1 or n_steps == 1
    nb = SB * CB
    blk = lambda s, t, gl: (s * n_steps + t, 0)
    blk3 = lambda s, t, gl: (s * n_steps + t, 0, 0)
    blk4 = lambda s, t, gl: (s * n_steps + t, 0, 0, 0)
    st4 = lambda s, t, gl: (s, 0, 0, 0)
    body = functools.partial(_gdn_scan_body, C=C, CB=CB, SB=SB, n_steps=n_steps)
    grid_spec = pltpu.PrefetchScalarGridSpec(
        num_scalar_prefetch=1,
        grid=(n_seq // SB, n_steps),
        in_specs=[pl.BlockSpec((nb, 2 * C, GDN_QK_WIDTH), blk3), pl.BlockSpec((nb * C, GDN_V_WIDTH), blk),
                  pl.BlockSpec((nb * C, GDN_HEADS * LANES), blk),
                  pl.BlockSpec((nb, GDN_HEADS, GDN_DK, 2 * C), blk4),
                  pl.BlockSpec((SB, GDN_HEADS, GDN_DK, GDN_DV), st4)],
        out_specs=[pl.BlockSpec((nb * C, GDN_V_WIDTH), blk),
                   pl.BlockSpec((SB, GDN_HEADS, GDN_DK, GDN_DV), st4)],
        scratch_shapes=[pltpu.VMEM((SB, GDN_HEADS, GDN_DK, GDN_DV), F32)],
    )
    return pl.pallas_call(
        body,
        grid_spec=grid_spec,
        out_shape=[jax.ShapeDtypeStruct((rows, GDN_V_WIDTH), F32), jax.ShapeDtypeStruct(s0.shape, F32)],
        compiler_params=_cp(("arbitrary", "arbitrary")),
        name="gdn_scan",
    )(gl, wq, ut, qk, kdt, s0)


def gdn(q, k, v, gb, s0, *, C, CB_prep, CB_scan, SB_scan, n_seq):
    wq, ut, qk, kdt, glv = gdn_prep(q, k, v, gb, C=C, CB=CB_prep)
    gl = glv[:, 0, :GDN_HEADS].reshape(-1)
    return gdn_scan(gl, wq, ut, qk, kdt, s0, C=C, CB=CB_scan, SB=SB_scan, n_seq=n_seq)


def _rope(y, cosb, sin_lo, sin_hi):
    return (y * cosb + pltpu.roll(y, LANES - PE_HALF, 1) * sin_lo + pltpu.roll(y, PE_HALF, 1) * sin_hi)


def _slab_norm(x, gain, lane, parts="np"):
    x2 = x * x
    is_nope = lane < NOPE_DIM
    is_pe = (lane >= PE_LO) & (lane < PE_LO + ROPE_DIM)
    if parts == "n":
        rinv = lax.rsqrt(jnp.sum(x2, axis=-1, keepdims=True) * (1.0 / NOPE_DIM) + EPS)
    elif parts == "p":
        rinv = lax.rsqrt(jnp.sum(x2, axis=-1, keepdims=True) * (1.0 / ROPE_DIM) + EPS)
    else:
        ms_n = jnp.sum(jnp.where(is_nope, x2, 0.0), axis=-1, keepdims=True) * (1.0 / NOPE_DIM)
        ms_p = jnp.sum(jnp.where(is_pe, x2, 0.0), axis=-1, keepdims=True) * (1.0 / ROPE_DIM)
        rinv = jnp.where(is_nope, lax.rsqrt(ms_n + EPS), lax.rsqrt(ms_p + EPS))
    return (x * rinv) * gain


def _mla_body(cq_ref, ckvr_ref, misc_ref, cos_ref, slo_ref, shi_ref, qan_ref, wqb_ref, qg_ref,
              kvn_ref, kg_ref, *rest, with_kv):
    if with_kv:
        wuk_ref, wuvt_ref, ckv_ref, kpe_ref, q_ref, k_ref, vt_ref = rest
    else:
        ckv_ref, kpe_ref, q_ref = rest
    cosb, slo, shi = cos_ref[...], slo_ref[...], shi_ref[...]
    lane = lax.broadcasted_iota(jnp.int32, cosb.shape, 1)
    cq = cq_ref[...]
    cqn = ((cq * lax.rsqrt(jnp.mean(cq * cq, axis=-1, keepdims=True) + EPS)) * qan_ref[...]).astype(BF16)
    c = ckvr_ref[...]
    ckv = (c * lax.rsqrt(jnp.mean(c * c, axis=-1, keepdims=True) + EPS)) * kvn_ref[...]
    ckv_ref[...] = ckv
    ckv_b = ckv.astype(BF16)
    kpe_raw = jnp.where((lane >= PE_LO) & (lane < PE_LO + ROPE_DIM), misc_ref[...], 0.0)
    kpe = _rope(_slab_norm(kpe_raw, kg_ref[...], lane, "p"), cosb, slo, shi)
    kpe = jnp.where(lane >= PE_LO, kpe, 0.0)
    kpe_ref[...] = kpe
    for h in range(MLA_HEADS):
        cols = slice(h * HEAD_PAD, (h + 1) * HEAD_PAD)
        qh = _dot(cqn, wqb_ref[:, cols])
        qh = _rope(_slab_norm(qh, qg_ref[...], lane), cosb, slo, shi)
        q_ref[h] = (qh * SCORE_SCALE_LOG2).astype(BF16)
        if with_kv:
            kh = _dot(ckv_b, wuk_ref[:, cols])
            k_ref[h] = (_slab_norm(kh, kg_ref[...], lane, "n") + kpe).astype(BF16)
            vt = _dot_nt(wuvt_ref[h], ckv_b)
            ones_row = lax.broadcasted_iota(jnp.int32, vt.shape, 0) == V_DIM
            vt_ref[h] = jnp.where(ones_row, 1.0, vt).astype(BF16)


def mla_prep(cq, ckv_raw, misc, rope_tabs, p, tm, with_kv):
    m = cq.shape[0]
    row = lambda i: (i, 0)
    fixed = lambda i: (0, 0)
    hw = MLA_HEADS * HEAD_PAD
    in_specs = ([pl.BlockSpec((tm, Q_RANK), row), pl.BlockSpec((tm, KV_RANK), row),
                 pl.BlockSpec((tm, MISC_W), row)] + [pl.BlockSpec((tm, LANES), row)] * 3
                + [pl.BlockSpec((1, Q_RANK), fixed), pl.BlockSpec((Q_RANK, hw), fixed),
                   pl.BlockSpec((1, LANES), fixed), pl.BlockSpec((1, KV_RANK), fixed),
                   pl.BlockSpec((1, LANES), fixed)])
    args = [cq, ckv_raw, misc, *rope_tabs, p["q_a_norm"], p["w_q_b"], p["q_gain"], p["kv_a_norm"], p["k_gain"]]
    out_specs = [pl.BlockSpec((tm, KV_RANK), row), pl.BlockSpec((tm, LANES), row),
                 pl.BlockSpec((MLA_HEADS, tm, HEAD_PAD), lambda i: (0, i, 0))]
    out_shape = [jax.ShapeDtypeStruct((m, KV_RANK), F32), jax.ShapeDtypeStruct((m, LANES), F32),
                 jax.ShapeDtypeStruct((MLA_HEADS, m, HEAD_PAD), BF16)]
    if with_kv:
        in_specs += [pl.BlockSpec((KV_RANK, hw), fixed),
                     pl.BlockSpec((MLA_HEADS, VT_ROWS, KV_RANK), lambda i: (0, 0, 0))]
        args += [p["w_uk"], p["w_uv_t"]]
        out_specs += [pl.BlockSpec((MLA_HEADS, tm, HEAD_PAD), lambda i: (0, i, 0)),
                      pl.BlockSpec((MLA_HEADS, VT_ROWS, tm), lambda i: (0, 0, i))]
        out_shape += [jax.ShapeDtypeStruct((MLA_HEADS, m, HEAD_PAD), BF16),
                      jax.ShapeDtypeStruct((MLA_HEADS, VT_ROWS, m), BF16)]
    return pl.pallas_call(
        functools.partial(_mla_body, with_kv=with_kv),
        grid=(m // tm,),
        in_specs=in_specs,
        out_specs=out_specs,
        out_shape=out_shape,
        compiler_params=_cp(("parallel",)),
        name="mla_prep",
    )(*args)


def _flash_body(qmap_ref, kmap_ref, q_ref, k_ref, vt_ref, kpre_ref, vtpre_ref, o_ref, m_scr, acc_scr,
                *, tq, tk, q_block0):
    step = pl.program_id(0)
    q0 = (qmap_ref[step] + q_block0) * tq
    k0 = kmap_ref[step] * tk

    @pl.when(k0 == P0)
    def _():
        sts = [_dot_nt(kpre_ref[h], q_ref[h]) for h in range(MLA_HEADS)]
        m0s = [jnp.max(st, axis=0, keepdims=True) for st in sts]
        for h in range(MLA_HEADS):
            m_scr[h] = m0s[h]
            acc_scr[h] = _dot(vtpre_ref[h], jnp.exp2(sts[h] - m0s[h]).astype(BF16))

    def update(masked):
        tqp = tq // FLASH_QPARTS
        tkp = tk // FLASH_KPARTS
        items = [(h, kp, qp) for h in range(MLA_HEADS) for kp in range(FLASH_KPARTS)
                 for qp in range(FLASH_QPARTS)]
        scores = lambda h, kp, qp: _dot_nt(k_ref[h, kp * tkp:(kp + 1) * tkp, :],
                                           q_ref[h, qp * tqp:(qp + 1) * tqp, :])
        pending = [scores(*it) for it in items[:FLASH_AHEAD]]
        for n, (h, kp, qp) in enumerate(items):
            st = pending.pop(0)
            if n + FLASH_AHEAD < len(items):
                pending.append(scores(*items[n + FLASH_AHEAD]))
            qc = slice(qp * tqp, (qp + 1) * tqp)
            if masked:
                kpos = k0 + kp * tkp + lax.broadcasted_iota(jnp.int32, (tkp, tqp), 0)
                qpos = q0 + qp * tqp + lax.broadcasted_iota(jnp.int32, (tkp, tqp), 1)
                st = jnp.where(kpos <= qpos, st, NEG_INF)
            m_prev = m_scr[h, :, qc]
            m_new = jnp.maximum(m_prev, jnp.max(st, axis=0, keepdims=True))
            e = jnp.exp2(st - m_new).astype(BF16)
            acc_scr[h, :, qc] = (jnp.exp2(m_prev - m_new) * acc_scr[h, :, qc]
                                 + _dot(vt_ref[h, :, kp * tkp:(kp + 1) * tkp], e))
            m_scr[h, :, qc] = m_new

    needs_mask = k0 + tk - 1 > q0
    pl.when(needs_mask)(lambda: update(True))
    pl.when(jnp.logical_not(needs_mask))(lambda: update(False))

    @pl.when(k0 + tk >= q0 + tq)
    def _():
        for h in range(MLA_HEADS):
            a = acc_scr[h]
            o = (a[0:V_DIM] * (1.0 / a[V_DIM:V_DIM + 1, :])).T
            o_ref[:, h * HEAD_PAD:(h + 1) * HEAD_PAD] = jnp.concatenate(
                [o, jnp.zeros((tq, HEAD_PAD - V_DIM), F32)], axis=1).astype(BF16)


def flash_prompt(q, k, vt, *, tq, tk, n_prefix):
    tp = q.shape[1]
    q_block0 = P0 // tq
    nq = (tp - P0) // tq
    qs, ks = [], []
    for qi in range(nq):
        last_k = ((qi + q_block0 + 1) * tq - 1) // tk
        for ki in range(P0 // tk, last_k + 1):
            qs.append(qi)
            ks.append(ki)
    qmap = jnp.asarray(qs, jnp.int32)
    kmap = jnp.asarray(ks, jnp.int32)
    k_pre = k[:, P0 - n_prefix:P0]
    vt_pre = vt[:, :, P0 - n_prefix:P0]
    body = functools.partial(_flash_body, tq=tq, tk=tk, q_block0=q_block0)
    whole = lambda s, qm, km: (0, 0, 0)
    grid_spec = pltpu.PrefetchScalarGridSpec(
        num_scalar_prefetch=2,
        grid=(len(qs),),
        in_specs=[pl.BlockSpec((MLA_HEADS, tq, HEAD_PAD), lambda s, qm, km: (0, qm[s] + q_block0, 0)),
                  pl.BlockSpec((MLA_HEADS, tk, HEAD_PAD), lambda s, qm, km: (0, km[s], 0)),
                  pl.BlockSpec((MLA_HEADS, VT_ROWS, tk), lambda s, qm, km: (0, 0, km[s])),
                  pl.BlockSpec((MLA_HEADS, n_prefix, HEAD_PAD), whole),
                  pl.BlockSpec((MLA_HEADS, VT_ROWS, n_prefix), whole)],
        out_specs=pl.BlockSpec((tq, MLA_HEADS * HEAD_PAD), lambda s, qm, km: (qm[s], 0)),
        scratch_shapes=[pltpu.VMEM((MLA_HEADS, 1, tq), F32), pltpu.VMEM((MLA_HEADS, VT_ROWS, tq), F32)],
    )
    return pl.pallas_call(
        body,
        grid_spec=grid_spec,
        out_shape=jax.ShapeDtypeStruct((tp - P0, MLA_HEADS * HEAD_PAD), BF16),
        compiler_params=_cp(("arbitrary",)),
        name="flash_prompt",
    )(qmap, kmap, q, k, vt, k_pre, vt_pre)


def _qabs_body(q_ref, w_ref, o_ref):
    o_ref[0] = _dot(q_ref[0], w_ref[0]).astype(BF16)


def q_absorb(q, w_abs):
    h, m, _ = q.shape
    blk = lambda i: (i, 0, 0)
    return pl.pallas_call(
        _qabs_body,
        grid=(h,),
        in_specs=[pl.BlockSpec((1, m, HEAD_PAD), blk), pl.BlockSpec((1, HEAD_PAD, KV_RANK), blk)],
        out_specs=pl.BlockSpec((1, m, KV_RANK), blk),
        out_shape=jax.ShapeDtypeStruct((h, m, KV_RANK), BF16),
        compiler_params=_cp(("parallel",)),
        name="q_absorb",
    )(q, w_abs)


def _decode_scores(waug, qpe, ckv_b, kpe_t, n_lq):
    kp = _dot_nt(waug, ckv_b)
    n_k = MLA_HEADS * NOPE_DIM
    kproj = kp[:n_k]
    ms = jnp.sum((kproj * kproj).reshape(MLA_HEADS, NOPE_DIM, kproj.shape[-1]), axis=1) * (1.0 / NOPE_DIM)
    rinv = lax.rsqrt(ms + EPS)
    s_nope = kp[n_k:]
    s_pe = _dot(qpe, kpe_t)
    rinv_rows = jnp.concatenate([rinv] * n_lq, axis=0)
    return s_nope * rinv_rows + s_pe


def _page_copies(pt_ref, ckv_hbm, kpet_hbm, ckv_buf, kpe_buf, sem, step, slot, *, PG, page):
    out = []
    for i in range(PG):
        pid = pt_ref[step * PG + i]
        out.append(pltpu.make_async_copy(ckv_hbm.at[0, pid], ckv_buf.at[slot, pl.ds(i * page, page)],
                                         sem.at[0, slot]))
        out.append(pltpu.make_async_copy(kpet_hbm.at[0, pid], kpe_buf.at[slot, :, pl.ds(i * page, page)],
                                         sem.at[1, slot]))
    return out


def _decode_body(pt_ref, wukt_ref, qabs_ref, qpe_ref, nckv_ref, nkpet_ref, ckv_hbm, kpet_hbm, o_ref,
                 ckv_buf, kpe_buf, sem, waug, ckvb, m_scr, l_scr, acc_scr, *, PG, L, n_groups, n_steps, page):
    s = pl.program_id(0)
    j = lax.rem(s, n_groups)
    slot = lax.rem(s, 2)
    n_k = MLA_HEADS * NOPE_DIM
    R = MLA_HEADS * L
    copies = functools.partial(_page_copies, pt_ref, ckv_hbm, kpet_hbm, ckv_buf, kpe_buf, sem, PG=PG, page=page)

    nxt = lax.rem(s + 1, n_steps)

    @pl.when(s == 0)
    def _():
        for n, c in enumerate(copies(0, 0)):
            c.start(priority=(n // 2) % 2)

    @pl.when(j == 0)
    def _():
        waug[0:n_k, :] = wukt_ref[...]
        waug[n_k:n_k + R, :] = qabs_ref[0]
        c_new = nckv_ref[0].astype(BF16)
        sc = _decode_scores(waug[...], qpe_ref[0], c_new, nkpet_ref[0].astype(BF16), L)
        r_l = lax.shift_right_logical(lax.broadcasted_iota(jnp.int32, sc.shape, 0),
                                      int(math.log2(MLA_HEADS)))
        t = lax.broadcasted_iota(jnp.int32, sc.shape, 1)
        sc = jnp.where((t <= r_l) & (t < L), sc, NEG_INF)
        m0 = jnp.max(sc, axis=-1, keepdims=True)
        e0 = jnp.exp2(sc - m0)
        m_scr[...] = m0
        l_scr[...] = jnp.sum(e0, axis=-1, keepdims=True)
        acc_scr[...] = _dot(e0.astype(BF16), c_new)

    for c in copies(s, slot):
        c.wait()

    w = waug[...]
    qpe = qpe_ref[0]

    def scores(p):
        tok = pl.ds(p * DECODE_SCORE_PAGES * page, DECODE_SCORE_PAGES * page)
        c2 = ckv_buf[slot, tok, :].astype(BF16)
        ckvb[p] = c2
        return _decode_scores(w, qpe, c2, kpe_buf[slot, :, tok].astype(BF16), L)

    ms, ls, accs = [m_scr[...]], [l_scr[...]], [acc_scr[...]]
    n_groups_step = PG // DECODE_SCORE_PAGES
    pending = [scores(p) for p in range(min(DECODE_AHEAD, n_groups_step))]
    next_copies = copies(nxt, 1 - slot)
    per_group = -(-len(next_copies) // max(n_groups_step // 4, 1))
    for p in range(n_groups_step):
        sc = pending.pop(0)
        if p + DECODE_AHEAD < n_groups_step:
            pending.append(scores(p + DECODE_AHEAD))
        for n, c in enumerate(next_copies[p * per_group:(p + 1) * per_group]):
            c.start(priority=(n // 2) % 2)
        m_p = jnp.max(sc, axis=-1, keepdims=True)
        e = jnp.exp2(sc - m_p)
        ms.append(m_p)
        ls.append(jnp.sum(e, axis=-1, keepdims=True))
        accs.append(_dot(e.astype(BF16), ckvb[p]))
    m = functools.reduce(jnp.maximum, ms)
    fs = [jnp.exp2(x - m) for x in ms]
    l = functools.reduce(lambda x, y: x + y, [f * x for f, x in zip(fs, ls)])
    acc = functools.reduce(lambda x, y: x + y, [f * x for f, x in zip(fs, accs)])
    m_scr[...], l_scr[...], acc_scr[...] = m, l, acc

    @pl.when(j == n_groups - 1)
    def _():
        o_ref[0] = acc / l

    @pl.when(s == n_steps - 1)
    def _():
        for c in copies(nxt, 1 - slot):
            c.wait()


def decode_attn(page_table, wuk_t, qabs, qpe, new_ckv, new_kpe_t, cache_ckv, cache_kpe_t, *, PG, L):
    b, n_pages = page_table.shape
    page = cache_ckv.shape[2]
    R = MLA_HEADS * L
    n_k = MLA_HEADS * NOPE_DIM
    n_groups = n_pages // PG
    per_b = lambda s, pt: (s // n_groups, 0, 0)
    fixed = lambda s, pt: (0, 0)
    body = functools.partial(_decode_body, PG=PG, L=L, n_groups=n_groups, n_steps=b * n_groups, page=page)
    grid_spec = pltpu.PrefetchScalarGridSpec(
        num_scalar_prefetch=1,
        grid=(b * n_groups,),
        in_specs=[pl.BlockSpec((n_k, KV_RANK), fixed), pl.BlockSpec((1, R, KV_RANK), per_b),
                  pl.BlockSpec((1, R, ROPE_DIM), per_b), pl.BlockSpec((1, SUBLANES, KV_RANK), per_b),
                  pl.BlockSpec((1, ROPE_DIM, SUBLANES), per_b),
                  pl.BlockSpec(memory_space=pl.ANY), pl.BlockSpec(memory_space=pl.ANY)],
        out_specs=pl.BlockSpec((1, R, KV_RANK), per_b),
        scratch_shapes=[pltpu.VMEM((2, PG * page, KV_RANK), F32), pltpu.VMEM((2, ROPE_DIM, PG * page), F32),
                        pltpu.SemaphoreType.DMA((2, 2)),
                        pltpu.VMEM((n_k + R, KV_RANK), BF16), pltpu.VMEM((PG // DECODE_SCORE_PAGES, DECODE_SCORE_PAGES * page, KV_RANK), BF16),
                        pltpu.VMEM((R, 1), F32), pltpu.VMEM((R, 1), F32), pltpu.VMEM((R, KV_RANK), F32)],
    )
    return pl.pallas_call(
        body,
        grid_spec=grid_spec,
        out_shape=jax.ShapeDtypeStruct((b, R, KV_RANK), F32),
        compiler_params=_cp(("arbitrary",)),
        name="decode_attn",
    )(page_table.reshape(-1), wuk_t, qabs, qpe, new_ckv, new_kpe_t, cache_ckv, cache_kpe_t)


def _latent_out_body(x_ref, w_ref, o_ref):
    o_ref[...] = _dot(x_ref[0].astype(BF16), w_ref[0]).astype(BF16)


def latent_out(o_lat, w_uv_h):
    h, m, _ = o_lat.shape
    return pl.pallas_call(
        _latent_out_body,
        grid=(h,),
        in_specs=[pl.BlockSpec((1, m, KV_RANK), lambda i: (i, 0, 0)),
                  pl.BlockSpec((1, KV_RANK, HEAD_PAD), lambda i: (i, 0, 0))],
        out_specs=pl.BlockSpec((m, HEAD_PAD), lambda i: (0, i)),
        out_shape=jax.ShapeDtypeStruct((m, h * HEAD_PAD), BF16),
        compiler_params=_cp(("parallel",)),
        name="latent_out",
    )(o_lat, w_uv_h)


def _back_body(h_ref, o_ref, z_ref, mla_ref, gg_ref, wog_ref, wom_ref, n2_ref, wg_ref, wu_ref, wd_ref, y_ref):
    h1 = h_ref[...] + _dot(mla_ref[...], wom_ref[...])
    for h in range(GDN_HEADS):
        cols = slice(h * LANES, (h + 1) * LANES)
        o = o_ref[:, cols]
        z = z_ref[:, cols]
        y = ((o * lax.rsqrt(jnp.mean(o * o, axis=-1, keepdims=True) + EPS)) * gg_ref[...]) * (z * _sigmoid(z))
        h1 = h1 + _dot(y.astype(BF16), wog_ref[cols, :])
    hn = ((h1 * lax.rsqrt(jnp.mean(h1 * h1, axis=-1, keepdims=True) + EPS)) * n2_ref[...]).astype(BF16)
    g = _dot(hn, wg_ref[...])
    u = _dot(hn, wu_ref[...])
    y_ref[...] = h1 + _dot(((g * _sigmoid(g)) * u).astype(BF16), wd_ref[...])


def layer_back(h, gdn_o, z, mla_o, p, tm, row_block0):
    m, d = h.shape
    dff = p["w_gate"].shape[1]
    row = lambda i: (i, 0)
    prow = lambda i: (i + row_block0, 0)
    fixed = lambda i: (0, 0)
    hw = MLA_HEADS * HEAD_PAD
    return pl.pallas_call(
        _back_body,
        grid=(m // tm,),
        in_specs=[pl.BlockSpec((tm, d), row), pl.BlockSpec((tm, GDN_V_WIDTH), prow),
                  pl.BlockSpec((tm, GDN_V_WIDTH), prow), pl.BlockSpec((tm, hw), row),
                  pl.BlockSpec((1, LANES), fixed), pl.BlockSpec((GDN_V_WIDTH, d), fixed),
                  pl.BlockSpec((hw, d), fixed), pl.BlockSpec((1, d), fixed),
                  pl.BlockSpec((d, dff), fixed), pl.BlockSpec((d, dff), fixed), pl.BlockSpec((dff, d), fixed)],
        out_specs=pl.BlockSpec((tm, d), row),
        out_shape=jax.ShapeDtypeStruct((m, d), F32),
        compiler_params=_cp(("parallel",)),
        name="layer_back",
    )(h, gdn_o, z, mla_o, p["gdn_norm"], p["w_o_gdn"], p["w_o_mla"], p["norm2"],
      p["w_gate"], p["w_up"], p["w_down"])


def _head_slabs(w, width):
    k, h, _ = w.shape
    return jnp.pad(w, ((0, 0), (0, 0), (0, HEAD_PAD - width))).reshape(k, h * HEAD_PAD)


def _pad_lanes(v, lo=0):
    return jnp.pad(v.astype(F32), (lo, LANES - lo - v.shape[0])).reshape(1, LANES)


def _prep_params(norm1, w_in, conv_w, a_log, dt_bias, gdn_norm, q_a_norm, w_q_b, q_nope_norm,
                 q_pe_norm, kv_a_norm, k_pe_norm, w_uk, w_uv, k_nope_norm, w_o, norm2,
                 w_gate, w_up, w_down):
    d = w_in.shape[0]
    o = 0
    cuts = {}
    for name, wdt in (("qkv", CONV_DIM), ("z", GDN_V_WIDTH), ("a", GDN_HEADS), ("b", GDN_HEADS),
                      ("cq", Q_RANK), ("ckv", KV_RANK), ("kpe", ROPE_DIM)):
        cuts[name] = w_in[:, o:o + wdt]
        o += wdt
    zeros = lambda n: jnp.zeros((d, n), w_in.dtype)
    misc = jnp.concatenate([cuts["a"], cuts["b"], zeros(PE_LO - 2 * GDN_HEADS), cuts["kpe"],
                            zeros(MISC_W - PE_LO - ROPE_DIM)], axis=1)
    w_r = jnp.concatenate([cuts["qkv"], cuts["z"], cuts["cq"], cuts["ckv"], misc], axis=1).astype(BF16)
    wqb = w_q_b.reshape(Q_RANK, MLA_HEADS, QK_DIM)
    gain_q = jnp.concatenate([q_nope_norm, q_pe_norm]).astype(F32)
    gain_k = jnp.concatenate([k_nope_norm, k_pe_norm]).astype(F32)
    w_uk_hdr = jnp.transpose(w_uk, (1, 2, 0))
    w_abs = jnp.pad(w_uk_hdr * k_nope_norm[None, :, None], ((0, 0), (0, HEAD_PAD - NOPE_DIM), (0, 0)))
    w_o_mla = jnp.pad(w_o[GDN_V_WIDTH:].reshape(MLA_HEADS, V_DIM, d),
                      ((0, 0), (0, HEAD_PAD - V_DIM), (0, 0))).reshape(MLA_HEADS * HEAD_PAD, d)
    return dict(
        norm1=norm1, w_in=w_r, conv_w=conv_w.astype(F32),
        a_log=_pad_lanes(a_log), dt_bias=_pad_lanes(dt_bias),
        gdn_norm=gdn_norm.reshape(1, GDN_DV).astype(F32),
        q_a_norm=q_a_norm.reshape(1, Q_RANK).astype(F32), w_q_b=_head_slabs(wqb, QK_DIM).astype(BF16),
        q_gain=_pad_lanes(gain_q), kv_a_norm=kv_a_norm.reshape(1, KV_RANK).astype(F32),
        w_uk=_head_slabs(w_uk, NOPE_DIM).astype(BF16), k_gain=_pad_lanes(gain_k),
        w_uv_t=jnp.pad(jnp.transpose(w_uv, (1, 2, 0)), ((0, 0), (0, VT_ROWS - V_DIM), (0, 0))).astype(BF16),
        w_uk_t=w_uk_hdr.reshape(MLA_HEADS * NOPE_DIM, KV_RANK).astype(BF16),
        w_abs=w_abs.astype(BF16),
        w_uv_h=jnp.pad(jnp.transpose(w_uv, (1, 0, 2)), ((0, 0), (0, 0), (0, HEAD_PAD - V_DIM))).astype(BF16),
        w_o_gdn=w_o[:GDN_V_WIDTH].astype(BF16), w_o_mla=w_o_mla.astype(BF16),
        norm2=norm2.reshape(1, d).astype(F32),
        w_gate=w_gate.astype(BF16), w_up=w_up.astype(BF16), w_down=w_down.astype(BF16),
    )


def _rope_tables(pos):
    inv_freq = np.power(ROPE_THETA, -np.arange(0, ROPE_DIM, 2, dtype=np.float64) / ROPE_DIM)
    ang = np.asarray(pos, np.float64)[:, None] * inv_freq[None, :]
    cos, sin = jnp.asarray(np.cos(ang), F32), jnp.asarray(np.sin(ang), F32)
    n = len(pos)
    ones = jnp.ones((n, PE_LO), F32)
    z = lambda w: jnp.zeros((n, w), F32)
    tail = LANES - PE_LO - ROPE_DIM
    cosb = jnp.concatenate([ones, cos, cos, z(tail)], axis=1)
    sin_lo = jnp.concatenate([z(PE_LO), -sin, z(PE_HALF + tail)], axis=1)
    sin_hi = jnp.concatenate([z(PE_LO + PE_HALF), sin, z(tail)], axis=1)
    return cosb, sin_lo, sin_hi


def _prompt_group(x_prompt, meta_tokens, p):
    seq, d = x_prompt.shape
    tp = P0 + seq
    meta_row0 = P0 - N_META
    lead = jnp.concatenate([jnp.zeros((meta_row0, d), F32), meta_tokens.astype(F32)], axis=0)
    q, k, v, gb, pre_tail, z, cq, ckv_raw, misc = front_conv_prompt(
        x_prompt, lead, p["norm1"], p["w_in"], p["conv_w"], p["a_log"], p["dt_bias"], tm=P0,
        first_valid_row=meta_row0)
    s0 = jnp.zeros((1, GDN_HEADS, GDN_DK, GDN_DV), F32)
    gdn_o, s_fin = gdn(q, k, v, gb, s0, C=CHUNK, CB_prep=GDN_PREP_CHUNKS, CB_scan=GDN_SCAN_CHUNKS,
                       SB_scan=1, n_seq=1)
    tabs = _rope_tables(np.arange(tp) - meta_row0)
    ckv, kpe, qh, kh, vt = mla_prep(cq, ckv_raw, misc, tabs, p, tm=ROW_TILE, with_kv=True)
    mla_o = flash_prompt(qh, kh, vt, tq=FLASH_TQ, tk=FLASH_TK, n_prefix=N_META)
    y = layer_back(x_prompt, gdn_o, z, mla_o, p, tm=BACK_TILE, row_block0=P0 // BACK_TILE)
    new_conv = pre_tail[SUBLANES - (CONV_WIDTH - 1):]
    return (y, new_conv, s_fin[0], ckv[meta_row0:], kpe[meta_row0:, PE_LO:PE_LO + ROPE_DIM])


def _sample_group(x_sample, cache_conv, state_gdn, cache_ckv, cache_kpe, page_table, p):
    b, L, d = x_sample.shape
    m = b * L
    page = cache_ckv.shape[2]
    past_len = page_table.shape[1] * page
    x = x_sample.reshape(m, d)
    tm = min(m, ROW_TILE)
    qkv_pre, z, cq, ckv_raw, misc = front_proj(x, p["norm1"], p["w_in"], tm=tm)
    xp = jnp.concatenate([jnp.transpose(cache_conv.astype(F32), (1, 0, 2)),
                          jnp.transpose(qkv_pre.reshape(b, L, CONV_DIM), (1, 0, 2))], axis=0)
    misc_t = jnp.transpose(misc.reshape(b, L, MISC_W), (1, 0, 2))
    q, k, v, gb = conv_sample(xp, misc_t, p["conv_w"], p["a_log"], p["dt_bias"])
    C = SAMPLE_CHUNK
    to_chunks = lambda t: jnp.pad(jnp.transpose(t, (1, 0, 2)), ((0, 0), (0, C - L), (0, 0))).reshape(b * C, -1)
    gdn_o, s_fin = gdn(to_chunks(q), to_chunks(k), to_chunks(v), to_chunks(gb), state_gdn.astype(F32),
                       C=C, CB_prep=SAMPLE_PREP_CHUNKS if b % SAMPLE_PREP_CHUNKS == 0 else 1, CB_scan=1,
                       SB_scan=SAMPLE_SCAN_SEQS if b % SAMPLE_SCAN_SEQS == 0 else 1, n_seq=b)
    gdn_o = gdn_o.reshape(b, C, GDN_V_WIDTH)[:, :L].reshape(m, GDN_V_WIDTH)
    tabs = _rope_tables(np.tile(past_len + np.arange(L), b))
    ckv, kpe, qh = mla_prep(cq, ckv_raw, misc, tabs, p, tm=tm, with_kv=False)
    kpe32 = kpe[:, PE_LO:PE_LO + ROPE_DIM]
    qabs = q_absorb(qh, p["w_abs"])
    R = L * MLA_HEADS
    rows_lh = lambda t: jnp.transpose(t.reshape(MLA_HEADS, b, L, -1), (1, 2, 0, 3)).reshape(b, R, -1)
    qpe = rows_lh(qh[:, :, PE_LO:PE_LO + ROPE_DIM])
    new_ckv = jnp.pad(ckv.reshape(b, L, KV_RANK), ((0, 0), (0, SUBLANES - L), (0, 0)))
    new_kpe_t = jnp.pad(jnp.swapaxes(kpe32.reshape(b, L, ROPE_DIM), 1, 2), ((0, 0), (0, 0), (0, SUBLANES - L)))
    n_pages = page_table.shape[1]
    PG = DECODE_STEP_PAGES if n_pages % DECODE_STEP_PAGES == 0 else DECODE_SCORE_PAGES
    o_lat = decode_attn(page_table, p["w_uk_t"], rows_lh(qabs), qpe, new_ckv, new_kpe_t,
                        cache_ckv, jnp.swapaxes(cache_kpe, 2, 3), PG=PG, L=L)
    o_lat_h = jnp.transpose(o_lat.reshape(b, L, MLA_HEADS, KV_RANK), (2, 0, 1, 3)).reshape(MLA_HEADS, m, KV_RANK)
    mla_o = latent_out(o_lat_h, p["w_uv_h"])
    y = layer_back(x, gdn_o, z, mla_o, p, tm=tm, row_block0=0)
    new_conv = qkv_pre.reshape(b, L, CONV_DIM)[:, L - (CONV_WIDTH - 1):]
    return (y.reshape(b, L, d), new_conv, s_fin, ckv.reshape(b, L, KV_RANK), kpe32.reshape(b, L, ROPE_DIM))


def kernel(x_prompt, x_sample, cache_conv, state_gdn, cache_ckv, cache_kpe, page_table, meta_tokens, norm1, w_in, conv_w, a_log, dt_bias, gdn_norm, q_a_norm, w_q_b, q_nope_norm, q_pe_norm, kv_a_norm, k_pe_norm, w_uk, w_uv, k_nope_norm, w_o, norm2, w_gate, w_up, w_down):
    assert x_prompt.shape[0] == 1 and norm1.shape[0] == 1, "one prompt sequence, one layer"
    p = _prep_params(norm1[0], w_in[0], conv_w[0], a_log[0], dt_bias[0], gdn_norm[0], q_a_norm[0],
                     w_q_b[0], q_nope_norm[0], q_pe_norm[0], kv_a_norm[0], k_pe_norm[0], w_uk[0],
                     w_uv[0], k_nope_norm[0], w_o[0], norm2[0], w_gate[0], w_up[0], w_down[0])
    y_p, conv_p, st_p, ckv_p, kpe_p = _prompt_group(x_prompt[0], meta_tokens, p)
    y_s, conv_s, st_s, ckv_s, kpe_s = _sample_group(x_sample, cache_conv[0], state_gdn[0], cache_ckv,
                                                     cache_kpe, page_table, p)
    lead = lambda t: t[None, None]
    return (y_p[None], y_s, lead(conv_p), lead(st_p), lead(ckv_p), lead(kpe_p),
            conv_s[None], st_s[None], ckv_s[None], kpe_s[None])
```
